```python
import jax, jax.numpy as jnp
from jax import lax
import numpy as np

D_MODEL = 1024
BATCH = 2
SEQ = 8192
DEPTH = 2
DEC_BATCH = 128
DEC_SEQ = 4
PAST_LEN = 8192
PAGE_SIZE = 128

R_HEADS = 8
R_HEAD_DIM = 64
R_WIDTH = R_HEADS * R_HEAD_DIM
DECAY_LORA = 64
ICLR_LORA = 64
GATE_LORA = 128
SHIFT_W = 3 * R_WIDTH + DECAY_LORA + ICLR_LORA + GATE_LORA
A_HEADS = 8
KV_HEADS = 2
HEAD_DIM = 64
Q_WIDTH = A_HEADS * HEAD_DIM
KV_WIDTH = KV_HEADS * HEAD_DIM
GROUP = A_HEADS // KV_HEADS
WINDOW = 128
BLOCK = 128
ROPE_THETA = 10000.0
ATTN_SCALE = HEAD_DIM ** -0.5
IN_W = SHIFT_W + Q_WIDTH + 2 * KV_WIDTH + 2 * D_MODEL
D_FF = 4 * D_MODEL
ALPHA = (2 * DEPTH) ** 0.25
BETA = (8 * DEPTH) ** -0.25
LN_EPS = 1e-5
GN_EPS = 64e-5

kernel_name = 'rwkv7_swa_sink_gated_hybrid_step'


def layer_norm(x, g, b):
    xf = x.astype(jnp.float32)
    mu = jnp.mean(xf, -1, keepdims=True)
    var = jnp.mean(jnp.square(xf - mu), -1, keepdims=True)
    return ((xf - mu) * lax.rsqrt(var + LN_EPS)).astype(x.dtype) * g + b


def rope(x, pos):
    half = HEAD_DIM // 2
    inv_freq = ROPE_THETA ** (-jnp.arange(half, dtype=jnp.float32) / half)
    ang = pos.astype(jnp.float32)[:, None] * inv_freq[None, :]
    cos = jnp.cos(ang)[None, :, None, :]
    sin = jnp.sin(ang)[None, :, None, :]
    xf = x.astype(jnp.float32)
    x1, x2 = xf[..., :half], xf[..., half:]
    return jnp.concatenate([x1 * cos - x2 * sin, x2 * cos + x1 * sin], -1).astype(x.dtype)


def wkv7_scan(S0, r, logw, k, v, kk, a):
    def step(S, inp):
        r_t, lw_t, k_t, v_t, kk_t, a_t = inp
        sa = jnp.einsum('bhvk,bhk->bhv', S, -kk_t)
        S = (S * jnp.exp(lw_t)[:, :, None, :] + sa[..., None] * (kk_t * a_t)[:, :, None, :]
             + v_t[..., None] * k_t[:, :, None, :])
        return S, jnp.einsum('bhvk,bhk->bhv', S, r_t)
    xs = tuple(jnp.moveaxis(t, 1, 0) for t in (r, logw, k, v, kk, a))
    S, ys = lax.scan(step, S0, xs)
    return jnp.moveaxis(ys, 0, 1), S


def rwkv7_branch(z, z_prev, S0, P):
    B, T, _ = z.shape
    f32 = jnp.float32
    prev = jnp.concatenate([z_prev[:, None].astype(z.dtype), z[:, :-1]], axis=1)
    zs = (z + (prev - z) * P['mu_shift']).astype(f32)
    o1, o2, o3 = R_WIDTH, 2 * R_WIDTH, 3 * R_WIDTH
    o4, o5 = o3 + DECAY_LORA, o3 + DECAY_LORA + ICLR_LORA
    r, k, v = zs[..., :o1], zs[..., o1:o2], zs[..., o2:o3]
    wd, ad, gd = zs[..., o3:o4], zs[..., o4:o5], zs[..., o5:]
    pre_w = P['decay_base'].astype(f32) + jnp.tanh(wd) @ P['decay_up'].astype(f32)
    logw = -jnp.exp(-jax.nn.softplus(-pre_w) - 0.5)
    a = jax.nn.sigmoid(P['iclr_base'].astype(f32) + ad @ P['iclr_up'].astype(f32))
    g = jax.nn.sigmoid(gd) @ P['gate_up'].astype(f32)
    heads = lambda t: t.reshape(B, T, R_HEADS, R_HEAD_DIM)
    kk = heads(k * P['k_k'].astype(f32))
    kk = kk * lax.rsqrt(jnp.maximum(jnp.sum(kk * kk, -1, keepdims=True), 1e-24))
    k = k * (1.0 + (a - 1.0) * P['k_a'].astype(f32))
    r, logw, k, v, a = heads(r), heads(logw), heads(k), heads(v), heads(a)
    y, S = wkv7_scan(S0.astype(f32), r, logw, k, v, kk, a)
    mu = jnp.mean(y, -1, keepdims=True)
    var = jnp.mean(jnp.square(y - mu), -1, keepdims=True)
    y = ((y - mu) * lax.rsqrt(var + GN_EPS)).reshape(B, T, R_WIDTH)
    y = y * P['lnx_g'].astype(f32) + P['lnx_b'].astype(f32)
    bonus = jnp.sum(r * k * P['r_k'].astype(f32), -1, keepdims=True) * v
    y = (y + bonus.reshape(B, T, R_WIDTH)) * g
    return y.astype(z.dtype), S.astype(S0.dtype)


def sink_softmax(s, mask, sink):
    s = jnp.where(mask, s, -jnp.inf)
    sk = sink.astype(jnp.float32)[:, :, None, None]
    m = jnp.maximum(jnp.max(s, -1, keepdims=True), sk)
    p = jnp.exp(s - m)
    return p / (jnp.sum(p, -1, keepdims=True) + jnp.exp(sk - m))


def attn_prompt(q, k, v, sinks):
    B, T = q.shape[:2]
    nb = T // BLOCK
    f32 = jnp.float32
    qb = q.astype(f32).reshape(B, nb, BLOCK, KV_HEADS, GROUP, HEAD_DIM)
    def band(t):
        tb = t.astype(f32).reshape(B, nb, BLOCK, KV_HEADS, HEAD_DIM)
        prev = jnp.pad(tb[:, :-1], ((0, 0), (1, 0), (0, 0), (0, 0), (0, 0)))
        return jnp.concatenate([prev, tb], axis=2)
    kb, vb = band(k), band(v)
    s = jnp.einsum('bnqhgd,bnkhd->bnhgqk', qb, kb) * ATTN_SCALE
    qi = jnp.arange(BLOCK)[:, None] + BLOCK
    kj = jnp.arange(2 * BLOCK)[None, :]
    dist = qi - kj
    in_win = (dist >= 0) & (dist <= WINDOW)
    has_prev = (jnp.arange(nb) > 0)[:, None, None] | (kj >= BLOCK)[None]
    mask = (in_win[None] & has_prev)[:, None, None]
    p = sink_softmax(s, mask, sinks.reshape(KV_HEADS, GROUP))
    o = jnp.einsum('bnhgqk,bnkhd->bnqhgd', p, vb)
    return o.reshape(B, T, Q_WIDTH).astype(q.dtype)


def attn_sample(q, k, v, k_buf, v_buf, sinks):
    B, T = q.shape[:2]
    W = k_buf.shape[1]
    f32 = jnp.float32
    kc = jnp.concatenate([k_buf.astype(k.dtype), k], axis=1)
    vc = jnp.concatenate([v_buf.astype(v.dtype), v], axis=1)
    qpos = PAST_LEN + jnp.arange(T)
    kpos = PAST_LEN - W + jnp.arange(W + T)
    dist = qpos[:, None] - kpos[None, :]
    mask = (dist >= 0) & (dist <= WINDOW)
    qg = q.astype(f32).reshape(B, T, KV_HEADS, GROUP, HEAD_DIM)
    s = jnp.einsum('bqhgd,bkhd->bhgqk', qg, kc.astype(f32)) * ATTN_SCALE
    p = sink_softmax(s, mask, sinks.reshape(KV_HEADS, GROUP))
    o = jnp.einsum('bhgqk,bkhd->bqhgd', p, vc.astype(f32))
    return o.reshape(B, T, Q_WIDTH).astype(q.dtype), kc[:, -W:], vc[:, -W:]


def trunk_layer(x, pos, shift_prev, S0, k_buf, v_buf, wbuf, P):
    B, T, _ = x.shape
    z = jnp.einsum('btd,de->bte', x, P['w_in'])
    o_q = SHIFT_W
    o_k = o_q + Q_WIDTH
    o_v = o_k + KV_WIDTH
    o_ga = o_v + KV_WIDTH
    o_gb = o_ga + D_MODEL
    z_r = z[..., :o_q]
    q = rope(z[..., o_q:o_k].reshape(B, T, A_HEADS, HEAD_DIM), pos)
    k = rope(z[..., o_k:o_v].reshape(B, T, KV_HEADS, HEAD_DIM), pos)
    v = z[..., o_v:o_ga].reshape(B, T, KV_HEADS, HEAD_DIM)
    gate_r = jax.nn.sigmoid(z[..., o_ga:o_gb])
    gate_a = jax.nn.sigmoid(z[..., o_gb:])
    y_r, S_new = rwkv7_branch(z_r, shift_prev, S0, P)
    if k_buf is None:
        y_a = attn_prompt(q, k, v, P['sinks'])
        k_new, v_new = k[:, -wbuf:], v[:, -wbuf:]
    else:
        y_a, k_new, v_new = attn_sample(q, k, v, k_buf, v_buf, P['sinks'])
    mix = (gate_r * jnp.einsum('btc,cd->btd', y_r, P['w_br_rwkv'])
           + gate_a * jnp.einsum('btc,cd->btd', y_a, P['w_br_attn']))
    x = layer_norm(ALPHA * x + mix @ P['w_out'], P['ln1_g'], P['ln1_b'])
    h = jnp.square(jax.nn.relu(x @ P['w_ff_up']))
    x = layer_norm(ALPHA * x + h @ P['w_ff_down'], P['ln2_g'], P['ln2_b'])
    return x, z_r[:, -1], S_new, k_new, v_new


def setup_inputs(seed: int = 0) -> dict:
    key = jax.random.key(seed)
    ks = iter(jax.random.split(key, 32))
    f32 = jnp.float32
    nrm = lambda shape, s: jax.random.normal(next(ks), shape, f32) * s
    L = DEPTH
    wbuf = min(WINDOW, PAST_LEN)
    return {
        'x_prompt': nrm((BATCH, SEQ, D_MODEL), 1.0),
        'x_sample': nrm((DEC_BATCH, DEC_SEQ, D_MODEL), 1.0),
        'state_wkv': nrm((L, DEC_BATCH, R_HEADS, R_HEAD_DIM, R_HEAD_DIM), 0.5),
        'state_shift': nrm((L, DEC_BATCH, SHIFT_W), 1.0),
        'cache_k_win': nrm((L, DEC_BATCH, wbuf, KV_HEADS, HEAD_DIM), 1.0),
        'cache_v_win': nrm((L, DEC_BATCH, wbuf, KV_HEADS, HEAD_DIM), 1.0),
        'w_in': nrm((L, D_MODEL, IN_W), D_MODEL ** -0.5),
        'mu_shift': jax.random.uniform(next(ks), (L, SHIFT_W), f32),
        'decay_base': jax.random.uniform(next(ks), (L, R_WIDTH), f32, -6.0, -1.0),
        'decay_up': nrm((L, DECAY_LORA, R_WIDTH), 0.5 * DECAY_LORA ** -0.5),
        'iclr_base': nrm((L, R_WIDTH), 0.1),
        'iclr_up': nrm((L, ICLR_LORA, R_WIDTH), 0.5 * ICLR_LORA ** -0.5),
        'gate_up': nrm((L, GATE_LORA, R_WIDTH), GATE_LORA ** -0.5),
        'k_k': 0.85 + nrm((L, R_WIDTH), 0.05),
        'k_a': 1.0 + nrm((L, R_WIDTH), 0.05),
        'r_k': nrm((L, R_HEADS, R_HEAD_DIM), 0.1),
        'lnx_g': 1.0 + nrm((L, R_WIDTH), 0.05),
        'lnx_b': nrm((L, R_WIDTH), 0.02),
        'sinks': nrm((L, A_HEADS), 0.5),
        'w_br_rwkv': nrm((L, R_WIDTH, D_MODEL), BETA * R_WIDTH ** -0.5),
        'w_br_attn': nrm((L, Q_WIDTH, D_MODEL), BETA * Q_WIDTH ** -0.5),
        'w_out': nrm((L, D_MODEL, D_MODEL), BETA * D_MODEL ** -0.5),
        'ln1_g': 1.0 + nrm((L, D_MODEL), 0.05),
        'ln1_b': nrm((L, D_MODEL), 0.02),
        'w_ff_up': nrm((L, D_MODEL, D_FF), D_MODEL ** -0.5),
        'w_ff_down': nrm((L, D_FF, D_MODEL), BETA * D_FF ** -0.5),
        'ln2_g': 1.0 + nrm((L, D_MODEL), 0.05),
        'ln2_b': nrm((L, D_MODEL), 0.02),
    }


def reference(x_prompt, x_sample, state_wkv, state_shift, cache_k_win, cache_v_win,
              w_in, mu_shift, decay_base, decay_up, iclr_base, iclr_up, gate_up, k_k, k_a, r_k,
              lnx_g, lnx_b, sinks, w_br_rwkv, w_br_attn, w_out, ln1_g, ln1_b,
              w_ff_up, w_ff_down, ln2_g, ln2_b):
    wbuf = cache_k_win.shape[2]
    Bp, Tp, _ = x_prompt.shape
    Ts = x_sample.shape[1]
    pos_p = jnp.arange(Tp, dtype=jnp.int32)
    pos_s = PAST_LEN + jnp.arange(Ts, dtype=jnp.int32)
    hp, hs = x_prompt, x_sample
    p_wkv, p_shift, p_k, p_v = [], [], [], []
    s_wkv, s_shift, s_k, s_v = [], [], [], []
    for l in range(DEPTH):
        P = dict(w_in=w_in[l], mu_shift=mu_shift[l], decay_base=decay_base[l], decay_up=decay_up[l],
                 iclr_base=iclr_base[l], iclr_up=iclr_up[l], gate_up=gate_up[l], k_k=k_k[l], k_a=k_a[l],
                 r_k=r_k[l], lnx_g=lnx_g[l], lnx_b=lnx_b[l], sinks=sinks[l], w_br_rwkv=w_br_rwkv[l],
                 w_br_attn=w_br_attn[l], w_out=w_out[l], ln1_g=ln1_g[l], ln1_b=ln1_b[l],
                 w_ff_up=w_ff_up[l], w_ff_down=w_ff_down[l], ln2_g=ln2_g[l], ln2_b=ln2_b[l])
        shift0 = jnp.zeros((Bp, SHIFT_W), hp.dtype)
        S0 = jnp.zeros((Bp, R_HEADS, R_HEAD_DIM, R_HEAD_DIM), state_wkv.dtype)
        hp, sh, S, kb, vb = trunk_layer(hp, pos_p, shift0, S0, None, None, wbuf, P)
        p_wkv.append(S); p_shift.append(sh); p_k.append(kb); p_v.append(vb)
        hs, sh, S, kb, vb = trunk_layer(hs, pos_s, state_shift[l], state_wkv[l],
                                        cache_k_win[l], cache_v_win[l], wbuf, P)
        s_wkv.append(S); s_shift.append(sh); s_k.append(kb); s_v.append(vb)
    return (hp, hs,
            jnp.stack(p_wkv), jnp.stack(p_shift), jnp.stack(p_k), jnp.stack(p_v),
            jnp.stack(s_wkv), jnp.stack(s_shift), jnp.stack(s_k), jnp.stack(s_v))
```

```python
import functools
import math

import jax
import jax.numpy as jnp
from jax import lax
from jax.experimental import pallas as pl
from jax.experimental.pallas import tpu as pltpu

F32 = jnp.float32
BF16 = jnp.bfloat16

D_MODEL = 1024
DEPTH = 2
PAST_LEN = 8192
R_HEADS = 8
R_HEAD_DIM = 64
R_WIDTH = R_HEADS * R_HEAD_DIM
DECAY_LORA = 64
ICLR_LORA = 64
GATE_LORA = 128
SHIFT_W = 3 * R_WIDTH + DECAY_LORA + ICLR_LORA + GATE_LORA
A_HEADS = 8
KV_HEADS = 2
HEAD_DIM = 64
Q_WIDTH = A_HEADS * HEAD_DIM
KV_WIDTH = KV_HEADS * HEAD_DIM
GROUP = A_HEADS // KV_HEADS
WINDOW = 128
BLOCK = 128
ROPE_THETA = 10000.0
ATTN_SCALE = HEAD_DIM ** -0.5
D_FF = 4 * D_MODEL
ALPHA = (2 * DEPTH) ** 0.25
LN_EPS = 1e-5
GN_EPS = 64e-5
QKV_END = SHIFT_W + Q_WIDTH + 2 * KV_WIDTH

LANES = 128
SUBLANES = 8
VMEM_LIMIT = 56 * 1024 * 1024

SAMPLE_PAD = SUBLANES
N_PAIRS = R_HEADS // 2


def _cparams(*sem):
    return pltpu.CompilerParams(dimension_semantics=sem, vmem_limit_bytes=VMEM_LIMIT)


def _const_spec(shape):
    nd = len(shape)
    return pl.BlockSpec(shape, lambda *_: (0,) * nd)


def _dot(a, b):
    return jnp.dot(a.astype(BF16), b.astype(BF16), preferred_element_type=F32)


def _dot_nt(a, b):
    return lax.dot_general(a.astype(BF16), b.astype(BF16), (((1,), (1,)), ((), ())),
                           preferred_element_type=F32)


def _pmod(x, n):
    assert n & (n - 1) == 0
    return x & (n - 1)


def _pdiv(x, n):
    assert n & (n - 1) == 0
    return x >> (n.bit_length() - 1)


def _split(x):
    hi = x.astype(BF16)
    lo = (x - hi.astype(F32)).astype(BF16)
    return hi, lo


def _dot3(a, b):
    ah, al = _split(a)
    bh, bl = _split(b)
    d = functools.partial(jnp.dot, preferred_element_type=F32)
    return d(ah, bh) + (d(ah, bl) + d(al, bh))


def _dot3_nt(a, b):
    ah, al = _split(a)
    bh, bl = _split(b)
    d = functools.partial(lax.dot_general, dimension_numbers=(((1,), (1,)), ((), ())),
                          preferred_element_type=F32)
    return d(ah, bh) + (d(ah, bl) + d(al, bh))


def _rope_table_kernel(invf_ref, cos_ref, sa_ref, sb_ref, *, rows, offset, period):
    i = pl.program_id(0)
    row = lax.broadcasted_iota(jnp.int32, (rows, LANES), 0) + i * rows
    pos = offset + _pmod(row, period)
    ang = pos.astype(F32) * invf_ref[...]
    lane = lax.broadcasted_iota(jnp.int32, (rows, LANES), 1)
    first = _pmod(lane, HEAD_DIM) < (HEAD_DIM // 2)
    c = jnp.cos(ang)
    s = jnp.sin(ang)
    cos_ref[...] = c
    sa_ref[...] = jnp.where(first, -s, 0.0)
    sb_ref[...] = jnp.where(first, 0.0, s)


def _rope_tables(inv_freq, n_rows, offset, period):
    rows = min(n_rows, 1024)
    invf = jnp.tile(inv_freq, LANES // (HEAD_DIM // 2)).reshape(1, LANES)
    out = jax.ShapeDtypeStruct((n_rows, LANES), F32)
    return pl.pallas_call(
        functools.partial(_rope_table_kernel, rows=rows, offset=offset, period=period),
        grid=(n_rows // rows,),
        in_specs=[_const_spec((1, LANES))],
        out_specs=[pl.BlockSpec((rows, LANES), lambda i: (i, 0))] * 3,
        out_shape=[out] * 3,
        compiler_params=_cparams("parallel"),
        name="rope_tables",
    )(invf)


def _inproj_kernel(x_ref, w_ref, cos_ref, sa_ref, sb_ref, zr_ref, q_ref, k_ref, v_ref):
    xb = x_ref[...].astype(BF16)
    zr_ref[...] = jnp.dot(xb, w_ref[:, :SHIFT_W], preferred_element_type=F32)
    cos, sa, sb = cos_ref[...], sa_ref[...], sb_ref[...]

    def rope(t):
        return (t * cos + pltpu.roll(t, LANES - HEAD_DIM // 2, 1) * sa
                + pltpu.roll(t, HEAD_DIM // 2, 1) * sb)

    zq = jnp.dot(xb, w_ref[:, SHIFT_W:SHIFT_W + Q_WIDTH], preferred_element_type=F32)
    for j in range(Q_WIDTH // LANES):
        q_ref[:, j * LANES:(j + 1) * LANES] = rope(zq[:, j * LANES:(j + 1) * LANES])
    zkv = jnp.dot(xb, w_ref[:, SHIFT_W + Q_WIDTH:QKV_END], preferred_element_type=F32)
    k_ref[...] = rope(zkv[:, :KV_WIDTH])
    v_ref[...] = zkv[:, KV_WIDTH:]


def _inproj(x, w, tables, tab_blocks, tm):
    n = x.shape[0]
    row = lambda i: (i, 0)
    tab = pl.BlockSpec((tm, LANES), lambda i: (lax.rem(i, tab_blocks), 0))
    return pl.pallas_call(
        _inproj_kernel,
        grid=(n // tm,),
        in_specs=[pl.BlockSpec((tm, D_MODEL), row), _const_spec(w.shape), tab, tab, tab],
        out_specs=[pl.BlockSpec((tm, SHIFT_W), row), pl.BlockSpec((tm, Q_WIDTH), row),
                   pl.BlockSpec((tm, KV_WIDTH), row), pl.BlockSpec((tm, KV_WIDTH), row)],
        out_shape=[jax.ShapeDtypeStruct((n, SHIFT_W), F32), jax.ShapeDtypeStruct((n, Q_WIDTH), F32),
                   jax.ShapeDtypeStruct((n, KV_WIDTH), F32), jax.ShapeDtypeStruct((n, KV_WIDTH), F32)],
        compiler_params=_cparams("parallel"),
        name="inproj",
    )(x, w, *tables)


def _wkv_kernel(zr_ref, shift_ref, sin_ref, mu_ref, db_ref, lora_ref, ib_ref, gu_ref, kk_ref, ka_ref,
                rk_ref, lg_ref, lb_ref,
                y_ref, sout_ref,
                sbd, carry, prev_s, r_s, k_s, v_s, kk_s, b_s, lw_s, cw_s, y_s, bon_s, g_s,
                *, n_seq, seq_rows, chunk, valid_rows):
    t_step = pl.program_id(1)
    n_steps = pl.num_programs(1)
    rows = n_seq * seq_rows
    per_seq_state = n_seq > 1
    n_chunks = rows // chunk
    c2, c4 = 2 * chunk, 4 * chunk

    z = zr_ref[...]
    prev_s[...] = pltpu.roll(z, 1, 0)
    if per_seq_state:
        for g in range(n_seq):
            prev_s[g * seq_rows:g * seq_rows + 1, :] = shift_ref[g]
    else:
        @pl.when(t_step == 0)
        def _():
            carry[...] = shift_ref[0]
        prev_s[0:1, :] = carry[...]
        carry[...] = z[rows - 1:rows, :]
    prev = prev_s[...]
    zs = z + (prev - z) * mu_ref[...]

    o1, o2, o3 = R_WIDTH, 2 * R_WIDTH, 3 * R_WIDTH
    o4 = o3 + DECAY_LORA + ICLR_LORA
    r = zs[:, :o1]
    k = zs[:, o1:o2]
    v = zs[:, o2:o3]
    wa = zs[:, o3:o4]
    gd = zs[:, o4:]

    lane = lax.broadcasted_iota(jnp.int32, (1, LANES), 1)
    lane_even = lane < R_HEAD_DIM
    me = lane_even.astype(F32)
    mo = 1.0 - me
    bi = _pdiv(lax.broadcasted_iota(jnp.int32, (LANES, LANES), 0), R_HEAD_DIM)
    bj = _pdiv(lax.broadcasted_iota(jnp.int32, (LANES, LANES), 1), R_HEAD_DIM)
    m_bd = (bi == bj).astype(F32)
    bd_ones = m_bd.astype(BF16)

    def seg_sum(x):
        outs = []
        for j in range(R_WIDTH // LANES):
            hi, lo = _split(x[:, j * LANES:(j + 1) * LANES])
            outs.append(jnp.dot(hi, bd_ones, preferred_element_type=F32)
                        + jnp.dot(lo, bd_ones, preferred_element_type=F32))
        return jnp.concatenate(outs, axis=1)

    lora_in = jnp.where(lane_even, jnp.tanh(wa), wa)
    lora = _dot(lora_in, lora_ref[...])
    pre_w = db_ref[...] + lora[:, :R_WIDTH]
    neg = -pre_w
    softplus = jnp.maximum(neg, 0.0) + jnp.log1p(jnp.exp(-jnp.abs(neg)))
    logw = -jnp.exp(-softplus - 0.5)
    a = jax.nn.sigmoid(ib_ref[...] + lora[:, R_WIDTH:])
    g_s[...] = _dot(jax.nn.sigmoid(gd), gu_ref[...])
    kk = k * kk_ref[...]
    kk = kk * lax.rsqrt(jnp.maximum(seg_sum(kk * kk), 1e-24))
    kp = k * (1.0 + (a - 1.0) * ka_ref[...])
    bon_s[...] = seg_sum(r * kp * rk_ref[...]) * v

    if valid_rows < seq_rows:
        rid = _pmod(lax.broadcasted_iota(jnp.int32, (rows, 1), 0), seq_rows)
        ok = (rid < valid_rows).astype(F32)
        logw, kk, kp, v = logw * ok, kk * ok, kp * ok, v * ok

    r_s[...] = r
    k_s[...] = kp
    v_s[...] = v
    kk_s[...] = kk
    b_s[...] = kk * a
    lw_s[...] = logw

    tid = _pmod(lax.broadcasted_iota(jnp.int32, (rows, 1), 0), chunk)
    cw = logw
    sh = 1
    while sh < chunk:
        cw = cw + jnp.where(tid >= sh, pltpu.roll(cw, sh, 0), 0.0)
        sh *= 2
    cw_s[...] = cw

    si = lax.broadcasted_iota(jnp.int32, (c2, c2), 0)
    sj = lax.broadcasted_iota(jnp.int32, (c2, c2), 1)
    same = _pdiv(si, chunk) == _pdiv(sj, chunk)
    m_strict = (same & (_pmod(sj, chunk) < _pmod(si, chunk))).astype(F32)
    m_incl = (same & (_pmod(sj, chunk) <= _pmod(si, chunk))).astype(F32)
    eye = (si == sj).astype(F32)
    n_fact = int(math.log2(chunk))

    def load_state(g):
        for p in range(N_PAIRS):
            zero = jnp.zeros((R_HEAD_DIM, R_HEAD_DIM), F32)
            sbd[p, 0:R_HEAD_DIM, :] = jnp.concatenate([sin_ref[g, 2 * p], zero], axis=1)
            sbd[p, R_HEAD_DIM:LANES, :] = jnp.concatenate([zero, sin_ref[g, 2 * p + 1]], axis=1)

    def store_state(g):
        for p in range(N_PAIRS):
            sout_ref[g, 2 * p] = sbd[p, 0:R_HEAD_DIM, 0:R_HEAD_DIM]
            sout_ref[g, 2 * p + 1] = sbd[p, R_HEAD_DIM:LANES, R_HEAD_DIM:LANES]

    if not per_seq_state:
        @pl.when(t_step == 0)
        def _():
            load_state(0)

    def chunk_body(c, carry_):
        if per_seq_state:
            load_state(c)
        rs = pl.ds(pl.multiple_of(c * chunk, chunk), chunk)
        for p in range(N_PAIRS):
            cs = slice(p * LANES, (p + 1) * LANES)
            lw = lw_s[rs, cs]
            cwc = cw_s[rs, cs]
            w_in = jnp.exp(cwc)
            w_ex = jnp.exp(cwc - lw)
            w_inv = jnp.exp(-cwc)
            at = -kk_s[rs, cs] * w_ex
            rt = r_s[rs, cs] * w_in
            bt = b_s[rs, cs] * w_inv
            kt = k_s[rs, cs] * w_inv
            vv = v_s[rs, cs]
            w_end = w_in[chunk - 1:chunk, :]

            lhs = jnp.concatenate([at * me, at * mo, rt * me, rt * mo], axis=0)
            rhs = jnp.concatenate([bt, bt, kt, kt], axis=0)
            sc = _dot3_nt(lhs, rhs)
            l_ab = sc[:c2, :c2] * m_strict
            a_ak = sc[:c2, c2:] * m_strict
            a_rb = sc[c2:, :c2] * m_incl
            a_rk = sc[c2:, c2:] * m_incl

            pw = l_ab
            inv = eye + l_ab
            for _ in range(n_fact - 1):
                pw = _dot3(pw, pw)
                inv = _dot3(inv, eye + pw)

            vs = jnp.concatenate([vv * me, vv * mo], axis=0)
            s0 = sbd[p]
            x = _dot3_nt(lhs, s0)
            us = _dot3(inv, x[:c2] + _dot3(a_ak, vs))
            uv = jnp.concatenate([us, vs], axis=0)
            ys = x[c2:] + _dot3(jnp.concatenate([a_rb, a_rk], axis=1), uv)
            y_s[rs, cs] = ys[:chunk] + ys[chunk:]
            upd = _dot3(uv.T, rhs)
            sbd[p] = (s0 + upd * m_bd) * w_end
        if per_seq_state:
            store_state(c)
        return carry_

    lax.fori_loop(0, n_chunks, chunk_body, 0)

    if not per_seq_state:
        @pl.when(t_step == n_steps - 1)
        def _():
            store_state(0)

    y = y_s[...]
    inv_n = 1.0 / R_HEAD_DIM
    mean = seg_sum(y) * inv_n
    d = y - mean
    var = seg_sum(d * d) * inv_n
    yn = d * lax.rsqrt(var + GN_EPS)
    yn = yn * lg_ref[...] + lb_ref[...]
    y_ref[...] = ((yn + bon_s[...]) * g_s[...]).astype(y_ref.dtype)


def _wkv(zr, shift_in, s_in, P, *, n_seq, seq_rows, chunk, valid_rows, n_steps):
    rows = n_seq * seq_rows
    n_groups = shift_in.shape[0] // n_seq
    n = zr.shape[0]
    small = [P['mu_shift'], P['decay_base'], P['lora_up'], P['iclr_base'], P['gate_up'], P['k_k'], P['k_a'],
             P['r_k'], P['lnx_g'], P['lnx_b']]
    tok = lambda b, t: (b * n_steps + t, 0)
    seq3 = lambda b, t: (b, 0, 0)
    seq4 = lambda b, t: (b, 0, 0, 0)
    state_block = (n_seq, R_HEADS, R_HEAD_DIM, R_HEAD_DIM)
    tile = lambda w: pltpu.VMEM((rows, w), F32)
    return pl.pallas_call(
        functools.partial(_wkv_kernel, n_seq=n_seq, seq_rows=seq_rows, chunk=chunk, valid_rows=valid_rows),
        grid=(n_groups, n_steps),
        in_specs=[pl.BlockSpec((rows, SHIFT_W), tok), pl.BlockSpec((n_seq, 1, SHIFT_W), seq3),
                  pl.BlockSpec(state_block, seq4)] + [_const_spec(a.shape) for a in small],
        out_specs=[pl.BlockSpec((rows, R_WIDTH), tok), pl.BlockSpec(state_block, seq4)],
        out_shape=[jax.ShapeDtypeStruct((n, R_WIDTH), BF16), jax.ShapeDtypeStruct(s_in.shape, F32)],
        scratch_shapes=[pltpu.VMEM((N_PAIRS, LANES, LANES), F32), pltpu.VMEM((1, SHIFT_W), F32),
                        tile(SHIFT_W)] + [tile(R_WIDTH)] * 10,
        compiler_params=_cparams("parallel", "arbitrary"),
        name="wkv7",
    )(zr, shift_in, s_in, *small)


def _pair_kv(t):
    lane = lax.broadcasted_iota(jnp.int32, (1, LANES), 1)
    even = lane < HEAD_DIM
    sw = pltpu.roll(t, HEAD_DIM, 1)
    return jnp.where(even, t, sw), jnp.where(even, sw, t)


def _attend(q, kc, vc, mask, sink_col):
    s = _dot_nt(q, kc) * ATTN_SCALE
    s = jnp.where(mask, s, -jnp.inf)
    m = jnp.maximum(jnp.max(s, axis=-1, keepdims=True), sink_col)
    p = jnp.exp(s - m)
    den = jnp.sum(p, axis=-1, keepdims=True) + jnp.exp(sink_col - m)
    return _dot(p, vc) / den


def _attn_prompt_kernel(sink_ref, q_ref, kp_ref, kc_ref, vp_ref, vc_ref, o_ref):
    n = pl.program_id(1)
    lane = lax.broadcasted_iota(jnp.int32, (1, LANES), 1)
    even = lane < HEAD_DIM
    kcat = jnp.concatenate([kp_ref[...], kc_ref[...]], axis=0)
    vcat = jnp.concatenate([vp_ref[...], vc_ref[...]], axis=0)
    ks = _pair_kv(kcat)
    vs = _pair_kv(vcat)
    qi = _pmod(lax.broadcasted_iota(jnp.int32, (2 * BLOCK, 2 * BLOCK), 0), BLOCK) + BLOCK
    kj = lax.broadcasted_iota(jnp.int32, (2 * BLOCK, 2 * BLOCK), 1)
    dist = qi - kj
    mask = (dist >= 0) & (dist <= WINDOW) & ((n > 0) | (kj >= BLOCK))
    top = lax.broadcasted_iota(jnp.int32, (2 * BLOCK, 1), 0) < BLOCK
    for j in range(Q_WIDTH // LANES):
        qb = q_ref[:, j * LANES:(j + 1) * LANES]
        qs = jnp.concatenate([jnp.where(even, qb, 0.0), jnp.where(even, 0.0, qb)], axis=0)
        kvh = (2 * j) // GROUP
        sink_col = jnp.where(top, sink_ref[2 * j], sink_ref[2 * j + 1])
        o = _attend(qs, ks[kvh], vs[kvh], mask, sink_col)
        o_ref[:, j * LANES:(j + 1) * LANES] = jnp.where(even, o[:BLOCK], o[BLOCK:]).astype(o_ref.dtype)


def _attn_prompt(q, k, v, sinks, batch, seq):
    nb = seq // BLOCK
    cur = lambda b, n: (b * nb + n, 0)
    prv = lambda b, n: (b * nb + jnp.maximum(n - 1, 0), 0)
    kv = lambda f: pl.BlockSpec((BLOCK, KV_WIDTH), f)
    return pl.pallas_call(
        _attn_prompt_kernel,
        grid=(batch, nb),
        in_specs=[pl.BlockSpec(memory_space=pltpu.SMEM), pl.BlockSpec((BLOCK, Q_WIDTH), cur),
                  kv(prv), kv(cur), kv(prv), kv(cur)],
        out_specs=pl.BlockSpec((BLOCK, Q_WIDTH), cur),
        out_shape=jax.ShapeDtypeStruct((batch * seq, Q_WIDTH), BF16),
        compiler_params=_cparams("parallel", "parallel"),
        name="attn_prompt",
    )(sinks, q, k, k, v, v)


def _attn_sample_kernel(sink_ref, q_ref, kb_ref, kn_ref, vb_ref, vn_ref, o_ref, *, n_seq, wbuf):
    lane = lax.broadcasted_iota(jnp.int32, (1, LANES), 1)
    even = lane < HEAD_DIM
    rows = 2 * SAMPLE_PAD
    nk = wbuf + 2 * SAMPLE_PAD
    qi = _pmod(lax.broadcasted_iota(jnp.int32, (rows, nk), 0), SAMPLE_PAD)
    kj = lax.broadcasted_iota(jnp.int32, (rows, nk), 1)
    dist = qi + wbuf - kj
    mask = (dist >= 0) & (dist <= WINDOW)
    top = lax.broadcasted_iota(jnp.int32, (rows, 1), 0) < SAMPLE_PAD

    def body(g, carry_):
        pad = jnp.zeros((SAMPLE_PAD, KV_WIDTH), F32)
        kcat = jnp.concatenate([kb_ref[g], kn_ref[g], pad], axis=0)
        vcat = jnp.concatenate([vb_ref[g], vn_ref[g], pad], axis=0)
        ks = _pair_kv(kcat)
        vs = _pair_kv(vcat)
        for j in range(Q_WIDTH // LANES):
            qb = q_ref[g, :, j * LANES:(j + 1) * LANES]
            qs = jnp.concatenate([jnp.where(even, qb, 0.0), jnp.where(even, 0.0, qb)], axis=0)
            kvh = (2 * j) // GROUP
            sink_col = jnp.where(top, sink_ref[2 * j], sink_ref[2 * j + 1])
            o = _attend(qs, ks[kvh], vs[kvh], mask, sink_col)
            o_ref[g, :, j * LANES:(j + 1) * LANES] = jnp.where(
                even, o[:SAMPLE_PAD], o[SAMPLE_PAD:]).astype(o_ref.dtype)
        return carry_

    lax.fori_loop(0, n_seq, body, 0)


def _attn_sample(q, k, v, k_buf, v_buf, sinks, n_seq):
    b, wbuf = k_buf.shape[0], k_buf.shape[1]
    idx = lambda i: (i, 0, 0)
    new = pl.BlockSpec((n_seq, SAMPLE_PAD, KV_WIDTH), idx)
    buf = pl.BlockSpec((n_seq, wbuf, KV_WIDTH), idx)
    return pl.pallas_call(
        functools.partial(_attn_sample_kernel, n_seq=n_seq, wbuf=wbuf),
        grid=(b // n_seq,),
        in_specs=[pl.BlockSpec(memory_space=pltpu.SMEM), pl.BlockSpec((n_seq, SAMPLE_PAD, Q_WIDTH), idx),
                  buf, new, buf, new],
        out_specs=pl.BlockSpec((n_seq, SAMPLE_PAD, Q_WIDTH), idx),
        out_shape=jax.ShapeDtypeStruct((b, SAMPLE_PAD, Q_WIDTH), BF16),
        compiler_params=_cparams("parallel"),
        name="attn_sample",
    )(sinks, q, k_buf, k, v_buf, v)


def _layer_norm(x, g, b):
    mu = jnp.mean(x, axis=-1, keepdims=True)
    d = x - mu
    var = jnp.mean(d * d, axis=-1, keepdims=True)
    return d * lax.rsqrt(var + LN_EPS) * g + b


def _mix_ffn_kernel(x_ref, yr_ref, ya_ref, wg_ref, wbr_ref, wba_ref, wo_ref, g1_ref, b1_ref,
                    wu_ref, wd_ref, g2_ref, b2_ref, o_ref, *, ff_chunk):
    x = x_ref[...]
    xb = x.astype(BF16)
    gate_r = jax.nn.sigmoid(jnp.dot(xb, wg_ref[:, :D_MODEL], preferred_element_type=F32))
    mix = gate_r * jnp.dot(yr_ref[...], wbr_ref[...], preferred_element_type=F32)
    gate_a = jax.nn.sigmoid(jnp.dot(xb, wg_ref[:, D_MODEL:], preferred_element_type=F32))
    mix = mix + gate_a * jnp.dot(ya_ref[...], wba_ref[...], preferred_element_type=F32)
    x1 = _layer_norm(ALPHA * x + _dot(mix, wo_ref[...]), g1_ref[...], b1_ref[...])
    x1b = x1.astype(BF16)
    acc = ALPHA * x1
    for c in range(D_FF // ff_chunk):
        cs = slice(c * ff_chunk, (c + 1) * ff_chunk)
        h = jnp.maximum(jnp.dot(x1b, wu_ref[:, cs], preferred_element_type=F32), 0.0)
        acc = acc + _dot(h * h, wd_ref[cs, :])
    o_ref[...] = _layer_norm(acc, g2_ref[...], b2_ref[...])


def _mix_ffn(x, yr, ya, P, tm):
    n = x.shape[0]
    row = lambda i: (i, 0)
    ws = [P['w_gate'], P['w_br_rwkv'], P['w_br_attn'], P['w_out'], P['ln1_g'], P['ln1_b'],
          P['w_ff_up'], P['w_ff_down'], P['ln2_g'], P['ln2_b']]
    wspec = lambda a: pl.BlockSpec(a.shape, lambda i: (0, 0), pipeline_mode=pl.Buffered(1))
    return pl.pallas_call(
        functools.partial(_mix_ffn_kernel, ff_chunk=1024),
        grid=(n // tm,),
        in_specs=[pl.BlockSpec((tm, D_MODEL), row), pl.BlockSpec((tm, R_WIDTH), row),
                  pl.BlockSpec((tm, Q_WIDTH), row)] + [wspec(a) for a in ws],
        out_specs=pl.BlockSpec((tm, D_MODEL), row),
        out_shape=jax.ShapeDtypeStruct((n, D_MODEL), F32),
        compiler_params=_cparams("parallel"),
        name="mix_ffn",
    )(x, yr, ya, *ws)


def _layer_params(l, w_in, mu_shift, decay_base, decay_up, iclr_base, iclr_up, gate_up, k_k, k_a, r_k,
                  lnx_g, lnx_b, sinks, w_br_rwkv, w_br_attn, w_out, ln1_g, ln1_b, w_ff_up, w_ff_down,
                  ln2_g, ln2_b):
    row = lambda a: a[l].reshape(1, -1)
    zeros = jnp.zeros((DECAY_LORA, R_WIDTH), F32)
    lora_up = jnp.concatenate([jnp.concatenate([decay_up[l], zeros], axis=1),
                               jnp.concatenate([zeros, iclr_up[l]], axis=1)], axis=0)
    return dict(
        w_qkv=w_in[l][:, :QKV_END].astype(BF16), w_gate=w_in[l][:, QKV_END:].astype(BF16),
        mu_shift=row(mu_shift), decay_base=row(decay_base), lora_up=lora_up.astype(BF16),
        iclr_base=row(iclr_base), gate_up=gate_up[l].astype(BF16), k_k=row(k_k), k_a=row(k_a),
        r_k=row(r_k), lnx_g=row(lnx_g), lnx_b=row(lnx_b), sinks=sinks[l],
        w_br_rwkv=w_br_rwkv[l].astype(BF16), w_br_attn=w_br_attn[l].astype(BF16),
        w_out=w_out[l].astype(BF16), ln1_g=row(ln1_g), ln1_b=row(ln1_b),
        w_ff_up=w_ff_up[l].astype(BF16), w_ff_down=w_ff_down[l].astype(BF16),
        ln2_g=row(ln2_g), ln2_b=row(ln2_b))


def _prompt_layer(x, P, tables, batch, seq, wbuf):
    tm = 512
    zr, q, k, v = _inproj(x, P['w_qkv'], tables, seq // tm, tm)
    shift0 = jnp.zeros((batch, 1, SHIFT_W), F32)
    s0 = jnp.zeros((batch, R_HEADS, R_HEAD_DIM, R_HEAD_DIM), F32)
    yr, s_new = _wkv(zr, shift0, s0, P, n_seq=1, seq_rows=tm, chunk=64, valid_rows=tm, n_steps=seq // tm)
    ya = _attn_prompt(q, k, v, P['sinks'], batch, seq)
    x = _mix_ffn(x, yr, ya, P, tm)
    tail = lambda t, w: t.reshape(batch, seq, w)[:, seq - wbuf:].reshape(batch, wbuf, KV_HEADS, HEAD_DIM)
    return x, zr.reshape(batch, seq, SHIFT_W)[:, -1], s_new, tail(k, KV_WIDTH), tail(v, KV_WIDTH)


def _sample_layer(x, P, tables, shift_prev, s_prev, k_buf, v_buf, batch, seq):
    n = batch * SAMPLE_PAD
    tm = min(n, 512)
    n_seq = 16
    wbuf = k_buf.shape[1]
    zr, q, k, v = _inproj(x, P['w_qkv'], tables, 1, tm)
    yr, s_new = _wkv(zr, shift_prev[:, None, :], s_prev, P, n_seq=n_seq, seq_rows=SAMPLE_PAD,
                     chunk=SAMPLE_PAD, valid_rows=seq, n_steps=1)
    seq3 = lambda t, w: t.reshape(batch, SAMPLE_PAD, w)
    kb = k_buf.reshape(batch, wbuf, KV_WIDTH)
    vb = v_buf.reshape(batch, wbuf, KV_WIDTH)
    ya = _attn_sample(seq3(q, Q_WIDTH), seq3(k, KV_WIDTH), seq3(v, KV_WIDTH), kb, vb, P['sinks'], n_seq)
    x = _mix_ffn(x, yr, ya.reshape(n, Q_WIDTH), P, tm)
    window = lambda old, new: jnp.concatenate([old, seq3(new, KV_WIDTH)[:, :seq]], axis=1)[:, -wbuf:].reshape(
        batch, wbuf, KV_HEADS, HEAD_DIM)
    return x, seq3(zr, SHIFT_W)[:, seq - 1], s_new, window(kb, k), window(vb, v)


def kernel(x_prompt, x_sample, state_wkv, state_shift, cache_k_win, cache_v_win, w_in, mu_shift, decay_base, decay_up, iclr_base, iclr_up, gate_up, k_k, k_a, r_k, lnx_g, lnx_b, sinks, w_br_rwkv, w_br_attn, w_out, ln1_g, ln1_b, w_ff_up, w_ff_down, ln2_g, ln2_b):
    bp, tp, _ = x_prompt.shape
    bs, ts, _ = x_sample.shape
    wbuf = cache_k_win.shape[2]
    half = HEAD_DIM // 2
    inv_freq = ROPE_THETA ** (-jnp.arange(half, dtype=F32) / half)
    tab_p = _rope_tables(inv_freq, tp, 0, tp)
    tab_s = _rope_tables(inv_freq, min(bs * SAMPLE_PAD, 512), PAST_LEN, SAMPLE_PAD)

    hp = x_prompt.reshape(bp * tp, D_MODEL)
    hs = jnp.pad(x_sample, ((0, 0), (0, SAMPLE_PAD - ts), (0, 0))).reshape(bs * SAMPLE_PAD, D_MODEL)
    outs_p, outs_s = [], []
    for l in range(DEPTH):
        P = _layer_params(l, w_in, mu_shift, decay_base, decay_up, iclr_base, iclr_up, gate_up, k_k, k_a, r_k,
                          lnx_g, lnx_b, sinks, w_br_rwkv, w_br_attn, w_out, ln1_g, ln1_b, w_ff_up,
                          w_ff_down, ln2_g, ln2_b)
        hp, *st = _prompt_layer(hp, P, tab_p, bp, tp, wbuf)
        outs_p.append(st)
        hs, *st = _sample_layer(hs, P, tab_s, state_shift[l], state_wkv[l], cache_k_win[l], cache_v_win[l],
                                bs, ts)
        outs_s.append(st)
    stack = lambda outs, i: jnp.stack([o[i] for o in outs])
    y_p = hp.reshape(bp, tp, D_MODEL)
    y_s = hs.reshape(bs, SAMPLE_PAD, D_MODEL)[:, :ts]
    return (y_p, y_s,
            stack(outs_p, 1), stack(outs_p, 0), stack(outs_p, 2), stack(outs_p, 3),
            stack(outs_s, 1), stack(outs_s, 0), stack(outs_s, 2), stack(outs_s, 3))
```

```python
import functools
import math

import jax
import jax.numpy as jnp
from jax import lax
from jax.experimental import pallas as pl
from jax.experimental.pallas import tpu as pltpu

F32 = jnp.float32
BF16 = jnp.bfloat16

D_MODEL = 1024
DEPTH = 2
PAST_LEN = 8192
R_HEADS = 8
R_HEAD_DIM = 64
R_WIDTH = R_HEADS * R_HEAD_DIM
DECAY_LORA = 64
ICLR_LORA = 64
GATE_LORA = 128
SHIFT_W = 3 * R_WIDTH + DECAY_LORA + ICLR_LORA + GATE_LORA
A_HEADS = 8
KV_HEADS = 2
HEAD_DIM = 64
Q_WIDTH = A_HEADS * HEAD_DIM
KV_WIDTH = KV_HEADS * HEAD_DIM
GROUP = A_HEADS // KV_HEADS
WINDOW = 128
BLOCK = 128
ROPE_THETA = 10000.0
ATTN_SCALE = HEAD_DIM ** -0.5
D_FF = 4 * D_MODEL
ALPHA = (2 * DEPTH) ** 0.25
LN_EPS = 1e-5
GN_EPS = 64e-5
QKV_END = SHIFT_W + Q_WIDTH + 2 * KV_WIDTH

LANES = 128
SUBLANES = 8
VMEM_LIMIT = 56 * 1024 * 1024

SAMPLE_PAD = SUBLANES
N_PAIRS = R_HEADS // 2
WKV_CHUNK = LANES // 2


def _cparams(*sem):
    return pltpu.CompilerParams(dimension_semantics=sem, vmem_limit_bytes=VMEM_LIMIT)


def _const_spec(shape):
    nd = len(shape)
    return pl.BlockSpec(shape, lambda *_: (0,) * nd)


def _dot(a, b):
    return jnp.dot(a.astype(BF16), b.astype(BF16), preferred_element_type=F32)


def _dot_nt(a, b):
    return lax.dot_general(a.astype(BF16), b.astype(BF16), (((1,), (1,)), ((), ())),
                           preferred_element_type=F32)


def _pmod(x, n):
    assert n & (n - 1) == 0
    return x & (n - 1)


def _pdiv(x, n):
    assert n & (n - 1) == 0
    return x >> (n.bit_length() - 1)


def _split(x):
    hi = x.astype(BF16)
    lo = (x - hi.astype(F32)).astype(BF16)
    return hi, lo


def _dot3(a, b):
    ah, al = _split(a)
    bh, bl = _split(b)
    d = functools.partial(jnp.dot, preferred_element_type=F32)
    return d(ah, bh) + (d(ah, bl) + d(al, bh))


def _dot3_nt(a, b):
    ah, al = _split(a)
    bh, bl = _split(b)
    d = functools.partial(lax.dot_general, dimension_numbers=(((1,), (1,)), ((), ())),
                          preferred_element_type=F32)
    return d(ah, bh) + (d(ah, bl) + d(al, bh))


def _rope_table_kernel(invf_ref, cos_ref, sa_ref, sb_ref, *, rows, offset, period):
    i = pl.program_id(0)
    row = lax.broadcasted_iota(jnp.int32, (rows, LANES), 0) + i * rows
    pos = offset + _pmod(row, period)
    ang = pos.astype(F32) * invf_ref[...]
    lane = lax.broadcasted_iota(jnp.int32, (rows, LANES), 1)
    first = _pmod(lane, HEAD_DIM) < (HEAD_DIM // 2)
    c = jnp.cos(ang)
    s = jnp.sin(ang)
    cos_ref[...] = c
    sa_ref[...] = jnp.where(first, -s, 0.0)
    sb_ref[...] = jnp.where(first, 0.0, s)


def _rope_tables(inv_freq, n_rows, offset, period):
    rows = min(n_rows, 1024)
    invf = jnp.tile(inv_freq, LANES // (HEAD_DIM // 2)).reshape(1, LANES)
    out = jax.ShapeDtypeStruct((n_rows, LANES), F32)
    return pl.pallas_call(
        functools.partial(_rope_table_kernel, rows=rows, offset=offset, period=period),
        grid=(n_rows // rows,),
        in_specs=[_const_spec((1, LANES))],
        out_specs=[pl.BlockSpec((rows, LANES), lambda i: (i, 0))] * 3,
        out_shape=[out] * 3,
        compiler_params=_cparams("parallel"),
        name="rope_tables",
    )(invf)


def _inproj_kernel(x_ref, w_ref, cos_ref, sa_ref, sb_ref, zr_ref, q_ref, k_ref, v_ref):
    xb = x_ref[...].astype(BF16)
    zr_ref[...] = jnp.dot(xb, w_ref[:, :SHIFT_W], preferred_element_type=F32)
    cos, sa, sb = cos_ref[...], sa_ref[...], sb_ref[...]

    def rope(t):
        return (t * cos + pltpu.roll(t, LANES - HEAD_DIM // 2, 1) * sa
                + pltpu.roll(t, HEAD_DIM // 2, 1) * sb)

    zq = jnp.dot(xb, w_ref[:, SHIFT_W:SHIFT_W + Q_WIDTH], preferred_element_type=F32)
    for j in range(Q_WIDTH // LANES):
        q_ref[:, j * LANES:(j + 1) * LANES] = rope(zq[:, j * LANES:(j + 1) * LANES])
    zkv = jnp.dot(xb, w_ref[:, SHIFT_W + Q_WIDTH:QKV_END], preferred_element_type=F32)
    k_ref[...] = rope(zkv[:, :KV_WIDTH])
    v_ref[...] = zkv[:, KV_WIDTH:]


def _inproj(x, w, tables, tab_blocks, tm):
    n = x.shape[0]
    row = lambda i: (i, 0)
    tab = pl.BlockSpec((tm, LANES), lambda i: (lax.rem(i, tab_blocks), 0))
    return pl.pallas_call(
        _inproj_kernel,
        grid=(n // tm,),
        in_specs=[pl.BlockSpec((tm, D_MODEL), row), _const_spec(w.shape), tab, tab, tab],
        out_specs=[pl.BlockSpec((tm, SHIFT_W), row), pl.BlockSpec((tm, Q_WIDTH), row),
                   pl.BlockSpec((tm, KV_WIDTH), row), pl.BlockSpec((tm, KV_WIDTH), row)],
        out_shape=[jax.ShapeDtypeStruct((n, SHIFT_W), F32), jax.ShapeDtypeStruct((n, Q_WIDTH), F32),
                   jax.ShapeDtypeStruct((n, KV_WIDTH), F32), jax.ShapeDtypeStruct((n, KV_WIDTH), F32)],
        compiler_params=_cparams("parallel"),
        name="inproj",
    )(x, w, *tables)


def _wkv_kernel(zr_ref, shift_ref, sin_ref, mu_ref, db_ref, lora_ref, ib_ref, gu_ref, kk_ref, ka_ref,
                rk_ref, lg_ref, lb_ref,
                y_ref, sout_ref,
                sbd, carry, prev_s, r_s, k_s, v_s, kk_s, b_s, lw_s, cw_s, y_s, bon_s, g_s, xg_s, uy_s, bk_s,
                *, n_seq, seq_rows, chunk, seg, valid_rows):
    t_step = pl.program_id(1)
    n_steps = pl.num_programs(1)
    rows = n_seq * seq_rows
    per_seq_state = n_seq > 1
    n_chunks = rows // chunk
    c2 = 2 * chunk

    z = zr_ref[...]
    prev_s[...] = pltpu.roll(z, 1, 0)
    if per_seq_state:
        for g in range(n_seq):
            prev_s[g * seq_rows:g * seq_rows + 1, :] = shift_ref[g]
    else:
        @pl.when(t_step == 0)
        def _():
            carry[...] = shift_ref[0]
        prev_s[0:1, :] = carry[...]
        carry[...] = z[rows - 1:rows, :]
    prev = prev_s[...]
    zs = z + (prev - z) * mu_ref[...]

    o1, o2, o3 = R_WIDTH, 2 * R_WIDTH, 3 * R_WIDTH
    o4 = o3 + DECAY_LORA + ICLR_LORA
    r = zs[:, :o1]
    k = zs[:, o1:o2]
    v = zs[:, o2:o3]
    wa = zs[:, o3:o4]
    gd = zs[:, o4:]

    lane = lax.broadcasted_iota(jnp.int32, (1, LANES), 1)
    lane_even = lane < R_HEAD_DIM
    me = lane_even.astype(F32)
    mo = 1.0 - me
    bi = _pdiv(lax.broadcasted_iota(jnp.int32, (LANES, LANES), 0), R_HEAD_DIM)
    bj = _pdiv(lax.broadcasted_iota(jnp.int32, (LANES, LANES), 1), R_HEAD_DIM)
    m_bd = (bi == bj).astype(F32)
    bd_ones = m_bd.astype(BF16)

    def seg_sum(x):
        outs = []
        for j in range(R_WIDTH // LANES):
            hi, lo = _split(x[:, j * LANES:(j + 1) * LANES])
            outs.append(jnp.dot(hi, bd_ones, preferred_element_type=F32)
                        + jnp.dot(lo, bd_ones, preferred_element_type=F32))
        return jnp.concatenate(outs, axis=1)

    lora_in = jnp.where(lane_even, jnp.tanh(wa), wa)
    lora = _dot(lora_in, lora_ref[...])
    pre_w = db_ref[...] + lora[:, :R_WIDTH]
    neg = -pre_w
    softplus = jnp.maximum(neg, 0.0) + jnp.log1p(jnp.exp(-jnp.abs(neg)))
    logw = -jnp.exp(-softplus - 0.5)
    a = jax.nn.sigmoid(ib_ref[...] + lora[:, R_WIDTH:])
    g_s[...] = _dot(jax.nn.sigmoid(gd), gu_ref[...])
    kk = k * kk_ref[...]
    kk = kk * lax.rsqrt(jnp.maximum(seg_sum(kk * kk), 1e-24))
    kp = k * (1.0 + (a - 1.0) * ka_ref[...])
    bon_s[...] = seg_sum(r * kp * rk_ref[...]) * v

    if valid_rows < seq_rows:
        rid = _pmod(lax.broadcasted_iota(jnp.int32, (rows, 1), 0), seq_rows)
        ok = (rid < valid_rows).astype(F32)
        logw, kk, kp, v = logw * ok, kk * ok, kp * ok, v * ok

    r_s[...] = r
    k_s[...] = kp
    v_s[...] = v
    kk_s[...] = kk
    b_s[...] = kk * a
    lw_s[...] = logw

    tid = _pmod(lax.broadcasted_iota(jnp.int32, (rows, 1), 0), seg)
    cw = logw
    sh = 1
    while sh < seg:
        cw = cw + jnp.where(tid >= sh, pltpu.roll(cw, sh, 0), 0.0)
        sh *= 2
    cw_s[...] = cw

    si = lax.broadcasted_iota(jnp.int32, (c2, c2), 0)
    sj = lax.broadcasted_iota(jnp.int32, (c2, c2), 1)
    same = _pdiv(si, seg) == _pdiv(sj, seg)
    m_strict = (same & (_pmod(sj, seg) < _pmod(si, seg))).astype(F32)
    m_incl = (same & (_pmod(sj, seg) <= _pmod(si, seg))).astype(F32)
    eye = (si == sj).astype(F32)
    n_fact = int(math.log2(seg))

    def load_state(g):
        for p in range(N_PAIRS):
            zero = jnp.zeros((R_HEAD_DIM, R_HEAD_DIM), F32)
            sbd[p, 0:R_HEAD_DIM, :] = jnp.concatenate([sin_ref[g, 2 * p], zero], axis=1)
            sbd[p, R_HEAD_DIM:LANES, :] = jnp.concatenate([zero, sin_ref[g, 2 * p + 1]], axis=1)

    def store_state(g):
        for p in range(N_PAIRS):
            sout_ref[g, 2 * p] = sbd[p, 0:R_HEAD_DIM, 0:R_HEAD_DIM]
            sout_ref[g, 2 * p + 1] = sbd[p, R_HEAD_DIM:LANES, R_HEAD_DIM:LANES]

    if not per_seq_state:
        @pl.when(t_step == 0)
        def _():
            load_state(0)

    pairs = range(N_PAIRS)
    col = lambda p: slice(p * LANES, (p + 1) * LANES)
    stack2 = lambda t: jnp.concatenate([t * me, t * mo], axis=0)

    def phase1(c, carry_):
        rs = pl.ds(pl.multiple_of(c * chunk, chunk), chunk)
        ats, rts, vs, bts, kts = [], [], [], [], []
        for p in pairs:
            cwc = cw_s[rs, col(p)]
            w_in = jnp.exp(cwc)
            w_ex = jnp.exp(cwc - lw_s[rs, col(p)])
            w_inv = jnp.exp(-cwc)
            ats.append(stack2(-kk_s[rs, col(p)] * w_ex))
            rts.append(stack2(r_s[rs, col(p)] * w_in))
            vs.append(stack2(v_s[rs, col(p)]))
            bts.append(b_s[rs, col(p)] * w_inv)
            kts.append(k_s[rs, col(p)] * w_inv)
        sc = [_dot_nt(jnp.concatenate([ats[p], rts[p]], axis=0),
                      jnp.concatenate([bts[p], bts[p], kts[p], kts[p]], axis=0)) for p in pairs]
        l_ab = [sc[p][:c2, :c2] * m_strict for p in pairs]
        a_ak = [sc[p][:c2, c2:] * m_strict for p in pairs]
        a_rb = [sc[p][c2:, :c2] * m_incl for p in pairs]
        a_rk = [sc[p][c2:, c2:] * m_incl for p in pairs]

        pw = l_ab
        inv = [eye + l_ab[p] for p in pairs]
        for _ in range(n_fact - 1):
            pw = [_dot(pw[p], pw[p]) for p in pairs]
            inv = [_dot(inv[p], eye + pw[p]) for p in pairs]

        ta = [_dot(inv[p], jnp.concatenate([ats[p], a_ak[p]], axis=1)) for p in pairs]
        ab = [_dot(a_rb[p], ta[p]) for p in pairs]
        uy = [_dot(jnp.concatenate([ta[p][:, LANES:], ab[p][:, LANES:] + a_rk[p]], axis=0), vs[p])
              for p in pairs]
        for p in pairs:
            idx = c * N_PAIRS + p
            xg_s[idx] = jnp.concatenate([ta[p][:, :LANES], rts[p] + ab[p][:, :LANES]], axis=0)
            uy_s[idx] = uy[p]
            bk_s[idx] = jnp.concatenate([bts[p], kts[p]], axis=0)
        return carry_

    lax.fori_loop(0, n_chunks, phase1, 0)

    segs_per_chunk = chunk // seg

    def phase2(i, carry_):
        if per_seq_state:
            load_state(i)
        c = _pdiv(i, segs_per_chunk)
        off = pl.multiple_of(_pmod(i, segs_per_chunk) * seg, seg)
        tok = pl.ds(pl.multiple_of(i * seg, seg), seg)
        tail = pl.ds(pl.multiple_of(i * seg + seg - SUBLANES, SUBLANES), SUBLANES)
        def quarter(ref, p, q):
            return ref[c * N_PAIRS + p, pl.ds(pl.multiple_of(q * chunk + off, seg), seg), :]

        s0 = [sbd[p] for p in pairs]
        xg = [jnp.concatenate([quarter(xg_s, p, q) for q in range(4)], axis=0) for p in pairs]
        uyv = [_dot3_nt(xg[p], s0[p]) for p in pairs]
        uyv = [uyv[p] + jnp.concatenate([quarter(uy_s, p, q) for q in range(4)], axis=0)
               for p in pairs]
        uvt = [jnp.concatenate([uyv[p][:2 * seg], stack2(v_s[tok, col(p)])], axis=0).T for p in pairs]
        rhs = []
        for p in pairs:
            bt, kt = quarter(bk_s, p, 0), quarter(bk_s, p, 1)
            rhs.append(jnp.concatenate([bt, bt, kt, kt], axis=0))
        upd = [_dot3(uvt[p], rhs[p]) for p in pairs]
        for p in pairs:
            w_end = jnp.exp(cw_s[tail, col(p)][SUBLANES - 1:SUBLANES, :])
            sbd[p] = (s0[p] + upd[p] * m_bd) * w_end
            y_s[tok, col(p)] = uyv[p][2 * seg:3 * seg] + uyv[p][3 * seg:]
        if per_seq_state:
            store_state(i)
        return carry_

    lax.fori_loop(0, rows // seg, phase2, 0)

    if not per_seq_state:
        @pl.when(t_step == n_steps - 1)
        def _():
            store_state(0)

    y = y_s[...]
    inv_n = 1.0 / R_HEAD_DIM
    mean = seg_sum(y) * inv_n
    d = y - mean
    var = seg_sum(d * d) * inv_n
    yn = d * lax.rsqrt(var + GN_EPS)
    yn = yn * lg_ref[...] + lb_ref[...]
    y_ref[...] = ((yn + bon_s[...]) * g_s[...]).astype(y_ref.dtype)


def _wkv(zr, shift_in, s_in, P, *, n_seq, seq_rows, seg, valid_rows, n_steps):
    rows = n_seq * seq_rows
    chunk = WKV_CHUNK
    n_pc = (rows // chunk) * N_PAIRS
    n_groups = shift_in.shape[0] // n_seq
    n = zr.shape[0]
    small = [P['mu_shift'], P['decay_base'], P['lora_up'], P['iclr_base'], P['gate_up'], P['k_k'], P['k_a'],
             P['r_k'], P['lnx_g'], P['lnx_b']]
    tok = lambda b, t: (b * n_steps + t, 0)
    seq3 = lambda b, t: (b, 0, 0)
    seq4 = lambda b, t: (b, 0, 0, 0)
    state_block = (n_seq, R_HEADS, R_HEAD_DIM, R_HEAD_DIM)
    tile = lambda w: pltpu.VMEM((rows, w), F32)
    return pl.pallas_call(
        functools.partial(_wkv_kernel, n_seq=n_seq, seq_rows=seq_rows, chunk=chunk, seg=seg,
                          valid_rows=valid_rows),
        grid=(n_groups, n_steps),
        in_specs=[pl.BlockSpec((rows, SHIFT_W), tok), pl.BlockSpec((n_seq, 1, SHIFT_W), seq3),
                  pl.BlockSpec(state_block, seq4)] + [_const_spec(a.shape) for a in small],
        out_specs=[pl.BlockSpec((rows, R_WIDTH), tok), pl.BlockSpec(state_block, seq4)],
        out_shape=[jax.ShapeDtypeStruct((n, R_WIDTH), BF16), jax.ShapeDtypeStruct(s_in.shape, F32)],
        scratch_shapes=[pltpu.VMEM((N_PAIRS, LANES, LANES), F32), pltpu.VMEM((1, SHIFT_W), F32),
                        tile(SHIFT_W)] + [tile(R_WIDTH)] * 10 + [
                            pltpu.VMEM((n_pc, 4 * chunk, LANES), F32), pltpu.VMEM((n_pc, 4 * chunk, LANES), F32),
                            pltpu.VMEM((n_pc, 2 * chunk, LANES), F32)],
        compiler_params=_cparams("parallel", "arbitrary"),
        name="wkv7",
    )(zr, shift_in, s_in, *small)


def _pair_kv(t):
    lane = lax.broadcasted_iota(jnp.int32, (1, LANES), 1)
    even = lane < HEAD_DIM
    sw = pltpu.roll(t, HEAD_DIM, 1)
    return jnp.where(even, t, sw), jnp.where(even, sw, t)


def _attend(qs, kcs, vcs, mask, sink_cols):
    n = range(len(qs))
    s = [_dot_nt(qs[i], kcs[i]) * ATTN_SCALE for i in n]
    s = [jnp.where(mask, s[i], -jnp.inf) for i in n]
    m = [jnp.maximum(jnp.max(s[i], axis=-1, keepdims=True), sink_cols[i]) for i in n]
    p = [jnp.exp(s[i] - m[i]) for i in n]
    den = [jnp.sum(p[i], axis=-1, keepdims=True) + jnp.exp(sink_cols[i] - m[i]) for i in n]
    o = [_dot(p[i], vcs[i]) for i in n]
    return [o[i] / den[i] for i in n]


def _attn_prompt_kernel(sink_ref, q_ref, kp_ref, kc_ref, vp_ref, vc_ref, o_ref):
    n = pl.program_id(1)
    lane = lax.broadcasted_iota(jnp.int32, (1, LANES), 1)
    even = lane < HEAD_DIM
    kcat = jnp.concatenate([kp_ref[...], kc_ref[...]], axis=0)
    vcat = jnp.concatenate([vp_ref[...], vc_ref[...]], axis=0)
    ks = _pair_kv(kcat)
    vs = _pair_kv(vcat)
    qi = _pmod(lax.broadcasted_iota(jnp.int32, (2 * BLOCK, 2 * BLOCK), 0), BLOCK) + BLOCK
    kj = lax.broadcasted_iota(jnp.int32, (2 * BLOCK, 2 * BLOCK), 1)
    dist = qi - kj
    mask = (dist >= 0) & (dist <= WINDOW) & ((n > 0) | (kj >= BLOCK))
    top = lax.broadcasted_iota(jnp.int32, (2 * BLOCK, 1), 0) < BLOCK
    blocks = range(Q_WIDTH // LANES)
    qs, sink_cols = [], []
    for j in blocks:
        qb = q_ref[:, j * LANES:(j + 1) * LANES]
        qs.append(jnp.concatenate([jnp.where(even, qb, 0.0), jnp.where(even, 0.0, qb)], axis=0))
        sink_cols.append(jnp.where(top, sink_ref[2 * j], sink_ref[2 * j + 1]))
    kvh = [(2 * j) // GROUP for j in blocks]
    o = _attend(qs, [ks[h] for h in kvh], [vs[h] for h in kvh], mask, sink_cols)
    for j in blocks:
        o_ref[:, j * LANES:(j + 1) * LANES] = jnp.where(even, o[j][:BLOCK], o[j][BLOCK:]).astype(o_ref.dtype)


def _attn_prompt(q, k, v, sinks, batch, seq):
    nb = seq // BLOCK
    cur = lambda b, n: (b * nb + n, 0)
    prv = lambda b, n: (b * nb + jnp.maximum(n - 1, 0), 0)
    kv = lambda f: pl.BlockSpec((BLOCK, KV_WIDTH), f)
    return pl.pallas_call(
        _attn_prompt_kernel,
        grid=(batch, nb),
        in_specs=[pl.BlockSpec(memory_space=pltpu.SMEM), pl.BlockSpec((BLOCK, Q_WIDTH), cur),
                  kv(prv), kv(cur), kv(prv), kv(cur)],
        out_specs=pl.BlockSpec((BLOCK, Q_WIDTH), cur),
        out_shape=jax.ShapeDtypeStruct((batch * seq, Q_WIDTH), BF16),
        compiler_params=_cparams("parallel", "parallel"),
        name="attn_prompt",
    )(sinks, q, k, k, v, v)


def _attn_sample_kernel(sink_ref, q_ref, kb_ref, kn_ref, vb_ref, vn_ref, o_ref, *, n_seq, wbuf):
    lane = lax.broadcasted_iota(jnp.int32, (1, LANES), 1)
    even = lane < HEAD_DIM
    rows = GROUP * SAMPLE_PAD
    nk = wbuf + 2 * SAMPLE_PAD
    qi = _pmod(lax.broadcasted_iota(jnp.int32, (rows, nk), 0), SAMPLE_PAD)
    kj = lax.broadcasted_iota(jnp.int32, (rows, nk), 1)
    dist = qi + wbuf - kj
    mask = (dist >= 0) & (dist <= WINDOW)
    rid = _pdiv(lax.broadcasted_iota(jnp.int32, (rows, 1), 0), SAMPLE_PAD)
    sink_cols = []
    for h in range(KV_HEADS):
        col = jnp.zeros((rows, 1), F32)
        for gq in range(GROUP):
            col = jnp.where(rid == gq, sink_ref[h * GROUP + gq], col)
        sink_cols.append(col)
    unroll = 2
    pad = jnp.zeros((SAMPLE_PAD, KV_WIDTH), F32)
    blocks_per_kv = GROUP // 2

    def body(i, carry_):
        qs, kcs, vcs = [], [], []
        for u in range(unroll):
            g = i * unroll + u
            ks = _pair_kv(jnp.concatenate([kb_ref[g], kn_ref[g], pad], axis=0))
            vs = _pair_kv(jnp.concatenate([vb_ref[g], vn_ref[g], pad], axis=0))
            for h in range(KV_HEADS):
                parts = []
                for j in range(h * blocks_per_kv, (h + 1) * blocks_per_kv):
                    qb = q_ref[g, :, j * LANES:(j + 1) * LANES]
                    parts += [jnp.where(even, qb, 0.0), jnp.where(even, 0.0, qb)]
                qs.append(jnp.concatenate(parts, axis=0))
                kcs.append(ks[h])
                vcs.append(vs[h])
        o = _attend(qs, kcs, vcs, mask, sink_cols * unroll)
        for u in range(unroll):
            g = i * unroll + u
            for h in range(KV_HEADS):
                oh = o[u * KV_HEADS + h]
                for jj in range(blocks_per_kv):
                    j = h * blocks_per_kv + jj
                    r0 = 2 * jj * SAMPLE_PAD
                    o_ref[g, :, j * LANES:(j + 1) * LANES] = jnp.where(
                        even, oh[r0:r0 + SAMPLE_PAD], oh[r0 + SAMPLE_PAD:r0 + 2 * SAMPLE_PAD]).astype(o_ref.dtype)
        return carry_

    lax.fori_loop(0, n_seq // unroll, body, 0)


def _attn_sample(q, k, v, k_buf, v_buf, sinks, n_seq):
    b, wbuf = k_buf.shape[0], k_buf.shape[1]
    idx = lambda i: (i, 0, 0)
    new = pl.BlockSpec((n_seq, SAMPLE_PAD, KV_WIDTH), idx)
    buf = pl.BlockSpec((n_seq, wbuf, KV_WIDTH), idx)
    return pl.pallas_call(
        functools.partial(_attn_sample_kernel, n_seq=n_seq, wbuf=wbuf),
        grid=(b // n_seq,),
        in_specs=[pl.BlockSpec(memory_space=pltpu.SMEM), pl.BlockSpec((n_seq, SAMPLE_PAD, Q_WIDTH), idx),
                  buf, new, buf, new],
        out_specs=pl.BlockSpec((n_seq, SAMPLE_PAD, Q_WIDTH), idx),
        out_shape=jax.ShapeDtypeStruct((b, SAMPLE_PAD, Q_WIDTH), BF16),
        compiler_params=_cparams("parallel"),
        name="attn_sample",
    )(sinks, q, k_buf, k, v_buf, v)


def _layer_norm(x, g, b):
    mu = jnp.mean(x, axis=-1, keepdims=True)
    d = x - mu
    var = jnp.mean(d * d, axis=-1, keepdims=True)
    return d * lax.rsqrt(var + LN_EPS) * g + b


def _mix_ffn_kernel(x_ref, yr_ref, ya_ref, wg_ref, wbr_ref, wba_ref, wo_ref, g1_ref, b1_ref,
                    wu_ref, wd_ref, g2_ref, b2_ref, o_ref, *, ff_chunk):
    x = x_ref[...]
    xb = x.astype(BF16)
    gate_r = jax.nn.sigmoid(jnp.dot(xb, wg_ref[:, :D_MODEL], preferred_element_type=F32))
    mix = gate_r * jnp.dot(yr_ref[...], wbr_ref[...], preferred_element_type=F32)
    gate_a = jax.nn.sigmoid(jnp.dot(xb, wg_ref[:, D_MODEL:], preferred_element_type=F32))
    mix = mix + gate_a * jnp.dot(ya_ref[...], wba_ref[...], preferred_element_type=F32)
    x1 = _layer_norm(ALPHA * x + _dot(mix, wo_ref[...]), g1_ref[...], b1_ref[...])
    x1b = x1.astype(BF16)
    acc = ALPHA * x1
    for c in range(D_FF // ff_chunk):
        cs = slice(c * ff_chunk, (c + 1) * ff_chunk)
        h = jnp.maximum(jnp.dot(x1b, wu_ref[:, cs], preferred_element_type=F32), 0.0)
        acc = acc + _dot(h * h, wd_ref[cs, :])
    o_ref[...] = _layer_norm(acc, g2_ref[...], b2_ref[...])


def _mix_ffn(x, yr, ya, P, tm):
    n = x.shape[0]
    row = lambda i: (i, 0)
    ws = [P['w_gate'], P['w_br_rwkv'], P['w_br_attn'], P['w_out'], P['ln1_g'], P['ln1_b'],
          P['w_ff_up'], P['w_ff_down'], P['ln2_g'], P['ln2_b']]
    wspec = lambda a: pl.BlockSpec(a.shape, lambda i: (0, 0), pipeline_mode=pl.Buffered(1))
    return pl.pallas_call(
        functools.partial(_mix_ffn_kernel, ff_chunk=1024),
        grid=(n // tm,),
        in_specs=[pl.BlockSpec((tm, D_MODEL), row), pl.BlockSpec((tm, R_WIDTH), row),
                  pl.BlockSpec((tm, Q_WIDTH), row)] + [wspec(a) for a in ws],
        out_specs=pl.BlockSpec((tm, D_MODEL), row),
        out_shape=jax.ShapeDtypeStruct((n, D_MODEL), F32),
        compiler_params=_cparams("parallel"),
        name="mix_ffn",
    )(x, yr, ya, *ws)


def _layer_params(l, w_in, mu_shift, decay_base, decay_up, iclr_base, iclr_up, gate_up, k_k, k_a, r_k,
                  lnx_g, lnx_b, sinks, w_br_rwkv, w_br_attn, w_out, ln1_g, ln1_b, w_ff_up, w_ff_down,
                  ln2_g, ln2_b):
    row = lambda a: a[l].reshape(1, -1)
    zeros = jnp.zeros((DECAY_LORA, R_WIDTH), F32)
    lora_up = jnp.concatenate([jnp.concatenate([decay_up[l], zeros], axis=1),
                               jnp.concatenate([zeros, iclr_up[l]], axis=1)], axis=0)
    return dict(
        w_qkv=w_in[l][:, :QKV_END].astype(BF16), w_gate=w_in[l][:, QKV_END:].astype(BF16),
        mu_shift=row(mu_shift), decay_base=row(decay_base), lora_up=lora_up.astype(BF16),
        iclr_base=row(iclr_base), gate_up=gate_up[l].astype(BF16), k_k=row(k_k), k_a=row(k_a),
        r_k=row(r_k), lnx_g=row(lnx_g), lnx_b=row(lnx_b), sinks=sinks[l],
        w_br_rwkv=w_br_rwkv[l].astype(BF16), w_br_attn=w_br_attn[l].astype(BF16),
        w_out=w_out[l].astype(BF16), ln1_g=row(ln1_g), ln1_b=row(ln1_b),
        w_ff_up=w_ff_up[l].astype(BF16), w_ff_down=w_ff_down[l].astype(BF16),
        ln2_g=row(ln2_g), ln2_b=row(ln2_b))


def _prompt_layer(x, P, tables, batch, seq, wbuf):
    tm = 512
    zr, q, k, v = _inproj(x, P['w_qkv'], tables, seq // tm, tm)
    shift0 = jnp.zeros((batch, 1, SHIFT_W), F32)
    s0 = jnp.zeros((batch, R_HEADS, R_HEAD_DIM, R_HEAD_DIM), F32)
    yr, s_new = _wkv(zr, shift0, s0, P, n_seq=1, seq_rows=tm, seg=WKV_CHUNK, valid_rows=tm, n_steps=seq // tm)
    ya = _attn_prompt(q, k, v, P['sinks'], batch, seq)
    x = _mix_ffn(x, yr, ya, P, tm)
    tail = lambda t, w: t.reshape(batch, seq, w)[:, seq - wbuf:].reshape(batch, wbuf, KV_HEADS, HEAD_DIM)
    return x, zr.reshape(batch, seq, SHIFT_W)[:, -1], s_new, tail(k, KV_WIDTH), tail(v, KV_WIDTH)


def _sample_layer(x, P, tables, shift_prev, s_prev, k_buf, v_buf, batch, seq):
    n = batch * SAMPLE_PAD
    tm = min(n, 512)
    n_seq = 16
    wbuf = k_buf.shape[1]
    zr, q, k, v = _inproj(x, P['w_qkv'], tables, 1, tm)
    yr, s_new = _wkv(zr, shift_prev[:, None, :], s_prev, P, n_seq=n_seq, seq_rows=SAMPLE_PAD,
                     seg=SAMPLE_PAD, valid_rows=seq, n_steps=1)
    seq3 = lambda t, w: t.reshape(batch, SAMPLE_PAD, w)
    kb = k_buf.reshape(batch, wbuf, KV_WIDTH)
    vb = v_buf.reshape(batch, wbuf, KV_WIDTH)
    ya = _attn_sample(seq3(q, Q_WIDTH), seq3(k, KV_WIDTH), seq3(v, KV_WIDTH), kb, vb, P['sinks'], n_seq)
    x = _mix_ffn(x, yr, ya.reshape(n, Q_WIDTH), P, tm)
    window = lambda old, new: jnp.concatenate([old, seq3(new, KV_WIDTH)[:, :seq]], axis=1)[:, -wbuf:].reshape(
        batch, wbuf, KV_HEADS, HEAD_DIM)
    return x, seq3(zr, SHIFT_W)[:, seq - 1], s_new, window(kb, k), window(vb, v)


def kernel(x_prompt, x_sample, state_wkv, state_shift, cache_k_win, cache_v_win, w_in, mu_shift, decay_base, decay_up, iclr_base, iclr_up, gate_up, k_k, k_a, r_k, lnx_g, lnx_b, sinks, w_br_rwkv, w_br_attn, w_out, ln1_g, ln1_b, w_ff_up, w_ff_down, ln2_g, ln2_b):
    bp, tp, _ = x_prompt.shape
    bs, ts, _ = x_sample.shape
    wbuf = cache_k_win.shape[2]
    half = HEAD_DIM // 2
    inv_freq = ROPE_THETA ** (-jnp.arange(half, dtype=F32) / half)
    tab_p = _rope_tables(inv_freq, tp, 0, tp)
    tab_s = _rope_tables(inv_freq, min(bs * SAMPLE_PAD, 512), PAST_LEN, SAMPLE_PAD)

    hp = x_prompt.reshape(bp * tp, D_MODEL)
    hs = jnp.pad(x_sample, ((0, 0), (0, SAMPLE_PAD - ts), (0, 0))).reshape(bs * SAMPLE_PAD, D_MODEL)
    outs_p, outs_s = [], []
    for l in range(DEPTH):
        P = _layer_params(l, w_in, mu_shift, decay_base, decay_up, iclr_base, iclr_up, gate_up, k_k, k_a, r_k,
                          lnx_g, lnx_b, sinks, w_br_rwkv, w_br_attn, w_out, ln1_g, ln1_b, w_ff_up,
                          w_ff_down, ln2_g, ln2_b)
        hp, *st = _prompt_layer(hp, P, tab_p, bp, tp, wbuf)
        outs_p.append(st)
        hs, *st = _sample_layer(hs, P, tab_s, state_shift[l], state_wkv[l], cache_k_win[l], cache_v_win[l],
                                bs, ts)
        outs_s.append(st)
    stack = lambda outs, i: jnp.stack([o[i] for o in outs])
    y_p = hp.reshape(bp, tp, D_MODEL)
    y_s = hs.reshape(bs, SAMPLE_PAD, D_MODEL)[:, :ts]
    return (y_p, y_s,
            stack(outs_p, 1), stack(outs_p, 0), stack(outs_p, 2), stack(outs_p, 3),
            stack(outs_s, 1), stack(outs_s, 0), stack(outs_s, 2), stack(outs_s, 3))
```

```python
import functools
import math

import jax
import jax.numpy as jnp
from jax import lax
from jax.experimental import pallas as pl
from jax.experimental.pallas import tpu as pltpu

F32 = jnp.float32
BF16 = jnp.bfloat16

D_MODEL = 1024
DEPTH = 2
PAST_LEN = 8192
R_HEADS = 8
R_HEAD_DIM = 64
R_WIDTH = R_HEADS * R_HEAD_DIM
DECAY_LORA = 64
ICLR_LORA = 64
GATE_LORA = 128
SHIFT_W = 3 * R_WIDTH + DECAY_LORA + ICLR_LORA + GATE_LORA
A_HEADS = 8
KV_HEADS = 2
HEAD_DIM = 64
Q_WIDTH = A_HEADS * HEAD_DIM
KV_WIDTH = KV_HEADS * HEAD_DIM
GROUP = A_HEADS // KV_HEADS
WINDOW = 128
BLOCK = 128
ROPE_THETA = 10000.0
ATTN_SCALE = HEAD_DIM ** -0.5
D_FF = 4 * D_MODEL
ALPHA = (2 * DEPTH) ** 0.25
LN_EPS = 1e-5
GN_EPS = 64e-5
QKV_END = SHIFT_W + Q_WIDTH + 2 * KV_WIDTH

LANES = 128
SUBLANES = 8
VMEM_LIMIT = 56 * 1024 * 1024

SAMPLE_PAD = SUBLANES
N_PAIRS = R_HEADS // 2
WKV_CHUNK = LANES // 2
PHASE1_CHUNKS = 2


def _cparams(*sem):
    return pltpu.CompilerParams(dimension_semantics=sem, vmem_limit_bytes=VMEM_LIMIT)


def _const_spec(shape):
    nd = len(shape)
    return pl.BlockSpec(shape, lambda *_: (0,) * nd)


def _dot(a, b):
    return jnp.dot(a.astype(BF16), b.astype(BF16), preferred_element_type=F32)


def _dot_nt(a, b):
    return lax.dot_general(a.astype(BF16), b.astype(BF16), (((1,), (1,)), ((), ())),
                           preferred_element_type=F32)


def _pmod(x, n):
    assert n & (n - 1) == 0
    return x & (n - 1)


def _pdiv(x, n):
    assert n & (n - 1) == 0
    return x >> (n.bit_length() - 1)


def _split(x):
    hi = x.astype(BF16)
    lo = (x - hi.astype(F32)).astype(BF16)
    return hi, lo


def _dot2_nt(a, b):
    bh, bl = _split(b)
    d = functools.partial(lax.dot_general, dimension_numbers=(((1,), (1,)), ((), ())),
                          preferred_element_type=F32)
    ab = a.astype(BF16)
    return d(ab, bh) + d(ab, bl)


def _rope_table_kernel(invf_ref, cos_ref, sa_ref, sb_ref, *, rows, offset, period):
    i = pl.program_id(0)
    row = lax.broadcasted_iota(jnp.int32, (rows, LANES), 0) + i * rows
    pos = offset + _pmod(row, period)
    ang = pos.astype(F32) * invf_ref[...]
    lane = lax.broadcasted_iota(jnp.int32, (rows, LANES), 1)
    first = _pmod(lane, HEAD_DIM) < (HEAD_DIM // 2)
    c = jnp.cos(ang)
    s = jnp.sin(ang)
    cos_ref[...] = c
    sa_ref[...] = jnp.where(first, -s, 0.0)
    sb_ref[...] = jnp.where(first, 0.0, s)


def _rope_tables(inv_freq, n_rows, offset, period):
    rows = min(n_rows, 1024)
    invf = jnp.tile(inv_freq, LANES // (HEAD_DIM // 2)).reshape(1, LANES)
    out = jax.ShapeDtypeStruct((n_rows, LANES), F32)
    return pl.pallas_call(
        functools.partial(_rope_table_kernel, rows=rows, offset=offset, period=period),
        grid=(n_rows // rows,),
        in_specs=[_const_spec((1, LANES))],
        out_specs=[pl.BlockSpec((rows, LANES), lambda i: (i, 0))] * 3,
        out_shape=[out] * 3,
        compiler_params=_cparams("parallel"),
        name="rope_tables",
    )(invf)


def _inproj_kernel(x_ref, w_ref, cos_ref, sa_ref, sb_ref, zr_ref, q_ref, k_ref, v_ref):
    xb = x_ref[...].astype(BF16)
    zr_ref[...] = jnp.dot(xb, w_ref[:, :SHIFT_W], preferred_element_type=F32)
    cos, sa, sb = cos_ref[...], sa_ref[...], sb_ref[...]

    def rope(t):
        return (t * cos + pltpu.roll(t, LANES - HEAD_DIM // 2, 1) * sa
                + pltpu.roll(t, HEAD_DIM // 2, 1) * sb)

    zq = jnp.dot(xb, w_ref[:, SHIFT_W:SHIFT_W + Q_WIDTH], preferred_element_type=F32)
    for j in range(Q_WIDTH // LANES):
        q_ref[:, j * LANES:(j + 1) * LANES] = rope(zq[:, j * LANES:(j + 1) * LANES])
    zkv = jnp.dot(xb, w_ref[:, SHIFT_W + Q_WIDTH:QKV_END], preferred_element_type=F32)
    k_ref[...] = rope(zkv[:, :KV_WIDTH])
    v_ref[...] = zkv[:, KV_WIDTH:]


def _inproj(x, w, tables, tab_blocks, tm):
    n = x.shape[0]
    row = lambda i: (i, 0)
    tab = pl.BlockSpec((tm, LANES), lambda i: (lax.rem(i, tab_blocks), 0))
    return pl.pallas_call(
        _inproj_kernel,
        grid=(n // tm,),
        in_specs=[pl.BlockSpec((tm, D_MODEL), row), _const_spec(w.shape), tab, tab, tab],
        out_specs=[pl.BlockSpec((tm, SHIFT_W), row), pl.BlockSpec((tm, Q_WIDTH), row),
                   pl.BlockSpec((tm, KV_WIDTH), row), pl.BlockSpec((tm, KV_WIDTH), row)],
        out_shape=[jax.ShapeDtypeStruct((n, SHIFT_W), F32), jax.ShapeDtypeStruct((n, Q_WIDTH), F32),
                   jax.ShapeDtypeStruct((n, KV_WIDTH), F32), jax.ShapeDtypeStruct((n, KV_WIDTH), F32)],
        compiler_params=_cparams("parallel"),
        name="inproj",
    )(x, w, *tables)


def _wkv_kernel(zr_ref, shift_ref, sin_ref, mu_ref, db_ref, lora_ref, ib_ref, gu_ref, kk_ref, ka_ref,
                rk_ref, lg_ref, lb_ref,
                y_ref, sout_ref,
                sbd, carry, prev_s, r_s, k_s, v_s, kk_s, b_s, lw_s, cw_s, y_s, bon_s, g_s, xg_s, uy_s, bk_s,
                *, n_seq, seq_rows, chunk, seg, valid_rows):
    t_step = pl.program_id(1)
    n_steps = pl.num_programs(1)
    rows = n_seq * seq_rows
    per_seq_state = n_seq > 1
    n_chunks = rows // chunk
    c2 = 2 * chunk

    z = zr_ref[...]
    prev_s[...] = pltpu.roll(z, 1, 0)
    if per_seq_state:
        for g in range(n_seq):
            prev_s[g * seq_rows:g * seq_rows + 1, :] = shift_ref[g]
    else:
        @pl.when(t_step == 0)
        def _():
            carry[...] = shift_ref[0]
        prev_s[0:1, :] = carry[...]
        carry[...] = z[rows - 1:rows, :]
    prev = prev_s[...]
    zs = z + (prev - z) * mu_ref[...]

    o1, o2, o3 = R_WIDTH, 2 * R_WIDTH, 3 * R_WIDTH
    o4 = o3 + DECAY_LORA + ICLR_LORA
    r = zs[:, :o1]
    k = zs[:, o1:o2]
    v = zs[:, o2:o3]
    wa = zs[:, o3:o4]
    gd = zs[:, o4:]

    lane = lax.broadcasted_iota(jnp.int32, (1, LANES), 1)
    lane_even = lane < R_HEAD_DIM
    me = lane_even.astype(F32)
    mo = 1.0 - me
    bi = _pdiv(lax.broadcasted_iota(jnp.int32, (LANES, LANES), 0), R_HEAD_DIM)
    bj = _pdiv(lax.broadcasted_iota(jnp.int32, (LANES, LANES), 1), R_HEAD_DIM)
    m_bd = (bi == bj).astype(F32)
    bd_ones = m_bd.astype(BF16)

    def seg_sum(x):
        xb = x.astype(BF16)
        return jnp.concatenate([jnp.dot(xb[:, j * LANES:(j + 1) * LANES], bd_ones, preferred_element_type=F32)
                                for j in range(R_WIDTH // LANES)], axis=1)

    lora_in = jnp.where(lane_even, jnp.tanh(wa), wa)
    lora = _dot(lora_in, lora_ref[...])
    pre_w = db_ref[...] + lora[:, :R_WIDTH]
    logw = -math.exp(-0.5) * jax.nn.sigmoid(pre_w)
    a = jax.nn.sigmoid(ib_ref[...] + lora[:, R_WIDTH:])
    g_s[...] = _dot(jax.nn.sigmoid(gd), gu_ref[...])
    kk = k * kk_ref[...]
    kk = kk * lax.rsqrt(jnp.maximum(seg_sum(kk * kk), 1e-24))
    kp = k * (1.0 + (a - 1.0) * ka_ref[...])
    bon_s[...] = seg_sum(r * kp * rk_ref[...]) * v

    if valid_rows < seq_rows:
        rid = _pmod(lax.broadcasted_iota(jnp.int32, (rows, 1), 0), seq_rows)
        ok = (rid < valid_rows).astype(F32)
        logw, kk, kp, v = logw * ok, kk * ok, kp * ok, v * ok

    r_s[...] = r
    k_s[...] = kp
    v_s[...] = v
    kk_s[...] = kk
    b_s[...] = kk * a
    lw_s[...] = logw

    tid = _pmod(lax.broadcasted_iota(jnp.int32, (rows, 1), 0), seg)
    cw = logw
    sh = 1
    while sh < seg:
        cw = cw + jnp.where(tid >= sh, pltpu.roll(cw, sh, 0), 0.0)
        sh *= 2
    cw_s[...] = cw

    si = lax.broadcasted_iota(jnp.int32, (c2, c2), 0)
    sj = lax.broadcasted_iota(jnp.int32, (c2, c2), 1)
    same = _pdiv(si, seg) == _pdiv(sj, seg)
    m_strict = (same & (_pmod(sj, seg) < _pmod(si, seg))).astype(F32)
    m_incl = (same & (_pmod(sj, seg) <= _pmod(si, seg))).astype(F32)
    eye = (si == sj).astype(F32)
    n_fact = int(math.log2(seg))

    def load_state(g):
        for p in range(N_PAIRS):
            zero = jnp.zeros((R_HEAD_DIM, R_HEAD_DIM), F32)
            sbd[p, 0:R_HEAD_DIM, :] = jnp.concatenate([sin_ref[g, 2 * p], zero], axis=1)
            sbd[p, R_HEAD_DIM:LANES, :] = jnp.concatenate([zero, sin_ref[g, 2 * p + 1]], axis=1)

    def store_state(g):
        for p in range(N_PAIRS):
            sout_ref[g, 2 * p] = sbd[p, 0:R_HEAD_DIM, 0:R_HEAD_DIM]
            sout_ref[g, 2 * p + 1] = sbd[p, R_HEAD_DIM:LANES, R_HEAD_DIM:LANES]

    if not per_seq_state:
        @pl.when(t_step == 0)
        def _():
            load_state(0)

    pairs = range(N_PAIRS)
    col = lambda p: slice(p * LANES, (p + 1) * LANES)
    stack2 = lambda t: jnp.concatenate([t * me, t * mo], axis=0)

    def phase1(i, carry_):
        items = [(i * PHASE1_CHUNKS + u, p) for u in range(PHASE1_CHUNKS) for p in pairs]
        n = range(len(items))
        ats, rts, vs, bts, kts = [], [], [], [], []
        for c, p in items:
            rs = pl.ds(pl.multiple_of(c * chunk, chunk), chunk)
            cwc = cw_s[rs, col(p)]
            w_in = jnp.exp(cwc)
            w_ex = jnp.exp(cwc - lw_s[rs, col(p)])
            w_inv = jnp.exp(-cwc)
            ats.append(stack2(-kk_s[rs, col(p)] * w_ex))
            rts.append(stack2(r_s[rs, col(p)] * w_in))
            vs.append(stack2(v_s[rs, col(p)]))
            bts.append(b_s[rs, col(p)] * w_inv)
            kts.append(k_s[rs, col(p)] * w_inv)
        sc = [_dot_nt(jnp.concatenate([ats[j], rts[j]], axis=0),
                      jnp.concatenate([bts[j], bts[j], kts[j], kts[j]], axis=0)) for j in n]
        l_ab = [sc[j][:c2, :c2] * m_strict for j in n]
        a_ak = [sc[j][:c2, c2:] * m_strict for j in n]
        a_rb = [sc[j][c2:, :c2] * m_incl for j in n]
        a_rk = [sc[j][c2:, c2:] * m_incl for j in n]

        pw = l_ab
        inv = [eye + l_ab[j] for j in n]
        for _ in range(n_fact - 1):
            pw = [_dot(pw[j], pw[j]) for j in n]
            inv = [_dot(inv[j], eye + pw[j]) for j in n]

        ta = [_dot(inv[j], jnp.concatenate([ats[j], a_ak[j]], axis=1)) for j in n]
        ab = [_dot(a_rb[j], ta[j]) for j in n]
        uy = [_dot(jnp.concatenate([ta[j][:, LANES:], ab[j][:, LANES:] + a_rk[j]], axis=0), vs[j])
              for j in n]
        for j, (c, p) in enumerate(items):
            idx = c * N_PAIRS + p
            xg_s[idx] = jnp.concatenate([ta[j][:, :LANES], rts[j] + ab[j][:, :LANES]], axis=0)
            uy_s[idx] = uy[j]
            bk_s[idx] = jnp.concatenate([bts[j], kts[j]], axis=0)
        return carry_

    lax.fori_loop(0, n_chunks // PHASE1_CHUNKS, phase1, 0)

    segs_per_chunk = chunk // seg

    def phase2(i, carry_):
        if per_seq_state:
            load_state(i)
        c = _pdiv(i, segs_per_chunk)
        off = pl.multiple_of(_pmod(i, segs_per_chunk) * seg, seg)
        tok = pl.ds(pl.multiple_of(i * seg, seg), seg)
        tail = pl.ds(pl.multiple_of(i * seg + seg - SUBLANES, SUBLANES), SUBLANES)
        def quarter(ref, p, q):
            return ref[c * N_PAIRS + p, pl.ds(pl.multiple_of(q * chunk + off, seg), seg), :]

        s0 = [sbd[p] for p in pairs]
        xg = [jnp.concatenate([quarter(xg_s, p, q) for q in range(4)], axis=0) for p in pairs]
        uyv = [_dot2_nt(xg[p], s0[p]) for p in pairs]
        uyv = [uyv[p] + jnp.concatenate([quarter(uy_s, p, q) for q in range(4)], axis=0)
               for p in pairs]
        uvt = [jnp.concatenate([uyv[p][:2 * seg], stack2(v_s[tok, col(p)])], axis=0).T for p in pairs]
        rhs = []
        for p in pairs:
            bt, kt = quarter(bk_s, p, 0), quarter(bk_s, p, 1)
            rhs.append(jnp.concatenate([bt, bt, kt, kt], axis=0))
        upd = [_dot(uvt[p], rhs[p]) for p in pairs]
        for p in pairs:
            w_end = jnp.exp(cw_s[tail, col(p)][SUBLANES - 1:SUBLANES, :])
            sbd[p] = (s0[p] + upd[p] * m_bd) * w_end
            y_s[tok, col(p)] = uyv[p][2 * seg:3 * seg] + uyv[p][3 * seg:]
        if per_seq_state:
            store_state(i)
        return carry_

    lax.fori_loop(0, rows // seg, phase2, 0)

    if not per_seq_state:
        @pl.when(t_step == n_steps - 1)
        def _():
            store_state(0)

    y = y_s[...]
    inv_n = 1.0 / R_HEAD_DIM
    mean = seg_sum(y) * inv_n
    d = y - mean
    var = seg_sum(d * d) * inv_n
    yn = d * lax.rsqrt(var + GN_EPS)
    yn = yn * lg_ref[...] + lb_ref[...]
    y_ref[...] = ((yn + bon_s[...]) * g_s[...]).astype(y_ref.dtype)


def _wkv(zr, shift_in, s_in, P, *, n_seq, seq_rows, seg, valid_rows, n_steps):
    rows = n_seq * seq_rows
    chunk = WKV_CHUNK
    n_pc = (rows // chunk) * N_PAIRS
    n_groups = shift_in.shape[0] // n_seq
    n = zr.shape[0]
    small = [P['mu_shift'], P['decay_base'], P['lora_up'], P['iclr_base'], P['gate_up'], P['k_k'], P['k_a'],
             P['r_k'], P['lnx_g'], P['lnx_b']]
    tok = lambda b, t: (b * n_steps + t, 0)
    seq3 = lambda b, t: (b, 0, 0)
    seq4 = lambda b, t: (b, 0, 0, 0)
    state_block = (n_seq, R_HEADS, R_HEAD_DIM, R_HEAD_DIM)
    tile = lambda w: pltpu.VMEM((rows, w), F32)
    return pl.pallas_call(
        functools.partial(_wkv_kernel, n_seq=n_seq, seq_rows=seq_rows, chunk=chunk, seg=seg,
                          valid_rows=valid_rows),
        grid=(n_groups, n_steps),
        in_specs=[pl.BlockSpec((rows, SHIFT_W), tok), pl.BlockSpec((n_seq, 1, SHIFT_W), seq3),
                  pl.BlockSpec(state_block, seq4)] + [_const_spec(a.shape) for a in small],
        out_specs=[pl.BlockSpec((rows, R_WIDTH), tok), pl.BlockSpec(state_block, seq4)],
        out_shape=[jax.ShapeDtypeStruct((n, R_WIDTH), BF16), jax.ShapeDtypeStruct(s_in.shape, F32)],
        scratch_shapes=[pltpu.VMEM((N_PAIRS, LANES, LANES), F32), pltpu.VMEM((1, SHIFT_W), F32),
                        tile(SHIFT_W)] + [tile(R_WIDTH)] * 10 + [
                            pltpu.VMEM((n_pc, 4 * chunk, LANES), F32), pltpu.VMEM((n_pc, 4 * chunk, LANES), F32),
                            pltpu.VMEM((n_pc, 2 * chunk, LANES), F32)],
        compiler_params=_cparams("parallel", "arbitrary"),
        name="wkv7",
    )(zr, shift_in, s_in, *small)


def _pair_kv(t):
    lane = lax.broadcasted_iota(jnp.int32, (1, LANES), 1)
    even = lane < HEAD_DIM
    sw = pltpu.roll(t, HEAD_DIM, 1)
    return jnp.where(even, t, sw), jnp.where(even, sw, t)


def _attend(qs, kcs, vcs, mask, sink_cols):
    n = range(len(qs))
    s = [_dot_nt(qs[i], kcs[i]) * ATTN_SCALE for i in n]
    s = [jnp.where(mask, s[i], -jnp.inf) for i in n]
    m = [jnp.maximum(jnp.max(s[i], axis=-1, keepdims=True), sink_cols[i]) for i in n]
    p = [jnp.exp(s[i] - m[i]) for i in n]
    den = [jnp.sum(p[i], axis=-1, keepdims=True) + jnp.exp(sink_cols[i] - m[i]) for i in n]
    o = [_dot(p[i], vcs[i]) for i in n]
    return [o[i] / den[i] for i in n]


def _attn_prompt_kernel(sink_ref, q_ref, kp_ref, kc_ref, vp_ref, vc_ref, o_ref):
    n = pl.program_id(1)
    lane = lax.broadcasted_iota(jnp.int32, (1, LANES), 1)
    even = lane < HEAD_DIM
    kcat = jnp.concatenate([kp_ref[...], kc_ref[...]], axis=0)
    vcat = jnp.concatenate([vp_ref[...], vc_ref[...]], axis=0)
    ks = _pair_kv(kcat)
    vs = _pair_kv(vcat)
    qi = _pmod(lax.broadcasted_iota(jnp.int32, (2 * BLOCK, 2 * BLOCK), 0), BLOCK) + BLOCK
    kj = lax.broadcasted_iota(jnp.int32, (2 * BLOCK, 2 * BLOCK), 1)
    dist = qi - kj
    mask = (dist >= 0) & (dist <= WINDOW) & ((n > 0) | (kj >= BLOCK))
    top = lax.broadcasted_iota(jnp.int32, (2 * BLOCK, 1), 0) < BLOCK
    blocks = range(Q_WIDTH // LANES)
    qs, sink_cols = [], []
    for j in blocks:
        qb = q_ref[:, j * LANES:(j + 1) * LANES]
        qs.append(jnp.concatenate([jnp.where(even, qb, 0.0), jnp.where(even, 0.0, qb)], axis=0))
        sink_cols.append(jnp.where(top, sink_ref[2 * j], sink_ref[2 * j + 1]))
    kvh = [(2 * j) // GROUP for j in blocks]
    o = _attend(qs, [ks[h] for h in kvh], [vs[h] for h in kvh], mask, sink_cols)
    for j in blocks:
        o_ref[:, j * LANES:(j + 1) * LANES] = jnp.where(even, o[j][:BLOCK], o[j][BLOCK:]).astype(o_ref.dtype)


def _attn_prompt(q, k, v, sinks, batch, seq):
    nb = seq // BLOCK
    cur = lambda b, n: (b * nb + n, 0)
    prv = lambda b, n: (b * nb + jnp.maximum(n - 1, 0), 0)
    kv = lambda f: pl.BlockSpec((BLOCK, KV_WIDTH), f)
    return pl.pallas_call(
        _attn_prompt_kernel,
        grid=(batch, nb),
        in_specs=[pl.BlockSpec(memory_space=pltpu.SMEM), pl.BlockSpec((BLOCK, Q_WIDTH), cur),
                  kv(prv), kv(cur), kv(prv), kv(cur)],
        out_specs=pl.BlockSpec((BLOCK, Q_WIDTH), cur),
        out_shape=jax.ShapeDtypeStruct((batch * seq, Q_WIDTH), BF16),
        compiler_params=_cparams("parallel", "parallel"),
        name="attn_prompt",
    )(sinks, q, k, k, v, v)


def _attn_sample_kernel(sink_ref, q_ref, kb_ref, kn_ref, vb_ref, vn_ref, o_ref, *, n_seq, wbuf):
    lane = lax.broadcasted_iota(jnp.int32, (1, LANES), 1)
    even = lane < HEAD_DIM
    rows = GROUP * SAMPLE_PAD
    nk = wbuf + 2 * SAMPLE_PAD
    qi = _pmod(lax.broadcasted_iota(jnp.int32, (rows, nk), 0), SAMPLE_PAD)
    kj = lax.broadcasted_iota(jnp.int32, (rows, nk), 1)
    dist = qi + wbuf - kj
    mask = (dist >= 0) & (dist <= WINDOW)
    rid = _pdiv(lax.broadcasted_iota(jnp.int32, (rows, 1), 0), SAMPLE_PAD)
    sink_cols = []
    for h in range(KV_HEADS):
        col = jnp.zeros((rows, 1), F32)
        for gq in range(GROUP):
            col = jnp.where(rid == gq, sink_ref[h * GROUP + gq], col)
        sink_cols.append(col)
    unroll = 2
    pad = jnp.zeros((SAMPLE_PAD, KV_WIDTH), F32)
    blocks_per_kv = GROUP // 2

    def body(i, carry_):
        qs, kcs, vcs = [], [], []
        for u in range(unroll):
            g = i * unroll + u
            ks = _pair_kv(jnp.concatenate([kb_ref[g], kn_ref[g], pad], axis=0))
            vs = _pair_kv(jnp.concatenate([vb_ref[g], vn_ref[g], pad], axis=0))
            for h in range(KV_HEADS):
                parts = []
                for j in range(h * blocks_per_kv, (h + 1) * blocks_per_kv):
                    qb = q_ref[g, :, j * LANES:(j + 1) * LANES]
                    parts += [jnp.where(even, qb, 0.0), jnp.where(even, 0.0, qb)]
                qs.append(jnp.concatenate(parts, axis=0))
                kcs.append(ks[h])
                vcs.append(vs[h])
        o = _attend(qs, kcs, vcs, mask, sink_cols * unroll)
        for u in range(unroll):
            g = i * unroll + u
            for h in range(KV_HEADS):
                oh = o[u * KV_HEADS + h]
                for jj in range(blocks_per_kv):
                    j = h * blocks_per_kv + jj
                    r0 = 2 * jj * SAMPLE_PAD
                    o_ref[g, :, j * LANES:(j + 1) * LANES] = jnp.where(
                        even, oh[r0:r0 + SAMPLE_PAD], oh[r0 + SAMPLE_PAD:r0 + 2 * SAMPLE_PAD]).astype(o_ref.dtype)
        return carry_

    lax.fori_loop(0, n_seq // unroll, body, 0)


def _attn_sample(q, k, v, k_buf, v_buf, sinks, n_seq):
    b, wbuf = k_buf.shape[0], k_buf.shape[1]
    idx = lambda i: (i, 0, 0)
    new = pl.BlockSpec((n_seq, SAMPLE_PAD, KV_WIDTH), idx)
    buf = pl.BlockSpec((n_seq, wbuf, KV_WIDTH), idx)
    return pl.pallas_call(
        functools.partial(_attn_sample_kernel, n_seq=n_seq, wbuf=wbuf),
        grid=(b // n_seq,),
        in_specs=[pl.BlockSpec(memory_space=pltpu.SMEM), pl.BlockSpec((n_seq, SAMPLE_PAD, Q_WIDTH), idx),
                  buf, new, buf, new],
        out_specs=pl.BlockSpec((n_seq, SAMPLE_PAD, Q_WIDTH), idx),
        out_shape=jax.ShapeDtypeStruct((b, SAMPLE_PAD, Q_WIDTH), BF16),
        compiler_params=_cparams("parallel"),
        name="attn_sample",
    )(sinks, q, k_buf, k, v_buf, v)


def _layer_norm(x, g, b):
    mu = jnp.mean(x, axis=-1, keepdims=True)
    d = x - mu
    var = jnp.mean(d * d, axis=-1, keepdims=True)
    return d * lax.rsqrt(var + LN_EPS) * g + b


def _mix_ffn_kernel(x_ref, yr_ref, ya_ref, wg_ref, wbr_ref, wba_ref, wo_ref, g1_ref, b1_ref,
                    wu_ref, wd_ref, g2_ref, b2_ref, o_ref, *, ff_chunk):
    x = x_ref[...]
    xb = x.astype(BF16)
    gate_r = jax.nn.sigmoid(jnp.dot(xb, wg_ref[:, :D_MODEL], preferred_element_type=F32))
    mix = gate_r * jnp.dot(yr_ref[...], wbr_ref[...], preferred_element_type=F32)
    gate_a = jax.nn.sigmoid(jnp.dot(xb, wg_ref[:, D_MODEL:], preferred_element_type=F32))
    mix = mix + gate_a * jnp.dot(ya_ref[...], wba_ref[...], preferred_element_type=F32)
    x1 = _layer_norm(ALPHA * x + _dot(mix, wo_ref[...]), g1_ref[...], b1_ref[...])
    x1b = x1.astype(BF16)
    acc = ALPHA * x1
    for c in range(D_FF // ff_chunk):
        cs = slice(c * ff_chunk, (c + 1) * ff_chunk)
        h = jnp.maximum(jnp.dot(x1b, wu_ref[:, cs], preferred_element_type=F32), 0.0)
        acc = acc + _dot(h * h, wd_ref[cs, :])
    o_ref[...] = _layer_norm(acc, g2_ref[...], b2_ref[...])


def _mix_ffn(x, yr, ya, P, tm):
    n = x.shape[0]
    row = lambda i: (i, 0)
    ws = [P['w_gate'], P['w_br_rwkv'], P['w_br_attn'], P['w_out'], P['ln1_g'], P['ln1_b'],
          P['w_ff_up'], P['w_ff_down'], P['ln2_g'], P['ln2_b']]
    wspec = lambda a: pl.BlockSpec(a.shape, lambda i: (0, 0), pipeline_mode=pl.Buffered(1))
    return pl.pallas_call(
        functools.partial(_mix_ffn_kernel, ff_chunk=1024),
        grid=(n // tm,),
        in_specs=[pl.BlockSpec((tm, D_MODEL), row), pl.BlockSpec((tm, R_WIDTH), row),
                  pl.BlockSpec((tm, Q_WIDTH), row)] + [wspec(a) for a in ws],
        out_specs=pl.BlockSpec((tm, D_MODEL), row),
        out_shape=jax.ShapeDtypeStruct((n, D_MODEL), F32),
        compiler_params=_cparams("parallel"),
        name="mix_ffn",
    )(x, yr, ya, *ws)


def _layer_params(l, w_in, mu_shift, decay_base, decay_up, iclr_base, iclr_up, gate_up, k_k, k_a, r_k,
                  lnx_g, lnx_b, sinks, w_br_rwkv, w_br_attn, w_out, ln1_g, ln1_b, w_ff_up, w_ff_down,
                  ln2_g, ln2_b):
    row = lambda a: a[l].reshape(1, -1)
    zeros = jnp.zeros((DECAY_LORA, R_WIDTH), F32)
    lora_up = jnp.concatenate([jnp.concatenate([decay_up[l], zeros], axis=1),
                               jnp.concatenate([zeros, iclr_up[l]], axis=1)], axis=0)
    return dict(
        w_qkv=w_in[l][:, :QKV_END].astype(BF16), w_gate=w_in[l][:, QKV_END:].astype(BF16),
        mu_shift=row(mu_shift), decay_base=row(decay_base), lora_up=lora_up.astype(BF16),
        iclr_base=row(iclr_base), gate_up=gate_up[l].astype(BF16), k_k=row(k_k), k_a=row(k_a),
        r_k=row(r_k), lnx_g=row(lnx_g), lnx_b=row(lnx_b), sinks=sinks[l],
        w_br_rwkv=w_br_rwkv[l].astype(BF16), w_br_attn=w_br_attn[l].astype(BF16),
        w_out=w_out[l].astype(BF16), ln1_g=row(ln1_g), ln1_b=row(ln1_b),
        w_ff_up=w_ff_up[l].astype(BF16), w_ff_down=w_ff_down[l].astype(BF16),
        ln2_g=row(ln2_g), ln2_b=row(ln2_b))


def _prompt_layer(x, P, tables, batch, seq, wbuf):
    tm = 512
    zr, q, k, v = _inproj(x, P['w_qkv'], tables, seq // tm, tm)
    shift0 = jnp.zeros((batch, 1, SHIFT_W), F32)
    s0 = jnp.zeros((batch, R_HEADS, R_HEAD_DIM, R_HEAD_DIM), F32)
    yr, s_new = _wkv(zr, shift0, s0, P, n_seq=1, seq_rows=tm, seg=WKV_CHUNK, valid_rows=tm, n_steps=seq // tm)
    ya = _attn_prompt(q, k, v, P['sinks'], batch, seq)
    x = _mix_ffn(x, yr, ya, P, tm)
    tail = lambda t, w: t.reshape(batch, seq, w)[:, seq - wbuf:].reshape(batch, wbuf, KV_HEADS, HEAD_DIM)
    return x, zr.reshape(batch, seq, SHIFT_W)[:, -1], s_new, tail(k, KV_WIDTH), tail(v, KV_WIDTH)


def _sample_layer(x, P, tables, shift_prev, s_prev, k_buf, v_buf, batch, seq):
    n = batch * SAMPLE_PAD
    tm = min(n, 512)
    n_seq = 16
    wbuf = k_buf.shape[1]
    zr, q, k, v = _inproj(x, P['w_qkv'], tables, 1, tm)
    yr, s_new = _wkv(zr, shift_prev[:, None, :], s_prev, P, n_seq=n_seq, seq_rows=SAMPLE_PAD,
                     seg=SAMPLE_PAD, valid_rows=seq, n_steps=1)
    seq3 = lambda t, w: t.reshape(batch, SAMPLE_PAD, w)
    kb = k_buf.reshape(batch, wbuf, KV_WIDTH)
    vb = v_buf.reshape(batch, wbuf, KV_WIDTH)
    ya = _attn_sample(seq3(q, Q_WIDTH), seq3(k, KV_WIDTH), seq3(v, KV_WIDTH), kb, vb, P['sinks'], n_seq)
    x = _mix_ffn(x, yr, ya.reshape(n, Q_WIDTH), P, tm)
    window = lambda old, new: jnp.concatenate([old, seq3(new, KV_WIDTH)[:, :seq]], axis=1)[:, -wbuf:].reshape(
        batch, wbuf, KV_HEADS, HEAD_DIM)
    return x, seq3(zr, SHIFT_W)[:, seq - 1], s_new, window(kb, k), window(vb, v)


def kernel(x_prompt, x_sample, state_wkv, state_shift, cache_k_win, cache_v_win, w_in, mu_shift, decay_base, decay_up, iclr_base, iclr_up, gate_up, k_k, k_a, r_k, lnx_g, lnx_b, sinks, w_br_rwkv, w_br_attn, w_out, ln1_g, ln1_b, w_ff_up, w_ff_down, ln2_g, ln2_b):
    bp, tp, _ = x_prompt.shape
    bs, ts, _ = x_sample.shape
    wbuf = cache_k_win.shape[2]
    half = HEAD_DIM // 2
    inv_freq = ROPE_THETA ** (-jnp.arange(half, dtype=F32) / half)
    tab_p = _rope_tables(inv_freq, tp, 0, tp)
    tab_s = _rope_tables(inv_freq, min(bs * SAMPLE_PAD, 512), PAST_LEN, SAMPLE_PAD)

    hp = x_prompt.reshape(bp * tp, D_MODEL)
    hs = jnp.pad(x_sample, ((0, 0), (0, SAMPLE_PAD - ts), (0, 0))).reshape(bs * SAMPLE_PAD, D_MODEL)
    outs_p, outs_s = [], []
    for l in range(DEPTH):
        P = _layer_params(l, w_in, mu_shift, decay_base, decay_up, iclr_base, iclr_up, gate_up, k_k, k_a, r_k,
                          lnx_g, lnx_b, sinks, w_br_rwkv, w_br_attn, w_out, ln1_g, ln1_b, w_ff_up,
                          w_ff_down, ln2_g, ln2_b)
        hp, *st = _prompt_layer(hp, P, tab_p, bp, tp, wbuf)
        outs_p.append(st)
        hs, *st = _sample_layer(hs, P, tab_s, state_shift[l], state_wkv[l], cache_k_win[l], cache_v_win[l],
                                bs, ts)
        outs_s.append(st)
    stack = lambda outs, i: jnp.stack([o[i] for o in outs])
    y_p = hp.reshape(bp, tp, D_MODEL)
    y_s = hs.reshape(bs, SAMPLE_PAD, D_MODEL)[:, :ts]
    return (y_p, y_s,
            stack(outs_p, 1), stack(outs_p, 0), stack(outs_p, 2), stack(outs_p, 3),
            stack(outs_s, 1), stack(outs_s, 0), stack(outs_s, 2), stack(outs_s, 3))
```

```python
import functools
import math

import jax
import jax.numpy as jnp
from jax import lax
from jax.experimental import pallas as pl
from jax.experimental.pallas import tpu as pltpu

F32 = jnp.float32
BF16 = jnp.bfloat16

D_MODEL = 1024
DEPTH = 2
PAST_LEN = 8192
R_HEADS = 8
R_HEAD_DIM = 64
R_WIDTH = R_HEADS * R_HEAD_DIM
DECAY_LORA = 64
ICLR_LORA = 64
GATE_LORA = 128
SHIFT_W = 3 * R_WIDTH + DECAY_LORA + ICLR_LORA + GATE_LORA
A_HEADS = 8
KV_HEADS = 2
HEAD_DIM = 64
Q_WIDTH = A_HEADS * HEAD_DIM
KV_WIDTH = KV_HEADS * HEAD_DIM
GROUP = A_HEADS // KV_HEADS
WINDOW = 128
BLOCK = 128
ROPE_THETA = 10000.0
ATTN_SCALE = HEAD_DIM ** -0.5
D_FF = 4 * D_MODEL
ALPHA = (2 * DEPTH) ** 0.25
LN_EPS = 1e-5
GN_EPS = 64e-5
QKV_END = SHIFT_W + Q_WIDTH + 2 * KV_WIDTH

LANES = 128
SUBLANES = 8
VMEM_LIMIT = 56 * 1024 * 1024

SAMPLE_PAD = SUBLANES
N_PAIRS = R_HEADS // 2
WKV_CHUNK = LANES // 2
PHASE1_CHUNKS = 2


def _cparams(*sem):
    return pltpu.CompilerParams(dimension_semantics=sem, vmem_limit_bytes=VMEM_LIMIT)


def _const_spec(shape):
    nd = len(shape)
    return pl.BlockSpec(shape, lambda *_: (0,) * nd)


def _layer_spec(a, layer, single_buffer=False):
    mode = pl.Buffered(1) if single_buffer else None
    return pl.BlockSpec((None,) + a.shape[1:], lambda *_: (layer, 0, 0), pipeline_mode=mode)


def _dot(a, b):
    return jnp.dot(a.astype(BF16), b.astype(BF16), preferred_element_type=F32)


def _dot_nt(a, b):
    return lax.dot_general(a.astype(BF16), b.astype(BF16), (((1,), (1,)), ((), ())),
                           preferred_element_type=F32)


def _pmod(x, n):
    assert n & (n - 1) == 0
    return x & (n - 1)


def _pdiv(x, n):
    assert n & (n - 1) == 0
    return x >> (n.bit_length() - 1)


def _split(x):
    hi = x.astype(BF16)
    lo = (x - hi.astype(F32)).astype(BF16)
    return hi, lo


def _dot2_nt(a, b):
    bh, bl = _split(b)
    d = functools.partial(lax.dot_general, dimension_numbers=(((1,), (1,)), ((), ())),
                          preferred_element_type=F32)
    ab = a.astype(BF16)
    return d(ab, bh) + d(ab, bl)


def _rope_table_kernel(invf_ref, cos_ref, sa_ref, sb_ref, *, rows, offset, period):
    i = pl.program_id(0)
    row = lax.broadcasted_iota(jnp.int32, (rows, LANES), 0) + i * rows
    pos = offset + _pmod(row, period)
    ang = pos.astype(F32) * invf_ref[...]
    lane = lax.broadcasted_iota(jnp.int32, (rows, LANES), 1)
    first = _pmod(lane, HEAD_DIM) < (HEAD_DIM // 2)
    c = jnp.cos(ang)
    s = jnp.sin(ang)
    cos_ref[...] = c
    sa_ref[...] = jnp.where(first, -s, 0.0)
    sb_ref[...] = jnp.where(first, 0.0, s)


def _rope_tables(inv_freq, n_rows, offset, period):
    rows = min(n_rows, 1024)
    invf = jnp.tile(inv_freq, LANES // (HEAD_DIM // 2)).reshape(1, LANES)
    out = jax.ShapeDtypeStruct((n_rows, LANES), F32)
    return pl.pallas_call(
        functools.partial(_rope_table_kernel, rows=rows, offset=offset, period=period),
        grid=(n_rows // rows,),
        in_specs=[_const_spec((1, LANES))],
        out_specs=[pl.BlockSpec((rows, LANES), lambda i: (i, 0))] * 3,
        out_shape=[out] * 3,
        compiler_params=_cparams("parallel"),
        name="rope_tables",
    )(invf)


def _inproj_kernel(x_ref, w_ref, cos_ref, sa_ref, sb_ref, zr_ref, q_ref, k_ref, v_ref):
    xb = x_ref[...].astype(BF16)
    zr_ref[...] = jnp.dot(xb, w_ref[:, :SHIFT_W], preferred_element_type=F32)
    cos, sa, sb = cos_ref[...], sa_ref[...], sb_ref[...]

    def rope(t):
        return (t * cos + pltpu.roll(t, LANES - HEAD_DIM // 2, 1) * sa
                + pltpu.roll(t, HEAD_DIM // 2, 1) * sb)

    zq = jnp.dot(xb, w_ref[:, SHIFT_W:SHIFT_W + Q_WIDTH], preferred_element_type=F32)
    for j in range(Q_WIDTH // LANES):
        q_ref[:, j * LANES:(j + 1) * LANES] = rope(zq[:, j * LANES:(j + 1) * LANES])
    zkv = jnp.dot(xb, w_ref[:, SHIFT_W + Q_WIDTH:QKV_END], preferred_element_type=F32)
    k_ref[...] = rope(zkv[:, :KV_WIDTH])
    v_ref[...] = zkv[:, KV_WIDTH:]


def _inproj(x, w_in, layer, tables, tab_blocks, tm):
    n = x.shape[0]
    row = lambda i: (i, 0)
    tab = pl.BlockSpec((tm, LANES), lambda i: (lax.rem(i, tab_blocks), 0))
    w_spec = pl.BlockSpec((None, D_MODEL, QKV_END), lambda i: (layer, 0, 0))
    return pl.pallas_call(
        _inproj_kernel,
        grid=(n // tm,),
        in_specs=[pl.BlockSpec((tm, D_MODEL), row), w_spec, tab, tab, tab],
        out_specs=[pl.BlockSpec((tm, SHIFT_W), row), pl.BlockSpec((tm, Q_WIDTH), row),
                   pl.BlockSpec((tm, KV_WIDTH), row), pl.BlockSpec((tm, KV_WIDTH), row)],
        out_shape=[jax.ShapeDtypeStruct((n, SHIFT_W), F32), jax.ShapeDtypeStruct((n, Q_WIDTH), F32),
                   jax.ShapeDtypeStruct((n, KV_WIDTH), F32), jax.ShapeDtypeStruct((n, KV_WIDTH), F32)],
        compiler_params=_cparams("parallel"),
        name="inproj",
    )(x, w_in, *tables)


def _wkv_kernel(zr_ref, shift_ref, sin_ref, mu_ref, db_ref, lora_ref, ib_ref, gu_ref, kk_ref, ka_ref,
                rk_ref, lg_ref, lb_ref,
                y_ref, sout_ref,
                sbd, carry, prev_s, r_s, k_s, v_s, kk_s, b_s, lw_s, cw_s, y_s, bon_s, g_s, xg_s, uy_s, bk_s,
                *, layer, n_seq, seq_rows, chunk, seg, valid_rows, carried_state, lanes):
    t_step = pl.program_id(1)
    n_steps = pl.num_programs(1)
    rows = n_seq * seq_rows
    n_chunks = rows // chunk
    c2 = 2 * chunk
    par = lambda ref: ref[layer:layer + 1, :]

    z = zr_ref[...].reshape(rows, SHIFT_W)
    prev_s[...] = pltpu.roll(z, 1, 0)

    @pl.when(t_step == 0)
    def _():
        for g in range(n_seq):
            carry[g:g + 1, :] = shift_ref[g]
    for g in range(n_seq):
        prev_s[g * seq_rows:g * seq_rows + 1, :] = carry[g:g + 1, :]
        carry[g:g + 1, :] = z[(g + 1) * seq_rows - 1:(g + 1) * seq_rows, :]
    prev = prev_s[...]
    zs = z + (prev - z) * par(mu_ref)

    o1, o2, o3 = R_WIDTH, 2 * R_WIDTH, 3 * R_WIDTH
    o4 = o3 + DECAY_LORA + ICLR_LORA
    r = zs[:, :o1]
    k = zs[:, o1:o2]
    v = zs[:, o2:o3]
    wa = zs[:, o3:o4]
    gd = zs[:, o4:]

    lane = lax.broadcasted_iota(jnp.int32, (1, LANES), 1)
    lane_even = lane < R_HEAD_DIM
    me = lane_even.astype(F32)
    mo = 1.0 - me
    bi = _pdiv(lax.broadcasted_iota(jnp.int32, (LANES, LANES), 0), R_HEAD_DIM)
    bj = _pdiv(lax.broadcasted_iota(jnp.int32, (LANES, LANES), 1), R_HEAD_DIM)
    m_bd = (bi == bj).astype(F32)
    bd_ones = m_bd.astype(BF16)

    def seg_sum(x):
        xb = x.astype(BF16)
        return jnp.concatenate([jnp.dot(xb[:, j * LANES:(j + 1) * LANES], bd_ones, preferred_element_type=F32)
                                for j in range(R_WIDTH // LANES)], axis=1)

    lora_in = jnp.where(lane_even, jnp.tanh(wa), wa)
    lora = _dot(lora_in, lora_ref[...])
    pre_w = par(db_ref) + lora[:, :R_WIDTH]
    logw = -math.exp(-0.5) * jax.nn.sigmoid(pre_w)
    a = jax.nn.sigmoid(par(ib_ref) + lora[:, R_WIDTH:])
    g_s[...] = _dot(jax.nn.sigmoid(gd), gu_ref[...])
    kk = k * par(kk_ref)
    kk = kk * lax.rsqrt(jnp.maximum(seg_sum(kk * kk), 1e-24))
    kp = k * (1.0 + (a - 1.0) * par(ka_ref))
    bon_s[...] = seg_sum(r * kp * par(rk_ref)) * v

    if valid_rows < seq_rows:
        rid = _pmod(lax.broadcasted_iota(jnp.int32, (rows, 1), 0), seq_rows)
        ok = (rid < valid_rows).astype(F32)
        logw, kk, kp, v = logw * ok, kk * ok, kp * ok, v * ok

    r_s[...] = r
    k_s[...] = kp
    v_s[...] = v
    kk_s[...] = kk
    b_s[...] = kk * a
    lw_s[...] = logw

    tid = _pmod(lax.broadcasted_iota(jnp.int32, (rows, 1), 0), seg)
    cw = logw
    sh = 1
    while sh < seg:
        cw = cw + jnp.where(tid >= sh, pltpu.roll(cw, sh, 0), 0.0)
        sh *= 2
    cw_s[...] = cw

    si = lax.broadcasted_iota(jnp.int32, (c2, c2), 0)
    sj = lax.broadcasted_iota(jnp.int32, (c2, c2), 1)
    same = _pdiv(si, seg) == _pdiv(sj, seg)
    m_strict = (same & (_pmod(sj, seg) < _pmod(si, seg))).astype(F32)
    m_incl = (same & (_pmod(sj, seg) <= _pmod(si, seg))).astype(F32)
    eye = (si == sj).astype(F32)
    n_fact = int(math.log2(seg))

    def load_state(g, u):
        for p in range(N_PAIRS):
            zero = jnp.zeros((R_HEAD_DIM, R_HEAD_DIM), F32)
            sbd[u * N_PAIRS + p, 0:R_HEAD_DIM, :] = jnp.concatenate([sin_ref[g, 2 * p], zero], axis=1)
            sbd[u * N_PAIRS + p, R_HEAD_DIM:LANES, :] = jnp.concatenate([zero, sin_ref[g, 2 * p + 1]], axis=1)

    def store_state(g, u):
        for p in range(N_PAIRS):
            sout_ref[g, 2 * p] = sbd[u * N_PAIRS + p, 0:R_HEAD_DIM, 0:R_HEAD_DIM]
            sout_ref[g, 2 * p + 1] = sbd[u * N_PAIRS + p, R_HEAD_DIM:LANES, R_HEAD_DIM:LANES]

    if carried_state:
        @pl.when(t_step == 0)
        def _():
            for u in range(lanes):
                load_state(u, u)

    pairs = range(N_PAIRS)
    col = lambda p: slice(p * LANES, (p + 1) * LANES)
    stack2 = lambda t: jnp.concatenate([t * me, t * mo], axis=0)

    def phase1(i, carry_):
        items = [(i * PHASE1_CHUNKS + u, p) for u in range(PHASE1_CHUNKS) for p in pairs]
        n = range(len(items))
        ats, rts, vs, bts, kts = [], [], [], [], []
        for c, p in items:
            rs = pl.ds(pl.multiple_of(c * chunk, chunk), chunk)
            cwc = cw_s[rs, col(p)]
            w_in = jnp.exp(cwc)
            w_ex = jnp.exp(cwc - lw_s[rs, col(p)])
            w_inv = jnp.exp(-cwc)
            ats.append(stack2(-kk_s[rs, col(p)] * w_ex))
            rts.append(stack2(r_s[rs, col(p)] * w_in))
            vs.append(stack2(v_s[rs, col(p)]))
            bts.append(b_s[rs, col(p)] * w_inv)
            kts.append(k_s[rs, col(p)] * w_inv)
        sc = [_dot_nt(jnp.concatenate([ats[j], rts[j]], axis=0),
                      jnp.concatenate([bts[j], bts[j], kts[j], kts[j]], axis=0)) for j in n]
        l_ab = [sc[j][:c2, :c2] * m_strict for j in n]
        a_ak = [sc[j][:c2, c2:] * m_strict for j in n]
        a_rb = [sc[j][c2:, :c2] * m_incl for j in n]
        a_rk = [sc[j][c2:, c2:] * m_incl for j in n]

        pw = l_ab
        inv = [eye + l_ab[j] for j in n]
        for _ in range(n_fact - 1):
            pw = [_dot(pw[j], pw[j]) for j in n]
            inv = [_dot(inv[j], eye + pw[j]) for j in n]

        ta = [_dot(inv[j], jnp.concatenate([ats[j], a_ak[j]], axis=1)) for j in n]
        ab = [_dot(a_rb[j], ta[j]) for j in n]
        uy = [_dot(jnp.concatenate([ta[j][:, LANES:], ab[j][:, LANES:] + a_rk[j]], axis=0), vs[j])
              for j in n]
        for j, (c, p) in enumerate(items):
            idx = c * N_PAIRS + p
            xg_s[idx] = jnp.concatenate([ta[j][:, :LANES], rts[j] + ab[j][:, :LANES]], axis=0)
            uy_s[idx] = uy[j]
            bk_s[idx] = jnp.concatenate([bts[j], kts[j]], axis=0)
        return carry_

    lax.fori_loop(0, n_chunks // PHASE1_CHUNKS, phase1, 0)

    segs_per_chunk = chunk // seg
    segs_per_seq = seq_rows // seg

    def phase2(i, carry_):
        segs = [u * segs_per_seq + i if carried_state else i * lanes + u for u in range(lanes)]
        if not carried_state:
            for u in range(lanes):
                load_state(segs[u], u)
        items = [(u, p) for u in range(lanes) for p in pairs]
        n = range(len(items))

        def quarter(ref, u, p, q):
            sg = segs[u]
            off = _pmod(sg, segs_per_chunk) * seg
            return ref[_pdiv(sg, segs_per_chunk) * N_PAIRS + p,
                       pl.ds(pl.multiple_of(q * chunk + off, seg), seg), :]

        tok = [pl.ds(pl.multiple_of(sg * seg, seg), seg) for sg in segs]
        tail = [pl.ds(pl.multiple_of(sg * seg + seg - SUBLANES, SUBLANES), SUBLANES) for sg in segs]
        s0 = [sbd[u * N_PAIRS + p] for u, p in items]
        xg = [jnp.concatenate([quarter(xg_s, u, p, q) for q in range(4)], axis=0) for u, p in items]
        uyv = [_dot2_nt(xg[j], s0[j]) for j in n]
        uyv = [uyv[j] + jnp.concatenate([quarter(uy_s, u, p, q) for q in range(4)], axis=0)
               for j, (u, p) in enumerate(items)]
        uvt = [jnp.concatenate([uyv[j][:2 * seg], stack2(v_s[tok[u], col(p)])], axis=0).T
               for j, (u, p) in enumerate(items)]
        rhs = []
        for u, p in items:
            bt, kt = quarter(bk_s, u, p, 0), quarter(bk_s, u, p, 1)
            rhs.append(jnp.concatenate([bt, bt, kt, kt], axis=0))
        upd = [_dot(uvt[j], rhs[j]) for j in n]
        for j, (u, p) in enumerate(items):
            w_end = jnp.exp(cw_s[tail[u], col(p)][SUBLANES - 1:SUBLANES, :])
            sbd[u * N_PAIRS + p] = (s0[j] + upd[j] * m_bd) * w_end
            y_s[tok[u], col(p)] = uyv[j][2 * seg:3 * seg] + uyv[j][3 * seg:]
        if not carried_state:
            for u in range(lanes):
                store_state(segs[u], u)
        return carry_

    lax.fori_loop(0, rows // seg // lanes, phase2, 0)

    if carried_state:
        @pl.when(t_step == n_steps - 1)
        def _():
            for u in range(lanes):
                store_state(u, u)

    y = y_s[...]
    inv_n = 1.0 / R_HEAD_DIM
    mean = seg_sum(y) * inv_n
    d = y - mean
    var = seg_sum(d * d) * inv_n
    yn = d * lax.rsqrt(var + GN_EPS)
    yn = yn * par(lg_ref) + par(lb_ref)
    y_ref[...] = ((yn + bon_s[...]) * g_s[...]).reshape(n_seq, seq_rows, R_WIDTH).astype(y_ref.dtype)


def _wkv(zr, shift_in, s_in, P, layer, *, n_seq, seq_rows, seg, valid_rows, carried_state, lanes):
    b, t_len, _ = zr.shape
    rows = n_seq * seq_rows
    chunk = WKV_CHUNK
    n_pc = (rows // chunk) * N_PAIRS
    assert (lanes == n_seq) if carried_state else (t_len == seq_rows and n_seq % lanes == 0)
    params = [P['mu_shift'], P['decay_base'], P['lora_up'], P['iclr_base'], P['gate_up'], P['k_k'], P['k_a'],
              P['r_k'], P['lnx_g'], P['lnx_b']]
    param_specs = [_layer_spec(a, layer) if a.ndim == 3 else _const_spec(a.shape) for a in params]
    tok = lambda g, t: (g, t, 0)
    seq3 = lambda g, t: (g, 0, 0)
    seq4 = lambda g, t: (g, 0, 0, 0)
    state_block = (n_seq, R_HEADS, R_HEAD_DIM, R_HEAD_DIM)
    tile = lambda w: pltpu.VMEM((rows, w), F32)
    return pl.pallas_call(
        functools.partial(_wkv_kernel, layer=layer, n_seq=n_seq, seq_rows=seq_rows, chunk=chunk, seg=seg,
                          valid_rows=valid_rows, carried_state=carried_state, lanes=lanes),
        grid=(b // n_seq, t_len // seq_rows),
        in_specs=[pl.BlockSpec((n_seq, seq_rows, SHIFT_W), tok), pl.BlockSpec((n_seq, 1, SHIFT_W), seq3),
                  pl.BlockSpec(state_block, seq4)] + param_specs,
        out_specs=[pl.BlockSpec((n_seq, seq_rows, R_WIDTH), tok), pl.BlockSpec(state_block, seq4)],
        out_shape=[jax.ShapeDtypeStruct((b, t_len, R_WIDTH), BF16), jax.ShapeDtypeStruct(s_in.shape, F32)],
        scratch_shapes=[pltpu.VMEM((lanes * N_PAIRS, LANES, LANES), F32), pltpu.VMEM((n_seq, SHIFT_W), F32),
                        tile(SHIFT_W)] + [tile(R_WIDTH)] * 10 + [
                            pltpu.VMEM((n_pc, 4 * chunk, LANES), F32), pltpu.VMEM((n_pc, 4 * chunk, LANES), F32),
                            pltpu.VMEM((n_pc, 2 * chunk, LANES), F32)],
        compiler_params=_cparams("parallel", "arbitrary"),
        name="wkv7",
    )(zr, shift_in, s_in, *params)


def _pair_kv(t):
    lane = lax.broadcasted_iota(jnp.int32, (1, LANES), 1)
    even = lane < HEAD_DIM
    sw = pltpu.roll(t, HEAD_DIM, 1)
    return jnp.where(even, t, sw), jnp.where(even, sw, t)


def _attend(qs, kcs, vcs, mask, sink_cols):
    n = range(len(qs))
    s = [_dot_nt(qs[i], kcs[i]) * ATTN_SCALE for i in n]
    s = [jnp.where(mask, s[i], -jnp.inf) for i in n]
    m = [jnp.maximum(jnp.max(s[i], axis=-1, keepdims=True), sink_cols[i]) for i in n]
    p = [jnp.exp(s[i] - m[i]) for i in n]
    den = [jnp.sum(p[i], axis=-1, keepdims=True) + jnp.exp(sink_cols[i] - m[i]) for i in n]
    o = [_dot(p[i], vcs[i]) for i in n]
    return [o[i] / den[i] for i in n]


def _attn_prompt_kernel(sink_ref, q_ref, kp_ref, kc_ref, vp_ref, vc_ref, o_ref, *, layer):
    n = pl.program_id(1)
    sink = lambda head: sink_ref[layer * A_HEADS + head]
    lane = lax.broadcasted_iota(jnp.int32, (1, LANES), 1)
    even = lane < HEAD_DIM
    kcat = jnp.concatenate([kp_ref[...], kc_ref[...]], axis=0)
    vcat = jnp.concatenate([vp_ref[...], vc_ref[...]], axis=0)
    ks = _pair_kv(kcat)
    vs = _pair_kv(vcat)
    qi = _pmod(lax.broadcasted_iota(jnp.int32, (2 * BLOCK, 2 * BLOCK), 0), BLOCK) + BLOCK
    kj = lax.broadcasted_iota(jnp.int32, (2 * BLOCK, 2 * BLOCK), 1)
    dist = qi - kj
    mask = (dist >= 0) & (dist <= WINDOW) & ((n > 0) | (kj >= BLOCK))
    top = lax.broadcasted_iota(jnp.int32, (2 * BLOCK, 1), 0) < BLOCK
    blocks = range(Q_WIDTH // LANES)
    qs, sink_cols = [], []
    for j in blocks:
        qb = q_ref[:, j * LANES:(j + 1) * LANES]
        qs.append(jnp.concatenate([jnp.where(even, qb, 0.0), jnp.where(even, 0.0, qb)], axis=0))
        sink_cols.append(jnp.where(top, sink(2 * j), sink(2 * j + 1)))
    kvh = [(2 * j) // GROUP for j in blocks]
    o = _attend(qs, [ks[h] for h in kvh], [vs[h] for h in kvh], mask, sink_cols)
    for j in blocks:
        o_ref[:, j * LANES:(j + 1) * LANES] = jnp.where(even, o[j][:BLOCK], o[j][BLOCK:]).astype(o_ref.dtype)


def _attn_prompt(q, k, v, sinks, layer, batch, seq):
    nb = seq // BLOCK
    cur = lambda b, n: (b * nb + n, 0)
    prv = lambda b, n: (b * nb + jnp.maximum(n - 1, 0), 0)
    kv = lambda f: pl.BlockSpec((BLOCK, KV_WIDTH), f)
    return pl.pallas_call(
        functools.partial(_attn_prompt_kernel, layer=layer),
        grid=(batch, nb),
        in_specs=[pl.BlockSpec(memory_space=pltpu.SMEM), pl.BlockSpec((BLOCK, Q_WIDTH), cur),
                  kv(prv), kv(cur), kv(prv), kv(cur)],
        out_specs=pl.BlockSpec((BLOCK, Q_WIDTH), cur),
        out_shape=jax.ShapeDtypeStruct((batch * seq, Q_WIDTH), BF16),
        compiler_params=_cparams("parallel", "parallel"),
        name="attn_prompt",
    )(sinks, q, k, k, v, v)


def _attn_sample_kernel(sink_ref, q_ref, kb_ref, kn_ref, vb_ref, vn_ref, o_ref, *, layer, n_seq, wbuf):
    lane = lax.broadcasted_iota(jnp.int32, (1, LANES), 1)
    even = lane < HEAD_DIM
    rows = GROUP * SAMPLE_PAD
    nk = wbuf + 2 * SAMPLE_PAD
    qi = _pmod(lax.broadcasted_iota(jnp.int32, (rows, nk), 0), SAMPLE_PAD)
    kj = lax.broadcasted_iota(jnp.int32, (rows, nk), 1)
    dist = qi + wbuf - kj
    mask = (dist >= 0) & (dist <= WINDOW)
    rid = _pdiv(lax.broadcasted_iota(jnp.int32, (rows, 1), 0), SAMPLE_PAD)
    sink_cols = []
    for h in range(KV_HEADS):
        col = jnp.zeros((rows, 1), F32)
        for gq in range(GROUP):
            col = jnp.where(rid == gq, sink_ref[layer * A_HEADS + h * GROUP + gq], col)
        sink_cols.append(col)
    unroll = 2
    pad = jnp.zeros((SAMPLE_PAD, KV_WIDTH), F32)
    blocks_per_kv = GROUP // 2

    def body(i, carry_):
        qs, kcs, vcs = [], [], []
        for u in range(unroll):
            g = i * unroll + u
            ks = _pair_kv(jnp.concatenate([kb_ref[g], kn_ref[g], pad], axis=0))
            vs = _pair_kv(jnp.concatenate([vb_ref[g], vn_ref[g], pad], axis=0))
            for h in range(KV_HEADS):
                parts = []
                for j in range(h * blocks_per_kv, (h + 1) * blocks_per_kv):
                    qb = q_ref[g, :, j * LANES:(j + 1) * LANES]
                    parts += [jnp.where(even, qb, 0.0), jnp.where(even, 0.0, qb)]
                qs.append(jnp.concatenate(parts, axis=0))
                kcs.append(ks[h])
                vcs.append(vs[h])
        o = _attend(qs, kcs, vcs, mask, sink_cols * unroll)
        for u in range(unroll):
            g = i * unroll + u
            for h in range(KV_HEADS):
                oh = o[u * KV_HEADS + h]
                for jj in range(blocks_per_kv):
                    j = h * blocks_per_kv + jj
                    r0 = 2 * jj * SAMPLE_PAD
                    o_ref[g, :, j * LANES:(j + 1) * LANES] = jnp.where(
                        even, oh[r0:r0 + SAMPLE_PAD], oh[r0 + SAMPLE_PAD:r0 + 2 * SAMPLE_PAD]).astype(o_ref.dtype)
        return carry_

    lax.fori_loop(0, n_seq // unroll, body, 0)


def _attn_sample(q, k, v, k_buf, v_buf, sinks, layer, n_seq):
    b, wbuf = k_buf.shape[0], k_buf.shape[1]
    idx = lambda i: (i, 0, 0)
    new = pl.BlockSpec((n_seq, SAMPLE_PAD, KV_WIDTH), idx)
    buf = pl.BlockSpec((n_seq, wbuf, KV_WIDTH), idx)
    return pl.pallas_call(
        functools.partial(_attn_sample_kernel, layer=layer, n_seq=n_seq, wbuf=wbuf),
        grid=(b // n_seq,),
        in_specs=[pl.BlockSpec(memory_space=pltpu.SMEM), pl.BlockSpec((n_seq, SAMPLE_PAD, Q_WIDTH), idx),
                  buf, new, buf, new],
        out_specs=pl.BlockSpec((n_seq, SAMPLE_PAD, Q_WIDTH), idx),
        out_shape=jax.ShapeDtypeStruct((b, SAMPLE_PAD, Q_WIDTH), BF16),
        compiler_params=_cparams("parallel"),
        name="attn_sample",
    )(sinks, q, k_buf, k, v_buf, v)


def _layer_norm(x, g, b):
    mu = jnp.mean(x, axis=-1, keepdims=True)
    d = x - mu
    var = jnp.mean(d * d, axis=-1, keepdims=True)
    return d * lax.rsqrt(var + LN_EPS) * g + b


def _mix_ffn_kernel(x_ref, yr_ref, ya_ref, win_ref, wbr_ref, wba_ref, wo_ref, g1_ref, b1_ref,
                    wu_ref, wd_ref, g2_ref, b2_ref, o_ref, *, layer, ff_chunk):
    par = lambda ref: ref[layer:layer + 1, :]
    x = x_ref[...]
    xb = x.astype(BF16)
    gate_r = jax.nn.sigmoid(jnp.dot(xb, win_ref[:, QKV_END:QKV_END + D_MODEL], preferred_element_type=F32))
    mix = gate_r * jnp.dot(yr_ref[...], wbr_ref[...], preferred_element_type=F32)
    gate_a = jax.nn.sigmoid(jnp.dot(xb, win_ref[:, QKV_END + D_MODEL:], preferred_element_type=F32))
    mix = mix + gate_a * jnp.dot(ya_ref[...], wba_ref[...], preferred_element_type=F32)
    x1 = _layer_norm(ALPHA * x + _dot(mix, wo_ref[...]), par(g1_ref), par(b1_ref))
    x1b = x1.astype(BF16)
    acc = ALPHA * x1
    for c in range(D_FF // ff_chunk):
        cs = slice(c * ff_chunk, (c + 1) * ff_chunk)
        h = jnp.maximum(jnp.dot(x1b, wu_ref[:, cs], preferred_element_type=F32), 0.0)
        acc = acc + _dot(h * h, wd_ref[cs, :])
    o_ref[...] = _layer_norm(acc, par(g2_ref), par(b2_ref))


def _mix_ffn(x, yr, ya, P, layer, tm):
    n = x.shape[0]
    row = lambda i: (i, 0)
    ws = [P['w_in'], P['w_br_rwkv'], P['w_br_attn'], P['w_out'], P['ln1_g'], P['ln1_b'],
          P['w_ff_up'], P['w_ff_down'], P['ln2_g'], P['ln2_b']]
    wspec = lambda a: _layer_spec(a, layer, single_buffer=True) if a.ndim == 3 else _const_spec(a.shape)
    return pl.pallas_call(
        functools.partial(_mix_ffn_kernel, layer=layer, ff_chunk=1024),
        grid=(n // tm,),
        in_specs=[pl.BlockSpec((tm, D_MODEL), row), pl.BlockSpec((tm, R_WIDTH), row),
                  pl.BlockSpec((tm, Q_WIDTH), row)] + [wspec(a) for a in ws],
        out_specs=pl.BlockSpec((tm, D_MODEL), row),
        out_shape=jax.ShapeDtypeStruct((n, D_MODEL), F32),
        compiler_params=_cparams("parallel"),
        name="mix_ffn",
    )(x, yr, ya, *ws)


def _prepare_params(w_in, mu_shift, decay_base, decay_up, iclr_base, iclr_up, gate_up, k_k, k_a, r_k,
                    lnx_g, lnx_b, sinks, w_br_rwkv, w_br_attn, w_out, ln1_g, ln1_b, w_ff_up, w_ff_down,
                    ln2_g, ln2_b):
    depth = w_in.shape[0]
    zeros = jnp.zeros((depth, DECAY_LORA, R_WIDTH), F32)
    lora_up = jnp.concatenate([jnp.concatenate([decay_up, zeros], axis=2),
                               jnp.concatenate([zeros, iclr_up], axis=2)], axis=1)
    bf = lambda a: a.astype(BF16)
    return dict(
        w_in=bf(w_in), mu_shift=mu_shift, decay_base=decay_base, lora_up=bf(lora_up), iclr_base=iclr_base,
        gate_up=bf(gate_up), k_k=k_k, k_a=k_a, r_k=r_k.reshape(depth, R_WIDTH), lnx_g=lnx_g, lnx_b=lnx_b,
        sinks=sinks.reshape(depth * A_HEADS), w_br_rwkv=bf(w_br_rwkv), w_br_attn=bf(w_br_attn),
        w_out=bf(w_out), ln1_g=ln1_g, ln1_b=ln1_b, w_ff_up=bf(w_ff_up), w_ff_down=bf(w_ff_down),
        ln2_g=ln2_g, ln2_b=ln2_b)


def _prompt_layer(x, P, layer, tables, batch, seq, wbuf):
    tm = 512
    zr, q, k, v = _inproj(x, P['w_in'], layer, tables, seq // tm, tm)
    shift0 = jnp.zeros((batch, 1, SHIFT_W), F32)
    s0 = jnp.zeros((batch, R_HEADS, R_HEAD_DIM, R_HEAD_DIM), F32)
    wkv_rows = tm // batch
    yr, s_new = _wkv(zr.reshape(batch, seq, SHIFT_W), shift0, s0, P, layer, n_seq=batch, seq_rows=wkv_rows,
                     seg=WKV_CHUNK, valid_rows=wkv_rows, carried_state=True, lanes=batch)
    ya = _attn_prompt(q, k, v, P['sinks'], layer, batch, seq)
    x = _mix_ffn(x, yr.reshape(batch * seq, R_WIDTH), ya, P, layer, tm)
    tail = lambda t, w: t.reshape(batch, seq, w)[:, seq - wbuf:].reshape(batch, wbuf, KV_HEADS, HEAD_DIM)
    return x, zr.reshape(batch, seq, SHIFT_W)[:, -1], s_new, tail(k, KV_WIDTH), tail(v, KV_WIDTH)


def _sample_layer(x, P, layer, tables, shift_prev, s_prev, k_buf, v_buf, batch, seq):
    n = batch * SAMPLE_PAD
    tm = min(n, 512)
    n_seq = 16
    wbuf = k_buf.shape[1]
    zr, q, k, v = _inproj(x, P['w_in'], layer, tables, 1, tm)
    seq3 = lambda t, w: t.reshape(batch, SAMPLE_PAD, w)
    yr, s_new = _wkv(seq3(zr, SHIFT_W), shift_prev[:, None, :], s_prev, P, layer, n_seq=n_seq,
                     seq_rows=SAMPLE_PAD, seg=SAMPLE_PAD, valid_rows=seq, carried_state=False, lanes=2)
    kb = k_buf.reshape(batch, wbuf, KV_WIDTH)
    vb = v_buf.reshape(batch, wbuf, KV_WIDTH)
    ya = _attn_sample(seq3(q, Q_WIDTH), seq3(k, KV_WIDTH), seq3(v, KV_WIDTH), kb, vb, P['sinks'], layer, n_seq)
    x = _mix_ffn(x, yr.reshape(n, R_WIDTH), ya.reshape(n, Q_WIDTH), P, layer, tm)
    window = lambda old, new: jnp.concatenate([old, seq3(new, KV_WIDTH)[:, :seq]], axis=1)[:, -wbuf:].reshape(
        batch, wbuf, KV_HEADS, HEAD_DIM)
    return x, seq3(zr, SHIFT_W)[:, seq - 1], s_new, window(kb, k), window(vb, v)


def kernel(x_prompt, x_sample, state_wkv, state_shift, cache_k_win, cache_v_win, w_in, mu_shift, decay_base, decay_up, iclr_base, iclr_up, gate_up, k_k, k_a, r_k, lnx_g, lnx_b, sinks, w_br_rwkv, w_br_attn, w_out, ln1_g, ln1_b, w_ff_up, w_ff_down, ln2_g, ln2_b):
    bp, tp, _ = x_prompt.shape
    bs, ts, _ = x_sample.shape
    wbuf = cache_k_win.shape[2]
    half = HEAD_DIM // 2
    inv_freq = ROPE_THETA ** (-jnp.arange(half, dtype=F32) / half)
    tab_p = _rope_tables(inv_freq, tp, 0, tp)
    tab_s = _rope_tables(inv_freq, min(bs * SAMPLE_PAD, 512), PAST_LEN, SAMPLE_PAD)

    hp = x_prompt.reshape(bp * tp, D_MODEL)
    hs = jnp.pad(x_sample, ((0, 0), (0, SAMPLE_PAD - ts), (0, 0))).reshape(bs * SAMPLE_PAD, D_MODEL)
    outs_p, outs_s = [], []
    P = _prepare_params(w_in, mu_shift, decay_base, decay_up, iclr_base, iclr_up, gate_up, k_k, k_a, r_k,
                        lnx_g, lnx_b, sinks, w_br_rwkv, w_br_attn, w_out, ln1_g, ln1_b, w_ff_up,
                        w_ff_down, ln2_g, ln2_b)
    for l in range(DEPTH):
        hp, *st = _prompt_layer(hp, P, l, tab_p, bp, tp, wbuf)
        outs_p.append(st)
        hs, *st = _sample_layer(hs, P, l, tab_s, state_shift[l], state_wkv[l], cache_k_win[l], cache_v_win[l],
                                bs, ts)
        outs_s.append(st)
    stack = lambda outs, i: jnp.stack([o[i] for o in outs])
    y_p = hp.reshape(bp, tp, D_MODEL)
    y_s = hs.reshape(bs, SAMPLE_PAD, D_MODEL)[:, :ts]
    return (y_p, y_s,
            stack(outs_p, 1), stack(outs_p, 0), stack(outs_p, 2), stack(outs_p, 3),
            stack(outs_s, 1), stack(outs_s, 0), stack(outs_s, 2), stack(outs_s, 3))
```

```python
import functools
import math

import jax
import jax.numpy as jnp
from jax import lax
from jax.experimental import pallas as pl
from jax.experimental.pallas import tpu as pltpu

F32 = jnp.float32
BF16 = jnp.bfloat16

D_MODEL = 1024
DEPTH = 2
PAST_LEN = 8192
R_HEADS = 8
R_HEAD_DIM = 64
R_WIDTH = R_HEADS * R_HEAD_DIM
DECAY_LORA = 64
ICLR_LORA = 64
GATE_LORA = 128
SHIFT_W = 3 * R_WIDTH + DECAY_LORA + ICLR_LORA + GATE_LORA
A_HEADS = 8
KV_HEADS = 2
HEAD_DIM = 64
Q_WIDTH = A_HEADS * HEAD_DIM
KV_WIDTH = KV_HEADS * HEAD_DIM
GROUP = A_HEADS // KV_HEADS
WINDOW = 128
BLOCK = 128
ROPE_THETA = 10000.0
ATTN_SCALE = HEAD_DIM ** -0.5
D_FF = 4 * D_MODEL
ALPHA = (2 * DEPTH) ** 0.25
LN_EPS = 1e-5
GN_EPS = 64e-5
QKV_END = SHIFT_W + Q_WIDTH + 2 * KV_WIDTH

LANES = 128
SUBLANES = 8
VMEM_LIMIT = 56 * 1024 * 1024

SAMPLE_PAD = SUBLANES
N_PAIRS = R_HEADS // 2
WKV_CHUNK = LANES // 2
PHASE1_CHUNKS = 2


def _cparams(*sem):
    return pltpu.CompilerParams(dimension_semantics=sem, vmem_limit_bytes=VMEM_LIMIT)


def _const_spec(shape):
    nd = len(shape)
    return pl.BlockSpec(shape, lambda *_: (0,) * nd)


def _layer_spec(a, layer, single_buffer=False):
    mode = pl.Buffered(1) if single_buffer else None
    return pl.BlockSpec((None,) + a.shape[1:], lambda *_: (layer, 0, 0), pipeline_mode=mode)


def _dot(a, b):
    return jnp.dot(a.astype(BF16), b.astype(BF16), preferred_element_type=F32)


def _dot_nt(a, b):
    return lax.dot_general(a.astype(BF16), b.astype(BF16), (((1,), (1,)), ((), ())),
                           preferred_element_type=F32)


def _pmod(x, n):
    assert n & (n - 1) == 0
    return x & (n - 1)


def _pdiv(x, n):
    assert n & (n - 1) == 0
    return x >> (n.bit_length() - 1)


def _split(x):
    hi = x.astype(BF16)
    lo = (x - hi.astype(F32)).astype(BF16)
    return hi, lo


def _dot2_nt(a, b):
    bh, bl = _split(b)
    d = functools.partial(lax.dot_general, dimension_numbers=(((1,), (1,)), ((), ())),
                          preferred_element_type=F32)
    ab = a.astype(BF16)
    return d(ab, bh) + d(ab, bl)


def _rope_table_kernel(invf_ref, cos_ref, sa_ref, sb_ref, *, rows, offset, period):
    i = pl.program_id(0)
    row = lax.broadcasted_iota(jnp.int32, (rows, LANES), 0) + i * rows
    pos = offset + _pmod(row, period)
    ang = pos.astype(F32) * invf_ref[...]
    lane = lax.broadcasted_iota(jnp.int32, (rows, LANES), 1)
    first = _pmod(lane, HEAD_DIM) < (HEAD_DIM // 2)
    c = jnp.cos(ang)
    s = jnp.sin(ang)
    cos_ref[...] = c
    sa_ref[...] = jnp.where(first, -s, 0.0)
    sb_ref[...] = jnp.where(first, 0.0, s)


def _rope_tables(inv_freq, n_rows, offset, period):
    rows = min(n_rows, 1024)
    invf = jnp.tile(inv_freq, LANES // (HEAD_DIM // 2)).reshape(1, LANES)
    out = jax.ShapeDtypeStruct((n_rows, LANES), F32)
    return pl.pallas_call(
        functools.partial(_rope_table_kernel, rows=rows, offset=offset, period=period),
        grid=(n_rows // rows,),
        in_specs=[_const_spec((1, LANES))],
        out_specs=[pl.BlockSpec((rows, LANES), lambda i: (i, 0))] * 3,
        out_shape=[out] * 3,
        compiler_params=_cparams("parallel"),
        name="rope_tables",
    )(invf)


def _inproj_kernel(x_ref, w_ref, cos_ref, sa_ref, sb_ref, zr_ref, q_ref, k_ref, v_ref):
    xb = x_ref[...].astype(BF16)
    zr_ref[...] = jnp.dot(xb, w_ref[:, :SHIFT_W], preferred_element_type=F32)
    cos, sa, sb = cos_ref[...], sa_ref[...], sb_ref[...]

    def rope(t):
        return (t * cos + pltpu.roll(t, LANES - HEAD_DIM // 2, 1) * sa
                + pltpu.roll(t, HEAD_DIM // 2, 1) * sb)

    zq = jnp.dot(xb, w_ref[:, SHIFT_W:SHIFT_W + Q_WIDTH], preferred_element_type=F32)
    for j in range(Q_WIDTH // LANES):
        q_ref[:, j * LANES:(j + 1) * LANES] = rope(zq[:, j * LANES:(j + 1) * LANES])
    zkv = jnp.dot(xb, w_ref[:, SHIFT_W + Q_WIDTH:QKV_END], preferred_element_type=F32)
    k_ref[...] = rope(zkv[:, :KV_WIDTH])
    v_ref[...] = zkv[:, KV_WIDTH:]


def _inproj(x, w_in, layer, tables, tab_blocks, tm):
    n = x.shape[0]
    row = lambda i: (i, 0)
    tab = pl.BlockSpec((tm, LANES), lambda i: (lax.rem(i, tab_blocks), 0))
    w_spec = pl.BlockSpec((None, D_MODEL, QKV_END), lambda i: (layer, 0, 0))
    return pl.pallas_call(
        _inproj_kernel,
        grid=(n // tm,),
        in_specs=[pl.BlockSpec((tm, D_MODEL), row), w_spec, tab, tab, tab],
        out_specs=[pl.BlockSpec((tm, SHIFT_W), row), pl.BlockSpec((tm, Q_WIDTH), row),
                   pl.BlockSpec((tm, KV_WIDTH), row), pl.BlockSpec((tm, KV_WIDTH), row)],
        out_shape=[jax.ShapeDtypeStruct((n, SHIFT_W), F32), jax.ShapeDtypeStruct((n, Q_WIDTH), F32),
                   jax.ShapeDtypeStruct((n, KV_WIDTH), F32), jax.ShapeDtypeStruct((n, KV_WIDTH), F32)],
        compiler_params=_cparams("parallel"),
        name="inproj",
    )(x, w_in, *tables)


def _wkv_kernel(zr_ref, shift_ref, sin_ref, mu_ref, db_ref, lora_ref, ib_ref, gu_ref, kk_ref, ka_ref,
                rk_ref, lg_ref, lb_ref,
                y_ref, sout_ref,
                sbd, carry, prev_s, r_s, k_s, v_s, kk_s, b_s, lw_s, cw_s, y_s, bon_s, g_s, xg_s, uy_s, bk_s,
                *, layer, n_seq, seq_rows, chunk, seg, valid_rows, carried_state, lanes):
    t_step = pl.program_id(1)
    n_steps = pl.num_programs(1)
    rows = n_seq * seq_rows
    n_chunks = rows // chunk
    c2 = 2 * chunk
    par = lambda ref: ref[layer:layer + 1, :]

    z = zr_ref[...].reshape(rows, SHIFT_W)
    prev_s[...] = pltpu.roll(z, 1, 0)

    @pl.when(t_step == 0)
    def _():
        for g in range(n_seq):
            carry[g:g + 1, :] = shift_ref[g]
    for g in range(n_seq):
        prev_s[g * seq_rows:g * seq_rows + 1, :] = carry[g:g + 1, :]
        carry[g:g + 1, :] = z[(g + 1) * seq_rows - 1:(g + 1) * seq_rows, :]
    prev = prev_s[...]
    zs = z + (prev - z) * par(mu_ref)

    o1, o2, o3 = R_WIDTH, 2 * R_WIDTH, 3 * R_WIDTH
    o4 = o3 + DECAY_LORA + ICLR_LORA
    r = zs[:, :o1]
    k = zs[:, o1:o2]
    v = zs[:, o2:o3]
    wa = zs[:, o3:o4]
    gd = zs[:, o4:]

    lane = lax.broadcasted_iota(jnp.int32, (1, LANES), 1)
    lane_even = lane < R_HEAD_DIM
    me = lane_even.astype(F32)
    mo = 1.0 - me
    bi = _pdiv(lax.broadcasted_iota(jnp.int32, (LANES, LANES), 0), R_HEAD_DIM)
    bj = _pdiv(lax.broadcasted_iota(jnp.int32, (LANES, LANES), 1), R_HEAD_DIM)
    m_bd = (bi == bj).astype(F32)
    bd_ones = m_bd.astype(BF16)

    def seg_sum(x):
        xb = x.astype(BF16)
        return jnp.concatenate([jnp.dot(xb[:, j * LANES:(j + 1) * LANES], bd_ones, preferred_element_type=F32)
                                for j in range(R_WIDTH // LANES)], axis=1)

    lora_in = jnp.where(lane_even, jnp.tanh(wa), wa)
    lora = _dot(lora_in, lora_ref[...])
    pre_w = par(db_ref) + lora[:, :R_WIDTH]
    logw = -math.exp(-0.5) * jax.nn.sigmoid(pre_w)
    a = jax.nn.sigmoid(par(ib_ref) + lora[:, R_WIDTH:])
    g_s[...] = _dot(jax.nn.sigmoid(gd), gu_ref[...])
    kk = k * par(kk_ref)
    kk = kk * lax.rsqrt(jnp.maximum(seg_sum(kk * kk), 1e-24))
    kp = k * (1.0 + (a - 1.0) * par(ka_ref))
    bon_s[...] = seg_sum(r * kp * par(rk_ref)) * v

    if valid_rows < seq_rows:
        rid = _pmod(lax.broadcasted_iota(jnp.int32, (rows, 1), 0), seq_rows)
        ok = (rid < valid_rows).astype(F32)
        logw, kk, kp, v = logw * ok, kk * ok, kp * ok, v * ok

    r_s[...] = r
    k_s[...] = kp
    v_s[...] = v
    kk_s[...] = kk
    b_s[...] = kk * a
    lw_s[...] = logw

    tid = _pmod(lax.broadcasted_iota(jnp.int32, (rows, 1), 0), seg)
    cw = logw
    sh = 1
    while sh < seg:
        cw = cw + jnp.where(tid >= sh, pltpu.roll(cw, sh, 0), 0.0)
        sh *= 2
    cw_s[...] = cw

    si = lax.broadcasted_iota(jnp.int32, (c2, c2), 0)
    sj = lax.broadcasted_iota(jnp.int32, (c2, c2), 1)
    same = _pdiv(si, seg) == _pdiv(sj, seg)
    m_strict = (same & (_pmod(sj, seg) < _pmod(si, seg))).astype(F32)
    m_incl = (same & (_pmod(sj, seg) <= _pmod(si, seg))).astype(F32)
    eye = (si == sj).astype(F32)
    n_fact = int(math.log2(seg))

    def load_state(g, u):
        for p in range(N_PAIRS):
            zero = jnp.zeros((R_HEAD_DIM, R_HEAD_DIM), F32)
            sbd[u * N_PAIRS + p, 0:R_HEAD_DIM, :] = jnp.concatenate([sin_ref[g, 2 * p], zero], axis=1)
            sbd[u * N_PAIRS + p, R_HEAD_DIM:LANES, :] = jnp.concatenate([zero, sin_ref[g, 2 * p + 1]], axis=1)

    def store_state(g, u):
        for p in range(N_PAIRS):
            sout_ref[g, 2 * p] = sbd[u * N_PAIRS + p, 0:R_HEAD_DIM, 0:R_HEAD_DIM]
            sout_ref[g, 2 * p + 1] = sbd[u * N_PAIRS + p, R_HEAD_DIM:LANES, R_HEAD_DIM:LANES]

    if carried_state:
        @pl.when(t_step == 0)
        def _():
            for u in range(lanes):
                load_state(u, u)

    pairs = range(N_PAIRS)
    col = lambda p: slice(p * LANES, (p + 1) * LANES)
    stack2 = lambda t: jnp.concatenate([t * me, t * mo], axis=0)

    def phase1(i, carry_):
        items = [(i * PHASE1_CHUNKS + u, p) for u in range(PHASE1_CHUNKS) for p in pairs]
        n = range(len(items))
        ats, rts, vs, bts, kts = [], [], [], [], []
        for c, p in items:
            rs = pl.ds(pl.multiple_of(c * chunk, chunk), chunk)
            cwc = cw_s[rs, col(p)]
            w_in = jnp.exp(cwc)
            w_ex = jnp.exp(cwc - lw_s[rs, col(p)])
            w_inv = jnp.exp(-cwc)
            ats.append(stack2(-kk_s[rs, col(p)] * w_ex))
            rts.append(stack2(r_s[rs, col(p)] * w_in))
            vs.append(stack2(v_s[rs, col(p)]))
            bts.append(b_s[rs, col(p)] * w_inv)
            kts.append(k_s[rs, col(p)] * w_inv)
        sc = [_dot_nt(jnp.concatenate([ats[j], rts[j]], axis=0),
                      jnp.concatenate([bts[j], bts[j], kts[j], kts[j]], axis=0)) for j in n]
        l_ab = [sc[j][:c2, :c2] * m_strict for j in n]
        a_ak = [sc[j][:c2, c2:] * m_strict for j in n]
        a_rb = [sc[j][c2:, :c2] * m_incl for j in n]
        a_rk = [sc[j][c2:, c2:] * m_incl for j in n]

        pw = l_ab
        inv = [eye + l_ab[j] for j in n]
        for _ in range(n_fact - 1):
            pw = [_dot(pw[j], pw[j]) for j in n]
            inv = [_dot(inv[j], eye + pw[j]) for j in n]

        ta = [_dot(inv[j], jnp.concatenate([ats[j], a_ak[j]], axis=1)) for j in n]
        ab = [_dot(a_rb[j], ta[j]) for j in n]
        uy = [_dot(jnp.concatenate([ta[j][:, LANES:], ab[j][:, LANES:] + a_rk[j]], axis=0), vs[j])
              for j in n]
        for j, (c, p) in enumerate(items):
            idx = c * N_PAIRS + p
            xg_s[idx] = jnp.concatenate([ta[j][:, :LANES], rts[j] + ab[j][:, :LANES]], axis=0)
            uy_s[idx] = uy[j]
            bk_s[idx] = jnp.concatenate([bts[j], kts[j]], axis=0)
        return carry_

    lax.fori_loop(0, n_chunks // PHASE1_CHUNKS, phase1, 0)

    segs_per_chunk = chunk // seg
    segs_per_seq = seq_rows // seg

    def phase2(i, carry_):
        segs = [u * segs_per_seq + i if carried_state else i * lanes + u for u in range(lanes)]
        if not carried_state:
            for u in range(lanes):
                load_state(segs[u], u)
        items = [(u, p) for u in range(lanes) for p in pairs]
        n = range(len(items))

        def quarter(ref, u, p, q):
            sg = segs[u]
            off = _pmod(sg, segs_per_chunk) * seg
            return ref[_pdiv(sg, segs_per_chunk) * N_PAIRS + p,
                       pl.ds(pl.multiple_of(q * chunk + off, seg), seg), :]

        tok = [pl.ds(pl.multiple_of(sg * seg, seg), seg) for sg in segs]
        tail = [pl.ds(pl.multiple_of(sg * seg + seg - SUBLANES, SUBLANES), SUBLANES) for sg in segs]
        s0 = [sbd[u * N_PAIRS + p] for u, p in items]
        xg = [jnp.concatenate([quarter(xg_s, u, p, q) for q in range(4)], axis=0) for u, p in items]
        uyv = [_dot2_nt(xg[j], s0[j]) for j in n]
        uyv = [uyv[j] + jnp.concatenate([quarter(uy_s, u, p, q) for q in range(4)], axis=0)
               for j, (u, p) in enumerate(items)]
        uvt = [jnp.concatenate([uyv[j][:2 * seg], stack2(v_s[tok[u], col(p)])], axis=0).T
               for j, (u, p) in enumerate(items)]
        rhs = []
        for u, p in items:
            bt, kt = quarter(bk_s, u, p, 0), quarter(bk_s, u, p, 1)
            rhs.append(jnp.concatenate([bt, bt, kt, kt], axis=0))
        upd = [_dot(uvt[j], rhs[j]) for j in n]
        for j, (u, p) in enumerate(items):
            w_end = jnp.exp(cw_s[tail[u], col(p)][SUBLANES - 1:SUBLANES, :])
            sbd[u * N_PAIRS + p] = (s0[j] + upd[j] * m_bd) * w_end
            y_s[tok[u], col(p)] = uyv[j][2 * seg:3 * seg] + uyv[j][3 * seg:]
        if not carried_state:
            for u in range(lanes):
                store_state(segs[u], u)
        return carry_

    lax.fori_loop(0, rows // seg // lanes, phase2, 0)

    if carried_state:
        @pl.when(t_step == n_steps - 1)
        def _():
            for u in range(lanes):
                store_state(u, u)

    y = y_s[...]
    inv_n = 1.0 / R_HEAD_DIM
    mean = seg_sum(y) * inv_n
    d = y - mean
    var = seg_sum(d * d) * inv_n
    yn = d * lax.rsqrt(var + GN_EPS)
    yn = yn * par(lg_ref) + par(lb_ref)
    y_ref[...] = ((yn + bon_s[...]) * g_s[...]).reshape(n_seq, seq_rows, R_WIDTH).astype(y_ref.dtype)


def _wkv(zr, shift_in, s_all, P, layer, *, n_seq, seq_rows, seg, valid_rows, carried_state, lanes):
    b, t_len, _ = zr.shape
    rows = n_seq * seq_rows
    chunk = WKV_CHUNK
    n_pc = (rows // chunk) * N_PAIRS
    assert (lanes == n_seq) if carried_state else (t_len == seq_rows and n_seq % lanes == 0)
    params = [P['mu_shift'], P['decay_base'], P['lora_up'], P['iclr_base'], P['gate_up'], P['k_k'], P['k_a'],
              P['r_k'], P['lnx_g'], P['lnx_b']]
    param_specs = [_layer_spec(a, layer) if a.ndim == 3 else _const_spec(a.shape) for a in params]
    tok = lambda g, t: (g, t, 0)
    seq3 = lambda g, t: (g, 0, 0)
    state_spec = pl.BlockSpec((None, n_seq, R_HEADS, R_HEAD_DIM, R_HEAD_DIM), lambda g, t: (layer, g, 0, 0, 0))
    tile = lambda w: pltpu.VMEM((rows, w), F32)
    return pl.pallas_call(
        functools.partial(_wkv_kernel, layer=layer, n_seq=n_seq, seq_rows=seq_rows, chunk=chunk, seg=seg,
                          valid_rows=valid_rows, carried_state=carried_state, lanes=lanes),
        grid=(b // n_seq, t_len // seq_rows),
        in_specs=[pl.BlockSpec((n_seq, seq_rows, SHIFT_W), tok), pl.BlockSpec((n_seq, 1, SHIFT_W), seq3),
                  state_spec] + param_specs,
        out_specs=[pl.BlockSpec((n_seq, seq_rows, R_WIDTH), tok), state_spec],
        out_shape=[jax.ShapeDtypeStruct((b, t_len, R_WIDTH), BF16), jax.ShapeDtypeStruct(s_all.shape, F32)],
        input_output_aliases={2: 1},
        scratch_shapes=[pltpu.VMEM((lanes * N_PAIRS, LANES, LANES), F32), pltpu.VMEM((n_seq, SHIFT_W), F32),
                        tile(SHIFT_W)] + [tile(R_WIDTH)] * 10 + [
                            pltpu.VMEM((n_pc, 4 * chunk, LANES), F32), pltpu.VMEM((n_pc, 4 * chunk, LANES), F32),
                            pltpu.VMEM((n_pc, 2 * chunk, LANES), F32)],
        compiler_params=_cparams("parallel", "arbitrary"),
        name="wkv7",
    )(zr, shift_in, s_all, *params)


def _pair_kv(t):
    lane = lax.broadcasted_iota(jnp.int32, (1, LANES), 1)
    even = lane < HEAD_DIM
    sw = pltpu.roll(t, HEAD_DIM, 1)
    return jnp.where(even, t, sw), jnp.where(even, sw, t)


def _attend(qs, kcs, vcs, mask, sink_cols):
    n = range(len(qs))
    s = [_dot_nt(qs[i], kcs[i]) * ATTN_SCALE for i in n]
    s = [jnp.where(mask, s[i], -jnp.inf) for i in n]
    m = [jnp.maximum(jnp.max(s[i], axis=-1, keepdims=True), sink_cols[i]) for i in n]
    p = [jnp.exp(s[i] - m[i]) for i in n]
    den = [jnp.sum(p[i], axis=-1, keepdims=True) + jnp.exp(sink_cols[i] - m[i]) for i in n]
    o = [_dot(p[i], vcs[i]) for i in n]
    return [o[i] / den[i] for i in n]


def _attn_prompt_kernel(sink_ref, q_ref, kp_ref, kc_ref, vp_ref, vc_ref, o_ref, *, layer):
    n = pl.program_id(1)
    sink = lambda head: sink_ref[layer * A_HEADS + head]
    lane = lax.broadcasted_iota(jnp.int32, (1, LANES), 1)
    even = lane < HEAD_DIM
    kcat = jnp.concatenate([kp_ref[...], kc_ref[...]], axis=0)
    vcat = jnp.concatenate([vp_ref[...], vc_ref[...]], axis=0)
    ks = _pair_kv(kcat)
    vs = _pair_kv(vcat)
    qi = _pmod(lax.broadcasted_iota(jnp.int32, (2 * BLOCK, 2 * BLOCK), 0), BLOCK) + BLOCK
    kj = lax.broadcasted_iota(jnp.int32, (2 * BLOCK, 2 * BLOCK), 1)
    dist = qi - kj
    mask = (dist >= 0) & (dist <= WINDOW) & ((n > 0) | (kj >= BLOCK))
    top = lax.broadcasted_iota(jnp.int32, (2 * BLOCK, 1), 0) < BLOCK
    blocks = range(Q_WIDTH // LANES)
    qs, sink_cols = [], []
    for j in blocks:
        qb = q_ref[:, j * LANES:(j + 1) * LANES]
        qs.append(jnp.concatenate([jnp.where(even, qb, 0.0), jnp.where(even, 0.0, qb)], axis=0))
        sink_cols.append(jnp.where(top, sink(2 * j), sink(2 * j + 1)))
    kvh = [(2 * j) // GROUP for j in blocks]
    o = _attend(qs, [ks[h] for h in kvh], [vs[h] for h in kvh], mask, sink_cols)
    for j in blocks:
        o_ref[:, j * LANES:(j + 1) * LANES] = jnp.where(even, o[j][:BLOCK], o[j][BLOCK:]).astype(o_ref.dtype)


def _attn_prompt(q, k, v, sinks, layer, batch, seq):
    nb = seq // BLOCK
    cur = lambda b, n: (b * nb + n, 0)
    prv = lambda b, n: (b * nb + jnp.maximum(n - 1, 0), 0)
    kv = lambda f: pl.BlockSpec((BLOCK, KV_WIDTH), f)
    return pl.pallas_call(
        functools.partial(_attn_prompt_kernel, layer=layer),
        grid=(batch, nb),
        in_specs=[pl.BlockSpec(memory_space=pltpu.SMEM), pl.BlockSpec((BLOCK, Q_WIDTH), cur),
                  kv(prv), kv(cur), kv(prv), kv(cur)],
        out_specs=pl.BlockSpec((BLOCK, Q_WIDTH), cur),
        out_shape=jax.ShapeDtypeStruct((batch * seq, Q_WIDTH), BF16),
        compiler_params=_cparams("parallel", "parallel"),
        name="attn_prompt",
    )(sinks, q, k, k, v, v)


def _attn_sample_kernel(sink_ref, q_ref, kt_ref, kn_ref, vt_ref, vn_ref, o_ref, kt_out_ref, vt_out_ref,
                        *, layer, n_seq, wbuf, n_new):
    assert wbuf == LANES
    lane = lax.broadcasted_iota(jnp.int32, (1, LANES), 1)
    even = lane < HEAD_DIM
    rows = GROUP * SAMPLE_PAD
    n_pad = 2 * SAMPLE_PAD
    qt = _pmod(lax.broadcasted_iota(jnp.int32, (rows, 1), 0), SAMPLE_PAD)
    dist_c = qt + wbuf - lax.broadcasted_iota(jnp.int32, (rows, wbuf), 1)
    dist_n = qt - lax.broadcasted_iota(jnp.int32, (rows, n_pad), 1)
    mask_c = (dist_c >= 0) & (dist_c <= WINDOW)
    mask_n = (dist_n >= 0) & (dist_n <= WINDOW)
    rid = _pdiv(lax.broadcasted_iota(jnp.int32, (rows, 1), 0), SAMPLE_PAD)
    sink_cols = []
    for h in range(KV_HEADS):
        col = jnp.zeros((rows, 1), F32)
        for gq in range(GROUP):
            col = jnp.where(rid == gq, sink_ref[layer * A_HEADS + h * GROUP + gq], col)
        sink_cols.append(col)
    unroll = 2
    pad = jnp.zeros((SAMPLE_PAD, KV_WIDTH), F32)
    blocks_per_kv = GROUP // 2
    keep = lane < wbuf - n_new
    sub = lax.broadcasted_iota(jnp.int32, (SAMPLE_PAD, 1), 0)

    def new_columns(t_new):
        low = jnp.where(sub >= SAMPLE_PAD - n_new, pltpu.roll(t_new, SAMPLE_PAD - n_new, 0), 0.0)
        return jnp.concatenate([jnp.zeros((wbuf - SAMPLE_PAD, KV_WIDTH), F32), low], axis=0).T

    def body(i, carry_):
        items = [(i * unroll + u, h) for u in range(unroll) for h in range(KV_HEADS)]
        n = range(len(items))
        qs, kc, vc, kn, vn = [], [], [], [], []
        for u in range(unroll):
            g = i * unroll + u
            kn2 = _pair_kv(jnp.concatenate([kn_ref[g], pad], axis=0))
            vn2 = _pair_kv(jnp.concatenate([vn_ref[g], pad], axis=0))
            for h in range(KV_HEADS):
                parts = []
                for j in range(h * blocks_per_kv, (h + 1) * blocks_per_kv):
                    qb = q_ref[g, :, j * LANES:(j + 1) * LANES] * ATTN_SCALE
                    parts += [jnp.where(even, qb, 0.0), jnp.where(even, 0.0, qb)]
                qs.append(jnp.concatenate(parts, axis=0))
                kt, vt = kt_ref[g, h], vt_ref[g, h]
                kc.append(jnp.concatenate([kt, kt], axis=0))
                vc.append(jnp.concatenate([vt, vt], axis=0))
                kn.append(kn2[h])
                vn.append(vn2[h])
        s_c = [jnp.where(mask_c, _dot(qs[j], kc[j]), -jnp.inf) for j in n]
        s_n = [jnp.where(mask_n, _dot_nt(qs[j], kn[j]), -jnp.inf) for j in n]
        m = [jnp.maximum(jnp.maximum(jnp.max(s_c[j], axis=-1, keepdims=True),
                                     jnp.max(s_n[j], axis=-1, keepdims=True)), sink_cols[items[j][1]])
             for j in n]
        p_c = [jnp.exp(s_c[j] - m[j]) for j in n]
        p_n = [jnp.exp(s_n[j] - m[j]) for j in n]
        den = [jnp.sum(p_c[j], axis=-1, keepdims=True) + jnp.sum(p_n[j], axis=-1, keepdims=True)
               + jnp.exp(sink_cols[items[j][1]] - m[j]) for j in n]
        o = [(_dot_nt(p_c[j], vc[j]) + _dot(p_n[j], vn[j])) / den[j] for j in n]
        for j, (g, h) in enumerate(items):
            for jj in range(blocks_per_kv):
                blk = h * blocks_per_kv + jj
                r0 = 2 * jj * SAMPLE_PAD
                o_ref[g, :, blk * LANES:(blk + 1) * LANES] = jnp.where(
                    even, o[j][r0:r0 + SAMPLE_PAD], o[j][r0 + SAMPLE_PAD:r0 + 2 * SAMPLE_PAD]).astype(o_ref.dtype)
        for u in range(unroll):
            g = i * unroll + u
            k_cols, v_cols = new_columns(kn_ref[g]), new_columns(vn_ref[g])
            for h in range(KV_HEADS):
                hs = slice(h * HEAD_DIM, (h + 1) * HEAD_DIM)
                kt_out_ref[g, h] = jnp.where(keep, pltpu.roll(kt_ref[g, h], wbuf - n_new, 1), k_cols[hs])
                vt_out_ref[g, h] = jnp.where(keep, pltpu.roll(vt_ref[g, h], wbuf - n_new, 1), v_cols[hs])
        return carry_

    lax.fori_loop(0, n_seq // unroll, body, 0)


def _attn_sample(q, k, v, kt_all, vt_all, sinks, layer, n_seq, n_new):
    _, b, _, _, wbuf = kt_all.shape
    idx = lambda i: (i, 0, 0)
    new = pl.BlockSpec((n_seq, SAMPLE_PAD, KV_WIDTH), idx)
    buf = pl.BlockSpec((None, n_seq, KV_HEADS, HEAD_DIM, wbuf), lambda i: (layer, i, 0, 0, 0))
    win = pl.BlockSpec((n_seq, KV_HEADS, HEAD_DIM, wbuf), lambda i: (i, 0, 0, 0))
    win_shape = jax.ShapeDtypeStruct((b, KV_HEADS, HEAD_DIM, wbuf), F32)
    return pl.pallas_call(
        functools.partial(_attn_sample_kernel, layer=layer, n_seq=n_seq, wbuf=wbuf, n_new=n_new),
        grid=(b // n_seq,),
        in_specs=[pl.BlockSpec(memory_space=pltpu.SMEM), pl.BlockSpec((n_seq, SAMPLE_PAD, Q_WIDTH), idx),
                  buf, new, buf, new],
        out_specs=[pl.BlockSpec((n_seq, SAMPLE_PAD, Q_WIDTH), idx), win, win],
        out_shape=[jax.ShapeDtypeStruct((b, SAMPLE_PAD, Q_WIDTH), BF16), win_shape, win_shape],
        compiler_params=_cparams("parallel"),
        name="attn_sample",
    )(sinks, q, kt_all, k, vt_all, v)


def _layer_norm(x, g, b):
    mu = jnp.mean(x, axis=-1, keepdims=True)
    d = x - mu
    var = jnp.mean(d * d, axis=-1, keepdims=True)
    return d * lax.rsqrt(var + LN_EPS) * g + b


def _mix_ffn_kernel(x_ref, yr_ref, ya_ref, win_ref, wbr_ref, wba_ref, wo_ref, g1_ref, b1_ref,
                    wu_ref, wd_ref, g2_ref, b2_ref, o_ref, *, layer, ff_chunk):
    par = lambda ref: ref[layer:layer + 1, :]
    x = x_ref[...]
    xb = x.astype(BF16)
    gate_r = jax.nn.sigmoid(jnp.dot(xb, win_ref[:, QKV_END:QKV_END + D_MODEL], preferred_element_type=F32))
    mix = gate_r * jnp.dot(yr_ref[...], wbr_ref[...], preferred_element_type=F32)
    gate_a = jax.nn.sigmoid(jnp.dot(xb, win_ref[:, QKV_END + D_MODEL:], preferred_element_type=F32))
    mix = mix + gate_a * jnp.dot(ya_ref[...], wba_ref[...], preferred_element_type=F32)
    x1 = _layer_norm(ALPHA * x + _dot(mix, wo_ref[...]), par(g1_ref), par(b1_ref))
    x1b = x1.astype(BF16)
    acc = ALPHA * x1
    for c in range(D_FF // ff_chunk):
        cs = slice(c * ff_chunk, (c + 1) * ff_chunk)
        h = jnp.maximum(jnp.dot(x1b, wu_ref[:, cs], preferred_element_type=F32), 0.0)
        acc = acc + _dot(h * h, wd_ref[cs, :])
    o_ref[...] = _layer_norm(acc, par(g2_ref), par(b2_ref))


def _mix_ffn(x, yr, ya, P, layer, tm):
    n = x.shape[0]
    row = lambda i: (i, 0)
    ws = [P['w_in'], P['w_br_rwkv'], P['w_br_attn'], P['w_out'], P['ln1_g'], P['ln1_b'],
          P['w_ff_up'], P['w_ff_down'], P['ln2_g'], P['ln2_b']]
    wspec = lambda a: _layer_spec(a, layer, single_buffer=True) if a.ndim == 3 else _const_spec(a.shape)
    return pl.pallas_call(
        functools.partial(_mix_ffn_kernel, layer=layer, ff_chunk=1024),
        grid=(n // tm,),
        in_specs=[pl.BlockSpec((tm, D_MODEL), row), pl.BlockSpec((tm, R_WIDTH), row),
                  pl.BlockSpec((tm, Q_WIDTH), row)] + [wspec(a) for a in ws],
        out_specs=pl.BlockSpec((tm, D_MODEL), row),
        out_shape=jax.ShapeDtypeStruct((n, D_MODEL), F32),
        compiler_params=_cparams("parallel"),
        name="mix_ffn",
    )(x, yr, ya, *ws)


def _prepare_params(w_in, mu_shift, decay_base, decay_up, iclr_base, iclr_up, gate_up, k_k, k_a, r_k,
                    lnx_g, lnx_b, sinks, w_br_rwkv, w_br_attn, w_out, ln1_g, ln1_b, w_ff_up, w_ff_down,
                    ln2_g, ln2_b):
    depth = w_in.shape[0]
    zeros = jnp.zeros((depth, DECAY_LORA, R_WIDTH), F32)
    lora_up = jnp.concatenate([jnp.concatenate([decay_up, zeros], axis=2),
                               jnp.concatenate([zeros, iclr_up], axis=2)], axis=1)
    bf = lambda a: a.astype(BF16)
    return dict(
        w_in=bf(w_in), mu_shift=mu_shift, decay_base=decay_base, lora_up=bf(lora_up), iclr_base=iclr_base,
        gate_up=bf(gate_up), k_k=k_k, k_a=k_a, r_k=r_k.reshape(depth, R_WIDTH), lnx_g=lnx_g, lnx_b=lnx_b,
        sinks=sinks.reshape(depth * A_HEADS), w_br_rwkv=bf(w_br_rwkv), w_br_attn=bf(w_br_attn),
        w_out=bf(w_out), ln1_g=ln1_g, ln1_b=ln1_b, w_ff_up=bf(w_ff_up), w_ff_down=bf(w_ff_down),
        ln2_g=ln2_g, ln2_b=ln2_b)


def _prompt_layer(x, P, layer, tables, s_all, batch, seq, wbuf):
    tm = 512
    zr, q, k, v = _inproj(x, P['w_in'], layer, tables, seq // tm, tm)
    shift0 = jnp.zeros((batch, 1, SHIFT_W), F32)
    wkv_rows = tm // batch
    yr, s_all = _wkv(zr.reshape(batch, seq, SHIFT_W), shift0, s_all, P, layer, n_seq=batch, seq_rows=wkv_rows,
                     seg=WKV_CHUNK, valid_rows=wkv_rows, carried_state=True, lanes=batch)
    ya = _attn_prompt(q, k, v, P['sinks'], layer, batch, seq)
    x = _mix_ffn(x, yr.reshape(batch * seq, R_WIDTH), ya, P, layer, tm)
    tail = lambda t, w: t.reshape(batch, seq, w)[:, seq - wbuf:].reshape(batch, wbuf, KV_HEADS, HEAD_DIM)
    return x, s_all, zr.reshape(batch, seq, SHIFT_W)[:, -1], tail(k, KV_WIDTH), tail(v, KV_WIDTH)


def _sample_layer(x, P, layer, tables, shift_prev, s_all, kt_all, vt_all, batch, seq):
    n = batch * SAMPLE_PAD
    tm = min(n, 512)
    n_seq = 16
    zr, q, k, v = _inproj(x, P['w_in'], layer, tables, 1, tm)
    seq3 = lambda t, w: t.reshape(batch, SAMPLE_PAD, w)
    yr, s_all = _wkv(seq3(zr, SHIFT_W), shift_prev[:, None, :], s_all, P, layer, n_seq=n_seq,
                     seq_rows=SAMPLE_PAD, seg=SAMPLE_PAD, valid_rows=seq, carried_state=False, lanes=2)
    ya, kt_new, vt_new = _attn_sample(seq3(q, Q_WIDTH), seq3(k, KV_WIDTH), seq3(v, KV_WIDTH), kt_all, vt_all,
                                      P['sinks'], layer, n_seq, seq)
    x = _mix_ffn(x, yr.reshape(n, R_WIDTH), ya.reshape(n, Q_WIDTH), P, layer, tm)
    return x, s_all, seq3(zr, SHIFT_W)[:, seq - 1], kt_new, vt_new


def kernel(x_prompt, x_sample, state_wkv, state_shift, cache_k_win, cache_v_win, w_in, mu_shift, decay_base, decay_up, iclr_base, iclr_up, gate_up, k_k, k_a, r_k, lnx_g, lnx_b, sinks, w_br_rwkv, w_br_attn, w_out, ln1_g, ln1_b, w_ff_up, w_ff_down, ln2_g, ln2_b):
    bp, tp, _ = x_prompt.shape
    bs, ts, _ = x_sample.shape
    wbuf = cache_k_win.shape[2]
    half = HEAD_DIM // 2
    inv_freq = ROPE_THETA ** (-jnp.arange(half, dtype=F32) / half)
    tab_p = _rope_tables(inv_freq, tp, 0, tp)
    tab_s = _rope_tables(inv_freq, min(bs * SAMPLE_PAD, 512), PAST_LEN, SAMPLE_PAD)

    hp = x_prompt.reshape(bp * tp, D_MODEL)
    hs = jnp.pad(x_sample, ((0, 0), (0, SAMPLE_PAD - ts), (0, 0))).reshape(bs * SAMPLE_PAD, D_MODEL)
    outs_p, outs_s = [], []
    P = _prepare_params(w_in, mu_shift, decay_base, decay_up, iclr_base, iclr_up, gate_up, k_k, k_a, r_k,
                        lnx_g, lnx_b, sinks, w_br_rwkv, w_br_attn, w_out, ln1_g, ln1_b, w_ff_up,
                        w_ff_down, ln2_g, ln2_b)
    to_lanes = lambda c: jnp.transpose(c, (0, 1, 3, 4, 2))
    from_lanes = lambda c: jnp.transpose(c, (0, 1, 4, 2, 3))
    kt_all, vt_all = to_lanes(cache_k_win), to_lanes(cache_v_win)
    s_all_p = jnp.zeros((DEPTH, bp, R_HEADS, R_HEAD_DIM, R_HEAD_DIM), F32)
    s_all_s = state_wkv
    for l in range(DEPTH):
        hp, s_all_p, *st = _prompt_layer(hp, P, l, tab_p, s_all_p, bp, tp, wbuf)
        outs_p.append(st)
        hs, s_all_s, *st = _sample_layer(hs, P, l, tab_s, state_shift[l], s_all_s, kt_all, vt_all, bs, ts)
        outs_s.append(st)
    stack = lambda outs, i: jnp.stack([o[i] for o in outs])
    y_p = hp.reshape(bp, tp, D_MODEL)
    y_s = hs.reshape(bs, SAMPLE_PAD, D_MODEL)[:, :ts]
    return (y_p, y_s,
            s_all_p, stack(outs_p, 0), stack(outs_p, 1), stack(outs_p, 2),
            s_all_s, stack(outs_s, 0), from_lanes(stack(outs_s, 1)), from_lanes(stack(outs_s, 2)))
```

```python
import functools
import math

import jax
import jax.numpy as jnp
from jax import lax
from jax.experimental import pallas as pl
from jax.experimental.pallas import tpu as pltpu

F32 = jnp.float32
BF16 = jnp.bfloat16

D_MODEL = 1024
DEPTH = 2
PAST_LEN = 8192
R_HEADS = 8
R_HEAD_DIM = 64
R_WIDTH = R_HEADS * R_HEAD_DIM
DECAY_LORA = 64
ICLR_LORA = 64
GATE_LORA = 128
SHIFT_W = 3 * R_WIDTH + DECAY_LORA + ICLR_LORA + GATE_LORA
A_HEADS = 8
KV_HEADS = 2
HEAD_DIM = 64
Q_WIDTH = A_HEADS * HEAD_DIM
KV_WIDTH = KV_HEADS * HEAD_DIM
GROUP = A_HEADS // KV_HEADS
WINDOW = 128
BLOCK = 128
ROPE_THETA = 10000.0
ATTN_SCALE = HEAD_DIM ** -0.5
D_FF = 4 * D_MODEL
ALPHA = (2 * DEPTH) ** 0.25
LN_EPS = 1e-5
GN_EPS = 64e-5
QKV_END = SHIFT_W + Q_WIDTH + 2 * KV_WIDTH

LANES = 128
SUBLANES = 8
VMEM_LIMIT = 56 * 1024 * 1024

SAMPLE_PAD = SUBLANES
N_PAIRS = R_HEADS // 2
WKV_CHUNK = LANES // 2
PHASE1_CHUNKS = 2


def _cparams(*sem):
    return pltpu.CompilerParams(dimension_semantics=sem, vmem_limit_bytes=VMEM_LIMIT)


def _const_spec(shape):
    nd = len(shape)
    return pl.BlockSpec(shape, lambda *_: (0,) * nd)


def _layer_spec(a, layer, single_buffer=False):
    mode = pl.Buffered(1) if single_buffer else None
    return pl.BlockSpec((None,) + a.shape[1:], lambda *_: (layer, 0, 0), pipeline_mode=mode)


def _dot(a, b):
    return jnp.dot(a.astype(BF16), b.astype(BF16), preferred_element_type=F32)


def _dot_nt(a, b):
    return lax.dot_general(a.astype(BF16), b.astype(BF16), (((1,), (1,)), ((), ())),
                           preferred_element_type=F32)


def _pmod(x, n):
    assert n & (n - 1) == 0
    return x & (n - 1)


def _pdiv(x, n):
    assert n & (n - 1) == 0
    return x >> (n.bit_length() - 1)


def _split(x):
    hi = x.astype(BF16)
    lo = (x - hi.astype(F32)).astype(BF16)
    return hi, lo


def _dot2_nt(a, b):
    bh, bl = _split(b)
    d = functools.partial(lax.dot_general, dimension_numbers=(((1,), (1,)), ((), ())),
                          preferred_element_type=F32)
    ab = a.astype(BF16)
    return d(ab, bh) + d(ab, bl)


def _rope_table_kernel(invf_ref, cos_ref, sa_ref, sb_ref, *, rows, offset, period):
    i = pl.program_id(0)
    row = lax.broadcasted_iota(jnp.int32, (rows, LANES), 0) + i * rows
    pos = offset + _pmod(row, period)
    ang = pos.astype(F32) * invf_ref[...]
    lane = lax.broadcasted_iota(jnp.int32, (rows, LANES), 1)
    first = _pmod(lane, HEAD_DIM) < (HEAD_DIM // 2)
    c = jnp.cos(ang)
    s = jnp.sin(ang)
    cos_ref[...] = c
    sa_ref[...] = jnp.where(first, -s, 0.0)
    sb_ref[...] = jnp.where(first, 0.0, s)


def _rope_tables(inv_freq, n_rows, offset, period):
    rows = min(n_rows, 1024)
    invf = jnp.tile(inv_freq, LANES // (HEAD_DIM // 2)).reshape(1, LANES)
    out = jax.ShapeDtypeStruct((n_rows, LANES), F32)
    return pl.pallas_call(
        functools.partial(_rope_table_kernel, rows=rows, offset=offset, period=period),
        grid=(n_rows // rows,),
        in_specs=[_const_spec((1, LANES))],
        out_specs=[pl.BlockSpec((rows, LANES), lambda i: (i, 0))] * 3,
        out_shape=[out] * 3,
        compiler_params=_cparams("parallel"),
        name="rope_tables",
    )(invf)


def _inproj_kernel(x_ref, w_ref, cos_ref, sa_ref, sb_ref, zr_ref, q_ref, k_ref, v_ref):
    xb = x_ref[...].astype(BF16)
    zr_ref[...] = jnp.dot(xb, w_ref[:, :SHIFT_W], preferred_element_type=F32)
    cos, sa, sb = cos_ref[...], sa_ref[...], sb_ref[...]

    def rope(t):
        return (t * cos + pltpu.roll(t, LANES - HEAD_DIM // 2, 1) * sa
                + pltpu.roll(t, HEAD_DIM // 2, 1) * sb)

    zq = jnp.dot(xb, w_ref[:, SHIFT_W:SHIFT_W + Q_WIDTH], preferred_element_type=F32)
    for j in range(Q_WIDTH // LANES):
        q_ref[:, j * LANES:(j + 1) * LANES] = rope(zq[:, j * LANES:(j + 1) * LANES])
    zkv = jnp.dot(xb, w_ref[:, SHIFT_W + Q_WIDTH:QKV_END], preferred_element_type=F32)
    k_ref[...] = rope(zkv[:, :KV_WIDTH])
    v_ref[...] = zkv[:, KV_WIDTH:]


def _inproj(x, w_in, layer, tables, tab_blocks, tm):
    n = x.shape[0]
    row = lambda i: (i, 0)
    tab = pl.BlockSpec((tm, LANES), lambda i: (lax.rem(i, tab_blocks), 0))
    w_spec = pl.BlockSpec((None, D_MODEL, QKV_END), lambda i: (layer, 0, 0))
    return pl.pallas_call(
        _inproj_kernel,
        grid=(n // tm,),
        in_specs=[pl.BlockSpec((tm, D_MODEL), row), w_spec, tab, tab, tab],
        out_specs=[pl.BlockSpec((tm, SHIFT_W), row), pl.BlockSpec((tm, Q_WIDTH), row),
                   pl.BlockSpec((tm, KV_WIDTH), row), pl.BlockSpec((tm, KV_WIDTH), row)],
        out_shape=[jax.ShapeDtypeStruct((n, SHIFT_W), F32), jax.ShapeDtypeStruct((n, Q_WIDTH), F32),
                   jax.ShapeDtypeStruct((n, KV_WIDTH), F32), jax.ShapeDtypeStruct((n, KV_WIDTH), F32)],
        compiler_params=_cparams("parallel"),
        name="inproj",
    )(x, w_in, *tables)


def _pair_block_diag():
    bi = _pdiv(lax.broadcasted_iota(jnp.int32, (LANES, LANES), 0), R_HEAD_DIM)
    bj = _pdiv(lax.broadcasted_iota(jnp.int32, (LANES, LANES), 1), R_HEAD_DIM)
    return (bi == bj).astype(F32)


def _head_sum(x):
    ones = _pair_block_diag().astype(BF16)
    xb = x.astype(BF16)
    return jnp.concatenate([jnp.dot(xb[:, j * LANES:(j + 1) * LANES], ones, preferred_element_type=F32)
                            for j in range(x.shape[1] // LANES)], axis=1)


def _token_shift(z, shift_ref, carry, prev_s, mu, t_step, n_seq, seq_rows):
    prev_s[...] = pltpu.roll(z, 1, 0)

    @pl.when(t_step == 0)
    def _():
        for g in range(n_seq):
            carry[g:g + 1, :] = shift_ref[g]
    for g in range(n_seq):
        prev_s[g * seq_rows:g * seq_rows + 1, :] = carry[g:g + 1, :]
        carry[g:g + 1, :] = z[(g + 1) * seq_rows - 1:(g + 1) * seq_rows, :]
    return z + (prev_s[...] - z) * mu


def _token_features(zs, decay_base, lora_up, iclr_base, gate_up, k_k, k_a, r_k):
    o1, o2, o3 = R_WIDTH, 2 * R_WIDTH, 3 * R_WIDTH
    o4 = o3 + DECAY_LORA + ICLR_LORA
    r = zs[:, :o1]
    k = zs[:, o1:o2]
    v = zs[:, o2:o3]
    wa = zs[:, o3:o4]
    gd = zs[:, o4:]
    lane = lax.broadcasted_iota(jnp.int32, (1, LANES), 1)
    lora_in = jnp.where(lane < DECAY_LORA, jnp.tanh(wa), wa)
    lora = _dot(lora_in, lora_up)
    pre_w = decay_base + lora[:, :R_WIDTH]
    logw = -math.exp(-0.5) * jax.nn.sigmoid(pre_w)
    a = jax.nn.sigmoid(iclr_base + lora[:, R_WIDTH:])
    g = _dot(jax.nn.sigmoid(gd), gate_up)
    kk = k * k_k
    kk = kk * lax.rsqrt(jnp.maximum(_head_sum(kk * kk), 1e-24))
    kp = k * (1.0 + (a - 1.0) * k_a)
    bonus = _head_sum(r * kp * r_k) * v
    return r, kp, v, kk, a, logw, g, bonus


def _group_norm_out(y, bonus, g, lnx_g, lnx_b):
    inv_n = 1.0 / R_HEAD_DIM
    d = y - _head_sum(y) * inv_n
    var = _head_sum(d * d) * inv_n
    return (d * lax.rsqrt(var + GN_EPS) * lnx_g + lnx_b + bonus) * g


def _wkv_kernel(zr_ref, shift_ref, sin_ref, mu_ref, db_ref, lora_ref, ib_ref, gu_ref, kk_ref, ka_ref,
                rk_ref, lg_ref, lb_ref,
                y_ref, sout_ref,
                sbd, carry, prev_s, r_s, k_s, v_s, kk_s, b_s, lw_s, cw_s, y_s, bon_s, g_s, xg_s, uy_s, bk_s,
                *, layer, n_seq, seq_rows, chunk, seg, valid_rows, carried_state, lanes):
    t_step = pl.program_id(1)
    n_steps = pl.num_programs(1)
    rows = n_seq * seq_rows
    n_chunks = rows // chunk
    c2 = 2 * chunk
    par = lambda ref: ref[layer:layer + 1, :]

    z = zr_ref[...].reshape(rows, SHIFT_W)
    zs = _token_shift(z, shift_ref, carry, prev_s, par(mu_ref), t_step, n_seq, seq_rows)
    r, kp, v, kk, a, logw, g, bonus = _token_features(
        zs, par(db_ref), lora_ref[...], par(ib_ref), gu_ref[...], par(kk_ref), par(ka_ref), par(rk_ref))
    g_s[...] = g
    bon_s[...] = bonus

    lane = lax.broadcasted_iota(jnp.int32, (1, LANES), 1)
    me = (lane < R_HEAD_DIM).astype(F32)
    mo = 1.0 - me
    m_bd = _pair_block_diag()

    if valid_rows < seq_rows:
        rid = _pmod(lax.broadcasted_iota(jnp.int32, (rows, 1), 0), seq_rows)
        ok = (rid < valid_rows).astype(F32)
        logw, kk, kp, v = logw * ok, kk * ok, kp * ok, v * ok

    r_s[...] = r
    k_s[...] = kp
    v_s[...] = v
    kk_s[...] = kk
    b_s[...] = kk * a
    lw_s[...] = logw

    tid = _pmod(lax.broadcasted_iota(jnp.int32, (rows, 1), 0), seg)
    cw = logw
    sh = 1
    while sh < seg:
        cw = cw + jnp.where(tid >= sh, pltpu.roll(cw, sh, 0), 0.0)
        sh *= 2
    cw_s[...] = cw

    si = lax.broadcasted_iota(jnp.int32, (c2, c2), 0)
    sj = lax.broadcasted_iota(jnp.int32, (c2, c2), 1)
    same = _pdiv(si, seg) == _pdiv(sj, seg)
    m_strict = (same & (_pmod(sj, seg) < _pmod(si, seg))).astype(F32)
    m_incl = (same & (_pmod(sj, seg) <= _pmod(si, seg))).astype(F32)
    eye = (si == sj).astype(F32)
    n_fact = int(math.log2(seg))

    def load_state(g, u):
        for p in range(N_PAIRS):
            zero = jnp.zeros((R_HEAD_DIM, R_HEAD_DIM), F32)
            sbd[u * N_PAIRS + p, 0:R_HEAD_DIM, :] = jnp.concatenate([sin_ref[g, 2 * p], zero], axis=1)
            sbd[u * N_PAIRS + p, R_HEAD_DIM:LANES, :] = jnp.concatenate([zero, sin_ref[g, 2 * p + 1]], axis=1)

    def store_state(g, u):
        for p in range(N_PAIRS):
            sout_ref[g, 2 * p] = sbd[u * N_PAIRS + p, 0:R_HEAD_DIM, 0:R_HEAD_DIM]
            sout_ref[g, 2 * p + 1] = sbd[u * N_PAIRS + p, R_HEAD_DIM:LANES, R_HEAD_DIM:LANES]

    if carried_state:
        @pl.when(t_step == 0)
        def _():
            for u in range(lanes):
                load_state(u, u)

    pairs = range(N_PAIRS)
    col = lambda p: slice(p * LANES, (p + 1) * LANES)
    stack2 = lambda t: jnp.concatenate([t * me, t * mo], axis=0)

    def phase1(i, carry_):
        items = [(i * PHASE1_CHUNKS + u, p) for u in range(PHASE1_CHUNKS) for p in pairs]
        n = range(len(items))
        ats, rts, vs, bts, kts = [], [], [], [], []
        for c, p in items:
            rs = pl.ds(pl.multiple_of(c * chunk, chunk), chunk)
            cwc = cw_s[rs, col(p)]
            w_in = jnp.exp(cwc)
            w_ex = jnp.exp(cwc - lw_s[rs, col(p)])
            w_inv = jnp.exp(-cwc)
            ats.append(stack2(-kk_s[rs, col(p)] * w_ex))
            rts.append(stack2(r_s[rs, col(p)] * w_in))
            vs.append(stack2(v_s[rs, col(p)]))
            bts.append(b_s[rs, col(p)] * w_inv)
            kts.append(k_s[rs, col(p)] * w_inv)
        sc = [_dot_nt(jnp.concatenate([ats[j], rts[j]], axis=0),
                      jnp.concatenate([bts[j], bts[j], kts[j], kts[j]], axis=0)) for j in n]
        l_ab = [sc[j][:c2, :c2] * m_strict for j in n]
        a_ak = [sc[j][:c2, c2:] * m_strict for j in n]
        a_rb = [sc[j][c2:, :c2] * m_incl for j in n]
        a_rk = [sc[j][c2:, c2:] * m_incl for j in n]

        pw = l_ab
        inv = [eye + l_ab[j] for j in n]
        for _ in range(n_fact - 1):
            pw = [_dot(pw[j], pw[j]) for j in n]
            inv = [_dot(inv[j], eye + pw[j]) for j in n]

        ta = [_dot(inv[j], jnp.concatenate([ats[j], a_ak[j]], axis=1)) for j in n]
        ab = [_dot(a_rb[j], ta[j]) for j in n]
        uy = [_dot(jnp.concatenate([ta[j][:, LANES:], ab[j][:, LANES:] + a_rk[j]], axis=0), vs[j])
              for j in n]
        for j, (c, p) in enumerate(items):
            idx = c * N_PAIRS + p
            xg_s[idx] = jnp.concatenate([ta[j][:, :LANES], rts[j] + ab[j][:, :LANES]], axis=0)
            uy_s[idx] = uy[j]
            bk_s[idx] = jnp.concatenate([bts[j], kts[j]], axis=0)
        return carry_

    lax.fori_loop(0, n_chunks // PHASE1_CHUNKS, phase1, 0)

    segs_per_chunk = chunk // seg
    segs_per_seq = seq_rows // seg

    def phase2(i, carry_):
        segs = [u * segs_per_seq + i if carried_state else i * lanes + u for u in range(lanes)]
        if not carried_state:
            for u in range(lanes):
                load_state(segs[u], u)
        items = [(u, p) for u in range(lanes) for p in pairs]
        n = range(len(items))

        def quarter(ref, u, p, q):
            sg = segs[u]
            off = _pmod(sg, segs_per_chunk) * seg
            return ref[_pdiv(sg, segs_per_chunk) * N_PAIRS + p,
                       pl.ds(pl.multiple_of(q * chunk + off, seg), seg), :]

        tok = [pl.ds(pl.multiple_of(sg * seg, seg), seg) for sg in segs]
        tail = [pl.ds(pl.multiple_of(sg * seg + seg - SUBLANES, SUBLANES), SUBLANES) for sg in segs]
        s0 = [sbd[u * N_PAIRS + p] for u, p in items]
        xg = [jnp.concatenate([quarter(xg_s, u, p, q) for q in range(4)], axis=0) for u, p in items]
        uyv = [_dot2_nt(xg[j], s0[j]) for j in n]
        uyv = [uyv[j] + jnp.concatenate([quarter(uy_s, u, p, q) for q in range(4)], axis=0)
               for j, (u, p) in enumerate(items)]
        uvt = [jnp.concatenate([uyv[j][:2 * seg], stack2(v_s[tok[u], col(p)])], axis=0).T
               for j, (u, p) in enumerate(items)]
        rhs = []
        for u, p in items:
            bt, kt = quarter(bk_s, u, p, 0), quarter(bk_s, u, p, 1)
            rhs.append(jnp.concatenate([bt, bt, kt, kt], axis=0))
        upd = [_dot(uvt[j], rhs[j]) for j in n]
        for j, (u, p) in enumerate(items):
            w_end = jnp.exp(cw_s[tail[u], col(p)][SUBLANES - 1:SUBLANES, :])
            sbd[u * N_PAIRS + p] = (s0[j] + upd[j] * m_bd) * w_end
            y_s[tok[u], col(p)] = uyv[j][2 * seg:3 * seg] + uyv[j][3 * seg:]
        if not carried_state:
            for u in range(lanes):
                store_state(segs[u], u)
        return carry_

    lax.fori_loop(0, rows // seg // lanes, phase2, 0)

    if carried_state:
        @pl.when(t_step == n_steps - 1)
        def _():
            for u in range(lanes):
                store_state(u, u)

    out = _group_norm_out(y_s[...], bon_s[...], g_s[...], par(lg_ref), par(lb_ref))
    y_ref[...] = out.reshape(n_seq, seq_rows, R_WIDTH).astype(y_ref.dtype)


def _wkv(zr, shift_in, s_all, P, layer, *, n_seq, seq_rows, seg, valid_rows, carried_state, lanes):
    b, t_len, _ = zr.shape
    rows = n_seq * seq_rows
    chunk = WKV_CHUNK
    n_pc = (rows // chunk) * N_PAIRS
    assert (lanes == n_seq) if carried_state else (t_len == seq_rows and n_seq % lanes == 0)
    params = [P['mu_shift'], P['decay_base'], P['lora_up'], P['iclr_base'], P['gate_up'], P['k_k'], P['k_a'],
              P['r_k'], P['lnx_g'], P['lnx_b']]
    param_specs = [_layer_spec(a, layer) if a.ndim == 3 else _const_spec(a.shape) for a in params]
    tok = lambda g, t: (g, t, 0)
    seq3 = lambda g, t: (g, 0, 0)
    state_spec = pl.BlockSpec((None, n_seq, R_HEADS, R_HEAD_DIM, R_HEAD_DIM), lambda g, t: (layer, g, 0, 0, 0))
    tile = lambda w: pltpu.VMEM((rows, w), F32)
    return pl.pallas_call(
        functools.partial(_wkv_kernel, layer=layer, n_seq=n_seq, seq_rows=seq_rows, chunk=chunk, seg=seg,
                          valid_rows=valid_rows, carried_state=carried_state, lanes=lanes),
        grid=(b // n_seq, t_len // seq_rows),
        in_specs=[pl.BlockSpec((n_seq, seq_rows, SHIFT_W), tok), pl.BlockSpec((n_seq, 1, SHIFT_W), seq3),
                  state_spec] + param_specs,
        out_specs=[pl.BlockSpec((n_seq, seq_rows, R_WIDTH), tok), state_spec],
        out_shape=[jax.ShapeDtypeStruct((b, t_len, R_WIDTH), BF16), jax.ShapeDtypeStruct(s_all.shape, F32)],
        input_output_aliases={2: 1},
        scratch_shapes=[pltpu.VMEM((lanes * N_PAIRS, LANES, LANES), F32), pltpu.VMEM((n_seq, SHIFT_W), F32),
                        tile(SHIFT_W)] + [tile(R_WIDTH)] * 10 + [
                            pltpu.VMEM((n_pc, 4 * chunk, LANES), F32), pltpu.VMEM((n_pc, 4 * chunk, LANES), F32),
                            pltpu.VMEM((n_pc, 2 * chunk, LANES), F32)],
        compiler_params=_cparams("parallel", "arbitrary"),
        name="wkv7",
    )(zr, shift_in, s_all, *params)


def _wkv_prep_kernel(zr_ref, shift_ref, mu_ref, db_ref, lora_ref, ib_ref, gu_ref, kk_ref, ka_ref, rk_ref,
                     r_ref, k_ref, v_ref, kko_ref, b_ref, w_ref, g_ref, bon_ref, carry, prev_s,
                     *, layer, n_seq, seq_rows):
    par = lambda ref: ref[layer:layer + 1, :]
    z = zr_ref[...].reshape(n_seq * seq_rows, SHIFT_W)
    zs = _token_shift(z, shift_ref, carry, prev_s, par(mu_ref), pl.program_id(1), n_seq, seq_rows)
    r, kp, v, kk, a, logw, g, bonus = _token_features(
        zs, par(db_ref), lora_ref[...], par(ib_ref), gu_ref[...], par(kk_ref), par(ka_ref), par(rk_ref))
    r_ref[...] = r
    k_ref[...] = kp
    v_ref[...] = v
    kko_ref[...] = kk
    b_ref[...] = kk * a
    w_ref[...] = jnp.exp(logw)
    g_ref[...] = g
    bon_ref[...] = bonus


LANE_GROUP = 4


def _wkv_lanes_kernel(r_ref, k_ref, v_ref, kk_ref, b_ref, w_ref, g_ref, bon_ref, lg_ref, lb_ref, s_ref,
                      y_ref, sout_ref, ft_s, yt_s, ytok_s, *, layer, n_new, whole_state):
    nb = ft_s.shape[-1]
    n = R_HEAD_DIM
    feats = (r_ref, k_ref, v_ref, kk_ref, b_ref, w_ref)
    for fi, ref in enumerate(feats):
        for t in range(n_new):
            ft_s[fi, t] = ref[pl.ds(t, nb, stride=SAMPLE_PAD), :].T
    load_s = (lambda hh, v: s_ref[layer, hh, v]) if whole_state else (lambda hh, v: s_ref[hh, v])

    def store_s(hh, v, val):
        if whole_state:
            sout_ref[layer, hh, v] = val
        else:
            sout_ref[hh, v] = val

    for hh in range(2):
        hs = slice(hh * n, (hh + 1) * n)

        def rows8(vg, carry_):
            v0 = pl.multiple_of(vg * SUBLANES, SUBLANES)
            vrows = [ft_s[2, t, pl.ds(hh * n + v0, SUBLANES), :] for t in range(n_new)]
            ys = [[None] * SUBLANES for _ in range(n_new)]
            for j0 in range(0, SUBLANES, LANE_GROUP):
                js = range(j0, j0 + LANE_GROUP)
                sv = {j: load_s(hh, v0 + j) for j in js}
                for t in range(n_new):
                    nk, w, bb = -ft_s[3, t, hs, :], ft_s[5, t, hs, :], ft_s[4, t, hs, :]
                    kt, rt = ft_s[1, t, hs, :], ft_s[0, t, hs, :]
                    sa = {j: jnp.sum(sv[j] * nk, axis=0, keepdims=True) for j in js}
                    sv = {j: sv[j] * w + sa[j] * bb + vrows[t][j:j + 1, :] * kt for j in js}
                    for j in js:
                        ys[t][j] = jnp.sum(sv[j] * rt, axis=0, keepdims=True)
                for j in js:
                    store_s(hh, v0 + j, sv[j])
            for t in range(n_new):
                yt_s[t, pl.ds(hh * n + v0, SUBLANES), :] = jnp.concatenate(ys[t], axis=0)
            return carry_

        lax.fori_loop(0, n // SUBLANES, rows8, 0)

    if whole_state:
        for l in range(s_ref.shape[0]):
            if l != layer:
                sout_ref[l] = s_ref[l]

    ytok_s[...] = jnp.zeros(ytok_s.shape, F32)
    for t in range(n_new):
        ytok_s[pl.ds(t, nb, stride=SAMPLE_PAD), :] = yt_s[t].T
    out = _group_norm_out(ytok_s[...], bon_ref[...], g_ref[...], lg_ref[layer:layer + 1, :],
                          lb_ref[layer:layer + 1, :])
    y_ref[...] = out.astype(y_ref.dtype)


def _wkv_short(zr, shift_in, s_lanes, P, layer, n_new):
    b, seq_rows, _ = zr.shape
    depth = s_lanes.shape[0]
    n_tok = b * seq_rows
    n_seq = 32
    params = [P['mu_shift'], P['decay_base'], P['lora_up'], P['iclr_base'], P['gate_up'], P['k_k'], P['k_a'],
              P['r_k']]
    param_specs = [_layer_spec(a, layer) if a.ndim == 3 else _const_spec(a.shape) for a in params]
    rows = n_seq * seq_rows
    tok_out = pl.BlockSpec((rows, R_WIDTH), lambda g, t: (g, 0))
    feats = pl.pallas_call(
        functools.partial(_wkv_prep_kernel, layer=layer, n_seq=n_seq, seq_rows=seq_rows),
        grid=(b // n_seq, 1),
        in_specs=[pl.BlockSpec((n_seq, seq_rows, SHIFT_W), lambda g, t: (g, 0, 0)),
                  pl.BlockSpec((n_seq, 1, SHIFT_W), lambda g, t: (g, 0, 0))] + param_specs,
        out_specs=[tok_out] * 8,
        out_shape=[jax.ShapeDtypeStruct((n_tok, R_WIDTH), F32)] * 8,
        scratch_shapes=[pltpu.VMEM((n_seq, SHIFT_W), F32), pltpu.VMEM((rows, SHIFT_W), F32)],
        compiler_params=_cparams("parallel", "arbitrary"),
        name="wkv7_prep",
    )(zr, shift_in, *params)

    whole = layer == 0
    pair_col = pl.BlockSpec((n_tok, LANES), lambda p: (0, p))
    par_col = pl.BlockSpec((depth, LANES), lambda p: (0, p))
    n = R_HEAD_DIM
    if whole:
        state_spec = pl.BlockSpec((depth, 2, n, n, b), lambda p: (0, p, 0, 0, 0))
    else:
        state_spec = pl.BlockSpec((None, 2, n, n, b), lambda p: (layer, p, 0, 0, 0))
    y, s_new = pl.pallas_call(
        functools.partial(_wkv_lanes_kernel, layer=layer, n_new=n_new, whole_state=whole),
        grid=(N_PAIRS,),
        in_specs=[pair_col] * 8 + [par_col, par_col, state_spec],
        out_specs=[pair_col, state_spec],
        out_shape=[jax.ShapeDtypeStruct((n_tok, R_WIDTH), BF16), jax.ShapeDtypeStruct(s_lanes.shape, F32)],
        scratch_shapes=[pltpu.VMEM((6, n_new, LANES, b), F32), pltpu.VMEM((n_new, LANES, b), F32),
                        pltpu.VMEM((n_tok, LANES), F32)],
        input_output_aliases={} if whole else {10: 1},
        compiler_params=_cparams("arbitrary"),
        name="wkv7_lanes",
    )(*feats, P['lnx_g'], P['lnx_b'], s_lanes)
    return y, s_new


def _pair_kv(t):
    lane = lax.broadcasted_iota(jnp.int32, (1, LANES), 1)
    even = lane < HEAD_DIM
    sw = pltpu.roll(t, HEAD_DIM, 1)
    return jnp.where(even, t, sw), jnp.where(even, sw, t)


def _attend(qs, kcs, vcs, mask, sink_cols):
    n = range(len(qs))
    s = [_dot_nt(qs[i], kcs[i]) * ATTN_SCALE for i in n]
    s = [jnp.where(mask, s[i], -jnp.inf) for i in n]
    m = [jnp.maximum(jnp.max(s[i], axis=-1, keepdims=True), sink_cols[i]) for i in n]
    p = [jnp.exp(s[i] - m[i]) for i in n]
    den = [jnp.sum(p[i], axis=-1, keepdims=True) + jnp.exp(sink_cols[i] - m[i]) for i in n]
    o = [_dot(p[i], vcs[i]) for i in n]
    return [o[i] / den[i] for i in n]


def _attn_prompt_kernel(sink_ref, q_ref, kp_ref, kc_ref, vp_ref, vc_ref, o_ref, *, layer):
    n = pl.program_id(1)
    sink = lambda head: sink_ref[layer * A_HEADS + head]
    lane = lax.broadcasted_iota(jnp.int32, (1, LANES), 1)
    even = lane < HEAD_DIM
    kcat = jnp.concatenate([kp_ref[...], kc_ref[...]], axis=0)
    vcat = jnp.concatenate([vp_ref[...], vc_ref[...]], axis=0)
    ks = _pair_kv(kcat)
    vs = _pair_kv(vcat)
    qi = _pmod(lax.broadcasted_iota(jnp.int32, (2 * BLOCK, 2 * BLOCK), 0), BLOCK) + BLOCK
    kj = lax.broadcasted_iota(jnp.int32, (2 * BLOCK, 2 * BLOCK), 1)
    dist = qi - kj
    mask = (dist >= 0) & (dist <= WINDOW) & ((n > 0) | (kj >= BLOCK))
    top = lax.broadcasted_iota(jnp.int32, (2 * BLOCK, 1), 0) < BLOCK
    blocks = range(Q_WIDTH // LANES)
    qs, sink_cols = [], []
    for j in blocks:
        qb = q_ref[:, j * LANES:(j + 1) * LANES]
        qs.append(jnp.concatenate([jnp.where(even, qb, 0.0), jnp.where(even, 0.0, qb)], axis=0))
        sink_cols.append(jnp.where(top, sink(2 * j), sink(2 * j + 1)))
    kvh = [(2 * j) // GROUP for j in blocks]
    o = _attend(qs, [ks[h] for h in kvh], [vs[h] for h in kvh], mask, sink_cols)
    for j in blocks:
        o_ref[:, j * LANES:(j + 1) * LANES] = jnp.where(even, o[j][:BLOCK], o[j][BLOCK:]).astype(o_ref.dtype)


def _attn_prompt(q, k, v, sinks, layer, batch, seq):
    nb = seq // BLOCK
    cur = lambda b, n: (b * nb + n, 0)
    prv = lambda b, n: (b * nb + jnp.maximum(n - 1, 0), 0)
    kv = lambda f: pl.BlockSpec((BLOCK, KV_WIDTH), f)
    return pl.pallas_call(
        functools.partial(_attn_prompt_kernel, layer=layer),
        grid=(batch, nb),
        in_specs=[pl.BlockSpec(memory_space=pltpu.SMEM), pl.BlockSpec((BLOCK, Q_WIDTH), cur),
                  kv(prv), kv(cur), kv(prv), kv(cur)],
        out_specs=pl.BlockSpec((BLOCK, Q_WIDTH), cur),
        out_shape=jax.ShapeDtypeStruct((batch * seq, Q_WIDTH), BF16),
        compiler_params=_cparams("parallel", "parallel"),
        name="attn_prompt",
    )(sinks, q, k, k, v, v)


def _attn_sample_kernel(sink_ref, q_ref, kt_ref, kn_ref, vt_ref, vn_ref, o_ref, kt_out_ref, vt_out_ref,
                        *, layer, n_seq, wbuf, n_new):
    assert wbuf == LANES
    lane = lax.broadcasted_iota(jnp.int32, (1, LANES), 1)
    even = lane < HEAD_DIM
    rows = GROUP * SAMPLE_PAD
    n_pad = 2 * SAMPLE_PAD
    qt = _pmod(lax.broadcasted_iota(jnp.int32, (rows, 1), 0), SAMPLE_PAD)
    dist_c = qt + wbuf - lax.broadcasted_iota(jnp.int32, (rows, wbuf), 1)
    dist_n = qt - lax.broadcasted_iota(jnp.int32, (rows, n_pad), 1)
    mask_c = (dist_c >= 0) & (dist_c <= WINDOW)
    mask_n = (dist_n >= 0) & (dist_n <= WINDOW)
    rid = _pdiv(lax.broadcasted_iota(jnp.int32, (rows, 1), 0), SAMPLE_PAD)
    sink_cols = []
    for h in range(KV_HEADS):
        col = jnp.zeros((rows, 1), F32)
        for gq in range(GROUP):
            col = jnp.where(rid == gq, sink_ref[layer * A_HEADS + h * GROUP + gq], col)
        sink_cols.append(col)
    unroll = 2
    pad = jnp.zeros((SAMPLE_PAD, KV_WIDTH), F32)
    blocks_per_kv = GROUP // 2
    keep = lane < wbuf - n_new
    sub = lax.broadcasted_iota(jnp.int32, (SAMPLE_PAD, 1), 0)

    def new_columns(t_new):
        low = jnp.where(sub >= SAMPLE_PAD - n_new, pltpu.roll(t_new, SAMPLE_PAD - n_new, 0), 0.0)
        return jnp.concatenate([jnp.zeros((wbuf - SAMPLE_PAD, KV_WIDTH), F32), low], axis=0).T

    def body(i, carry_):
        items = [(i * unroll + u, h) for u in range(unroll) for h in range(KV_HEADS)]
        n = range(len(items))
        qs, kc, vc, kn, vn = [], [], [], [], []
        for u in range(unroll):
            g = i * unroll + u
            kn2 = _pair_kv(jnp.concatenate([kn_ref[g], pad], axis=0))
            vn2 = _pair_kv(jnp.concatenate([vn_ref[g], pad], axis=0))
            for h in range(KV_HEADS):
                parts = []
                for j in range(h * blocks_per_kv, (h + 1) * blocks_per_kv):
                    qb = q_ref[g, :, j * LANES:(j + 1) * LANES] * ATTN_SCALE
                    parts += [jnp.where(even, qb, 0.0), jnp.where(even, 0.0, qb)]
                qs.append(jnp.concatenate(parts, axis=0))
                kt, vt = kt_ref[g, h], vt_ref[g, h]
                kc.append(jnp.concatenate([kt, kt], axis=0))
                vc.append(jnp.concatenate([vt, vt], axis=0))
                kn.append(kn2[h])
                vn.append(vn2[h])
        s_c = [jnp.where(mask_c, _dot(qs[j], kc[j]), -jnp.inf) for j in n]
        s_n = [jnp.where(mask_n, _dot_nt(qs[j], kn[j]), -jnp.inf) for j in n]
        m = [jnp.maximum(jnp.maximum(jnp.max(s_c[j], axis=-1, keepdims=True),
                                     jnp.max(s_n[j], axis=-1, keepdims=True)), sink_cols[items[j][1]])
             for j in n]
        p_c = [jnp.exp(s_c[j] - m[j]) for j in n]
        p_n = [jnp.exp(s_n[j] - m[j]) for j in n]
        den = [jnp.sum(p_c[j], axis=-1, keepdims=True) + jnp.sum(p_n[j], axis=-1, keepdims=True)
               + jnp.exp(sink_cols[items[j][1]] - m[j]) for j in n]
        o = [(_dot_nt(p_c[j], vc[j]) + _dot(p_n[j], vn[j])) / den[j] for j in n]
        for j, (g, h) in enumerate(items):
            for jj in range(blocks_per_kv):
                blk = h * blocks_per_kv + jj
                r0 = 2 * jj * SAMPLE_PAD
                o_ref[g, :, blk * LANES:(blk + 1) * LANES] = jnp.where(
                    even, o[j][r0:r0 + SAMPLE_PAD], o[j][r0 + SAMPLE_PAD:r0 + 2 * SAMPLE_PAD]).astype(o_ref.dtype)
        for u in range(unroll):
            g = i * unroll + u
            k_cols, v_cols = new_columns(kn_ref[g]), new_columns(vn_ref[g])
            for h in range(KV_HEADS):
                hs = slice(h * HEAD_DIM, (h + 1) * HEAD_DIM)
                kt_out_ref[g, h] = jnp.where(keep, pltpu.roll(kt_ref[g, h], wbuf - n_new, 1), k_cols[hs])
                vt_out_ref[g, h] = jnp.where(keep, pltpu.roll(vt_ref[g, h], wbuf - n_new, 1), v_cols[hs])
        return carry_

    lax.fori_loop(0, n_seq // unroll, body, 0)


def _attn_sample(q, k, v, kt_all, vt_all, sinks, layer, n_seq, n_new):
    _, b, _, _, wbuf = kt_all.shape
    idx = lambda i: (i, 0, 0)
    new = pl.BlockSpec((n_seq, SAMPLE_PAD, KV_WIDTH), idx)
    buf = pl.BlockSpec((None, n_seq, KV_HEADS, HEAD_DIM, wbuf), lambda i: (layer, i, 0, 0, 0))
    win = pl.BlockSpec((n_seq, KV_HEADS, HEAD_DIM, wbuf), lambda i: (i, 0, 0, 0))
    win_shape = jax.ShapeDtypeStruct((b, KV_HEADS, HEAD_DIM, wbuf), F32)
    return pl.pallas_call(
        functools.partial(_attn_sample_kernel, layer=layer, n_seq=n_seq, wbuf=wbuf, n_new=n_new),
        grid=(b // n_seq,),
        in_specs=[pl.BlockSpec(memory_space=pltpu.SMEM), pl.BlockSpec((n_seq, SAMPLE_PAD, Q_WIDTH), idx),
                  buf, new, buf, new],
        out_specs=[pl.BlockSpec((n_seq, SAMPLE_PAD, Q_WIDTH), idx), win, win],
        out_shape=[jax.ShapeDtypeStruct((b, SAMPLE_PAD, Q_WIDTH), BF16), win_shape, win_shape],
        compiler_params=_cparams("parallel"),
        name="attn_sample",
    )(sinks, q, kt_all, k, vt_all, v)


def _layer_norm(x, g, b):
    mu = jnp.mean(x, axis=-1, keepdims=True)
    d = x - mu
    var = jnp.mean(d * d, axis=-1, keepdims=True)
    return d * lax.rsqrt(var + LN_EPS) * g + b


def _mix_ffn_kernel(x_ref, yr_ref, ya_ref, win_ref, wbr_ref, wba_ref, wo_ref, g1_ref, b1_ref,
                    wu_ref, wd_ref, g2_ref, b2_ref, o_ref, *, layer, ff_chunk):
    par = lambda ref: ref[layer:layer + 1, :]
    x = x_ref[...]
    xb = x.astype(BF16)
    gate_r = jax.nn.sigmoid(jnp.dot(xb, win_ref[:, QKV_END:QKV_END + D_MODEL], preferred_element_type=F32))
    mix = gate_r * jnp.dot(yr_ref[...], wbr_ref[...], preferred_element_type=F32)
    gate_a = jax.nn.sigmoid(jnp.dot(xb, win_ref[:, QKV_END + D_MODEL:], preferred_element_type=F32))
    mix = mix + gate_a * jnp.dot(ya_ref[...], wba_ref[...], preferred_element_type=F32)
    x1 = _layer_norm(ALPHA * x + _dot(mix, wo_ref[...]), par(g1_ref), par(b1_ref))
    x1b = x1.astype(BF16)
    acc = ALPHA * x1
    for c in range(D_FF // ff_chunk):
        cs = slice(c * ff_chunk, (c + 1) * ff_chunk)
        h = jnp.maximum(jnp.dot(x1b, wu_ref[:, cs], preferred_element_type=F32), 0.0)
        acc = acc + _dot(h * h, wd_ref[cs, :])
    o_ref[...] = _layer_norm(acc, par(g2_ref), par(b2_ref))


def _mix_ffn(x, yr, ya, P, layer, tm):
    n = x.shape[0]
    row = lambda i: (i, 0)
    ws = [P['w_in'], P['w_br_rwkv'], P['w_br_attn'], P['w_out'], P['ln1_g'], P['ln1_b'],
          P['w_ff_up'], P['w_ff_down'], P['ln2_g'], P['ln2_b']]
    wspec = lambda a: _layer_spec(a, layer, single_buffer=True) if a.ndim == 3 else _const_spec(a.shape)
    return pl.pallas_call(
        functools.partial(_mix_ffn_kernel, layer=layer, ff_chunk=1024),
        grid=(n // tm,),
        in_specs=[pl.BlockSpec((tm, D_MODEL), row), pl.BlockSpec((tm, R_WIDTH), row),
                  pl.BlockSpec((tm, Q_WIDTH), row)] + [wspec(a) for a in ws],
        out_specs=pl.BlockSpec((tm, D_MODEL), row),
        out_shape=jax.ShapeDtypeStruct((n, D_MODEL), F32),
        compiler_params=_cparams("parallel"),
        name="mix_ffn",
    )(x, yr, ya, *ws)


def _prepare_params(w_in, mu_shift, decay_base, decay_up, iclr_base, iclr_up, gate_up, k_k, k_a, r_k,
                    lnx_g, lnx_b, sinks, w_br_rwkv, w_br_attn, w_out, ln1_g, ln1_b, w_ff_up, w_ff_down,
                    ln2_g, ln2_b):
    depth = w_in.shape[0]
    zeros = jnp.zeros((depth, DECAY_LORA, R_WIDTH), F32)
    lora_up = jnp.concatenate([jnp.concatenate([decay_up, zeros], axis=2),
                               jnp.concatenate([zeros, iclr_up], axis=2)], axis=1)
    bf = lambda a: a.astype(BF16)
    return dict(
        w_in=bf(w_in), mu_shift=mu_shift, decay_base=decay_base, lora_up=bf(lora_up), iclr_base=iclr_base,
        gate_up=bf(gate_up), k_k=k_k, k_a=k_a, r_k=r_k.reshape(depth, R_WIDTH), lnx_g=lnx_g, lnx_b=lnx_b,
        sinks=sinks.reshape(depth * A_HEADS), w_br_rwkv=bf(w_br_rwkv), w_br_attn=bf(w_br_attn),
        w_out=bf(w_out), ln1_g=ln1_g, ln1_b=ln1_b, w_ff_up=bf(w_ff_up), w_ff_down=bf(w_ff_down),
        ln2_g=ln2_g, ln2_b=ln2_b)


def _prompt_layer(x, P, layer, tables, s_all, batch, seq, wbuf):
    tm = 512
    zr, q, k, v = _inproj(x, P['w_in'], layer, tables, seq // tm, tm)
    shift0 = jnp.zeros((batch, 1, SHIFT_W), F32)
    wkv_rows = tm // batch
    yr, s_all = _wkv(zr.reshape(batch, seq, SHIFT_W), shift0, s_all, P, layer, n_seq=batch, seq_rows=wkv_rows,
                     seg=WKV_CHUNK, valid_rows=wkv_rows, carried_state=True, lanes=batch)
    ya = _attn_prompt(q, k, v, P['sinks'], layer, batch, seq)
    x = _mix_ffn(x, yr.reshape(batch * seq, R_WIDTH), ya, P, layer, tm)
    tail = lambda t, w: t.reshape(batch, seq, w)[:, seq - wbuf:].reshape(batch, wbuf, KV_HEADS, HEAD_DIM)
    return x, s_all, zr.reshape(batch, seq, SHIFT_W)[:, -1], tail(k, KV_WIDTH), tail(v, KV_WIDTH)


def _sample_layer(x, P, layer, tables, shift_prev, s_all, kt_all, vt_all, batch, seq):
    n = batch * SAMPLE_PAD
    tm = min(n, 512)
    n_seq = 16
    zr, q, k, v = _inproj(x, P['w_in'], layer, tables, 1, tm)
    seq3 = lambda t, w: t.reshape(batch, SAMPLE_PAD, w)
    yr, s_all = _wkv_short(seq3(zr, SHIFT_W), shift_prev[:, None, :], s_all, P, layer, seq)
    ya, kt_new, vt_new = _attn_sample(seq3(q, Q_WIDTH), seq3(k, KV_WIDTH), seq3(v, KV_WIDTH), kt_all, vt_all,
                                      P['sinks'], layer, n_seq, seq)
    x = _mix_ffn(x, yr.reshape(n, R_WIDTH), ya.reshape(n, Q_WIDTH), P, layer, tm)
    return x, s_all, seq3(zr, SHIFT_W)[:, seq - 1], kt_new, vt_new


def kernel(x_prompt, x_sample, state_wkv, state_shift, cache_k_win, cache_v_win, w_in, mu_shift, decay_base, decay_up, iclr_base, iclr_up, gate_up, k_k, k_a, r_k, lnx_g, lnx_b, sinks, w_br_rwkv, w_br_attn, w_out, ln1_g, ln1_b, w_ff_up, w_ff_down, ln2_g, ln2_b):
    bp, tp, _ = x_prompt.shape
    bs, ts, _ = x_sample.shape
    wbuf = cache_k_win.shape[2]
    half = HEAD_DIM // 2
    inv_freq = ROPE_THETA ** (-jnp.arange(half, dtype=F32) / half)
    tab_p = _rope_tables(inv_freq, tp, 0, tp)
    tab_s = _rope_tables(inv_freq, min(bs * SAMPLE_PAD, 512), PAST_LEN, SAMPLE_PAD)

    hp = x_prompt.reshape(bp * tp, D_MODEL)
    hs = jnp.pad(x_sample, ((0, 0), (0, SAMPLE_PAD - ts), (0, 0))).reshape(bs * SAMPLE_PAD, D_MODEL)
    outs_p, outs_s = [], []
    P = _prepare_params(w_in, mu_shift, decay_base, decay_up, iclr_base, iclr_up, gate_up, k_k, k_a, r_k,
                        lnx_g, lnx_b, sinks, w_br_rwkv, w_br_attn, w_out, ln1_g, ln1_b, w_ff_up,
                        w_ff_down, ln2_g, ln2_b)
    to_lanes = lambda c: jnp.transpose(c, (0, 1, 3, 4, 2))
    from_lanes = lambda c: jnp.transpose(c, (0, 1, 4, 2, 3))
    kt_all, vt_all = to_lanes(cache_k_win), to_lanes(cache_v_win)
    s_all_p = jnp.zeros((DEPTH, bp, R_HEADS, R_HEAD_DIM, R_HEAD_DIM), F32)
    s_all_s = jnp.transpose(state_wkv, (0, 2, 3, 4, 1))
    for l in range(DEPTH):
        hp, s_all_p, *st = _prompt_layer(hp, P, l, tab_p, s_all_p, bp, tp, wbuf)
        outs_p.append(st)
        hs, s_all_s, *st = _sample_layer(hs, P, l, tab_s, state_shift[l], s_all_s, kt_all, vt_all, bs, ts)
        outs_s.append(st)
    stack = lambda outs, i: jnp.stack([o[i] for o in outs])
    y_p = hp.reshape(bp, tp, D_MODEL)
    y_s = hs.reshape(bs, SAMPLE_PAD, D_MODEL)[:, :ts]
    return (y_p, y_s,
            s_all_p, stack(outs_p, 0), stack(outs_p, 1), stack(outs_p, 2),
            jnp.transpose(s_all_s, (0, 4, 1, 2, 3)), stack(outs_s, 0), from_lanes(stack(outs_s, 1)),
            from_lanes(stack(outs_s, 2)))
```

```python
import functools
import math

import jax
import jax.numpy as jnp
from jax import lax
from jax.experimental import pallas as pl
from jax.experimental.pallas import tpu as pltpu

F32 = jnp.float32
BF16 = jnp.bfloat16

D_MODEL = 1024
DEPTH = 2
PAST_LEN = 8192
R_HEADS = 8
R_HEAD_DIM = 64
R_WIDTH = R_HEADS * R_HEAD_DIM
DECAY_LORA = 64
ICLR_LORA = 64
GATE_LORA = 128
SHIFT_W = 3 * R_WIDTH + DECAY_LORA + ICLR_LORA + GATE_LORA
A_HEADS = 8
KV_HEADS = 2
HEAD_DIM = 64
Q_WIDTH = A_HEADS * HEAD_DIM
KV_WIDTH = KV_HEADS * HEAD_DIM
GROUP = A_HEADS // KV_HEADS
WINDOW = 128
BLOCK = 128
ROPE_THETA = 10000.0
ATTN_SCALE = HEAD_DIM ** -0.5
D_FF = 4 * D_MODEL
ALPHA = (2 * DEPTH) ** 0.25
LN_EPS = 1e-5
GN_EPS = 64e-5
QKV_END = SHIFT_W + Q_WIDTH + 2 * KV_WIDTH

LANES = 128
SUBLANES = 8
VMEM_LIMIT = 56 * 1024 * 1024

SAMPLE_PAD = SUBLANES
N_PAIRS = R_HEADS // 2
WKV_CHUNK = LANES // 2
PHASE1_CHUNKS = 4


def _cparams(*sem):
    return pltpu.CompilerParams(dimension_semantics=sem, vmem_limit_bytes=VMEM_LIMIT)


def _const_spec(shape):
    nd = len(shape)
    return pl.BlockSpec(shape, lambda *_: (0,) * nd)


def _layer_spec(a, layer, single_buffer=False):
    mode = pl.Buffered(1) if single_buffer else None
    return pl.BlockSpec((None,) + a.shape[1:], lambda *_: (layer, 0, 0), pipeline_mode=mode)


def _dot(a, b):
    return jnp.dot(a.astype(BF16), b.astype(BF16), preferred_element_type=F32)


def _dot_nt(a, b):
    return lax.dot_general(a.astype(BF16), b.astype(BF16), (((1,), (1,)), ((), ())),
                           preferred_element_type=F32)


def _pmod(x, n):
    assert n & (n - 1) == 0
    return x & (n - 1)


def _pdiv(x, n):
    assert n & (n - 1) == 0
    return x >> (n.bit_length() - 1)


def _split(x):
    hi = x.astype(BF16)
    lo = (x - hi.astype(F32)).astype(BF16)
    return hi, lo


def _dot2_nt(a, b):
    bh, bl = _split(b)
    d = functools.partial(lax.dot_general, dimension_numbers=(((1,), (1,)), ((), ())),
                          preferred_element_type=F32)
    ab = a.astype(BF16)
    return d(ab, bh) + d(ab, bl)


def _rope_table_kernel(invf_ref, cos_ref, sa_ref, sb_ref, *, rows, offset, period):
    i = pl.program_id(0)
    row = lax.broadcasted_iota(jnp.int32, (rows, LANES), 0) + i * rows
    pos = offset + _pmod(row, period)
    ang = pos.astype(F32) * invf_ref[...]
    lane = lax.broadcasted_iota(jnp.int32, (rows, LANES), 1)
    first = _pmod(lane, HEAD_DIM) < (HEAD_DIM // 2)
    c = jnp.cos(ang)
    s = jnp.sin(ang)
    cos_ref[...] = c
    sa_ref[...] = jnp.where(first, -s, 0.0)
    sb_ref[...] = jnp.where(first, 0.0, s)


def _rope_tables(inv_freq, n_rows, offset, period):
    rows = min(n_rows, 1024)
    invf = jnp.tile(inv_freq, LANES // (HEAD_DIM // 2)).reshape(1, LANES)
    out = jax.ShapeDtypeStruct((n_rows, LANES), F32)
    return pl.pallas_call(
        functools.partial(_rope_table_kernel, rows=rows, offset=offset, period=period),
        grid=(n_rows // rows,),
        in_specs=[_const_spec((1, LANES))],
        out_specs=[pl.BlockSpec((rows, LANES), lambda i: (i, 0))] * 3,
        out_shape=[out] * 3,
        compiler_params=_cparams("parallel"),
        name="rope_tables",
    )(invf)


def _inproj_kernel(x_ref, w_ref, cos_ref, sa_ref, sb_ref, zr_ref, q_ref, k_ref, v_ref):
    xb = x_ref[...].astype(BF16)
    zr_ref[...] = jnp.dot(xb, w_ref[:, :SHIFT_W], preferred_element_type=F32)
    cos, sa, sb = cos_ref[...], sa_ref[...], sb_ref[...]

    def rope(t):
        return (t * cos + pltpu.roll(t, LANES - HEAD_DIM // 2, 1) * sa
                + pltpu.roll(t, HEAD_DIM // 2, 1) * sb)

    zq = jnp.dot(xb, w_ref[:, SHIFT_W:SHIFT_W + Q_WIDTH], preferred_element_type=F32)
    for j in range(Q_WIDTH // LANES):
        q_ref[:, j * LANES:(j + 1) * LANES] = rope(zq[:, j * LANES:(j + 1) * LANES])
    zkv = jnp.dot(xb, w_ref[:, SHIFT_W + Q_WIDTH:QKV_END], preferred_element_type=F32)
    k_ref[...] = rope(zkv[:, :KV_WIDTH])
    v_ref[...] = zkv[:, KV_WIDTH:]


def _inproj(x, w_in, layer, tables, tab_blocks, tm):
    n = x.shape[0]
    row = lambda i: (i, 0)
    tab = pl.BlockSpec((tm, LANES), lambda i: (lax.rem(i, tab_blocks), 0))
    w_spec = pl.BlockSpec((None, D_MODEL, QKV_END), lambda i: (layer, 0, 0))
    return pl.pallas_call(
        _inproj_kernel,
        grid=(n // tm,),
        in_specs=[pl.BlockSpec((tm, D_MODEL), row), w_spec, tab, tab, tab],
        out_specs=[pl.BlockSpec((tm, SHIFT_W), row), pl.BlockSpec((tm, Q_WIDTH), row),
                   pl.BlockSpec((tm, KV_WIDTH), row), pl.BlockSpec((tm, KV_WIDTH), row)],
        out_shape=[jax.ShapeDtypeStruct((n, SHIFT_W), F32), jax.ShapeDtypeStruct((n, Q_WIDTH), F32),
                   jax.ShapeDtypeStruct((n, KV_WIDTH), F32), jax.ShapeDtypeStruct((n, KV_WIDTH), F32)],
        compiler_params=_cparams("parallel"),
        name="inproj",
    )(x, w_in, *tables)


def _pair_block_diag():
    bi = _pdiv(lax.broadcasted_iota(jnp.int32, (LANES, LANES), 0), R_HEAD_DIM)
    bj = _pdiv(lax.broadcasted_iota(jnp.int32, (LANES, LANES), 1), R_HEAD_DIM)
    return (bi == bj).astype(F32)


def _head_sum(x):
    ones = _pair_block_diag().astype(BF16)
    xb = x.astype(BF16)
    return jnp.concatenate([jnp.dot(xb[:, j * LANES:(j + 1) * LANES], ones, preferred_element_type=F32)
                            for j in range(x.shape[1] // LANES)], axis=1)


def _token_shift(z, shift_ref, carry, prev_s, mu, t_step, n_seq, seq_rows):
    prev_s[...] = pltpu.roll(z, 1, 0)

    @pl.when(t_step == 0)
    def _():
        for g in range(n_seq):
            carry[g:g + 1, :] = shift_ref[g]
    for g in range(n_seq):
        prev_s[g * seq_rows:g * seq_rows + 1, :] = carry[g:g + 1, :]
        carry[g:g + 1, :] = z[(g + 1) * seq_rows - 1:(g + 1) * seq_rows, :]
    return z + (prev_s[...] - z) * mu


def _token_features(zs, decay_base, lora_up, iclr_base, gate_up, k_k, k_a, r_k):
    o1, o2, o3 = R_WIDTH, 2 * R_WIDTH, 3 * R_WIDTH
    o4 = o3 + DECAY_LORA + ICLR_LORA
    r = zs[:, :o1]
    k = zs[:, o1:o2]
    v = zs[:, o2:o3]
    wa = zs[:, o3:o4]
    gd = zs[:, o4:]
    lane = lax.broadcasted_iota(jnp.int32, (1, LANES), 1)
    lora_in = jnp.where(lane < DECAY_LORA, jnp.tanh(wa), wa)
    lora = _dot(lora_in, lora_up)
    pre_w = decay_base + lora[:, :R_WIDTH]
    logw = -math.exp(-0.5) * jax.nn.sigmoid(pre_w)
    a = jax.nn.sigmoid(iclr_base + lora[:, R_WIDTH:])
    g = _dot(jax.nn.sigmoid(gd), gate_up)
    kk = k * k_k
    kk = kk * lax.rsqrt(jnp.maximum(_head_sum(kk * kk), 1e-24))
    kp = k * (1.0 + (a - 1.0) * k_a)
    bonus = _head_sum(r * kp * r_k) * v
    return r, kp, v, kk, a, logw, g, bonus


def _group_norm_out(y, bonus, g, lnx_g, lnx_b):
    inv_n = 1.0 / R_HEAD_DIM
    d = y - _head_sum(y) * inv_n
    var = _head_sum(d * d) * inv_n
    return (d * lax.rsqrt(var + GN_EPS) * lnx_g + lnx_b + bonus) * g


def _wkv_kernel(zr_ref, shift_ref, sin_ref, mu_ref, db_ref, lora_ref, ib_ref, gu_ref, kk_ref, ka_ref,
                rk_ref, lg_ref, lb_ref,
                y_ref, sout_ref,
                sbd, carry, prev_s, r_s, k_s, v_s, kk_s, b_s, lw_s, cw_s, y_s, bon_s, g_s, xg_s, uy_s, bk_s,
                *, layer, n_seq, seq_rows, chunk):
    t_step = pl.program_id(1)
    n_steps = pl.num_programs(1)
    rows = n_seq * seq_rows
    n_chunks = rows // chunk
    c2 = 2 * chunk
    par = lambda ref: ref[layer:layer + 1, :]

    z = zr_ref[...].reshape(rows, SHIFT_W)
    zs = _token_shift(z, shift_ref, carry, prev_s, par(mu_ref), t_step, n_seq, seq_rows)
    r, kp, v, kk, a, logw, g, bonus = _token_features(
        zs, par(db_ref), lora_ref[...], par(ib_ref), gu_ref[...], par(kk_ref), par(ka_ref), par(rk_ref))
    g_s[...] = g
    bon_s[...] = bonus
    r_s[...] = r
    k_s[...] = kp
    v_s[...] = v
    kk_s[...] = kk
    b_s[...] = kk * a
    lw_s[...] = logw

    tid = _pmod(lax.broadcasted_iota(jnp.int32, (rows, 1), 0), chunk)
    cw = logw
    sh = 1
    while sh < chunk:
        cw = cw + jnp.where(tid >= sh, pltpu.roll(cw, sh, 0), 0.0)
        sh *= 2
    cw_s[...] = cw

    lane = lax.broadcasted_iota(jnp.int32, (1, LANES), 1)
    left = (lane < R_HEAD_DIM).astype(F32)
    right = 1.0 - left
    head_lanes = (left, right)
    m_bd = _pair_block_diag()
    ti = lax.broadcasted_iota(jnp.int32, (chunk, LANES), 0)
    tj = _pmod(lax.broadcasted_iota(jnp.int32, (chunk, LANES), 1), chunk)
    tri_strict = (tj < ti).astype(F32)
    tri_incl = (tj <= ti).astype(F32)
    eye = (tj == ti).astype(F32)
    is_left = lax.broadcasted_iota(jnp.int32, (chunk, LANES), 1) < chunk
    n_fact = int(math.log2(chunk))

    def load_state(u):
        for p in range(N_PAIRS):
            zero = jnp.zeros((R_HEAD_DIM, R_HEAD_DIM), F32)
            sbd[u * N_PAIRS + p, 0:R_HEAD_DIM, :] = jnp.concatenate([sin_ref[u, 2 * p], zero], axis=1)
            sbd[u * N_PAIRS + p, R_HEAD_DIM:LANES, :] = jnp.concatenate([zero, sin_ref[u, 2 * p + 1]], axis=1)

    def store_state(u):
        for p in range(N_PAIRS):
            sout_ref[u, 2 * p] = sbd[u * N_PAIRS + p, 0:R_HEAD_DIM, 0:R_HEAD_DIM]
            sout_ref[u, 2 * p + 1] = sbd[u * N_PAIRS + p, R_HEAD_DIM:LANES, R_HEAD_DIM:LANES]

    @pl.when(t_step == 0)
    def _():
        for u in range(n_seq):
            load_state(u)

    pairs = range(N_PAIRS)
    col = lambda p: slice(p * LANES, (p + 1) * LANES)
    stack2 = lambda t: jnp.concatenate([t * left, t * right], axis=0)
    half = lambda t: t[:, :chunk]

    def phase1(i, carry_):
        groups = [(i * PHASE1_CHUNKS + u, p) for u in range(PHASE1_CHUNKS) for p in pairs]
        at, rt, vv, bk = [], [], [], []
        for c, p in groups:
            rs = pl.ds(pl.multiple_of(c * chunk, chunk), chunk)
            cwc = cw_s[rs, col(p)]
            w_in = jnp.exp(cwc)
            w_ex = jnp.exp(cwc - lw_s[rs, col(p)])
            w_inv = jnp.exp(-cwc)
            at.append(-kk_s[rs, col(p)] * w_ex)
            rt.append(r_s[rs, col(p)] * w_in)
            vv.append(v_s[rs, col(p)])
            bk.append(jnp.concatenate([b_s[rs, col(p)] * w_inv, k_s[rs, col(p)] * w_inv], axis=0))
        items = [(gi, h) for gi in range(len(groups)) for h in range(2)]
        n = range(len(items))
        ats = [at[gi] * head_lanes[h] for gi, h in items]
        rts = [rt[gi] * head_lanes[h] for gi, h in items]
        sc = [_dot_nt(jnp.concatenate([ats[j], rts[j]], axis=0), bk[items[j][0]]) for j in n]
        top = [sc[j][:chunk] * tri_strict for j in n]
        bot = [sc[j][chunk:] * tri_incl for j in n]
        top_sw = [pltpu.roll(top[j], chunk, 1) for j in n]
        bot_sw = [pltpu.roll(bot[j], chunk, 1) for j in n]

        pt = [jnp.where(is_left, top[j], eye) for j in n]
        for _ in range(n_fact - 1):
            z = [_dot(half(pt[j]), pt[j]) for j in n]
            pt = [jnp.where(is_left, z[j], pt[j] + z[j]) for j in n]
        pt = [pt[j] + _dot(half(pt[j]), pt[j]) for j in n]
        tm = [half(pltpu.roll(pt[j], chunk, 1)) for j in n]

        ta = [_dot(tm[j], jnp.concatenate([ats[j], top_sw[j]], axis=1)) for j in n]
        ab = [_dot(half(bot[j]), ta[j]) for j in n]
        yk = [ab[j][:, LANES:] + bot_sw[j] for j in n]
        uy = [_dot(jnp.concatenate([half(ta[j][:, LANES:]), half(yk[j])], axis=0),
                   vv[items[j][0]] * head_lanes[items[j][1]]) for j in n]
        for j, (gi, h) in enumerate(items):
            c, p = groups[gi]
            idx = c * N_PAIRS + p
            xg_s[idx, h * chunk:(h + 1) * chunk, :] = ta[j][:, :LANES]
            xg_s[idx, c2 + h * chunk:c2 + (h + 1) * chunk, :] = rts[j] + ab[j][:, :LANES]
            uy_s[idx, h * chunk:(h + 1) * chunk, :] = uy[j][:chunk]
            uy_s[idx, c2 + h * chunk:c2 + (h + 1) * chunk, :] = uy[j][chunk:]
        for gi, (c, p) in enumerate(groups):
            bk_s[c * N_PAIRS + p] = bk[gi]
        return carry_

    lax.fori_loop(0, n_chunks // PHASE1_CHUNKS, phase1, 0)

    chunks_per_seq = seq_rows // chunk

    def phase2(i, carry_):
        cks = [u * chunks_per_seq + i for u in range(n_seq)]
        items = [(u, p) for u in range(n_seq) for p in pairs]
        n = range(len(items))
        tok = [pl.ds(pl.multiple_of(ck * chunk, chunk), chunk) for ck in cks]
        tail = [pl.ds(pl.multiple_of(ck * chunk + chunk - SUBLANES, SUBLANES), SUBLANES) for ck in cks]
        s0 = [sbd[u * N_PAIRS + p] for u, p in items]
        uyv = [_dot2_nt(xg_s[cks[u] * N_PAIRS + p], s0[j]) + uy_s[cks[u] * N_PAIRS + p]
               for j, (u, p) in enumerate(items)]
        uvt = [jnp.concatenate([uyv[j][:c2], stack2(v_s[tok[u], col(p)])], axis=0).T
               for j, (u, p) in enumerate(items)]
        rhs = []
        for u, p in items:
            bkc = bk_s[cks[u] * N_PAIRS + p]
            bt, kt = bkc[:chunk], bkc[chunk:]
            rhs.append(jnp.concatenate([bt, bt, kt, kt], axis=0))
        upd = [_dot(uvt[j], rhs[j]) for j in n]
        for j, (u, p) in enumerate(items):
            w_end = jnp.exp(cw_s[tail[u], col(p)][SUBLANES - 1:SUBLANES, :])
            sbd[u * N_PAIRS + p] = (s0[j] + upd[j] * m_bd) * w_end
            y_s[tok[u], col(p)] = uyv[j][c2:c2 + chunk] + uyv[j][c2 + chunk:]
        return carry_

    lax.fori_loop(0, chunks_per_seq, phase2, 0)

    @pl.when(t_step == n_steps - 1)
    def _():
        for u in range(n_seq):
            store_state(u)

    out = _group_norm_out(y_s[...], bon_s[...], g_s[...], par(lg_ref), par(lb_ref))
    y_ref[...] = out.reshape(n_seq, seq_rows, R_WIDTH).astype(y_ref.dtype)


def _wkv(zr, shift_in, s_all, P, layer, *, seq_rows):
    b, t_len, _ = zr.shape
    n_seq = b
    rows = n_seq * seq_rows
    chunk = WKV_CHUNK
    n_pc = (rows // chunk) * N_PAIRS
    assert seq_rows % chunk == 0 and rows % (chunk * PHASE1_CHUNKS) == 0 and t_len % seq_rows == 0
    params = [P['mu_shift'], P['decay_base'], P['lora_up'], P['iclr_base'], P['gate_up'], P['k_k'], P['k_a'],
              P['r_k'], P['lnx_g'], P['lnx_b']]
    param_specs = [_layer_spec(a, layer) if a.ndim == 3 else _const_spec(a.shape) for a in params]
    tok = lambda g, t: (g, t, 0)
    seq3 = lambda g, t: (g, 0, 0)
    state_spec = pl.BlockSpec((None, n_seq, R_HEADS, R_HEAD_DIM, R_HEAD_DIM), lambda g, t: (layer, g, 0, 0, 0))
    tile = lambda w: pltpu.VMEM((rows, w), F32)
    return pl.pallas_call(
        functools.partial(_wkv_kernel, layer=layer, n_seq=n_seq, seq_rows=seq_rows, chunk=chunk),
        grid=(b // n_seq, t_len // seq_rows),
        in_specs=[pl.BlockSpec((n_seq, seq_rows, SHIFT_W), tok), pl.BlockSpec((n_seq, 1, SHIFT_W), seq3),
                  state_spec] + param_specs,
        out_specs=[pl.BlockSpec((n_seq, seq_rows, R_WIDTH), tok), state_spec],
        out_shape=[jax.ShapeDtypeStruct((b, t_len, R_WIDTH), BF16), jax.ShapeDtypeStruct(s_all.shape, F32)],
        input_output_aliases={2: 1},
        scratch_shapes=[pltpu.VMEM((n_seq * N_PAIRS, LANES, LANES), F32), pltpu.VMEM((n_seq, SHIFT_W), F32),
                        tile(SHIFT_W)] + [tile(R_WIDTH)] * 10 + [
                            pltpu.VMEM((n_pc, 4 * chunk, LANES), F32), pltpu.VMEM((n_pc, 4 * chunk, LANES), F32),
                            pltpu.VMEM((n_pc, 2 * chunk, LANES), F32)],
        compiler_params=_cparams("parallel", "arbitrary"),
        name="wkv7",
    )(zr, shift_in, s_all, *params)


def _wkv_prep_kernel(zr_ref, shift_ref, mu_ref, db_ref, lora_ref, ib_ref, gu_ref, kk_ref, ka_ref, rk_ref,
                     r_ref, k_ref, v_ref, kko_ref, b_ref, w_ref, g_ref, bon_ref, carry, prev_s,
                     *, layer, n_seq, seq_rows):
    par = lambda ref: ref[layer:layer + 1, :]
    z = zr_ref[...].reshape(n_seq * seq_rows, SHIFT_W)
    zs = _token_shift(z, shift_ref, carry, prev_s, par(mu_ref), pl.program_id(1), n_seq, seq_rows)
    r, kp, v, kk, a, logw, g, bonus = _token_features(
        zs, par(db_ref), lora_ref[...], par(ib_ref), gu_ref[...], par(kk_ref), par(ka_ref), par(rk_ref))
    r_ref[...] = r
    k_ref[...] = kp
    v_ref[...] = v
    kko_ref[...] = kk
    b_ref[...] = kk * a
    w_ref[...] = jnp.exp(logw)
    g_ref[...] = g
    bon_ref[...] = bonus


LANE_GROUP = 4


def _wkv_lanes_kernel(r_ref, k_ref, v_ref, kk_ref, b_ref, w_ref, g_ref, bon_ref, lg_ref, lb_ref, s_ref,
                      y_ref, sout_ref, ft_s, yt_s, ytok_s, *, layer, n_new, whole_state):
    nb = ft_s.shape[-1]
    n = R_HEAD_DIM
    feats = (r_ref, k_ref, v_ref, kk_ref, b_ref, w_ref)
    for fi, ref in enumerate(feats):
        for t in range(n_new):
            ft_s[fi, t] = ref[pl.ds(t, nb, stride=SAMPLE_PAD), :].T
    load_s = (lambda hh, v: s_ref[layer, hh, v]) if whole_state else (lambda hh, v: s_ref[hh, v])

    def store_s(hh, v, val):
        if whole_state:
            sout_ref[layer, hh, v] = val
        else:
            sout_ref[hh, v] = val

    for hh in range(2):
        hs = slice(hh * n, (hh + 1) * n)

        def rows8(vg, carry_):
            v0 = pl.multiple_of(vg * SUBLANES, SUBLANES)
            vrows = [ft_s[2, t, pl.ds(hh * n + v0, SUBLANES), :] for t in range(n_new)]
            ys = [[None] * SUBLANES for _ in range(n_new)]
            for j0 in range(0, SUBLANES, LANE_GROUP):
                js = range(j0, j0 + LANE_GROUP)
                sv = {j: load_s(hh, v0 + j) for j in js}
                for t in range(n_new):
                    nk, w, bb = -ft_s[3, t, hs, :], ft_s[5, t, hs, :], ft_s[4, t, hs, :]
                    kt, rt = ft_s[1, t, hs, :], ft_s[0, t, hs, :]
                    sa = {j: jnp.sum(sv[j] * nk, axis=0, keepdims=True) for j in js}
                    sv = {j: sv[j] * w + sa[j] * bb + vrows[t][j:j + 1, :] * kt for j in js}
                    for j in js:
                        ys[t][j] = jnp.sum(sv[j] * rt, axis=0, keepdims=True)
                for j in js:
                    store_s(hh, v0 + j, sv[j])
            for t in range(n_new):
                yt_s[t, pl.ds(hh * n + v0, SUBLANES), :] = jnp.concatenate(ys[t], axis=0)
            return carry_

        lax.fori_loop(0, n // SUBLANES, rows8, 0)

    if whole_state:
        for l in range(s_ref.shape[0]):
            if l != layer:
                sout_ref[l] = s_ref[l]

    ytok_s[...] = jnp.zeros(ytok_s.shape, F32)
    for t in range(n_new):
        ytok_s[pl.ds(t, nb, stride=SAMPLE_PAD), :] = yt_s[t].T
    out = _group_norm_out(ytok_s[...], bon_ref[...], g_ref[...], lg_ref[layer:layer + 1, :],
                          lb_ref[layer:layer + 1, :])
    y_ref[...] = out.astype(y_ref.dtype)


def _wkv_short(zr, shift_in, s_lanes, P, layer, n_new):
    b, seq_rows, _ = zr.shape
    depth = s_lanes.shape[0]
    n_tok = b * seq_rows
    n_seq = 32
    params = [P['mu_shift'], P['decay_base'], P['lora_up'], P['iclr_base'], P['gate_up'], P['k_k'], P['k_a'],
              P['r_k']]
    param_specs = [_layer_spec(a, layer) if a.ndim == 3 else _const_spec(a.shape) for a in params]
    rows = n_seq * seq_rows
    tok_out = pl.BlockSpec((rows, R_WIDTH), lambda g, t: (g, 0))
    feats = pl.pallas_call(
        functools.partial(_wkv_prep_kernel, layer=layer, n_seq=n_seq, seq_rows=seq_rows),
        grid=(b // n_seq, 1),
        in_specs=[pl.BlockSpec((n_seq, seq_rows, SHIFT_W), lambda g, t: (g, 0, 0)),
                  pl.BlockSpec((n_seq, 1, SHIFT_W), lambda g, t: (g, 0, 0))] + param_specs,
        out_specs=[tok_out] * 8,
        out_shape=[jax.ShapeDtypeStruct((n_tok, R_WIDTH), F32)] * 8,
        scratch_shapes=[pltpu.VMEM((n_seq, SHIFT_W), F32), pltpu.VMEM((rows, SHIFT_W), F32)],
        compiler_params=_cparams("parallel", "arbitrary"),
        name="wkv7_prep",
    )(zr, shift_in, *params)

    whole = layer == 0
    pair_col = pl.BlockSpec((n_tok, LANES), lambda p: (0, p))
    par_col = pl.BlockSpec((depth, LANES), lambda p: (0, p))
    n = R_HEAD_DIM
    if whole:
        state_spec = pl.BlockSpec((depth, 2, n, n, b), lambda p: (0, p, 0, 0, 0))
    else:
        state_spec = pl.BlockSpec((None, 2, n, n, b), lambda p: (layer, p, 0, 0, 0))
    y, s_new = pl.pallas_call(
        functools.partial(_wkv_lanes_kernel, layer=layer, n_new=n_new, whole_state=whole),
        grid=(N_PAIRS,),
        in_specs=[pair_col] * 8 + [par_col, par_col, state_spec],
        out_specs=[pair_col, state_spec],
        out_shape=[jax.ShapeDtypeStruct((n_tok, R_WIDTH), BF16), jax.ShapeDtypeStruct(s_lanes.shape, F32)],
        scratch_shapes=[pltpu.VMEM((6, n_new, LANES, b), F32), pltpu.VMEM((n_new, LANES, b), F32),
                        pltpu.VMEM((n_tok, LANES), F32)],
        input_output_aliases={} if whole else {10: 1},
        compiler_params=_cparams("arbitrary"),
        name="wkv7_lanes",
    )(*feats, P['lnx_g'], P['lnx_b'], s_lanes)
    return y, s_new


def _pair_kv(t):
    lane = lax.broadcasted_iota(jnp.int32, (1, LANES), 1)
    even = lane < HEAD_DIM
    sw = pltpu.roll(t, HEAD_DIM, 1)
    return jnp.where(even, t, sw), jnp.where(even, sw, t)


def _attend(qs, kcs, vcs, mask, sink_cols):
    n = range(len(qs))
    s = [_dot_nt(qs[i], kcs[i]) * ATTN_SCALE for i in n]
    s = [jnp.where(mask, s[i], -jnp.inf) for i in n]
    m = [jnp.maximum(jnp.max(s[i], axis=-1, keepdims=True), sink_cols[i]) for i in n]
    p = [jnp.exp(s[i] - m[i]) for i in n]
    den = [jnp.sum(p[i], axis=-1, keepdims=True) + jnp.exp(sink_cols[i] - m[i]) for i in n]
    o = [_dot(p[i], vcs[i]) for i in n]
    return [o[i] / den[i] for i in n]


def _attn_prompt_kernel(sink_ref, q_ref, kp_ref, kc_ref, vp_ref, vc_ref, o_ref, *, layer):
    n = pl.program_id(1)
    sink = lambda head: sink_ref[layer * A_HEADS + head]
    lane = lax.broadcasted_iota(jnp.int32, (1, LANES), 1)
    even = lane < HEAD_DIM
    kcat = jnp.concatenate([kp_ref[...], kc_ref[...]], axis=0)
    vcat = jnp.concatenate([vp_ref[...], vc_ref[...]], axis=0)
    ks = _pair_kv(kcat)
    vs = _pair_kv(vcat)
    qi = _pmod(lax.broadcasted_iota(jnp.int32, (2 * BLOCK, 2 * BLOCK), 0), BLOCK) + BLOCK
    kj = lax.broadcasted_iota(jnp.int32, (2 * BLOCK, 2 * BLOCK), 1)
    dist = qi - kj
    mask = (dist >= 0) & (dist <= WINDOW) & ((n > 0) | (kj >= BLOCK))
    top = lax.broadcasted_iota(jnp.int32, (2 * BLOCK, 1), 0) < BLOCK
    blocks = range(Q_WIDTH // LANES)
    qs, sink_cols = [], []
    for j in blocks:
        qb = q_ref[:, j * LANES:(j + 1) * LANES]
        qs.append(jnp.concatenate([jnp.where(even, qb, 0.0), jnp.where(even, 0.0, qb)], axis=0))
        sink_cols.append(jnp.where(top, sink(2 * j), sink(2 * j + 1)))
    kvh = [(2 * j) // GROUP for j in blocks]
    o = _attend(qs, [ks[h] for h in kvh], [vs[h] for h in kvh], mask, sink_cols)
    for j in blocks:
        o_ref[:, j * LANES:(j + 1) * LANES] = jnp.where(even, o[j][:BLOCK], o[j][BLOCK:]).astype(o_ref.dtype)


def _attn_prompt(q, k, v, sinks, layer, batch, seq):
    nb = seq // BLOCK
    cur = lambda b, n: (b * nb + n, 0)
    prv = lambda b, n: (b * nb + jnp.maximum(n - 1, 0), 0)
    kv = lambda f: pl.BlockSpec((BLOCK, KV_WIDTH), f)
    return pl.pallas_call(
        functools.partial(_attn_prompt_kernel, layer=layer),
        grid=(batch, nb),
        in_specs=[pl.BlockSpec(memory_space=pltpu.SMEM), pl.BlockSpec((BLOCK, Q_WIDTH), cur),
                  kv(prv), kv(cur), kv(prv), kv(cur)],
        out_specs=pl.BlockSpec((BLOCK, Q_WIDTH), cur),
        out_shape=jax.ShapeDtypeStruct((batch * seq, Q_WIDTH), BF16),
        compiler_params=_cparams("parallel", "parallel"),
        name="attn_prompt",
    )(sinks, q, k, k, v, v)


def _attn_sample_kernel(sink_ref, q_ref, kt_ref, kn_ref, vt_ref, vn_ref, o_ref, kt_out_ref, vt_out_ref,
                        *, layer, n_seq, wbuf, n_new):
    assert wbuf == LANES
    lane = lax.broadcasted_iota(jnp.int32, (1, LANES), 1)
    even = lane < HEAD_DIM
    rows = GROUP * SAMPLE_PAD
    n_pad = 2 * SAMPLE_PAD
    qt = _pmod(lax.broadcasted_iota(jnp.int32, (rows, 1), 0), SAMPLE_PAD)
    dist_c = qt + wbuf - lax.broadcasted_iota(jnp.int32, (rows, wbuf), 1)
    dist_n = qt - lax.broadcasted_iota(jnp.int32, (rows, n_pad), 1)
    mask_c = (dist_c >= 0) & (dist_c <= WINDOW)
    mask_n = (dist_n >= 0) & (dist_n <= WINDOW)
    rid = _pdiv(lax.broadcasted_iota(jnp.int32, (rows, 1), 0), SAMPLE_PAD)
    sink_cols = []
    for h in range(KV_HEADS):
        col = jnp.zeros((rows, 1), F32)
        for gq in range(GROUP):
            col = jnp.where(rid == gq, sink_ref[layer * A_HEADS + h * GROUP + gq], col)
        sink_cols.append(col)
    unroll = 2
    pad = jnp.zeros((SAMPLE_PAD, KV_WIDTH), F32)
    blocks_per_kv = GROUP // 2
    keep = lane < wbuf - n_new
    sub = lax.broadcasted_iota(jnp.int32, (SAMPLE_PAD, 1), 0)

    def new_columns(t_new):
        low = jnp.where(sub >= SAMPLE_PAD - n_new, pltpu.roll(t_new, SAMPLE_PAD - n_new, 0), 0.0)
        return jnp.concatenate([jnp.zeros((wbuf - SAMPLE_PAD, KV_WIDTH), F32), low], axis=0).T

    def body(i, carry_):
        items = [(i * unroll + u, h) for u in range(unroll) for h in range(KV_HEADS)]
        n = range(len(items))
        qs, kc, vc, kn, vn = [], [], [], [], []
        for u in range(unroll):
            g = i * unroll + u
            kn2 = _pair_kv(jnp.concatenate([kn_ref[g], pad], axis=0))
            vn2 = _pair_kv(jnp.concatenate([vn_ref[g], pad], axis=0))
            for h in range(KV_HEADS):
                parts = []
                for j in range(h * blocks_per_kv, (h + 1) * blocks_per_kv):
                    qb = q_ref[g, :, j * LANES:(j + 1) * LANES] * ATTN_SCALE
                    parts += [jnp.where(even, qb, 0.0), jnp.where(even, 0.0, qb)]
                qs.append(jnp.concatenate(parts, axis=0))
                kt, vt = kt_ref[g, h], vt_ref[g, h]
                kc.append(jnp.concatenate([kt, kt], axis=0))
                vc.append(jnp.concatenate([vt, vt], axis=0))
                kn.append(kn2[h])
                vn.append(vn2[h])
        s_c = [jnp.where(mask_c, _dot(qs[j], kc[j]), -jnp.inf) for j in n]
        s_n = [jnp.where(mask_n, _dot_nt(qs[j], kn[j]), -jnp.inf) for j in n]
        m = [jnp.maximum(jnp.maximum(jnp.max(s_c[j], axis=-1, keepdims=True),
                                     jnp.max(s_n[j], axis=-1, keepdims=True)), sink_cols[items[j][1]])
             for j in n]
        p_c = [jnp.exp(s_c[j] - m[j]) for j in n]
        p_n = [jnp.exp(s_n[j] - m[j]) for j in n]
        den = [jnp.sum(p_c[j], axis=-1, keepdims=True) + jnp.sum(p_n[j], axis=-1, keepdims=True)
               + jnp.exp(sink_cols[items[j][1]] - m[j]) for j in n]
        o = [(_dot_nt(p_c[j], vc[j]) + _dot(p_n[j], vn[j])) / den[j] for j in n]
        for j, (g, h) in enumerate(items):
            for jj in range(blocks_per_kv):
                blk = h * blocks_per_kv + jj
                r0 = 2 * jj * SAMPLE_PAD
                o_ref[g, :, blk * LANES:(blk + 1) * LANES] = jnp.where(
                    even, o[j][r0:r0 + SAMPLE_PAD], o[j][r0 + SAMPLE_PAD:r0 + 2 * SAMPLE_PAD]).astype(o_ref.dtype)
        for u in range(unroll):
            g = i * unroll + u
            k_cols, v_cols = new_columns(kn_ref[g]), new_columns(vn_ref[g])
            for h in range(KV_HEADS):
                hs = slice(h * HEAD_DIM, (h + 1) * HEAD_DIM)
                kt_out_ref[g, h] = jnp.where(keep, pltpu.roll(kt_ref[g, h], wbuf - n_new, 1), k_cols[hs])
                vt_out_ref[g, h] = jnp.where(keep, pltpu.roll(vt_ref[g, h], wbuf - n_new, 1), v_cols[hs])
        return carry_

    lax.fori_loop(0, n_seq // unroll, body, 0)


def _attn_sample(q, k, v, kt_all, vt_all, sinks, layer, n_seq, n_new):
    _, b, _, _, wbuf = kt_all.shape
    idx = lambda i: (i, 0, 0)
    new = pl.BlockSpec((n_seq, SAMPLE_PAD, KV_WIDTH), idx)
    buf = pl.BlockSpec((None, n_seq, KV_HEADS, HEAD_DIM, wbuf), lambda i: (layer, i, 0, 0, 0))
    win = pl.BlockSpec((n_seq, KV_HEADS, HEAD_DIM, wbuf), lambda i: (i, 0, 0, 0))
    win_shape = jax.ShapeDtypeStruct((b, KV_HEADS, HEAD_DIM, wbuf), F32)
    return pl.pallas_call(
        functools.partial(_attn_sample_kernel, layer=layer, n_seq=n_seq, wbuf=wbuf, n_new=n_new),
        grid=(b // n_seq,),
        in_specs=[pl.BlockSpec(memory_space=pltpu.SMEM), pl.BlockSpec((n_seq, SAMPLE_PAD, Q_WIDTH), idx),
                  buf, new, buf, new],
        out_specs=[pl.BlockSpec((n_seq, SAMPLE_PAD, Q_WIDTH), idx), win, win],
        out_shape=[jax.ShapeDtypeStruct((b, SAMPLE_PAD, Q_WIDTH), BF16), win_shape, win_shape],
        compiler_params=_cparams("parallel"),
        name="attn_sample",
    )(sinks, q, kt_all, k, vt_all, v)


def _layer_norm(x, g, b):
    mu = jnp.mean(x, axis=-1, keepdims=True)
    d = x - mu
    var = jnp.mean(d * d, axis=-1, keepdims=True)
    return d * lax.rsqrt(var + LN_EPS) * g + b


def _mix_ffn_kernel(x_ref, yr_ref, ya_ref, win_ref, wbr_ref, wba_ref, wo_ref, g1_ref, b1_ref,
                    wu_ref, wd_ref, g2_ref, b2_ref, o_ref, *, layer, ff_chunk):
    par = lambda ref: ref[layer:layer + 1, :]
    x = x_ref[...]
    xb = x.astype(BF16)
    gate_r = jax.nn.sigmoid(jnp.dot(xb, win_ref[:, QKV_END:QKV_END + D_MODEL], preferred_element_type=F32))
    mix = gate_r * jnp.dot(yr_ref[...], wbr_ref[...], preferred_element_type=F32)
    gate_a = jax.nn.sigmoid(jnp.dot(xb, win_ref[:, QKV_END + D_MODEL:], preferred_element_type=F32))
    mix = mix + gate_a * jnp.dot(ya_ref[...], wba_ref[...], preferred_element_type=F32)
    x1 = _layer_norm(ALPHA * x + _dot(mix, wo_ref[...]), par(g1_ref), par(b1_ref))
    x1b = x1.astype(BF16)
    acc = ALPHA * x1
    for c in range(D_FF // ff_chunk):
        cs = slice(c * ff_chunk, (c + 1) * ff_chunk)
        h = jnp.maximum(jnp.dot(x1b, wu_ref[:, cs], preferred_element_type=F32), 0.0)
        acc = acc + _dot(h * h, wd_ref[cs, :])
    o_ref[...] = _layer_norm(acc, par(g2_ref), par(b2_ref))


def _mix_ffn(x, yr, ya, P, layer, tm):
    n = x.shape[0]
    row = lambda i: (i, 0)
    ws = [P['w_in'], P['w_br_rwkv'], P['w_br_attn'], P['w_out'], P['ln1_g'], P['ln1_b'],
          P['w_ff_up'], P['w_ff_down'], P['ln2_g'], P['ln2_b']]
    wspec = lambda a: _layer_spec(a, layer, single_buffer=True) if a.ndim == 3 else _const_spec(a.shape)
    return pl.pallas_call(
        functools.partial(_mix_ffn_kernel, layer=layer, ff_chunk=1024),
        grid=(n // tm,),
        in_specs=[pl.BlockSpec((tm, D_MODEL), row), pl.BlockSpec((tm, R_WIDTH), row),
                  pl.BlockSpec((tm, Q_WIDTH), row)] + [wspec(a) for a in ws],
        out_specs=pl.BlockSpec((tm, D_MODEL), row),
        out_shape=jax.ShapeDtypeStruct((n, D_MODEL), F32),
        compiler_params=_cparams("parallel"),
        name="mix_ffn",
    )(x, yr, ya, *ws)


def _prepare_params(w_in, mu_shift, decay_base, decay_up, iclr_base, iclr_up, gate_up, k_k, k_a, r_k,
                    lnx_g, lnx_b, sinks, w_br_rwkv, w_br_attn, w_out, ln1_g, ln1_b, w_ff_up, w_ff_down,
                    ln2_g, ln2_b):
    depth = w_in.shape[0]
    zeros = jnp.zeros((depth, DECAY_LORA, R_WIDTH), F32)
    lora_up = jnp.concatenate([jnp.concatenate([decay_up, zeros], axis=2),
                               jnp.concatenate([zeros, iclr_up], axis=2)], axis=1)
    bf = lambda a: a.astype(BF16)
    return dict(
        w_in=bf(w_in), mu_shift=mu_shift, decay_base=decay_base, lora_up=bf(lora_up), iclr_base=iclr_base,
        gate_up=bf(gate_up), k_k=k_k, k_a=k_a, r_k=r_k.reshape(depth, R_WIDTH), lnx_g=lnx_g, lnx_b=lnx_b,
        sinks=sinks.reshape(depth * A_HEADS), w_br_rwkv=bf(w_br_rwkv), w_br_attn=bf(w_br_attn),
        w_out=bf(w_out), ln1_g=ln1_g, ln1_b=ln1_b, w_ff_up=bf(w_ff_up), w_ff_down=bf(w_ff_down),
        ln2_g=ln2_g, ln2_b=ln2_b)


def _prompt_layer(x, P, layer, tables, s_all, batch, seq, wbuf):
    tm = 512
    zr, q, k, v = _inproj(x, P['w_in'], layer, tables, seq // tm, tm)
    shift0 = jnp.zeros((batch, 1, SHIFT_W), F32)
    wkv_rows = tm // batch
    yr, s_all = _wkv(zr.reshape(batch, seq, SHIFT_W), shift0, s_all, P, layer, seq_rows=wkv_rows)
    ya = _attn_prompt(q, k, v, P['sinks'], layer, batch, seq)
    x = _mix_ffn(x, yr.reshape(batch * seq, R_WIDTH), ya, P, layer, tm)
    tail = lambda t, w: t.reshape(batch, seq, w)[:, seq - wbuf:].reshape(batch, wbuf, KV_HEADS, HEAD_DIM)
    return x, s_all, zr.reshape(batch, seq, SHIFT_W)[:, -1], tail(k, KV_WIDTH), tail(v, KV_WIDTH)


def _sample_layer(x, P, layer, tables, shift_prev, s_all, kt_all, vt_all, batch, seq):
    n = batch * SAMPLE_PAD
    tm = min(n, 512)
    n_seq = 16
    zr, q, k, v = _inproj(x, P['w_in'], layer, tables, 1, tm)
    seq3 = lambda t, w: t.reshape(batch, SAMPLE_PAD, w)
    yr, s_all = _wkv_short(seq3(zr, SHIFT_W), shift_prev[:, None, :], s_all, P, layer, seq)
    ya, kt_new, vt_new = _attn_sample(seq3(q, Q_WIDTH), seq3(k, KV_WIDTH), seq3(v, KV_WIDTH), kt_all, vt_all,
                                      P['sinks'], layer, n_seq, seq)
    x = _mix_ffn(x, yr.reshape(n, R_WIDTH), ya.reshape(n, Q_WIDTH), P, layer, tm)
    return x, s_all, seq3(zr, SHIFT_W)[:, seq - 1], kt_new, vt_new


def kernel(x_prompt, x_sample, state_wkv, state_shift, cache_k_win, cache_v_win, w_in, mu_shift, decay_base, decay_up, iclr_base, iclr_up, gate_up, k_k, k_a, r_k, lnx_g, lnx_b, sinks, w_br_rwkv, w_br_attn, w_out, ln1_g, ln1_b, w_ff_up, w_ff_down, ln2_g, ln2_b):
    bp, tp, _ = x_prompt.shape
    bs, ts, _ = x_sample.shape
    wbuf = cache_k_win.shape[2]
    half = HEAD_DIM // 2
    inv_freq = ROPE_THETA ** (-jnp.arange(half, dtype=F32) / half)
    tab_p = _rope_tables(inv_freq, tp, 0, tp)
    tab_s = _rope_tables(inv_freq, min(bs * SAMPLE_PAD, 512), PAST_LEN, SAMPLE_PAD)

    hp = x_prompt.reshape(bp * tp, D_MODEL)
    hs = jnp.pad(x_sample, ((0, 0), (0, SAMPLE_PAD - ts), (0, 0))).reshape(bs * SAMPLE_PAD, D_MODEL)
    outs_p, outs_s = [], []
    P = _prepare_params(w_in, mu_shift, decay_base, decay_up, iclr_base, iclr_up, gate_up, k_k, k_a, r_k,
                        lnx_g, lnx_b, sinks, w_br_rwkv, w_br_attn, w_out, ln1_g, ln1_b, w_ff_up,
                        w_ff_down, ln2_g, ln2_b)
    to_lanes = lambda c: jnp.transpose(c, (0, 1, 3, 4, 2))
    from_lanes = lambda c: jnp.transpose(c, (0, 1, 4, 2, 3))
    kt_all, vt_all = to_lanes(cache_k_win), to_lanes(cache_v_win)
    s_all_p = jnp.zeros((DEPTH, bp, R_HEADS, R_HEAD_DIM, R_HEAD_DIM), F32)
    s_all_s = jnp.transpose(state_wkv, (0, 2, 3, 4, 1))
    for l in range(DEPTH):
        hp, s_all_p, *st = _prompt_layer(hp, P, l, tab_p, s_all_p, bp, tp, wbuf)
        outs_p.append(st)
        hs, s_all_s, *st = _sample_layer(hs, P, l, tab_s, state_shift[l], s_all_s, kt_all, vt_all, bs, ts)
        outs_s.append(st)
    stack = lambda outs, i: jnp.stack([o[i] for o in outs])
    y_p = hp.reshape(bp, tp, D_MODEL)
    y_s = hs.reshape(bs, SAMPLE_PAD, D_MODEL)[:, :ts]
    return (y_p, y_s,
            s_all_p, stack(outs_p, 0), stack(outs_p, 1), stack(outs_p, 2),
            jnp.transpose(s_all_s, (0, 4, 1, 2, 3)), stack(outs_s, 0), from_lanes(stack(outs_s, 1)),
            from_lanes(stack(outs_s, 2)))
```

```python
import functools
import math

import jax
import jax.numpy as jnp
from jax import lax
from jax.experimental import pallas as pl
from jax.experimental.pallas import tpu as pltpu

F32 = jnp.float32
BF16 = jnp.bfloat16

D_MODEL = 1024
DEPTH = 2
PAST_LEN = 8192
R_HEADS = 8
R_HEAD_DIM = 64
R_WIDTH = R_HEADS * R_HEAD_DIM
DECAY_LORA = 64
ICLR_LORA = 64
GATE_LORA = 128
SHIFT_W = 3 * R_WIDTH + DECAY_LORA + ICLR_LORA + GATE_LORA
A_HEADS = 8
KV_HEADS = 2
HEAD_DIM = 64
Q_WIDTH = A_HEADS * HEAD_DIM
KV_WIDTH = KV_HEADS * HEAD_DIM
GROUP = A_HEADS // KV_HEADS
WINDOW = 128
BLOCK = 128
ROPE_THETA = 10000.0
ATTN_SCALE = HEAD_DIM ** -0.5
D_FF = 4 * D_MODEL
ALPHA = (2 * DEPTH) ** 0.25
LN_EPS = 1e-5
GN_EPS = 64e-5
QKV_END = SHIFT_W + Q_WIDTH + 2 * KV_WIDTH

LANES = 128
SUBLANES = 8
VMEM_LIMIT = 56 * 1024 * 1024

SAMPLE_PAD = SUBLANES
N_PAIRS = R_HEADS // 2
WKV_CHUNK = LANES // 2
SHIFT_BLOCK = 2 * LANES
assert SHIFT_W % SHIFT_BLOCK == 0
PHASE1_CHUNKS = 4


def _cparams(*sem):
    return pltpu.CompilerParams(dimension_semantics=sem, vmem_limit_bytes=VMEM_LIMIT)


def _const_spec(shape):
    nd = len(shape)
    return pl.BlockSpec(shape, lambda *_: (0,) * nd)


def _layer_spec(a, layer, single_buffer=False):
    mode = pl.Buffered(1) if single_buffer else None
    return pl.BlockSpec((None,) + a.shape[1:], lambda *_: (layer, 0, 0), pipeline_mode=mode)


def _dot(a, b):
    return jnp.dot(a.astype(BF16), b.astype(BF16), preferred_element_type=F32)


def _dot_nt(a, b):
    return lax.dot_general(a.astype(BF16), b.astype(BF16), (((1,), (1,)), ((), ())),
                           preferred_element_type=F32)


def _pmod(x, n):
    assert n & (n - 1) == 0
    return x & (n - 1)


def _pdiv(x, n):
    assert n & (n - 1) == 0
    return x >> (n.bit_length() - 1)


def _split(x):
    hi = x.astype(BF16)
    lo = (x - hi.astype(F32)).astype(BF16)
    return hi, lo


def _dot2_nt(a, b):
    bh, bl = _split(b)
    d = functools.partial(lax.dot_general, dimension_numbers=(((1,), (1,)), ((), ())),
                          preferred_element_type=F32)
    ab = a.astype(BF16)
    return d(ab, bh) + d(ab, bl)


def _rope_table_kernel(invf_ref, cos_ref, sa_ref, sb_ref, *, rows, offset, period):
    i = pl.program_id(0)
    row = lax.broadcasted_iota(jnp.int32, (rows, LANES), 0) + i * rows
    pos = offset + _pmod(row, period)
    ang = pos.astype(F32) * invf_ref[...]
    lane = lax.broadcasted_iota(jnp.int32, (rows, LANES), 1)
    first = _pmod(lane, HEAD_DIM) < (HEAD_DIM // 2)
    c = jnp.cos(ang)
    s = jnp.sin(ang)
    cos_ref[...] = c
    sa_ref[...] = jnp.where(first, -s, 0.0)
    sb_ref[...] = jnp.where(first, 0.0, s)


def _rope_tables(inv_freq, n_rows, offset, period):
    rows = min(n_rows, 1024)
    invf = jnp.tile(inv_freq, LANES // (HEAD_DIM // 2)).reshape(1, LANES)
    out = jax.ShapeDtypeStruct((n_rows, LANES), F32)
    return pl.pallas_call(
        functools.partial(_rope_table_kernel, rows=rows, offset=offset, period=period),
        grid=(n_rows // rows,),
        in_specs=[_const_spec((1, LANES))],
        out_specs=[pl.BlockSpec((rows, LANES), lambda i: (i, 0))] * 3,
        out_shape=[out] * 3,
        compiler_params=_cparams("parallel"),
        name="rope_tables",
    )(invf)


def _inproj_kernel(x_ref, w_ref, cos_ref, sa_ref, sb_ref, mu_ref, shift_ref,
                   zs_ref, zlast_ref, q_ref, k_ref, v_ref, carry, first_s, zraw_s,
                   *, layer, n_seq, seq_rows, last_row, tiles_per_seq):
    xb = x_ref[...].astype(BF16)
    tm = xb.shape[0]
    t_step = lax.rem(pl.program_id(0), tiles_per_seq)

    @pl.when(pl.program_id(0) == 0)
    def _():
        first_s[...] = jnp.zeros(first_s.shape, F32)

    @pl.when(t_step == 0)
    def _():
        carry[...] = shift_ref[0] if n_seq == 1 else shift_ref[...]
    lane_groups = SHIFT_BLOCK // LANES
    for c in range(SHIFT_W // LANES):
        if n_seq == 1:
            first_s[c, 0:1, :] = carry[:, c * LANES:(c + 1) * LANES]
        else:
            first_s[c, pl.ds(0, n_seq, stride=seq_rows), :] = carry[:, c * LANES:(c + 1) * LANES]
    is_first = _pmod(lax.broadcasted_iota(jnp.int32, (tm, 1), 0), seq_rows) == 0
    for j in range(SHIFT_W // SHIFT_BLOCK):
        cs = slice(j * SHIFT_BLOCK, (j + 1) * SHIFT_BLOCK)
        z = jnp.dot(xb, w_ref[:, cs], preferred_element_type=F32)
        first = jnp.concatenate([first_s[j * lane_groups + c] for c in range(lane_groups)], axis=1)
        prev = jnp.where(is_first, first, pltpu.roll(z, 1, 0))
        zs_ref[:, cs] = z + (prev - z) * mu_ref[layer:layer + 1, cs]
        if n_seq == 1:
            carry[:, cs] = z[last_row:last_row + 1, :]
        else:
            for c in range(lane_groups):
                zraw_s[c] = z[:, c * LANES:(c + 1) * LANES]
                carry[:, cs.start + c * LANES:cs.start + (c + 1) * LANES] = zraw_s[
                    c, pl.ds(last_row, n_seq, stride=seq_rows), :]
    if n_seq == 1:
        zlast_ref[0] = carry[...]
    else:
        zlast_ref[...] = carry[...]
    cos, sa, sb = cos_ref[...], sa_ref[...], sb_ref[...]

    def rope(t):
        return (t * cos + pltpu.roll(t, LANES - HEAD_DIM // 2, 1) * sa
                + pltpu.roll(t, HEAD_DIM // 2, 1) * sb)

    zq = jnp.dot(xb, w_ref[:, SHIFT_W:SHIFT_W + Q_WIDTH], preferred_element_type=F32)
    for j in range(Q_WIDTH // LANES):
        q_ref[:, j * LANES:(j + 1) * LANES] = rope(zq[:, j * LANES:(j + 1) * LANES])
    zkv = jnp.dot(xb, w_ref[:, SHIFT_W + Q_WIDTH:QKV_END], preferred_element_type=F32)
    k_ref[...] = rope(zkv[:, :KV_WIDTH])
    v_ref[...] = zkv[:, KV_WIDTH:]


def _inproj(x, P, layer, tables, tab_blocks, tm, shift_in, *, seq_rows, last_row):
    n = x.shape[0]
    b = shift_in.shape[0]
    n_seq = tm // seq_rows
    if n_seq == 1:
        tiles_per_seq = (n // b) // tm
        shift_in = shift_in.reshape(b, 1, SHIFT_W)
        seq_spec = pl.BlockSpec((1, 1, SHIFT_W), lambda i: (i // tiles_per_seq, 0, 0))
    else:
        tiles_per_seq = 1
        seq_spec = pl.BlockSpec((n_seq, SHIFT_W), lambda i: (i, 0))
    row = lambda i: (i, 0)
    tab = pl.BlockSpec((tm, LANES), lambda i: (lax.rem(i, tab_blocks), 0))
    w_spec = pl.BlockSpec((None, D_MODEL, QKV_END), lambda i: (layer, 0, 0))
    return pl.pallas_call(
        functools.partial(_inproj_kernel, layer=layer, n_seq=n_seq, seq_rows=seq_rows, last_row=last_row,
                          tiles_per_seq=tiles_per_seq),
        grid=(n // tm,),
        in_specs=[pl.BlockSpec((tm, D_MODEL), row), w_spec, tab, tab, tab, _const_spec(P['mu_shift'].shape),
                  seq_spec],
        out_specs=[pl.BlockSpec((tm, SHIFT_W), row), seq_spec, pl.BlockSpec((tm, Q_WIDTH), row),
                   pl.BlockSpec((tm, KV_WIDTH), row), pl.BlockSpec((tm, KV_WIDTH), row)],
        out_shape=[jax.ShapeDtypeStruct((n, SHIFT_W), F32), jax.ShapeDtypeStruct(shift_in.shape, F32),
                   jax.ShapeDtypeStruct((n, Q_WIDTH), F32),
                   jax.ShapeDtypeStruct((n, KV_WIDTH), F32), jax.ShapeDtypeStruct((n, KV_WIDTH), F32)],
        scratch_shapes=[pltpu.VMEM((n_seq, SHIFT_W), F32), pltpu.VMEM((SHIFT_W // LANES, tm, LANES), F32),
                        pltpu.VMEM((SHIFT_BLOCK // LANES, tm, LANES), F32)],
        compiler_params=_cparams("arbitrary"),
        name="inproj",
    )(x, P['w_in'], *tables, P['mu_shift'], shift_in)


def _pair_block_diag():
    bi = _pdiv(lax.broadcasted_iota(jnp.int32, (LANES, LANES), 0), R_HEAD_DIM)
    bj = _pdiv(lax.broadcasted_iota(jnp.int32, (LANES, LANES), 1), R_HEAD_DIM)
    return (bi == bj).astype(F32)


def _head_sum(x):
    ones = _pair_block_diag().astype(BF16)
    xb = x.astype(BF16)
    return jnp.concatenate([jnp.dot(xb[:, j * LANES:(j + 1) * LANES], ones, preferred_element_type=F32)
                            for j in range(x.shape[1] // LANES)], axis=1)


def _token_features(zs, decay_base, lora_up, iclr_base, gate_up, k_k, k_a, r_k):
    o1, o2, o3 = R_WIDTH, 2 * R_WIDTH, 3 * R_WIDTH
    o4 = o3 + DECAY_LORA + ICLR_LORA
    r = zs[:, :o1]
    k = zs[:, o1:o2]
    v = zs[:, o2:o3]
    wa = zs[:, o3:o4]
    gd = zs[:, o4:]
    lane = lax.broadcasted_iota(jnp.int32, (1, LANES), 1)
    lora_in = jnp.where(lane < DECAY_LORA, jnp.tanh(wa), wa)
    lora = _dot(lora_in, lora_up)
    pre_w = decay_base + lora[:, :R_WIDTH]
    logw = -math.exp(-0.5) * jax.nn.sigmoid(pre_w)
    a = jax.nn.sigmoid(iclr_base + lora[:, R_WIDTH:])
    g = _dot(jax.nn.sigmoid(gd), gate_up)
    kk = k * k_k
    kk = kk * lax.rsqrt(jnp.maximum(_head_sum(kk * kk), 1e-24))
    kp = k * (1.0 + (a - 1.0) * k_a)
    bonus = _head_sum(r * kp * r_k) * v
    return r, kp, v, kk, a, logw, g, bonus


def _group_norm_out(y, bonus, g, lnx_g, lnx_b):
    inv_n = 1.0 / R_HEAD_DIM
    d = y - _head_sum(y) * inv_n
    var = _head_sum(d * d) * inv_n
    return (d * lax.rsqrt(var + GN_EPS) * lnx_g + lnx_b + bonus) * g


def _wkv_kernel(zs_ref, sin_ref, db_ref, lora_ref, ib_ref, gu_ref, kk_ref, ka_ref,
                rk_ref, lg_ref, lb_ref,
                y_ref, sout_ref,
                sbd, r_s, k_s, v_s, kk_s, b_s, lw_s, cw_s, y_s, bon_s, g_s, xg_s, uy_s, bk_s,
                *, layer, n_seq, seq_rows, chunk):
    t_step = pl.program_id(1)
    n_steps = pl.num_programs(1)
    rows = n_seq * seq_rows
    n_chunks = rows // chunk
    c2 = 2 * chunk
    par = lambda ref: ref[layer:layer + 1, :]

    zs = zs_ref[...].reshape(rows, SHIFT_W)
    r, kp, v, kk, a, logw, g, bonus = _token_features(
        zs, par(db_ref), lora_ref[...], par(ib_ref), gu_ref[...], par(kk_ref), par(ka_ref), par(rk_ref))
    g_s[...] = g
    bon_s[...] = bonus
    r_s[...] = r
    k_s[...] = kp
    v_s[...] = v
    kk_s[...] = kk
    b_s[...] = kk * a
    lw_s[...] = logw

    ci = lax.broadcasted_iota(jnp.int32, (chunk, chunk), 0)
    cj = lax.broadcasted_iota(jnp.int32, (chunk, chunk), 1)
    tril_ones = (cj <= ci).astype(BF16)
    lw_hi, lw_lo = _split(logw)
    for c in range(n_chunks):
        cr = slice(c * chunk, (c + 1) * chunk)
        cw_s[cr, :] = (jnp.dot(tril_ones, lw_hi[cr], preferred_element_type=F32)
                       + jnp.dot(tril_ones, lw_lo[cr], preferred_element_type=F32))

    lane = lax.broadcasted_iota(jnp.int32, (1, LANES), 1)
    left = (lane < R_HEAD_DIM).astype(F32)
    right = 1.0 - left
    head_lanes = (left, right)
    m_bd = _pair_block_diag()
    ti = lax.broadcasted_iota(jnp.int32, (chunk, LANES), 0)
    tj = _pmod(lax.broadcasted_iota(jnp.int32, (chunk, LANES), 1), chunk)
    tri_strict = (tj < ti).astype(F32)
    tri_incl = (tj <= ti).astype(F32)
    eye = (tj == ti).astype(F32)
    is_left = lax.broadcasted_iota(jnp.int32, (chunk, LANES), 1) < chunk
    n_fact = int(math.log2(chunk))

    def load_state(u):
        for p in range(N_PAIRS):
            zero = jnp.zeros((R_HEAD_DIM, R_HEAD_DIM), F32)
            sbd[u * N_PAIRS + p, 0:R_HEAD_DIM, :] = jnp.concatenate([sin_ref[u, 2 * p], zero], axis=1)
            sbd[u * N_PAIRS + p, R_HEAD_DIM:LANES, :] = jnp.concatenate([zero, sin_ref[u, 2 * p + 1]], axis=1)

    def store_state(u):
        for p in range(N_PAIRS):
            sout_ref[u, 2 * p] = sbd[u * N_PAIRS + p, 0:R_HEAD_DIM, 0:R_HEAD_DIM]
            sout_ref[u, 2 * p + 1] = sbd[u * N_PAIRS + p, R_HEAD_DIM:LANES, R_HEAD_DIM:LANES]

    @pl.when(t_step == 0)
    def _():
        for u in range(n_seq):
            load_state(u)

    pairs = range(N_PAIRS)
    col = lambda p: slice(p * LANES, (p + 1) * LANES)
    stack2 = lambda t: jnp.concatenate([t * left, t * right], axis=0)
    half = lambda t: t[:, :chunk]

    def phase1(i, carry_):
        groups = [(i * PHASE1_CHUNKS + u, p) for u in range(PHASE1_CHUNKS) for p in pairs]
        at, rt, vv, bk = [], [], [], []
        for c, p in groups:
            rs = pl.ds(pl.multiple_of(c * chunk, chunk), chunk)
            cwc = cw_s[rs, col(p)]
            w_in = jnp.exp(cwc)
            w_ex = jnp.exp(cwc - lw_s[rs, col(p)])
            w_inv = jnp.exp(-cwc)
            at.append(-kk_s[rs, col(p)] * w_ex)
            rt.append(r_s[rs, col(p)] * w_in)
            vv.append(v_s[rs, col(p)])
            bk.append(jnp.concatenate([b_s[rs, col(p)] * w_inv, k_s[rs, col(p)] * w_inv], axis=0))
        items = [(gi, h) for gi in range(len(groups)) for h in range(2)]
        n = range(len(items))
        ats = [at[gi] * head_lanes[h] for gi, h in items]
        rts = [rt[gi] * head_lanes[h] for gi, h in items]
        sc = [_dot_nt(jnp.concatenate([ats[j], rts[j]], axis=0), bk[items[j][0]]) for j in n]
        top = [sc[j][:chunk] * tri_strict for j in n]
        bot = [sc[j][chunk:] * tri_incl for j in n]
        top_sw = [pltpu.roll(top[j], chunk, 1) for j in n]
        bot_sw = [pltpu.roll(bot[j], chunk, 1) for j in n]

        pt = [jnp.where(is_left, top[j], eye) for j in n]
        for _ in range(n_fact - 1):
            z = [_dot(half(pt[j]), pt[j]) for j in n]
            pt = [jnp.where(is_left, z[j], pt[j] + z[j]) for j in n]
        pt = [pt[j] + _dot(half(pt[j]), pt[j]) for j in n]
        tm = [half(pltpu.roll(pt[j], chunk, 1)) for j in n]

        ta = [_dot(tm[j], jnp.concatenate([ats[j], top_sw[j]], axis=1)) for j in n]
        ab = [_dot(half(bot[j]), ta[j]) for j in n]
        yk = [ab[j][:, LANES:] + bot_sw[j] for j in n]
        uy = [_dot(jnp.concatenate([half(ta[j][:, LANES:]), half(yk[j])], axis=0),
                   vv[items[j][0]] * head_lanes[items[j][1]]) for j in n]
        for j, (gi, h) in enumerate(items):
            c, p = groups[gi]
            idx = c * N_PAIRS + p
            xg_s[idx, h * chunk:(h + 1) * chunk, :] = ta[j][:, :LANES]
            xg_s[idx, c2 + h * chunk:c2 + (h + 1) * chunk, :] = rts[j] + ab[j][:, :LANES]
            uy_s[idx, h * chunk:(h + 1) * chunk, :] = uy[j][:chunk]
            uy_s[idx, c2 + h * chunk:c2 + (h + 1) * chunk, :] = uy[j][chunk:]
        for gi, (c, p) in enumerate(groups):
            bk_s[c * N_PAIRS + p] = bk[gi]
        return carry_

    lax.fori_loop(0, n_chunks // PHASE1_CHUNKS, phase1, 0)

    chunks_per_seq = seq_rows // chunk

    def phase2(i, carry_):
        cks = [u * chunks_per_seq + i for u in range(n_seq)]
        items = [(u, p) for u in range(n_seq) for p in pairs]
        n = range(len(items))
        tok = [pl.ds(pl.multiple_of(ck * chunk, chunk), chunk) for ck in cks]
        tail = [pl.ds(pl.multiple_of(ck * chunk + chunk - SUBLANES, SUBLANES), SUBLANES) for ck in cks]
        s0 = [sbd[u * N_PAIRS + p] for u, p in items]
        uyv = [_dot_nt(xg_s[cks[u] * N_PAIRS + p], s0[j]) + uy_s[cks[u] * N_PAIRS + p]
               for j, (u, p) in enumerate(items)]
        uvt = [jnp.concatenate([uyv[j][:c2], stack2(v_s[tok[u], col(p)])], axis=0).T
               for j, (u, p) in enumerate(items)]
        rhs = []
        for u, p in items:
            bkc = bk_s[cks[u] * N_PAIRS + p]
            bt, kt = bkc[:chunk], bkc[chunk:]
            rhs.append(jnp.concatenate([bt, bt, kt, kt], axis=0))
        upd = [_dot(uvt[j], rhs[j]) for j in n]
        for j, (u, p) in enumerate(items):
            w_end = jnp.exp(cw_s[tail[u], col(p)][SUBLANES - 1:SUBLANES, :])
            sbd[u * N_PAIRS + p] = (s0[j] + upd[j] * m_bd) * w_end
            y_s[tok[u], col(p)] = uyv[j][c2:c2 + chunk] + uyv[j][c2 + chunk:]
        return carry_

    lax.fori_loop(0, chunks_per_seq, phase2, 0)

    @pl.when(t_step == n_steps - 1)
    def _():
        for u in range(n_seq):
            store_state(u)

    out = _group_norm_out(y_s[...], bon_s[...], g_s[...], par(lg_ref), par(lb_ref))
    y_ref[...] = out.reshape(n_seq, seq_rows, R_WIDTH).astype(y_ref.dtype)


def _wkv(zs, s_all, P, layer, *, seq_rows):
    b, t_len, _ = zs.shape
    n_seq = b
    rows = n_seq * seq_rows
    chunk = WKV_CHUNK
    n_pc = (rows // chunk) * N_PAIRS
    assert seq_rows % chunk == 0 and rows % (chunk * PHASE1_CHUNKS) == 0 and t_len % seq_rows == 0
    params = [P['decay_base'], P['lora_up'], P['iclr_base'], P['gate_up'], P['k_k'], P['k_a'],
              P['r_k'], P['lnx_g'], P['lnx_b']]
    param_specs = [_layer_spec(a, layer) if a.ndim == 3 else _const_spec(a.shape) for a in params]
    tok = lambda g, t: (g, t, 0)
    state_spec = pl.BlockSpec((None, n_seq, R_HEADS, R_HEAD_DIM, R_HEAD_DIM), lambda g, t: (layer, g, 0, 0, 0))
    tile = lambda w: pltpu.VMEM((rows, w), F32)
    return pl.pallas_call(
        functools.partial(_wkv_kernel, layer=layer, n_seq=n_seq, seq_rows=seq_rows, chunk=chunk),
        grid=(b // n_seq, t_len // seq_rows),
        in_specs=[pl.BlockSpec((n_seq, seq_rows, SHIFT_W), tok), state_spec] + param_specs,
        out_specs=[pl.BlockSpec((n_seq, seq_rows, R_WIDTH), tok), state_spec],
        out_shape=[jax.ShapeDtypeStruct((b, t_len, R_WIDTH), BF16), jax.ShapeDtypeStruct(s_all.shape, F32)],
        input_output_aliases={1: 1},
        scratch_shapes=[pltpu.VMEM((n_seq * N_PAIRS, LANES, LANES), F32)] + [tile(R_WIDTH)] * 10 + [
            pltpu.VMEM((n_pc, 4 * chunk, LANES), F32), pltpu.VMEM((n_pc, 4 * chunk, LANES), F32),
            pltpu.VMEM((n_pc, 2 * chunk, LANES), F32)],
        compiler_params=_cparams("parallel", "arbitrary"),
        name="wkv7",
    )(zs, s_all, *params)


def _wkv_prep_kernel(zs_ref, db_ref, lora_ref, ib_ref, gu_ref, kk_ref, ka_ref, rk_ref,
                     r_ref, k_ref, v_ref, kko_ref, b_ref, w_ref, g_ref, bon_ref, *, layer):
    par = lambda ref: ref[layer:layer + 1, :]
    r, kp, v, kk, a, logw, g, bonus = _token_features(
        zs_ref[...], par(db_ref), lora_ref[...], par(ib_ref), gu_ref[...], par(kk_ref), par(ka_ref), par(rk_ref))
    r_ref[...] = r
    k_ref[...] = kp
    v_ref[...] = v
    kko_ref[...] = kk
    b_ref[...] = kk * a
    w_ref[...] = jnp.exp(logw)
    g_ref[...] = g
    bon_ref[...] = bonus


LANE_GROUP = 4


def _wkv_lanes_kernel(r_ref, k_ref, v_ref, kk_ref, b_ref, w_ref, g_ref, bon_ref, lg_ref, lb_ref, s_ref,
                      y_ref, sout_ref, ft_s, yt_s, ytok_s, *, layer, n_new, whole_state):
    nb = ft_s.shape[-1]
    n = R_HEAD_DIM
    feats = (r_ref, k_ref, v_ref, kk_ref, b_ref, w_ref)
    for fi, ref in enumerate(feats):
        for t in range(n_new):
            ft_s[fi, t] = ref[pl.ds(t, nb, stride=SAMPLE_PAD), :].T
    load_s = (lambda hh, v: s_ref[layer, hh, v]) if whole_state else (lambda hh, v: s_ref[hh, v])

    def store_s(hh, v, val):
        if whole_state:
            sout_ref[layer, hh, v] = val
        else:
            sout_ref[hh, v] = val

    for hh in range(2):
        hs = slice(hh * n, (hh + 1) * n)

        def rows8(vg, carry_):
            v0 = pl.multiple_of(vg * SUBLANES, SUBLANES)
            vrows = [ft_s[2, t, pl.ds(hh * n + v0, SUBLANES), :] for t in range(n_new)]
            ys = [[None] * SUBLANES for _ in range(n_new)]
            for j0 in range(0, SUBLANES, LANE_GROUP):
                js = range(j0, j0 + LANE_GROUP)
                sv = {j: load_s(hh, v0 + j) for j in js}
                for t in range(n_new):
                    nk, w, bb = -ft_s[3, t, hs, :], ft_s[5, t, hs, :], ft_s[4, t, hs, :]
                    kt, rt = ft_s[1, t, hs, :], ft_s[0, t, hs, :]
                    sa = {j: jnp.sum(sv[j] * nk, axis=0, keepdims=True) for j in js}
                    sv = {j: sv[j] * w + sa[j] * bb + vrows[t][j:j + 1, :] * kt for j in js}
                    for j in js:
                        ys[t][j] = jnp.sum(sv[j] * rt, axis=0, keepdims=True)
                for j in js:
                    store_s(hh, v0 + j, sv[j])
            for t in range(n_new):
                yt_s[t, pl.ds(hh * n + v0, SUBLANES), :] = jnp.concatenate(ys[t], axis=0)
            return carry_

        lax.fori_loop(0, n // SUBLANES, rows8, 0)

    if whole_state:
        for l in range(s_ref.shape[0]):
            if l != layer:
                sout_ref[l] = s_ref[l]

    ytok_s[...] = jnp.zeros(ytok_s.shape, F32)
    for t in range(n_new):
        ytok_s[pl.ds(t, nb, stride=SAMPLE_PAD), :] = yt_s[t].T
    out = _group_norm_out(ytok_s[...], bon_ref[...], g_ref[...], lg_ref[layer:layer + 1, :],
                          lb_ref[layer:layer + 1, :])
    y_ref[...] = out.astype(y_ref.dtype)


def _wkv_short(zs, s_lanes, P, layer, n_new):
    n_tok = zs.shape[0]
    depth, b = s_lanes.shape[0], s_lanes.shape[-1]
    rows = 256
    params = [P['decay_base'], P['lora_up'], P['iclr_base'], P['gate_up'], P['k_k'], P['k_a'], P['r_k']]
    param_specs = [_layer_spec(a, layer) if a.ndim == 3 else _const_spec(a.shape) for a in params]
    tok_out = pl.BlockSpec((rows, R_WIDTH), lambda g: (g, 0))
    feats = pl.pallas_call(
        functools.partial(_wkv_prep_kernel, layer=layer),
        grid=(n_tok // rows,),
        in_specs=[pl.BlockSpec((rows, SHIFT_W), lambda g: (g, 0))] + param_specs,
        out_specs=[tok_out] * 8,
        out_shape=[jax.ShapeDtypeStruct((n_tok, R_WIDTH), F32)] * 8,
        compiler_params=_cparams("parallel"),
        name="wkv7_prep",
    )(zs, *params)

    whole = layer == 0
    pair_col = pl.BlockSpec((n_tok, LANES), lambda p: (0, p))
    par_col = pl.BlockSpec((depth, LANES), lambda p: (0, p))
    n = R_HEAD_DIM
    if whole:
        state_spec = pl.BlockSpec((depth, 2, n, n, b), lambda p: (0, p, 0, 0, 0))
    else:
        state_spec = pl.BlockSpec((None, 2, n, n, b), lambda p: (layer, p, 0, 0, 0))
    y, s_new = pl.pallas_call(
        functools.partial(_wkv_lanes_kernel, layer=layer, n_new=n_new, whole_state=whole),
        grid=(N_PAIRS,),
        in_specs=[pair_col] * 8 + [par_col, par_col, state_spec],
        out_specs=[pair_col, state_spec],
        out_shape=[jax.ShapeDtypeStruct((n_tok, R_WIDTH), BF16), jax.ShapeDtypeStruct(s_lanes.shape, F32)],
        scratch_shapes=[pltpu.VMEM((6, n_new, LANES, b), F32), pltpu.VMEM((n_new, LANES, b), F32),
                        pltpu.VMEM((n_tok, LANES), F32)],
        input_output_aliases={} if whole else {10: 1},
        compiler_params=_cparams("arbitrary"),
        name="wkv7_lanes",
    )(*feats, P['lnx_g'], P['lnx_b'], s_lanes)
    return y, s_new


def _pair_kv(t):
    lane = lax.broadcasted_iota(jnp.int32, (1, LANES), 1)
    even = lane < HEAD_DIM
    sw = pltpu.roll(t, HEAD_DIM, 1)
    return jnp.where(even, t, sw), jnp.where(even, sw, t)


def _attend(qs, kcs, vcs, mask, sink_cols):
    n = range(len(qs))
    s = [_dot_nt(qs[i], kcs[i]) * ATTN_SCALE for i in n]
    s = [jnp.where(mask, s[i], -jnp.inf) for i in n]
    m = [jnp.maximum(jnp.max(s[i], axis=-1, keepdims=True), sink_cols[i]) for i in n]
    p = [jnp.exp(s[i] - m[i]) for i in n]
    den = [jnp.sum(p[i], axis=-1, keepdims=True) + jnp.exp(sink_cols[i] - m[i]) for i in n]
    o = [_dot(p[i], vcs[i]) for i in n]
    return [o[i] / den[i] for i in n]


def _attn_prompt_kernel(sink_ref, q_ref, kp_ref, kc_ref, vp_ref, vc_ref, o_ref, *, layer):
    n = pl.program_id(1)
    sink = lambda head: sink_ref[layer * A_HEADS + head]
    lane = lax.broadcasted_iota(jnp.int32, (1, LANES), 1)
    even = lane < HEAD_DIM
    kcat = jnp.concatenate([kp_ref[...], kc_ref[...]], axis=0)
    vcat = jnp.concatenate([vp_ref[...], vc_ref[...]], axis=0)
    ks = _pair_kv(kcat)
    vs = _pair_kv(vcat)
    qi = _pmod(lax.broadcasted_iota(jnp.int32, (2 * BLOCK, 2 * BLOCK), 0), BLOCK) + BLOCK
    kj = lax.broadcasted_iota(jnp.int32, (2 * BLOCK, 2 * BLOCK), 1)
    dist = qi - kj
    mask = (dist >= 0) & (dist <= WINDOW) & ((n > 0) | (kj >= BLOCK))
    top = lax.broadcasted_iota(jnp.int32, (2 * BLOCK, 1), 0) < BLOCK
    blocks = range(Q_WIDTH // LANES)
    qs, sink_cols = [], []
    for j in blocks:
        qb = q_ref[:, j * LANES:(j + 1) * LANES]
        qs.append(jnp.concatenate([jnp.where(even, qb, 0.0), jnp.where(even, 0.0, qb)], axis=0))
        sink_cols.append(jnp.where(top, sink(2 * j), sink(2 * j + 1)))
    kvh = [(2 * j) // GROUP for j in blocks]
    o = _attend(qs, [ks[h] for h in kvh], [vs[h] for h in kvh], mask, sink_cols)
    for j in blocks:
        o_ref[:, j * LANES:(j + 1) * LANES] = jnp.where(even, o[j][:BLOCK], o[j][BLOCK:]).astype(o_ref.dtype)


def _attn_prompt(q, k, v, sinks, layer, batch, seq):
    nb = seq // BLOCK
    cur = lambda b, n: (b * nb + n, 0)
    prv = lambda b, n: (b * nb + jnp.maximum(n - 1, 0), 0)
    kv = lambda f: pl.BlockSpec((BLOCK, KV_WIDTH), f)
    return pl.pallas_call(
        functools.partial(_attn_prompt_kernel, layer=layer),
        grid=(batch, nb),
        in_specs=[pl.BlockSpec(memory_space=pltpu.SMEM), pl.BlockSpec((BLOCK, Q_WIDTH), cur),
                  kv(prv), kv(cur), kv(prv), kv(cur)],
        out_specs=pl.BlockSpec((BLOCK, Q_WIDTH), cur),
        out_shape=jax.ShapeDtypeStruct((batch * seq, Q_WIDTH), BF16),
        compiler_params=_cparams("parallel", "parallel"),
        name="attn_prompt",
    )(sinks, q, k, k, v, v)


def _attn_sample_kernel(sink_ref, q_ref, kt_ref, kn_ref, vt_ref, vn_ref, o_ref, kt_out_ref, vt_out_ref,
                        *, layer, n_seq, wbuf, n_new):
    assert wbuf == LANES
    lane = lax.broadcasted_iota(jnp.int32, (1, LANES), 1)
    even = lane < HEAD_DIM
    rows = GROUP * SAMPLE_PAD
    n_pad = 2 * SAMPLE_PAD
    qt = _pmod(lax.broadcasted_iota(jnp.int32, (rows, 1), 0), SAMPLE_PAD)
    dist_c = qt + wbuf - lax.broadcasted_iota(jnp.int32, (rows, wbuf), 1)
    dist_n = qt - lax.broadcasted_iota(jnp.int32, (rows, n_pad), 1)
    mask_c = (dist_c >= 0) & (dist_c <= WINDOW)
    mask_n = (dist_n >= 0) & (dist_n <= WINDOW)
    rid = _pdiv(lax.broadcasted_iota(jnp.int32, (rows, 1), 0), SAMPLE_PAD)
    sink_cols = []
    for h in range(KV_HEADS):
        col = jnp.zeros((rows, 1), F32)
        for gq in range(GROUP):
            col = jnp.where(rid == gq, sink_ref[layer * A_HEADS + h * GROUP + gq], col)
        sink_cols.append(col)
    unroll = 2
    pad = jnp.zeros((SAMPLE_PAD, KV_WIDTH), F32)
    blocks_per_kv = GROUP // 2
    keep = lane < wbuf - n_new
    sub = lax.broadcasted_iota(jnp.int32, (SAMPLE_PAD, 1), 0)

    def new_columns(t_new):
        low = jnp.where(sub >= SAMPLE_PAD - n_new, pltpu.roll(t_new, SAMPLE_PAD - n_new, 0), 0.0)
        return jnp.concatenate([jnp.zeros((wbuf - SAMPLE_PAD, KV_WIDTH), F32), low], axis=0).T

    def body(i, carry_):
        items = [(i * unroll + u, h) for u in range(unroll) for h in range(KV_HEADS)]
        n = range(len(items))
        qs, kc, vc, kn, vn = [], [], [], [], []
        for u in range(unroll):
            g = i * unroll + u
            kn2 = _pair_kv(jnp.concatenate([kn_ref[g], pad], axis=0))
            vn2 = _pair_kv(jnp.concatenate([vn_ref[g], pad], axis=0))
            for h in range(KV_HEADS):
                parts = []
                for j in range(h * blocks_per_kv, (h + 1) * blocks_per_kv):
                    qb = q_ref[g, :, j * LANES:(j + 1) * LANES] * ATTN_SCALE
                    parts += [jnp.where(even, qb, 0.0), jnp.where(even, 0.0, qb)]
                qs.append(jnp.concatenate(parts, axis=0))
                kt, vt = kt_ref[g, h], vt_ref[g, h]
                kc.append(jnp.concatenate([kt, kt], axis=0))
                vc.append(jnp.concatenate([vt, vt], axis=0))
                kn.append(kn2[h])
                vn.append(vn2[h])
        s_c = [jnp.where(mask_c, _dot(qs[j], kc[j]), -jnp.inf) for j in n]
        s_n = [jnp.where(mask_n, _dot_nt(qs[j], kn[j]), -jnp.inf) for j in n]
        m = [jnp.maximum(jnp.maximum(jnp.max(s_c[j], axis=-1, keepdims=True),
                                     jnp.max(s_n[j], axis=-1, keepdims=True)), sink_cols[items[j][1]])
             for j in n]
        p_c = [jnp.exp(s_c[j] - m[j]) for j in n]
        p_n = [jnp.exp(s_n[j] - m[j]) for j in n]
        den = [jnp.sum(p_c[j], axis=-1, keepdims=True) + jnp.sum(p_n[j], axis=-1, keepdims=True)
               + jnp.exp(sink_cols[items[j][1]] - m[j]) for j in n]
        o = [(_dot_nt(p_c[j], vc[j]) + _dot(p_n[j], vn[j])) / den[j] for j in n]
        for j, (g, h) in enumerate(items):
            for jj in range(blocks_per_kv):
                blk = h * blocks_per_kv + jj
                r0 = 2 * jj * SAMPLE_PAD
                o_ref[g, :, blk * LANES:(blk + 1) * LANES] = jnp.where(
                    even, o[j][r0:r0 + SAMPLE_PAD], o[j][r0 + SAMPLE_PAD:r0 + 2 * SAMPLE_PAD]).astype(o_ref.dtype)
        for u in range(unroll):
            g = i * unroll + u
            k_cols, v_cols = new_columns(kn_ref[g]), new_columns(vn_ref[g])
            for h in range(KV_HEADS):
                hs = slice(h * HEAD_DIM, (h + 1) * HEAD_DIM)
                kt_out_ref[g, h] = jnp.where(keep, pltpu.roll(kt_ref[g, h], wbuf - n_new, 1), k_cols[hs])
                vt_out_ref[g, h] = jnp.where(keep, pltpu.roll(vt_ref[g, h], wbuf - n_new, 1), v_cols[hs])
        return carry_

    lax.fori_loop(0, n_seq // unroll, body, 0)


def _attn_sample(q, k, v, kt_all, vt_all, sinks, layer, n_seq, n_new):
    _, b, _, _, wbuf = kt_all.shape
    idx = lambda i: (i, 0, 0)
    new = pl.BlockSpec((n_seq, SAMPLE_PAD, KV_WIDTH), idx)
    buf = pl.BlockSpec((None, n_seq, KV_HEADS, HEAD_DIM, wbuf), lambda i: (layer, i, 0, 0, 0))
    win = pl.BlockSpec((n_seq, KV_HEADS, HEAD_DIM, wbuf), lambda i: (i, 0, 0, 0))
    win_shape = jax.ShapeDtypeStruct((b, KV_HEADS, HEAD_DIM, wbuf), F32)
    return pl.pallas_call(
        functools.partial(_attn_sample_kernel, layer=layer, n_seq=n_seq, wbuf=wbuf, n_new=n_new),
        grid=(b // n_seq,),
        in_specs=[pl.BlockSpec(memory_space=pltpu.SMEM), pl.BlockSpec((n_seq, SAMPLE_PAD, Q_WIDTH), idx),
                  buf, new, buf, new],
        out_specs=[pl.BlockSpec((n_seq, SAMPLE_PAD, Q_WIDTH), idx), win, win],
        out_shape=[jax.ShapeDtypeStruct((b, SAMPLE_PAD, Q_WIDTH), BF16), win_shape, win_shape],
        compiler_params=_cparams("parallel"),
        name="attn_sample",
    )(sinks, q, kt_all, k, vt_all, v)


def _layer_norm(x, g, b):
    mu = jnp.mean(x, axis=-1, keepdims=True)
    d = x - mu
    var = jnp.mean(d * d, axis=-1, keepdims=True)
    return d * lax.rsqrt(var + LN_EPS) * g + b


def _mix_ffn_kernel(x_ref, yr_ref, ya_ref, win_ref, wbr_ref, wba_ref, wo_ref, g1_ref, b1_ref,
                    wu_ref, wd_ref, g2_ref, b2_ref, o_ref, *, layer, ff_chunk):
    par = lambda ref: ref[layer:layer + 1, :]
    x = x_ref[...]
    xb = x.astype(BF16)
    gate_r = jax.nn.sigmoid(jnp.dot(xb, win_ref[:, QKV_END:QKV_END + D_MODEL], preferred_element_type=F32))
    mix = gate_r * jnp.dot(yr_ref[...], wbr_ref[...], preferred_element_type=F32)
    gate_a = jax.nn.sigmoid(jnp.dot(xb, win_ref[:, QKV_END + D_MODEL:], preferred_element_type=F32))
    mix = mix + gate_a * jnp.dot(ya_ref[...], wba_ref[...], preferred_element_type=F32)
    x1 = _layer_norm(ALPHA * x + _dot(mix, wo_ref[...]), par(g1_ref), par(b1_ref))
    x1b = x1.astype(BF16)
    acc = ALPHA * x1
    for c in range(D_FF // ff_chunk):
        cs = slice(c * ff_chunk, (c + 1) * ff_chunk)
        h = jnp.maximum(jnp.dot(x1b, wu_ref[:, cs], preferred_element_type=F32), 0.0)
        acc = acc + _dot(h * h, wd_ref[cs, :])
    o_ref[...] = _layer_norm(acc, par(g2_ref), par(b2_ref))


def _mix_ffn(x, yr, ya, P, layer, tm):
    n = x.shape[0]
    row = lambda i: (i, 0)
    ws = [P['w_in'], P['w_br_rwkv'], P['w_br_attn'], P['w_out'], P['ln1_g'], P['ln1_b'],
          P['w_ff_up'], P['w_ff_down'], P['ln2_g'], P['ln2_b']]
    wspec = lambda a: _layer_spec(a, layer, single_buffer=True) if a.ndim == 3 else _const_spec(a.shape)
    return pl.pallas_call(
        functools.partial(_mix_ffn_kernel, layer=layer, ff_chunk=1024),
        grid=(n // tm,),
        in_specs=[pl.BlockSpec((tm, D_MODEL), row), pl.BlockSpec((tm, R_WIDTH), row),
                  pl.BlockSpec((tm, Q_WIDTH), row)] + [wspec(a) for a in ws],
        out_specs=pl.BlockSpec((tm, D_MODEL), row),
        out_shape=jax.ShapeDtypeStruct((n, D_MODEL), F32),
        compiler_params=_cparams("parallel"),
        name="mix_ffn",
    )(x, yr, ya, *ws)


def _prepare_params(w_in, mu_shift, decay_base, decay_up, iclr_base, iclr_up, gate_up, k_k, k_a, r_k,
                    lnx_g, lnx_b, sinks, w_br_rwkv, w_br_attn, w_out, ln1_g, ln1_b, w_ff_up, w_ff_down,
                    ln2_g, ln2_b):
    depth = w_in.shape[0]
    zeros = jnp.zeros((depth, DECAY_LORA, R_WIDTH), F32)
    lora_up = jnp.concatenate([jnp.concatenate([decay_up, zeros], axis=2),
                               jnp.concatenate([zeros, iclr_up], axis=2)], axis=1)
    bf = lambda a: a.astype(BF16)
    return dict(
        w_in=bf(w_in), mu_shift=mu_shift, decay_base=decay_base, lora_up=bf(lora_up), iclr_base=iclr_base,
        gate_up=bf(gate_up), k_k=k_k, k_a=k_a, r_k=r_k.reshape(depth, R_WIDTH), lnx_g=lnx_g, lnx_b=lnx_b,
        sinks=sinks.reshape(depth * A_HEADS), w_br_rwkv=bf(w_br_rwkv), w_br_attn=bf(w_br_attn),
        w_out=bf(w_out), ln1_g=ln1_g, ln1_b=ln1_b, w_ff_up=bf(w_ff_up), w_ff_down=bf(w_ff_down),
        ln2_g=ln2_g, ln2_b=ln2_b)


def _prompt_layer(x, P, layer, tables, s_all, batch, seq, wbuf):
    tm = 512
    shift0 = jnp.zeros((batch, SHIFT_W), F32)
    zs, zlast, q, k, v = _inproj(x, P, layer, tables, seq // tm, tm, shift0, seq_rows=tm, last_row=tm - 1)
    wkv_rows = tm // batch
    yr, s_all = _wkv(zs.reshape(batch, seq, SHIFT_W), s_all, P, layer, seq_rows=wkv_rows)
    ya = _attn_prompt(q, k, v, P['sinks'], layer, batch, seq)
    x = _mix_ffn(x, yr.reshape(batch * seq, R_WIDTH), ya, P, layer, tm)
    tail = lambda t, w: t.reshape(batch, seq, w)[:, seq - wbuf:].reshape(batch, wbuf, KV_HEADS, HEAD_DIM)
    return x, s_all, zlast.reshape(batch, SHIFT_W), tail(k, KV_WIDTH), tail(v, KV_WIDTH)


def _sample_layer(x, P, layer, tables, shift_prev, s_all, kt_all, vt_all, batch, seq):
    n = batch * SAMPLE_PAD
    tm = min(n, 512)
    n_seq = 16
    zs, zlast, q, k, v = _inproj(x, P, layer, tables, 1, tm, shift_prev, seq_rows=SAMPLE_PAD, last_row=seq - 1)
    seq3 = lambda t, w: t.reshape(batch, SAMPLE_PAD, w)
    yr, s_all = _wkv_short(zs, s_all, P, layer, seq)
    ya, kt_new, vt_new = _attn_sample(seq3(q, Q_WIDTH), seq3(k, KV_WIDTH), seq3(v, KV_WIDTH), kt_all, vt_all,
                                      P['sinks'], layer, n_seq, seq)
    x = _mix_ffn(x, yr, ya.reshape(n, Q_WIDTH), P, layer, tm)
    return x, s_all, zlast.reshape(batch, SHIFT_W), kt_new, vt_new


def kernel(x_prompt, x_sample, state_wkv, state_shift, cache_k_win, cache_v_win, w_in, mu_shift, decay_base, decay_up, iclr_base, iclr_up, gate_up, k_k, k_a, r_k, lnx_g, lnx_b, sinks, w_br_rwkv, w_br_attn, w_out, ln1_g, ln1_b, w_ff_up, w_ff_down, ln2_g, ln2_b):
    bp, tp, _ = x_prompt.shape
    bs, ts, _ = x_sample.shape
    wbuf = cache_k_win.shape[2]
    half = HEAD_DIM // 2
    inv_freq = ROPE_THETA ** (-jnp.arange(half, dtype=F32) / half)
    tab_p = _rope_tables(inv_freq, tp, 0, tp)
    tab_s = _rope_tables(inv_freq, min(bs * SAMPLE_PAD, 512), PAST_LEN, SAMPLE_PAD)

    hp = x_prompt.reshape(bp * tp, D_MODEL)
    hs = jnp.pad(x_sample, ((0, 0), (0, SAMPLE_PAD - ts), (0, 0))).reshape(bs * SAMPLE_PAD, D_MODEL)
    outs_p, outs_s = [], []
    P = _prepare_params(w_in, mu_shift, decay_base, decay_up, iclr_base, iclr_up, gate_up, k_k, k_a, r_k,
                        lnx_g, lnx_b, sinks, w_br_rwkv, w_br_attn, w_out, ln1_g, ln1_b, w_ff_up,
                        w_ff_down, ln2_g, ln2_b)
    to_lanes = lambda c: jnp.transpose(c, (0, 1, 3, 4, 2))
    from_lanes = lambda c: jnp.transpose(c, (0, 1, 4, 2, 3))
    kt_all, vt_all = to_lanes(cache_k_win), to_lanes(cache_v_win)
    s_all_p = jnp.zeros((DEPTH, bp, R_HEADS, R_HEAD_DIM, R_HEAD_DIM), F32)
    s_all_s = jnp.transpose(state_wkv, (0, 2, 3, 4, 1))
    for l in range(DEPTH):
        hp, s_all_p, *st = _prompt_layer(hp, P, l, tab_p, s_all_p, bp, tp, wbuf)
        outs_p.append(st)
        hs, s_all_s, *st = _sample_layer(hs, P, l, tab_s, state_shift[l], s_all_s, kt_all, vt_all, bs, ts)
        outs_s.append(st)
    stack = lambda outs, i: jnp.stack([o[i] for o in outs])
    y_p = hp.reshape(bp, tp, D_MODEL)
    y_s = hs.reshape(bs, SAMPLE_PAD, D_MODEL)[:, :ts]
    return (y_p, y_s,
            s_all_p, stack(outs_p, 0), stack(outs_p, 1), stack(outs_p, 2),
            jnp.transpose(s_all_s, (0, 4, 1, 2, 3)), stack(outs_s, 0), from_lanes(stack(outs_s, 1)),
            from_lanes(stack(outs_s, 2)))
```

```python
import functools
import math

import jax
import jax.numpy as jnp
from jax import lax
from jax.experimental import pallas as pl
from jax.experimental.pallas import tpu as pltpu

F32 = jnp.float32
BF16 = jnp.bfloat16

D_MODEL = 1024
DEPTH = 2
PAST_LEN = 8192
R_HEADS = 8
R_HEAD_DIM = 64
R_WIDTH = R_HEADS * R_HEAD_DIM
DECAY_LORA = 64
ICLR_LORA = 64
GATE_LORA = 128
SHIFT_W = 3 * R_WIDTH + DECAY_LORA + ICLR_LORA + GATE_LORA
A_HEADS = 8
KV_HEADS = 2
HEAD_DIM = 64
Q_WIDTH = A_HEADS * HEAD_DIM
KV_WIDTH = KV_HEADS * HEAD_DIM
GROUP = A_HEADS // KV_HEADS
WINDOW = 128
BLOCK = 128
ROPE_THETA = 10000.0
ATTN_SCALE = HEAD_DIM ** -0.5
D_FF = 4 * D_MODEL
ALPHA = (2 * DEPTH) ** 0.25
LN_EPS = 1e-5
GN_EPS = 64e-5
QKV_END = SHIFT_W + Q_WIDTH + 2 * KV_WIDTH

LANES = 128
SUBLANES = 8
VMEM_LIMIT = 56 * 1024 * 1024

SAMPLE_PAD = SUBLANES
N_PAIRS = R_HEADS // 2
WKV_CHUNK = LANES // 2
ROW_GROUPS = 2
SHIFT_BLOCK = 2 * LANES
assert SHIFT_W % SHIFT_BLOCK == 0
PHASE1_CHUNKS = 4


def _cparams(*sem):
    return pltpu.CompilerParams(dimension_semantics=sem, vmem_limit_bytes=VMEM_LIMIT)


def _const_spec(shape):
    nd = len(shape)
    return pl.BlockSpec(shape, lambda *_: (0,) * nd)


def _layer_spec(a, layer, single_buffer=False):
    mode = pl.Buffered(1) if single_buffer else None
    return pl.BlockSpec((None,) + a.shape[1:], lambda *_: (layer, 0, 0), pipeline_mode=mode)


def _dot(a, b):
    return jnp.dot(a.astype(BF16), b.astype(BF16), preferred_element_type=F32)


def _dot_nt(a, b):
    return lax.dot_general(a.astype(BF16), b.astype(BF16), (((1,), (1,)), ((), ())),
                           preferred_element_type=F32)


def _pmod(x, n):
    assert n & (n - 1) == 0
    return x & (n - 1)


def _pdiv(x, n):
    assert n & (n - 1) == 0
    return x >> (n.bit_length() - 1)


def _split(x):
    hi = x.astype(BF16)
    lo = (x - hi.astype(F32)).astype(BF16)
    return hi, lo


def _dot2_nt(a, b):
    bh, bl = _split(b)
    d = functools.partial(lax.dot_general, dimension_numbers=(((1,), (1,)), ((), ())),
                          preferred_element_type=F32)
    ab = a.astype(BF16)
    return d(ab, bh) + d(ab, bl)


def _rope_table_kernel(invf_ref, cos_ref, sa_ref, sb_ref, *, rows, offset, period):
    i = pl.program_id(0)
    row = lax.broadcasted_iota(jnp.int32, (rows, LANES), 0) + i * rows
    pos = offset + _pmod(row, period)
    ang = pos.astype(F32) * invf_ref[...]
    lane = lax.broadcasted_iota(jnp.int32, (rows, LANES), 1)
    first = _pmod(lane, HEAD_DIM) < (HEAD_DIM // 2)
    c = jnp.cos(ang)
    s = jnp.sin(ang)
    cos_ref[...] = c
    sa_ref[...] = jnp.where(first, -s, 0.0)
    sb_ref[...] = jnp.where(first, 0.0, s)


def _rope_tables(inv_freq, n_rows, offset, period):
    rows = min(n_rows, 1024)
    invf = jnp.tile(inv_freq, LANES // (HEAD_DIM // 2)).reshape(1, LANES)
    out = jax.ShapeDtypeStruct((n_rows, LANES), F32)
    return pl.pallas_call(
        functools.partial(_rope_table_kernel, rows=rows, offset=offset, period=period),
        grid=(n_rows // rows,),
        in_specs=[_const_spec((1, LANES))],
        out_specs=[pl.BlockSpec((rows, LANES), lambda i: (i, 0))] * 3,
        out_shape=[out] * 3,
        compiler_params=_cparams("parallel"),
        name="rope_tables",
    )(invf)


def _inproj_kernel(x_ref, w_ref, cos_ref, sa_ref, sb_ref, mu_ref, shift_ref,
                   zs_ref, zlast_ref, q_ref, k_ref, v_ref, carry, first_s, zraw_s,
                   *, layer, n_seq, seq_rows, last_row, tiles_per_seq):
    xb = x_ref[...].astype(BF16)
    tm = xb.shape[0]
    t_step = lax.rem(pl.program_id(0), tiles_per_seq)

    @pl.when(pl.program_id(0) == 0)
    def _():
        first_s[...] = jnp.zeros(first_s.shape, F32)

    @pl.when(t_step == 0)
    def _():
        carry[...] = shift_ref[0] if n_seq == 1 else shift_ref[...]
    lane_groups = SHIFT_BLOCK // LANES
    for c in range(SHIFT_W // LANES):
        if n_seq == 1:
            first_s[c, 0:1, :] = carry[:, c * LANES:(c + 1) * LANES]
        else:
            first_s[c, pl.ds(0, n_seq, stride=seq_rows), :] = carry[:, c * LANES:(c + 1) * LANES]
    is_first = _pmod(lax.broadcasted_iota(jnp.int32, (tm, 1), 0), seq_rows) == 0
    for j in range(SHIFT_W // SHIFT_BLOCK):
        cs = slice(j * SHIFT_BLOCK, (j + 1) * SHIFT_BLOCK)
        z = jnp.dot(xb, w_ref[:, cs], preferred_element_type=F32)
        first = jnp.concatenate([first_s[j * lane_groups + c] for c in range(lane_groups)], axis=1)
        prev = jnp.where(is_first, first, pltpu.roll(z, 1, 0))
        zs_ref[:, cs] = z + (prev - z) * mu_ref[layer:layer + 1, cs]
        if n_seq == 1:
            carry[:, cs] = z[last_row:last_row + 1, :]
        else:
            for c in range(lane_groups):
                zraw_s[c] = z[:, c * LANES:(c + 1) * LANES]
                carry[:, cs.start + c * LANES:cs.start + (c + 1) * LANES] = zraw_s[
                    c, pl.ds(last_row, n_seq, stride=seq_rows), :]
    if n_seq == 1:
        zlast_ref[0] = carry[...]
    else:
        zlast_ref[...] = carry[...]
    cos, sa, sb = cos_ref[...], sa_ref[...], sb_ref[...]

    def rope(t):
        return (t * cos + pltpu.roll(t, LANES - HEAD_DIM // 2, 1) * sa
                + pltpu.roll(t, HEAD_DIM // 2, 1) * sb)

    zq = jnp.dot(xb, w_ref[:, SHIFT_W:SHIFT_W + Q_WIDTH], preferred_element_type=F32)
    for j in range(Q_WIDTH // LANES):
        q_ref[:, j * LANES:(j + 1) * LANES] = rope(zq[:, j * LANES:(j + 1) * LANES])
    zkv = jnp.dot(xb, w_ref[:, SHIFT_W + Q_WIDTH:QKV_END], preferred_element_type=F32)
    k_ref[...] = rope(zkv[:, :KV_WIDTH])
    v_ref[...] = zkv[:, KV_WIDTH:]


def _inproj(x, P, layer, tables, tab_blocks, tm, shift_in, *, seq_rows, last_row):
    n = x.shape[0]
    b = shift_in.shape[0]
    n_seq = tm // seq_rows
    if n_seq == 1:
        tiles_per_seq = (n // b) // tm
        shift_in = shift_in.reshape(b, 1, SHIFT_W)
        seq_spec = pl.BlockSpec((1, 1, SHIFT_W), lambda i: (i // tiles_per_seq, 0, 0))
    else:
        tiles_per_seq = 1
        seq_spec = pl.BlockSpec((n_seq, SHIFT_W), lambda i: (i, 0))
    row = lambda i: (i, 0)
    tab = pl.BlockSpec((tm, LANES), lambda i: (lax.rem(i, tab_blocks), 0))
    w_spec = pl.BlockSpec((None, D_MODEL, QKV_END), lambda i: (layer, 0, 0))
    return pl.pallas_call(
        functools.partial(_inproj_kernel, layer=layer, n_seq=n_seq, seq_rows=seq_rows, last_row=last_row,
                          tiles_per_seq=tiles_per_seq),
        grid=(n // tm,),
        in_specs=[pl.BlockSpec((tm, D_MODEL), row), w_spec, tab, tab, tab, _const_spec(P['mu_shift'].shape),
                  seq_spec],
        out_specs=[pl.BlockSpec((tm, SHIFT_W), row), seq_spec, pl.BlockSpec((tm, Q_WIDTH), row),
                   pl.BlockSpec((tm, KV_WIDTH), row), pl.BlockSpec((tm, KV_WIDTH), row)],
        out_shape=[jax.ShapeDtypeStruct((n, SHIFT_W), F32), jax.ShapeDtypeStruct(shift_in.shape, F32),
                   jax.ShapeDtypeStruct((n, Q_WIDTH), F32),
                   jax.ShapeDtypeStruct((n, KV_WIDTH), F32), jax.ShapeDtypeStruct((n, KV_WIDTH), F32)],
        scratch_shapes=[pltpu.VMEM((n_seq, SHIFT_W), F32), pltpu.VMEM((SHIFT_W // LANES, tm, LANES), F32),
                        pltpu.VMEM((SHIFT_BLOCK // LANES, tm, LANES), F32)],
        compiler_params=_cparams("arbitrary"),
        name="inproj",
    )(x, P['w_in'], *tables, P['mu_shift'], shift_in)


def _pair_block_diag():
    bi = _pdiv(lax.broadcasted_iota(jnp.int32, (LANES, LANES), 0), R_HEAD_DIM)
    bj = _pdiv(lax.broadcasted_iota(jnp.int32, (LANES, LANES), 1), R_HEAD_DIM)
    return (bi == bj).astype(F32)


def _head_sum(x):
    ones = _pair_block_diag().astype(BF16)
    xb = x.astype(BF16)
    return jnp.concatenate([jnp.dot(xb[:, j * LANES:(j + 1) * LANES], ones, preferred_element_type=F32)
                            for j in range(x.shape[1] // LANES)], axis=1)


def _token_features(zs, decay_base, lora_up, iclr_base, gate_up, k_k, k_a, r_k):
    o1, o2, o3 = R_WIDTH, 2 * R_WIDTH, 3 * R_WIDTH
    o4 = o3 + DECAY_LORA + ICLR_LORA
    r = zs[:, :o1]
    k = zs[:, o1:o2]
    v = zs[:, o2:o3]
    wa = zs[:, o3:o4]
    gd = zs[:, o4:]
    lane = lax.broadcasted_iota(jnp.int32, (1, LANES), 1)
    lora_in = jnp.where(lane < DECAY_LORA, jnp.tanh(wa), wa)
    lora = _dot(lora_in, lora_up)
    pre_w = decay_base + lora[:, :R_WIDTH]
    logw = -math.exp(-0.5) * jax.nn.sigmoid(pre_w)
    a = jax.nn.sigmoid(iclr_base + lora[:, R_WIDTH:])
    g = _dot(jax.nn.sigmoid(gd), gate_up)
    kk = k * k_k
    kk = kk * lax.rsqrt(jnp.maximum(_head_sum(kk * kk), 1e-24))
    kp = k * (1.0 + (a - 1.0) * k_a)
    bonus = _head_sum(r * kp * r_k) * v
    return r, kp, v, kk, a, logw, g, bonus


def _group_norm_out(y, bonus, g, lnx_g, lnx_b):
    inv_n = 1.0 / R_HEAD_DIM
    d = y - _head_sum(y) * inv_n
    var = _head_sum(d * d) * inv_n
    return (d * lax.rsqrt(var + GN_EPS) * lnx_g + lnx_b + bonus) * g


def _wkv_kernel(zs_ref, sin_ref, db_ref, lora_ref, ib_ref, gu_ref, kk_ref, ka_ref,
                rk_ref, lg_ref, lb_ref,
                y_ref, sout_ref,
                sbd, r_s, k_s, v_s, kk_s, b_s, lw_s, cw_s, y_s, bon_s, g_s, xg_s, uy_s, bk_s,
                *, layer, n_seq, seq_rows, chunk):
    t_step = pl.program_id(1)
    n_steps = pl.num_programs(1)
    rows = n_seq * seq_rows
    n_chunks = rows // chunk
    c2 = 2 * chunk
    par = lambda ref: ref[layer:layer + 1, :]

    zs = zs_ref[...].reshape(rows, SHIFT_W)
    r, kp, v, kk, a, logw, g, bonus = _token_features(
        zs, par(db_ref), lora_ref[...], par(ib_ref), gu_ref[...], par(kk_ref), par(ka_ref), par(rk_ref))
    g_s[...] = g
    bon_s[...] = bonus
    r_s[...] = r
    k_s[...] = kp
    v_s[...] = v
    kk_s[...] = kk
    b_s[...] = kk * a
    lw_s[...] = logw

    ci = lax.broadcasted_iota(jnp.int32, (chunk, chunk), 0)
    cj = lax.broadcasted_iota(jnp.int32, (chunk, chunk), 1)
    tril_ones = (cj <= ci).astype(BF16)
    lw_hi, lw_lo = _split(logw)
    for c in range(n_chunks):
        cr = slice(c * chunk, (c + 1) * chunk)
        cw_s[cr, :] = (jnp.dot(tril_ones, lw_hi[cr], preferred_element_type=F32)
                       + jnp.dot(tril_ones, lw_lo[cr], preferred_element_type=F32))

    lane = lax.broadcasted_iota(jnp.int32, (1, LANES), 1)
    left = (lane < R_HEAD_DIM).astype(F32)
    right = 1.0 - left
    head_lanes = (left, right)
    m_bd = _pair_block_diag()
    ti = lax.broadcasted_iota(jnp.int32, (chunk, LANES), 0)
    tj = _pmod(lax.broadcasted_iota(jnp.int32, (chunk, LANES), 1), chunk)
    tri_strict = (tj < ti).astype(F32)
    tri_incl = (tj <= ti).astype(F32)
    eye = (tj == ti).astype(F32)
    is_left = lax.broadcasted_iota(jnp.int32, (chunk, LANES), 1) < chunk
    n_fact = int(math.log2(chunk))

    def load_state(u):
        for p in range(N_PAIRS):
            zero = jnp.zeros((R_HEAD_DIM, R_HEAD_DIM), F32)
            sbd[u * N_PAIRS + p, 0:R_HEAD_DIM, :] = jnp.concatenate([sin_ref[u, 2 * p], zero], axis=1)
            sbd[u * N_PAIRS + p, R_HEAD_DIM:LANES, :] = jnp.concatenate([zero, sin_ref[u, 2 * p + 1]], axis=1)

    def store_state(u):
        for p in range(N_PAIRS):
            sout_ref[u, 2 * p] = sbd[u * N_PAIRS + p, 0:R_HEAD_DIM, 0:R_HEAD_DIM]
            sout_ref[u, 2 * p + 1] = sbd[u * N_PAIRS + p, R_HEAD_DIM:LANES, R_HEAD_DIM:LANES]

    @pl.when(t_step == 0)
    def _():
        for u in range(n_seq):
            load_state(u)

    pairs = range(N_PAIRS)
    col = lambda p: slice(p * LANES, (p + 1) * LANES)
    stack2 = lambda t: jnp.concatenate([t * left, t * right], axis=0)
    half = lambda t: t[:, :chunk]

    def phase1(i, carry_):
        groups = [(i * PHASE1_CHUNKS + u, p) for u in range(PHASE1_CHUNKS) for p in pairs]
        at, rt, vv, bk = [], [], [], []
        for c, p in groups:
            rs = pl.ds(pl.multiple_of(c * chunk, chunk), chunk)
            cwc = cw_s[rs, col(p)]
            w_in = jnp.exp(cwc)
            w_ex = jnp.exp(cwc - lw_s[rs, col(p)])
            w_inv = jnp.exp(-cwc)
            at.append(-kk_s[rs, col(p)] * w_ex)
            rt.append(r_s[rs, col(p)] * w_in)
            vv.append(v_s[rs, col(p)])
            bk.append(jnp.concatenate([b_s[rs, col(p)] * w_inv, k_s[rs, col(p)] * w_inv], axis=0))
        items = [(gi, h) for gi in range(len(groups)) for h in range(2)]
        n = range(len(items))
        ats = [at[gi] * head_lanes[h] for gi, h in items]
        rts = [rt[gi] * head_lanes[h] for gi, h in items]
        sc = [_dot_nt(jnp.concatenate([ats[j], rts[j]], axis=0), bk[items[j][0]]) for j in n]
        top = [sc[j][:chunk] * tri_strict for j in n]
        bot = [sc[j][chunk:] * tri_incl for j in n]
        top_sw = [pltpu.roll(top[j], chunk, 1) for j in n]
        bot_sw = [pltpu.roll(bot[j], chunk, 1) for j in n]

        pt = [jnp.where(is_left, top[j], eye) for j in n]
        for _ in range(n_fact - 1):
            z = [_dot(half(pt[j]), pt[j]) for j in n]
            pt = [jnp.where(is_left, z[j], pt[j] + z[j]) for j in n]
        pt = [pt[j] + _dot(half(pt[j]), pt[j]) for j in n]
        tm = [half(pltpu.roll(pt[j], chunk, 1)) for j in n]

        ta = [_dot(tm[j], jnp.concatenate([ats[j], top_sw[j]], axis=1)) for j in n]
        ab = [_dot(half(bot[j]), ta[j]) for j in n]
        yk = [ab[j][:, LANES:] + bot_sw[j] for j in n]
        uy = [_dot(jnp.concatenate([half(ta[j][:, LANES:]), half(yk[j])], axis=0),
                   vv[items[j][0]] * head_lanes[items[j][1]]) for j in n]
        for j, (gi, h) in enumerate(items):
            c, p = groups[gi]
            idx = c * N_PAIRS + p
            xg_s[idx, h * chunk:(h + 1) * chunk, :] = ta[j][:, :LANES]
            xg_s[idx, c2 + h * chunk:c2 + (h + 1) * chunk, :] = rts[j] + ab[j][:, :LANES]
            uy_s[idx, h * chunk:(h + 1) * chunk, :] = uy[j][:chunk]
            uy_s[idx, c2 + h * chunk:c2 + (h + 1) * chunk, :] = uy[j][chunk:]
        for gi, (c, p) in enumerate(groups):
            bk_s[c * N_PAIRS + p] = bk[gi]
        return carry_

    lax.fori_loop(0, n_chunks // PHASE1_CHUNKS, phase1, 0)

    chunks_per_seq = seq_rows // chunk

    def phase2(i, carry_):
        cks = [u * chunks_per_seq + i for u in range(n_seq)]
        items = [(u, p) for u in range(n_seq) for p in pairs]
        n = range(len(items))
        tok = [pl.ds(pl.multiple_of(ck * chunk, chunk), chunk) for ck in cks]
        tail = [pl.ds(pl.multiple_of(ck * chunk + chunk - SUBLANES, SUBLANES), SUBLANES) for ck in cks]
        s0 = [sbd[u * N_PAIRS + p] for u, p in items]
        uyv = [_dot_nt(xg_s[cks[u] * N_PAIRS + p], s0[j]) + uy_s[cks[u] * N_PAIRS + p]
               for j, (u, p) in enumerate(items)]
        uvt = [jnp.concatenate([uyv[j][:c2], stack2(v_s[tok[u], col(p)])], axis=0).T
               for j, (u, p) in enumerate(items)]
        rhs = []
        for u, p in items:
            bkc = bk_s[cks[u] * N_PAIRS + p]
            bt, kt = bkc[:chunk], bkc[chunk:]
            rhs.append(jnp.concatenate([bt, bt, kt, kt], axis=0))
        upd = [_dot(uvt[j], rhs[j]) for j in n]
        for j, (u, p) in enumerate(items):
            w_end = jnp.exp(cw_s[tail[u], col(p)][SUBLANES - 1:SUBLANES, :])
            sbd[u * N_PAIRS + p] = (s0[j] + upd[j] * m_bd) * w_end
            y_s[tok[u], col(p)] = uyv[j][c2:c2 + chunk] + uyv[j][c2 + chunk:]
        return carry_

    lax.fori_loop(0, chunks_per_seq, phase2, 0)

    @pl.when(t_step == n_steps - 1)
    def _():
        for u in range(n_seq):
            store_state(u)

    out = _group_norm_out(y_s[...], bon_s[...], g_s[...], par(lg_ref), par(lb_ref))
    y_ref[...] = out.reshape(n_seq, seq_rows, R_WIDTH).astype(y_ref.dtype)


def _wkv(zs, s_all, P, layer, *, seq_rows):
    b, t_len, _ = zs.shape
    n_seq = b
    rows = n_seq * seq_rows
    chunk = WKV_CHUNK
    n_pc = (rows // chunk) * N_PAIRS
    assert seq_rows % chunk == 0 and rows % (chunk * PHASE1_CHUNKS) == 0 and t_len % seq_rows == 0
    params = [P['decay_base'], P['lora_up'], P['iclr_base'], P['gate_up'], P['k_k'], P['k_a'],
              P['r_k'], P['lnx_g'], P['lnx_b']]
    param_specs = [_layer_spec(a, layer) if a.ndim == 3 else _const_spec(a.shape) for a in params]
    tok = lambda g, t: (g, t, 0)
    state_spec = pl.BlockSpec((None, n_seq, R_HEADS, R_HEAD_DIM, R_HEAD_DIM), lambda g, t: (layer, g, 0, 0, 0))
    tile = lambda w: pltpu.VMEM((rows, w), F32)
    return pl.pallas_call(
        functools.partial(_wkv_kernel, layer=layer, n_seq=n_seq, seq_rows=seq_rows, chunk=chunk),
        grid=(b // n_seq, t_len // seq_rows),
        in_specs=[pl.BlockSpec((n_seq, seq_rows, SHIFT_W), tok), state_spec] + param_specs,
        out_specs=[pl.BlockSpec((n_seq, seq_rows, R_WIDTH), tok), state_spec],
        out_shape=[jax.ShapeDtypeStruct((b, t_len, R_WIDTH), BF16), jax.ShapeDtypeStruct(s_all.shape, F32)],
        input_output_aliases={1: 1},
        scratch_shapes=[pltpu.VMEM((n_seq * N_PAIRS, LANES, LANES), F32)] + [tile(R_WIDTH)] * 10 + [
            pltpu.VMEM((n_pc, 4 * chunk, LANES), F32), pltpu.VMEM((n_pc, 4 * chunk, LANES), F32),
            pltpu.VMEM((n_pc, 2 * chunk, LANES), F32)],
        compiler_params=_cparams("parallel", "arbitrary"),
        name="wkv7",
    )(zs, s_all, *params)


def _wkv_prep_kernel(zs_ref, db_ref, lora_ref, ib_ref, gu_ref, kk_ref, ka_ref, rk_ref,
                     r_ref, k_ref, v_ref, kko_ref, b_ref, w_ref, g_ref, bon_ref, *, layer):
    par = lambda ref: ref[layer:layer + 1, :]
    r, kp, v, kk, a, logw, g, bonus = _token_features(
        zs_ref[...], par(db_ref), lora_ref[...], par(ib_ref), gu_ref[...], par(kk_ref), par(ka_ref), par(rk_ref))
    r_ref[...] = r
    k_ref[...] = kp
    v_ref[...] = v
    kko_ref[...] = kk
    b_ref[...] = kk * a
    w_ref[...] = jnp.exp(logw)
    g_ref[...] = g
    bon_ref[...] = bonus


LANE_GROUP = 4


def _wkv_lanes_kernel(r_ref, k_ref, v_ref, kk_ref, b_ref, w_ref, g_ref, bon_ref, lg_ref, lb_ref, s_ref,
                      y_ref, sout_ref, ft_s, yt_s, ytok_s, *, layer, n_new, whole_state):
    nb = ft_s.shape[-1]
    n = R_HEAD_DIM
    feats = (r_ref, k_ref, v_ref, kk_ref, b_ref, w_ref)
    for fi, ref in enumerate(feats):
        for t in range(n_new):
            ft_s[fi, t] = ref[pl.ds(t, nb, stride=SAMPLE_PAD), :].T
    load_s = (lambda hh, v: s_ref[layer, hh, v]) if whole_state else (lambda hh, v: s_ref[hh, v])

    def store_s(hh, v, val):
        if whole_state:
            sout_ref[layer, hh, v] = val
        else:
            sout_ref[hh, v] = val

    for hh in range(2):
        hs = slice(hh * n, (hh + 1) * n)

        def rows8(vg, carry_):
            v0 = pl.multiple_of(vg * SUBLANES, SUBLANES)
            vrows = [ft_s[2, t, pl.ds(hh * n + v0, SUBLANES), :] for t in range(n_new)]
            ys = [[None] * SUBLANES for _ in range(n_new)]
            for j0 in range(0, SUBLANES, LANE_GROUP):
                js = range(j0, j0 + LANE_GROUP)
                sv = {j: load_s(hh, v0 + j) for j in js}
                for t in range(n_new):
                    nk, w, bb = -ft_s[3, t, hs, :], ft_s[5, t, hs, :], ft_s[4, t, hs, :]
                    kt, rt = ft_s[1, t, hs, :], ft_s[0, t, hs, :]
                    sa = {j: jnp.sum(sv[j] * nk, axis=0, keepdims=True) for j in js}
                    sv = {j: sv[j] * w + sa[j] * bb + vrows[t][j:j + 1, :] * kt for j in js}
                    for j in js:
                        ys[t][j] = jnp.sum(sv[j] * rt, axis=0, keepdims=True)
                for j in js:
                    store_s(hh, v0 + j, sv[j])
            for t in range(n_new):
                yt_s[t, pl.ds(hh * n + v0, SUBLANES), :] = jnp.concatenate(ys[t], axis=0)
            return carry_

        lax.fori_loop(0, n // SUBLANES, rows8, 0)

    if whole_state:
        for l in range(s_ref.shape[0]):
            if l != layer:
                sout_ref[l] = s_ref[l]

    ytok_s[...] = jnp.zeros(ytok_s.shape, F32)
    for t in range(n_new):
        ytok_s[pl.ds(t, nb, stride=SAMPLE_PAD), :] = yt_s[t].T
    out = _group_norm_out(ytok_s[...], bon_ref[...], g_ref[...], lg_ref[layer:layer + 1, :],
                          lb_ref[layer:layer + 1, :])
    y_ref[...] = out.astype(y_ref.dtype)


def _wkv_short(zs, s_lanes, P, layer, n_new):
    n_tok = zs.shape[0]
    depth, b = s_lanes.shape[0], s_lanes.shape[-1]
    rows = 256
    params = [P['decay_base'], P['lora_up'], P['iclr_base'], P['gate_up'], P['k_k'], P['k_a'], P['r_k']]
    param_specs = [_layer_spec(a, layer) if a.ndim == 3 else _const_spec(a.shape) for a in params]
    tok_out = pl.BlockSpec((rows, R_WIDTH), lambda g: (g, 0))
    feats = pl.pallas_call(
        functools.partial(_wkv_prep_kernel, layer=layer),
        grid=(n_tok // rows,),
        in_specs=[pl.BlockSpec((rows, SHIFT_W), lambda g: (g, 0))] + param_specs,
        out_specs=[tok_out] * 8,
        out_shape=[jax.ShapeDtypeStruct((n_tok, R_WIDTH), F32)] * 8,
        compiler_params=_cparams("parallel"),
        name="wkv7_prep",
    )(zs, *params)

    whole = layer == 0
    pair_col = pl.BlockSpec((n_tok, LANES), lambda p: (0, p))
    par_col = pl.BlockSpec((depth, LANES), lambda p: (0, p))
    n = R_HEAD_DIM
    if whole:
        state_spec = pl.BlockSpec((depth, 2, n, n, b), lambda p: (0, p, 0, 0, 0))
    else:
        state_spec = pl.BlockSpec((None, 2, n, n, b), lambda p: (layer, p, 0, 0, 0))
    y, s_new = pl.pallas_call(
        functools.partial(_wkv_lanes_kernel, layer=layer, n_new=n_new, whole_state=whole),
        grid=(N_PAIRS,),
        in_specs=[pair_col] * 8 + [par_col, par_col, state_spec],
        out_specs=[pair_col, state_spec],
        out_shape=[jax.ShapeDtypeStruct((n_tok, R_WIDTH), BF16), jax.ShapeDtypeStruct(s_lanes.shape, F32)],
        scratch_shapes=[pltpu.VMEM((6, n_new, LANES, b), F32), pltpu.VMEM((n_new, LANES, b), F32),
                        pltpu.VMEM((n_tok, LANES), F32)],
        input_output_aliases={} if whole else {10: 1},
        compiler_params=_cparams("arbitrary"),
        name="wkv7_lanes",
    )(*feats, P['lnx_g'], P['lnx_b'], s_lanes)
    return y, s_new


def _pair_kv(t):
    lane = lax.broadcasted_iota(jnp.int32, (1, LANES), 1)
    even = lane < HEAD_DIM
    sw = pltpu.roll(t, HEAD_DIM, 1)
    return jnp.where(even, t, sw), jnp.where(even, sw, t)


def _attend(qs, kcs, vcs, mask, sink_cols):
    n = range(len(qs))
    s = [jnp.where(mask, _dot_nt(qs[i], kcs[i]), -jnp.inf) for i in n]
    m = [jnp.maximum(jnp.max(s[i], axis=-1, keepdims=True), sink_cols[i]) for i in n]
    p = [jnp.exp(s[i] - m[i]) for i in n]
    den = [jnp.sum(p[i], axis=-1, keepdims=True) + jnp.exp(sink_cols[i] - m[i]) for i in n]
    o = [_dot(p[i], vcs[i]) for i in n]
    return [o[i] / den[i] for i in n]


def _attn_prompt_kernel(sink_ref, q_ref, kp_ref, kc_ref, vp_ref, vc_ref, o_ref, *, layer):
    n = pl.program_id(1)
    sink = lambda head: sink_ref[layer * A_HEADS + head]
    lane = lax.broadcasted_iota(jnp.int32, (1, LANES), 1)
    even = lane < HEAD_DIM
    kcat = jnp.concatenate([kp_ref[...], kc_ref[...]], axis=0)
    vcat = jnp.concatenate([vp_ref[...], vc_ref[...]], axis=0)
    ks = _pair_kv(kcat)
    vs = _pair_kv(vcat)
    qi = _pmod(lax.broadcasted_iota(jnp.int32, (2 * BLOCK, 2 * BLOCK), 0), BLOCK) + BLOCK
    kj = lax.broadcasted_iota(jnp.int32, (2 * BLOCK, 2 * BLOCK), 1)
    dist = qi - kj
    mask = (dist >= 0) & (dist <= WINDOW) & ((n > 0) | (kj >= BLOCK))
    top = lax.broadcasted_iota(jnp.int32, (2 * BLOCK, 1), 0) < BLOCK
    blocks = range(Q_WIDTH // LANES)
    qs, sink_cols = [], []
    for j in blocks:
        qb = q_ref[:, j * LANES:(j + 1) * LANES] * ATTN_SCALE
        qs.append(jnp.concatenate([jnp.where(even, qb, 0.0), jnp.where(even, 0.0, qb)], axis=0))
        sink_cols.append(jnp.where(top, sink(2 * j), sink(2 * j + 1)))
    kvh = [(2 * j) // GROUP for j in blocks]
    o = _attend(qs, [ks[h] for h in kvh], [vs[h] for h in kvh], mask, sink_cols)
    for j in blocks:
        o_ref[:, j * LANES:(j + 1) * LANES] = jnp.where(even, o[j][:BLOCK], o[j][BLOCK:]).astype(o_ref.dtype)


def _attn_prompt(q, k, v, sinks, layer, batch, seq):
    nb = seq // BLOCK
    cur = lambda b, n: (b * nb + n, 0)
    prv = lambda b, n: (b * nb + jnp.maximum(n - 1, 0), 0)
    kv = lambda f: pl.BlockSpec((BLOCK, KV_WIDTH), f)
    return pl.pallas_call(
        functools.partial(_attn_prompt_kernel, layer=layer),
        grid=(batch, nb),
        in_specs=[pl.BlockSpec(memory_space=pltpu.SMEM), pl.BlockSpec((BLOCK, Q_WIDTH), cur),
                  kv(prv), kv(cur), kv(prv), kv(cur)],
        out_specs=pl.BlockSpec((BLOCK, Q_WIDTH), cur),
        out_shape=jax.ShapeDtypeStruct((batch * seq, Q_WIDTH), BF16),
        compiler_params=_cparams("parallel", "parallel"),
        name="attn_prompt",
    )(sinks, q, k, k, v, v)


def _attn_sample_kernel(sink_ref, q_ref, kt_ref, kn_ref, vt_ref, vn_ref, o_ref, kt_out_ref, vt_out_ref,
                        *, layer, n_seq, wbuf, n_new, whole):
    assert wbuf == LANES
    win = (lambda ref, g, h: ref[layer, g, h]) if whole else (lambda ref, g, h: ref[g, h])
    lane = lax.broadcasted_iota(jnp.int32, (1, LANES), 1)
    even = lane < HEAD_DIM
    rows = GROUP * SAMPLE_PAD
    n_pad = 2 * SAMPLE_PAD
    qt = _pmod(lax.broadcasted_iota(jnp.int32, (rows, 1), 0), SAMPLE_PAD)
    dist_c = qt + wbuf - lax.broadcasted_iota(jnp.int32, (rows, wbuf), 1)
    dist_n = qt - lax.broadcasted_iota(jnp.int32, (rows, n_pad), 1)
    mask_c = (dist_c >= 0) & (dist_c <= WINDOW)
    mask_n = (dist_n >= 0) & (dist_n <= WINDOW)
    rid = _pdiv(lax.broadcasted_iota(jnp.int32, (rows, 1), 0), SAMPLE_PAD)
    sink_cols = []
    for h in range(KV_HEADS):
        col = jnp.zeros((rows, 1), F32)
        for gq in range(GROUP):
            col = jnp.where(rid == gq, sink_ref[layer * A_HEADS + h * GROUP + gq], col)
        sink_cols.append(col)
    unroll = 2
    pad = jnp.zeros((SAMPLE_PAD, KV_WIDTH), F32)
    blocks_per_kv = GROUP // 2
    keep = lane < wbuf - n_new
    sub = lax.broadcasted_iota(jnp.int32, (SAMPLE_PAD, 1), 0)

    def new_columns(t_new):
        low = jnp.where(sub >= SAMPLE_PAD - n_new, pltpu.roll(t_new, SAMPLE_PAD - n_new, 0), 0.0)
        return jnp.concatenate([jnp.zeros((wbuf - SAMPLE_PAD, KV_WIDTH), F32), low], axis=0).T

    def body(i, carry_):
        items = [(i * unroll + u, h) for u in range(unroll) for h in range(KV_HEADS)]
        n = range(len(items))
        qs, kc, vc, kn, vn = [], [], [], [], []
        for u in range(unroll):
            g = i * unroll + u
            kn2 = _pair_kv(jnp.concatenate([kn_ref[g], pad], axis=0))
            vn2 = _pair_kv(jnp.concatenate([vn_ref[g], pad], axis=0))
            for h in range(KV_HEADS):
                parts = []
                for j in range(h * blocks_per_kv, (h + 1) * blocks_per_kv):
                    qb = q_ref[g, :, j * LANES:(j + 1) * LANES] * ATTN_SCALE
                    parts += [jnp.where(even, qb, 0.0), jnp.where(even, 0.0, qb)]
                qs.append(jnp.concatenate(parts, axis=0))
                kt, vt = win(kt_ref, g, h), win(vt_ref, g, h)
                kc.append(jnp.concatenate([kt, kt], axis=0))
                vc.append(jnp.concatenate([vt, vt], axis=0))
                kn.append(kn2[h])
                vn.append(vn2[h])
        s_c = [jnp.where(mask_c, _dot(qs[j], kc[j]), -jnp.inf) for j in n]
        s_n = [jnp.where(mask_n, _dot_nt(qs[j], kn[j]), -jnp.inf) for j in n]
        m = [jnp.maximum(jnp.maximum(jnp.max(s_c[j], axis=-1, keepdims=True),
                                     jnp.max(s_n[j], axis=-1, keepdims=True)), sink_cols[items[j][1]])
             for j in n]
        p_c = [jnp.exp(s_c[j] - m[j]) for j in n]
        p_n = [jnp.exp(s_n[j] - m[j]) for j in n]
        den = [jnp.sum(p_c[j], axis=-1, keepdims=True) + jnp.sum(p_n[j], axis=-1, keepdims=True)
               + jnp.exp(sink_cols[items[j][1]] - m[j]) for j in n]
        o = [(_dot_nt(p_c[j], vc[j]) + _dot(p_n[j], vn[j])) / den[j] for j in n]
        for j, (g, h) in enumerate(items):
            for jj in range(blocks_per_kv):
                blk = h * blocks_per_kv + jj
                r0 = 2 * jj * SAMPLE_PAD
                o_ref[g, :, blk * LANES:(blk + 1) * LANES] = jnp.where(
                    even, o[j][r0:r0 + SAMPLE_PAD], o[j][r0 + SAMPLE_PAD:r0 + 2 * SAMPLE_PAD]).astype(o_ref.dtype)
        for u in range(unroll):
            g = i * unroll + u
            k_cols, v_cols = new_columns(kn_ref[g]), new_columns(vn_ref[g])
            for h in range(KV_HEADS):
                hs = slice(h * HEAD_DIM, (h + 1) * HEAD_DIM)
                for src, dst, cols in ((kt_ref, kt_out_ref, k_cols), (vt_ref, vt_out_ref, v_cols)):
                    slid = jnp.where(keep, pltpu.roll(win(src, g, h), wbuf - n_new, 1), cols[hs])
                    if whole:
                        dst[layer, g, h] = slid
                    else:
                        dst[g, h] = slid
            if whole:
                for l in range(kt_ref.shape[0]):
                    if l != layer:
                        kt_out_ref[l, g] = kt_ref[l, g]
                        vt_out_ref[l, g] = vt_ref[l, g]
        return carry_

    lax.fori_loop(0, n_seq // unroll, body, 0)


def _attn_sample(q, k, v, kt_all, vt_all, sinks, layer, n_seq, n_new):
    depth, b, _, _, wbuf = kt_all.shape
    whole = layer == 0
    idx = lambda i: (i, 0, 0)
    new = pl.BlockSpec((n_seq, SAMPLE_PAD, KV_WIDTH), idx)
    if whole:
        buf = pl.BlockSpec((depth, n_seq, KV_HEADS, HEAD_DIM, wbuf), lambda i: (0, i, 0, 0, 0))
    else:
        buf = pl.BlockSpec((None, n_seq, KV_HEADS, HEAD_DIM, wbuf), lambda i: (layer, i, 0, 0, 0))
    win_shape = jax.ShapeDtypeStruct(kt_all.shape, F32)
    return pl.pallas_call(
        functools.partial(_attn_sample_kernel, layer=layer, n_seq=n_seq, wbuf=wbuf, n_new=n_new, whole=whole),
        grid=(b // n_seq,),
        in_specs=[pl.BlockSpec(memory_space=pltpu.SMEM), pl.BlockSpec((n_seq, SAMPLE_PAD, Q_WIDTH), idx),
                  buf, new, buf, new],
        out_specs=[pl.BlockSpec((n_seq, SAMPLE_PAD, Q_WIDTH), idx), buf, buf],
        out_shape=[jax.ShapeDtypeStruct((b, SAMPLE_PAD, Q_WIDTH), BF16), win_shape, win_shape],
        input_output_aliases={} if whole else {2: 1, 4: 2},
        compiler_params=_cparams("parallel"),
        name="attn_sample",
    )(sinks, q, kt_all, k, vt_all, v)


def _layer_norm(x, g, b):
    mu = jnp.mean(x, axis=-1, keepdims=True)
    d = x - mu
    var = jnp.mean(d * d, axis=-1, keepdims=True)
    return d * lax.rsqrt(var + LN_EPS) * g + b


def _mix_ffn_kernel(x_ref, yr_ref, ya_ref, win_ref, wbr_ref, wba_ref, wo_ref, g1_ref, b1_ref,
                    wu_ref, wd_ref, g2_ref, b2_ref, o_ref, *, layer, ff_chunk):
    par = lambda ref: ref[layer:layer + 1, :]
    dotf = functools.partial(jnp.dot, preferred_element_type=F32)
    tm = x_ref.shape[0]
    groups = [slice(i * tm // ROW_GROUPS, (i + 1) * tm // ROW_GROUPS) for i in range(ROW_GROUPS)]
    n = range(ROW_GROUPS)
    x = [x_ref[g, :] for g in groups]
    xb = [x[i].astype(BF16) for i in n]
    gate_r = [jax.nn.sigmoid(dotf(xb[i], win_ref[:, QKV_END:QKV_END + D_MODEL])) for i in n]
    mix = [gate_r[i] * dotf(yr_ref[groups[i], :], wbr_ref[...]) for i in n]
    gate_a = [jax.nn.sigmoid(dotf(xb[i], win_ref[:, QKV_END + D_MODEL:])) for i in n]
    mix = [mix[i] + gate_a[i] * dotf(ya_ref[groups[i], :], wba_ref[...]) for i in n]
    x1 = [_layer_norm(ALPHA * x[i] + _dot(mix[i], wo_ref[...]), par(g1_ref), par(b1_ref)) for i in n]
    x1b = [x1[i].astype(BF16) for i in n]
    acc = [ALPHA * x1[i] for i in n]
    for c in range(D_FF // ff_chunk):
        cs = slice(c * ff_chunk, (c + 1) * ff_chunk)
        h = [jnp.maximum(dotf(x1b[i], wu_ref[:, cs]), 0.0) for i in n]
        acc = [acc[i] + _dot(h[i] * h[i], wd_ref[cs, :]) for i in n]
    for i in n:
        o_ref[groups[i], :] = _layer_norm(acc[i], par(g2_ref), par(b2_ref))


def _mix_ffn(x, yr, ya, P, layer, tm):
    n = x.shape[0]
    row = lambda i: (i, 0)
    ws = [P['w_in'], P['w_br_rwkv'], P['w_br_attn'], P['w_out'], P['ln1_g'], P['ln1_b'],
          P['w_ff_up'], P['w_ff_down'], P['ln2_g'], P['ln2_b']]
    wspec = lambda a: _layer_spec(a, layer, single_buffer=True) if a.ndim == 3 else _const_spec(a.shape)
    return pl.pallas_call(
        functools.partial(_mix_ffn_kernel, layer=layer, ff_chunk=1024),
        grid=(n // tm,),
        in_specs=[pl.BlockSpec((tm, D_MODEL), row), pl.BlockSpec((tm, R_WIDTH), row),
                  pl.BlockSpec((tm, Q_WIDTH), row)] + [wspec(a) for a in ws],
        out_specs=pl.BlockSpec((tm, D_MODEL), row),
        out_shape=jax.ShapeDtypeStruct((n, D_MODEL), F32),
        compiler_params=_cparams("parallel"),
        name="mix_ffn",
    )(x, yr, ya, *ws)


def _prepare_params(w_in, mu_shift, decay_base, decay_up, iclr_base, iclr_up, gate_up, k_k, k_a, r_k,
                    lnx_g, lnx_b, sinks, w_br_rwkv, w_br_attn, w_out, ln1_g, ln1_b, w_ff_up, w_ff_down,
                    ln2_g, ln2_b):
    depth = w_in.shape[0]
    zeros = jnp.zeros((depth, DECAY_LORA, R_WIDTH), F32)
    lora_up = jnp.concatenate([jnp.concatenate([decay_up, zeros], axis=2),
                               jnp.concatenate([zeros, iclr_up], axis=2)], axis=1)
    bf = lambda a: a.astype(BF16)
    return dict(
        w_in=bf(w_in), mu_shift=mu_shift, decay_base=decay_base, lora_up=bf(lora_up), iclr_base=iclr_base,
        gate_up=bf(gate_up), k_k=k_k, k_a=k_a, r_k=r_k.reshape(depth, R_WIDTH), lnx_g=lnx_g, lnx_b=lnx_b,
        sinks=sinks.reshape(depth * A_HEADS), w_br_rwkv=bf(w_br_rwkv), w_br_attn=bf(w_br_attn),
        w_out=bf(w_out), ln1_g=ln1_g, ln1_b=ln1_b, w_ff_up=bf(w_ff_up), w_ff_down=bf(w_ff_down),
        ln2_g=ln2_g, ln2_b=ln2_b)


def _prompt_layer(x, P, layer, tables, s_all, batch, seq, wbuf):
    tm = 512
    shift0 = jnp.zeros((batch, SHIFT_W), F32)
    zs, zlast, q, k, v = _inproj(x, P, layer, tables, seq // tm, tm, shift0, seq_rows=tm, last_row=tm - 1)
    wkv_rows = tm // batch
    yr, s_all = _wkv(zs.reshape(batch, seq, SHIFT_W), s_all, P, layer, seq_rows=wkv_rows)
    ya = _attn_prompt(q, k, v, P['sinks'], layer, batch, seq)
    x = _mix_ffn(x, yr.reshape(batch * seq, R_WIDTH), ya, P, layer, tm)
    tail = lambda t, w: t.reshape(batch, seq, w)[:, seq - wbuf:].reshape(batch, wbuf, KV_HEADS, HEAD_DIM)
    return x, s_all, zlast.reshape(batch, SHIFT_W), tail(k, KV_WIDTH), tail(v, KV_WIDTH)


def _sample_layer(x, P, layer, tables, shift_prev, s_all, kt_all, vt_all, batch, seq):
    n = batch * SAMPLE_PAD
    tm = min(n, 512)
    n_seq = 16
    zs, zlast, q, k, v = _inproj(x, P, layer, tables, 1, tm, shift_prev, seq_rows=SAMPLE_PAD, last_row=seq - 1)
    seq3 = lambda t, w: t.reshape(batch, SAMPLE_PAD, w)
    yr, s_all = _wkv_short(zs, s_all, P, layer, seq)
    ya, kt_all, vt_all = _attn_sample(seq3(q, Q_WIDTH), seq3(k, KV_WIDTH), seq3(v, KV_WIDTH), kt_all, vt_all,
                                      P['sinks'], layer, n_seq, seq)
    x = _mix_ffn(x, yr, ya.reshape(n, Q_WIDTH), P, layer, tm)
    return x, s_all, kt_all, vt_all, zlast.reshape(batch, SHIFT_W)


def kernel(x_prompt, x_sample, state_wkv, state_shift, cache_k_win, cache_v_win, w_in, mu_shift, decay_base, decay_up, iclr_base, iclr_up, gate_up, k_k, k_a, r_k, lnx_g, lnx_b, sinks, w_br_rwkv, w_br_attn, w_out, ln1_g, ln1_b, w_ff_up, w_ff_down, ln2_g, ln2_b):
    bp, tp, _ = x_prompt.shape
    bs, ts, _ = x_sample.shape
    wbuf = cache_k_win.shape[2]
    half = HEAD_DIM // 2
    inv_freq = ROPE_THETA ** (-jnp.arange(half, dtype=F32) / half)
    tab_p = _rope_tables(inv_freq, tp, 0, tp)
    tab_s = _rope_tables(inv_freq, min(bs * SAMPLE_PAD, 512), PAST_LEN, SAMPLE_PAD)

    hp = x_prompt.reshape(bp * tp, D_MODEL)
    hs = jnp.pad(x_sample, ((0, 0), (0, SAMPLE_PAD - ts), (0, 0))).reshape(bs * SAMPLE_PAD, D_MODEL)
    outs_p, outs_s = [], []
    P = _prepare_params(w_in, mu_shift, decay_base, decay_up, iclr_base, iclr_up, gate_up, k_k, k_a, r_k,
                        lnx_g, lnx_b, sinks, w_br_rwkv, w_br_attn, w_out, ln1_g, ln1_b, w_ff_up,
                        w_ff_down, ln2_g, ln2_b)
    to_lanes = lambda c: jnp.transpose(c, (0, 1, 3, 4, 2))
    from_lanes = lambda c: jnp.transpose(c, (0, 1, 4, 2, 3))
    kt_all, vt_all = to_lanes(cache_k_win), to_lanes(cache_v_win)
    s_all_p = jnp.zeros((DEPTH, bp, R_HEADS, R_HEAD_DIM, R_HEAD_DIM), F32)
    s_all_s = jnp.transpose(state_wkv, (0, 2, 3, 4, 1))
    for l in range(DEPTH):
        hp, s_all_p, *st = _prompt_layer(hp, P, l, tab_p, s_all_p, bp, tp, wbuf)
        outs_p.append(st)
        hs, s_all_s, kt_all, vt_all, zlast = _sample_layer(hs, P, l, tab_s, state_shift[l], s_all_s, kt_all, vt_all,
                                                           bs, ts)
        outs_s.append(zlast)
    stack = lambda outs, i: jnp.stack([o[i] for o in outs])
    y_p = hp.reshape(bp, tp, D_MODEL)
    y_s = hs.reshape(bs, SAMPLE_PAD, D_MODEL)[:, :ts]
    return (y_p, y_s,
            s_all_p, stack(outs_p, 0), stack(outs_p, 1), stack(outs_p, 2),
            jnp.transpose(s_all_s, (0, 4, 1, 2, 3)), jnp.stack(outs_s), from_lanes(kt_all), from_lanes(vt_all))
```

```python
import functools
import math

import jax
import jax.numpy as jnp
from jax import lax
from jax.experimental import pallas as pl
from jax.experimental.pallas import tpu as pltpu

F32 = jnp.float32
BF16 = jnp.bfloat16

D_MODEL = 1024
DEPTH = 2
PAST_LEN = 8192
R_HEADS = 8
R_HEAD_DIM = 64
R_WIDTH = R_HEADS * R_HEAD_DIM
DECAY_LORA = 64
ICLR_LORA = 64
GATE_LORA = 128
SHIFT_W = 3 * R_WIDTH + DECAY_LORA + ICLR_LORA + GATE_LORA
A_HEADS = 8
KV_HEADS = 2
HEAD_DIM = 64
Q_WIDTH = A_HEADS * HEAD_DIM
KV_WIDTH = KV_HEADS * HEAD_DIM
GROUP = A_HEADS // KV_HEADS
WINDOW = 128
BLOCK = 128
ROPE_THETA = 10000.0
ATTN_SCALE = HEAD_DIM ** -0.5
D_FF = 4 * D_MODEL
ALPHA = (2 * DEPTH) ** 0.25
LN_EPS = 1e-5
GN_EPS = 64e-5
QKV_END = SHIFT_W + Q_WIDTH + 2 * KV_WIDTH

LANES = 128
SUBLANES = 8
VMEM_LIMIT = 56 * 1024 * 1024

SAMPLE_PAD = SUBLANES
N_PAIRS = R_HEADS // 2
WKV_CHUNK = LANES // 2
ROW_GROUPS = 2
SHIFT_BLOCK = 2 * LANES
assert SHIFT_W % SHIFT_BLOCK == 0
PHASE1_CHUNKS = 4


def _cparams(*sem):
    return pltpu.CompilerParams(dimension_semantics=sem, vmem_limit_bytes=VMEM_LIMIT)


def _const_spec(shape):
    nd = len(shape)
    return pl.BlockSpec(shape, lambda *_: (0,) * nd)


def _layer_spec(a, layer, single_buffer=False):
    mode = pl.Buffered(1) if single_buffer else None
    return pl.BlockSpec((None,) + a.shape[1:], lambda *_: (layer, 0, 0), pipeline_mode=mode)


def _dot(a, b):
    return jnp.dot(a.astype(BF16), b.astype(BF16), preferred_element_type=F32)


def _dot_nt(a, b):
    return lax.dot_general(a.astype(BF16), b.astype(BF16), (((1,), (1,)), ((), ())),
                           preferred_element_type=F32)


def _pmod(x, n):
    assert n & (n - 1) == 0
    return x & (n - 1)


def _pdiv(x, n):
    assert n & (n - 1) == 0
    return x >> (n.bit_length() - 1)


def _split(x):
    hi = x.astype(BF16)
    lo = (x - hi.astype(F32)).astype(BF16)
    return hi, lo


def _dot2_nt(a, b):
    bh, bl = _split(b)
    d = functools.partial(lax.dot_general, dimension_numbers=(((1,), (1,)), ((), ())),
                          preferred_element_type=F32)
    ab = a.astype(BF16)
    return d(ab, bh) + d(ab, bl)


def _rope_table_kernel(invf_ref, cos_ref, sa_ref, sb_ref, *, rows, offset, period):
    i = pl.program_id(0)
    row = lax.broadcasted_iota(jnp.int32, (rows, LANES), 0) + i * rows
    pos = offset + _pmod(row, period)
    ang = pos.astype(F32) * invf_ref[...]
    lane = lax.broadcasted_iota(jnp.int32, (rows, LANES), 1)
    first = _pmod(lane, HEAD_DIM) < (HEAD_DIM // 2)
    c = jnp.cos(ang)
    s = jnp.sin(ang)
    cos_ref[...] = c
    sa_ref[...] = jnp.where(first, -s, 0.0)
    sb_ref[...] = jnp.where(first, 0.0, s)


def _rope_tables(inv_freq, n_rows, offset, period):
    rows = min(n_rows, 1024)
    invf = jnp.tile(inv_freq, LANES // (HEAD_DIM // 2)).reshape(1, LANES)
    out = jax.ShapeDtypeStruct((n_rows, LANES), F32)
    return pl.pallas_call(
        functools.partial(_rope_table_kernel, rows=rows, offset=offset, period=period),
        grid=(n_rows // rows,),
        in_specs=[_const_spec((1, LANES))],
        out_specs=[pl.BlockSpec((rows, LANES), lambda i: (i, 0))] * 3,
        out_shape=[out] * 3,
        compiler_params=_cparams("parallel"),
        name="rope_tables",
    )(invf)


def _inproj_kernel(x_ref, w_ref, cos_ref, sa_ref, sb_ref, mu_ref, shift_ref,
                   zs_ref, zlast_ref, q_ref, k_ref, v_ref, carry, first_s, zraw_s,
                   *, layer, n_seq, seq_rows, last_row, tiles_per_seq):
    xb = x_ref[...].astype(BF16)
    tm = xb.shape[0]
    t_step = lax.rem(pl.program_id(0), tiles_per_seq)

    @pl.when(pl.program_id(0) == 0)
    def _():
        first_s[...] = jnp.zeros(first_s.shape, F32)

    @pl.when(t_step == 0)
    def _():
        carry[...] = shift_ref[0] if n_seq == 1 else shift_ref[...]
    lane_groups = SHIFT_BLOCK // LANES
    for c in range(SHIFT_W // LANES):
        if n_seq == 1:
            first_s[c, 0:1, :] = carry[:, c * LANES:(c + 1) * LANES]
        else:
            first_s[c, pl.ds(0, n_seq, stride=seq_rows), :] = carry[:, c * LANES:(c + 1) * LANES]
    is_first = _pmod(lax.broadcasted_iota(jnp.int32, (tm, 1), 0), seq_rows) == 0
    for j in range(SHIFT_W // SHIFT_BLOCK):
        cs = slice(j * SHIFT_BLOCK, (j + 1) * SHIFT_BLOCK)
        z = jnp.dot(xb, w_ref[:, cs], preferred_element_type=F32)
        first = jnp.concatenate([first_s[j * lane_groups + c] for c in range(lane_groups)], axis=1)
        prev = jnp.where(is_first, first, pltpu.roll(z, 1, 0))
        zs_ref[:, cs] = z + (prev - z) * mu_ref[layer:layer + 1, cs]
        if n_seq == 1:
            carry[:, cs] = z[last_row:last_row + 1, :]
        else:
            for c in range(lane_groups):
                zraw_s[c] = z[:, c * LANES:(c + 1) * LANES]
                carry[:, cs.start + c * LANES:cs.start + (c + 1) * LANES] = zraw_s[
                    c, pl.ds(last_row, n_seq, stride=seq_rows), :]
    if n_seq == 1:
        zlast_ref[0] = carry[...]
    else:
        zlast_ref[...] = carry[...]
    cos, sa, sb = cos_ref[...], sa_ref[...], sb_ref[...]

    def rope(t):
        return (t * cos + pltpu.roll(t, LANES - HEAD_DIM // 2, 1) * sa
                + pltpu.roll(t, HEAD_DIM // 2, 1) * sb)

    zq = jnp.dot(xb, w_ref[:, SHIFT_W:SHIFT_W + Q_WIDTH], preferred_element_type=F32)
    for j in range(Q_WIDTH // LANES):
        q_ref[:, j * LANES:(j + 1) * LANES] = rope(zq[:, j * LANES:(j + 1) * LANES])
    zkv = jnp.dot(xb, w_ref[:, SHIFT_W + Q_WIDTH:QKV_END], preferred_element_type=F32)
    k_ref[...] = rope(zkv[:, :KV_WIDTH])
    v_ref[...] = zkv[:, KV_WIDTH:]


def _inproj(x, P, layer, tables, tab_blocks, tm, shift_in, *, seq_rows, last_row):
    n = x.shape[0]
    b = shift_in.shape[0]
    n_seq = tm // seq_rows
    if n_seq == 1:
        tiles_per_seq = (n // b) // tm
        shift_in = shift_in.reshape(b, 1, SHIFT_W)
        seq_spec = pl.BlockSpec((1, 1, SHIFT_W), lambda i: (i // tiles_per_seq, 0, 0))
    else:
        tiles_per_seq = 1
        seq_spec = pl.BlockSpec((n_seq, SHIFT_W), lambda i: (i, 0))
    row = lambda i: (i, 0)
    tab = pl.BlockSpec((tm, LANES), lambda i: (lax.rem(i, tab_blocks), 0))
    w_spec = pl.BlockSpec((None, D_MODEL, QKV_END), lambda i: (layer, 0, 0))
    return pl.pallas_call(
        functools.partial(_inproj_kernel, layer=layer, n_seq=n_seq, seq_rows=seq_rows, last_row=last_row,
                          tiles_per_seq=tiles_per_seq),
        grid=(n // tm,),
        in_specs=[pl.BlockSpec((tm, D_MODEL), row), w_spec, tab, tab, tab, _const_spec(P['mu_shift'].shape),
                  seq_spec],
        out_specs=[pl.BlockSpec((tm, SHIFT_W), row), seq_spec, pl.BlockSpec((tm, Q_WIDTH), row),
                   pl.BlockSpec((tm, KV_WIDTH), row), pl.BlockSpec((tm, KV_WIDTH), row)],
        out_shape=[jax.ShapeDtypeStruct((n, SHIFT_W), F32), jax.ShapeDtypeStruct(shift_in.shape, F32),
                   jax.ShapeDtypeStruct((n, Q_WIDTH), F32),
                   jax.ShapeDtypeStruct((n, KV_WIDTH), F32), jax.ShapeDtypeStruct((n, KV_WIDTH), F32)],
        scratch_shapes=[pltpu.VMEM((n_seq, SHIFT_W), F32), pltpu.VMEM((SHIFT_W // LANES, tm, LANES), F32),
                        pltpu.VMEM((SHIFT_BLOCK // LANES, tm, LANES), F32)],
        compiler_params=_cparams("arbitrary"),
        name="inproj",
    )(x, P['w_in'], *tables, P['mu_shift'], shift_in)


def _pair_block_diag():
    bi = _pdiv(lax.broadcasted_iota(jnp.int32, (LANES, LANES), 0), R_HEAD_DIM)
    bj = _pdiv(lax.broadcasted_iota(jnp.int32, (LANES, LANES), 1), R_HEAD_DIM)
    return (bi == bj).astype(F32)


def _head_sum(x):
    ones = _pair_block_diag().astype(BF16)
    xb = x.astype(BF16)
    return jnp.concatenate([jnp.dot(xb[:, j * LANES:(j + 1) * LANES], ones, preferred_element_type=F32)
                            for j in range(x.shape[1] // LANES)], axis=1)


def _token_features(zs, decay_base, lora_up, iclr_base, gate_up, k_k, k_a, r_k):
    o1, o2, o3 = R_WIDTH, 2 * R_WIDTH, 3 * R_WIDTH
    o4 = o3 + DECAY_LORA + ICLR_LORA
    r = zs[:, :o1]
    k = zs[:, o1:o2]
    v = zs[:, o2:o3]
    wa = zs[:, o3:o4]
    gd = zs[:, o4:]
    lane = lax.broadcasted_iota(jnp.int32, (1, LANES), 1)
    lora_in = jnp.where(lane < DECAY_LORA, jnp.tanh(wa), wa)
    lora = _dot(lora_in, lora_up)
    pre_w = decay_base + lora[:, :R_WIDTH]
    logw = -math.exp(-0.5) * jax.nn.sigmoid(pre_w)
    a = jax.nn.sigmoid(iclr_base + lora[:, R_WIDTH:])
    g = _dot(jax.nn.sigmoid(gd), gate_up)
    kk = k * k_k
    kk = kk * lax.rsqrt(jnp.maximum(_head_sum(kk * kk), 1e-24))
    kp = k * (1.0 + (a - 1.0) * k_a)
    bonus = _head_sum(r * kp * r_k) * v
    return r, kp, v, kk, a, logw, g, bonus


def _group_norm_out(y, bonus, g, lnx_g, lnx_b):
    inv_n = 1.0 / R_HEAD_DIM
    d = y - _head_sum(y) * inv_n
    var = _head_sum(d * d) * inv_n
    return (d * lax.rsqrt(var + GN_EPS) * lnx_g + lnx_b + bonus) * g


def _wkv_kernel(zs_ref, sin_ref, db_ref, lora_ref, ib_ref, gu_ref, kk_ref, ka_ref,
                rk_ref, lg_ref, lb_ref,
                y_ref, sout_ref,
                sbd, r_s, k_s, v_s, kk_s, b_s, lw_s, cw_s, y_s, bon_s, g_s, xg_s, uy_s, bk_s,
                *, layer, n_seq, seq_rows, chunk):
    t_step = pl.program_id(1)
    n_steps = pl.num_programs(1)
    rows = n_seq * seq_rows
    n_chunks = rows // chunk
    c2 = 2 * chunk
    par = lambda ref: ref[layer:layer + 1, :]

    zs = zs_ref[...].reshape(rows, SHIFT_W)
    r, kp, v, kk, a, logw, g, bonus = _token_features(
        zs, par(db_ref), lora_ref[...], par(ib_ref), gu_ref[...], par(kk_ref), par(ka_ref), par(rk_ref))
    g_s[...] = g
    bon_s[...] = bonus
    r_s[...] = r
    k_s[...] = kp
    v_s[...] = v
    kk_s[...] = kk
    b_s[...] = kk * a
    lw_s[...] = logw

    ci = lax.broadcasted_iota(jnp.int32, (chunk, chunk), 0)
    cj = lax.broadcasted_iota(jnp.int32, (chunk, chunk), 1)
    tril_ones = (cj <= ci).astype(BF16)
    lw_hi, lw_lo = _split(logw)
    for c in range(n_chunks):
        cr = slice(c * chunk, (c + 1) * chunk)
        cw_s[cr, :] = (jnp.dot(tril_ones, lw_hi[cr], preferred_element_type=F32)
                       + jnp.dot(tril_ones, lw_lo[cr], preferred_element_type=F32))

    lane = lax.broadcasted_iota(jnp.int32, (1, LANES), 1)
    left = (lane < R_HEAD_DIM).astype(F32)
    right = 1.0 - left
    head_lanes = (left, right)
    m_bd = _pair_block_diag()
    ti = lax.broadcasted_iota(jnp.int32, (chunk, LANES), 0)
    tj = _pmod(lax.broadcasted_iota(jnp.int32, (chunk, LANES), 1), chunk)
    tri_strict = (tj < ti).astype(F32)
    tri_incl = (tj <= ti).astype(F32)
    eye = (tj == ti).astype(F32)
    is_left = lax.broadcasted_iota(jnp.int32, (chunk, LANES), 1) < chunk
    n_fact = int(math.log2(chunk))

    def load_state(u):
        for p in range(N_PAIRS):
            zero = jnp.zeros((R_HEAD_DIM, R_HEAD_DIM), F32)
            sbd[u * N_PAIRS + p, 0:R_HEAD_DIM, :] = jnp.concatenate([sin_ref[u, 2 * p], zero], axis=1)
            sbd[u * N_PAIRS + p, R_HEAD_DIM:LANES, :] = jnp.concatenate([zero, sin_ref[u, 2 * p + 1]], axis=1)

    def store_state(u):
        for p in range(N_PAIRS):
            sout_ref[u, 2 * p] = sbd[u * N_PAIRS + p, 0:R_HEAD_DIM, 0:R_HEAD_DIM]
            sout_ref[u, 2 * p + 1] = sbd[u * N_PAIRS + p, R_HEAD_DIM:LANES, R_HEAD_DIM:LANES]

    @pl.when(t_step == 0)
    def _():
        for u in range(n_seq):
            load_state(u)

    pairs = range(N_PAIRS)
    col = lambda p: slice(p * LANES, (p + 1) * LANES)
    stack2 = lambda t: jnp.concatenate([t * left, t * right], axis=0)
    half = lambda t: t[:, :chunk]

    def phase1(i, carry_):
        groups = [(i * PHASE1_CHUNKS + u, p) for u in range(PHASE1_CHUNKS) for p in pairs]
        at, rt, vv, bk = [], [], [], []
        for c, p in groups:
            rs = pl.ds(pl.multiple_of(c * chunk, chunk), chunk)
            cwc = cw_s[rs, col(p)]
            w_in = jnp.exp(cwc)
            w_ex = jnp.exp(cwc - lw_s[rs, col(p)])
            w_inv = jnp.exp(-cwc)
            at.append(-kk_s[rs, col(p)] * w_ex)
            rt.append(r_s[rs, col(p)] * w_in)
            vv.append(v_s[rs, col(p)])
            bk.append(jnp.concatenate([b_s[rs, col(p)] * w_inv, k_s[rs, col(p)] * w_inv], axis=0))
        items = [(gi, h) for gi in range(len(groups)) for h in range(2)]
        n = range(len(items))
        ats = [at[gi] * head_lanes[h] for gi, h in items]
        rts = [rt[gi] * head_lanes[h] for gi, h in items]
        sc = [_dot_nt(jnp.concatenate([ats[j], rts[j]], axis=0), bk[items[j][0]]) for j in n]
        top = [sc[j][:chunk] * tri_strict for j in n]
        bot = [sc[j][chunk:] * tri_incl for j in n]
        top_sw = [pltpu.roll(top[j], chunk, 1) for j in n]
        bot_sw = [pltpu.roll(bot[j], chunk, 1) for j in n]

        pt = [jnp.where(is_left, top[j], eye) for j in n]
        for _ in range(n_fact - 1):
            z = [_dot(half(pt[j]), pt[j]) for j in n]
            pt = [jnp.where(is_left, z[j], pt[j] + z[j]) for j in n]
        pt = [pt[j] + _dot(half(pt[j]), pt[j]) for j in n]
        tm = [half(pltpu.roll(pt[j], chunk, 1)) for j in n]

        ta = [_dot(tm[j], jnp.concatenate([ats[j], top_sw[j]], axis=1)) for j in n]
        ab = [_dot(half(bot[j]), ta[j]) for j in n]
        yk = [ab[j][:, LANES:] + bot_sw[j] for j in n]
        uy = [_dot(jnp.concatenate([half(ta[j][:, LANES:]), half(yk[j])], axis=0),
                   vv[items[j][0]] * head_lanes[items[j][1]]) for j in n]
        for j, (gi, h) in enumerate(items):
            c, p = groups[gi]
            idx = c * N_PAIRS + p
            xg_s[idx, h * chunk:(h + 1) * chunk, :] = ta[j][:, :LANES]
            xg_s[idx, c2 + h * chunk:c2 + (h + 1) * chunk, :] = rts[j] + ab[j][:, :LANES]
            uy_s[idx, h * chunk:(h + 1) * chunk, :] = uy[j][:chunk]
            uy_s[idx, c2 + h * chunk:c2 + (h + 1) * chunk, :] = uy[j][chunk:]
        for gi, (c, p) in enumerate(groups):
            bk_s[c * N_PAIRS + p] = bk[gi]
        return carry_

    lax.fori_loop(0, n_chunks // PHASE1_CHUNKS, phase1, 0)

    chunks_per_seq = seq_rows // chunk

    def phase2(i, carry_):
        cks = [u * chunks_per_seq + i for u in range(n_seq)]
        items = [(u, p) for u in range(n_seq) for p in pairs]
        n = range(len(items))
        tok = [pl.ds(pl.multiple_of(ck * chunk, chunk), chunk) for ck in cks]
        tail = [pl.ds(pl.multiple_of(ck * chunk + chunk - SUBLANES, SUBLANES), SUBLANES) for ck in cks]
        s0 = [sbd[u * N_PAIRS + p] for u, p in items]
        uyv = [_dot_nt(xg_s[cks[u] * N_PAIRS + p], s0[j]) + uy_s[cks[u] * N_PAIRS + p]
               for j, (u, p) in enumerate(items)]
        uvt = [jnp.concatenate([uyv[j][:c2], stack2(v_s[tok[u], col(p)])], axis=0).T
               for j, (u, p) in enumerate(items)]
        rhs = []
        for u, p in items:
            bkc = bk_s[cks[u] * N_PAIRS + p]
            bt, kt = bkc[:chunk], bkc[chunk:]
            rhs.append(jnp.concatenate([bt, bt, kt, kt], axis=0))
        upd = [_dot(uvt[j], rhs[j]) for j in n]
        for j, (u, p) in enumerate(items):
            w_end = jnp.exp(cw_s[tail[u], col(p)][SUBLANES - 1:SUBLANES, :])
            sbd[u * N_PAIRS + p] = (s0[j] + upd[j] * m_bd) * w_end
            y_s[tok[u], col(p)] = uyv[j][c2:c2 + chunk] + uyv[j][c2 + chunk:]
        return carry_

    lax.fori_loop(0, chunks_per_seq, phase2, 0)

    @pl.when(t_step == n_steps - 1)
    def _():
        for u in range(n_seq):
            store_state(u)

    out = _group_norm_out(y_s[...], bon_s[...], g_s[...], par(lg_ref), par(lb_ref))
    y_ref[...] = out.reshape(n_seq, seq_rows, R_WIDTH).astype(y_ref.dtype)


def _wkv(zs, s_all, P, layer, *, seq_rows):
    b, t_len, _ = zs.shape
    n_seq = b
    rows = n_seq * seq_rows
    chunk = WKV_CHUNK
    n_pc = (rows // chunk) * N_PAIRS
    assert seq_rows % chunk == 0 and rows % (chunk * PHASE1_CHUNKS) == 0 and t_len % seq_rows == 0
    params = [P['decay_base'], P['lora_up'], P['iclr_base'], P['gate_up'], P['k_k'], P['k_a'],
              P['r_k'], P['lnx_g'], P['lnx_b']]
    param_specs = [_layer_spec(a, layer) if a.ndim == 3 else _const_spec(a.shape) for a in params]
    tok = lambda g, t: (g, t, 0)
    state_spec = pl.BlockSpec((None, n_seq, R_HEADS, R_HEAD_DIM, R_HEAD_DIM), lambda g, t: (layer, g, 0, 0, 0))
    tile = lambda w: pltpu.VMEM((rows, w), F32)
    return pl.pallas_call(
        functools.partial(_wkv_kernel, layer=layer, n_seq=n_seq, seq_rows=seq_rows, chunk=chunk),
        grid=(b // n_seq, t_len // seq_rows),
        in_specs=[pl.BlockSpec((n_seq, seq_rows, SHIFT_W), tok), state_spec] + param_specs,
        out_specs=[pl.BlockSpec((n_seq, seq_rows, R_WIDTH), tok), state_spec],
        out_shape=[jax.ShapeDtypeStruct((b, t_len, R_WIDTH), BF16), jax.ShapeDtypeStruct(s_all.shape, F32)],
        input_output_aliases={1: 1},
        scratch_shapes=[pltpu.VMEM((n_seq * N_PAIRS, LANES, LANES), F32)] + [tile(R_WIDTH)] * 10 + [
            pltpu.VMEM((n_pc, 4 * chunk, LANES), F32), pltpu.VMEM((n_pc, 4 * chunk, LANES), F32),
            pltpu.VMEM((n_pc, 2 * chunk, LANES), F32)],
        compiler_params=_cparams("parallel", "arbitrary"),
        name="wkv7",
    )(zs, s_all, *params)


def _wkv_prep_kernel(zs_ref, db_ref, lora_ref, ib_ref, gu_ref, kk_ref, ka_ref, rk_ref,
                     r_ref, k_ref, v_ref, kko_ref, b_ref, w_ref, g_ref, bon_ref, *, layer):
    par = lambda ref: ref[layer:layer + 1, :]
    r, kp, v, kk, a, logw, g, bonus = _token_features(
        zs_ref[...], par(db_ref), lora_ref[...], par(ib_ref), gu_ref[...], par(kk_ref), par(ka_ref), par(rk_ref))
    r_ref[...] = r
    k_ref[...] = kp
    v_ref[...] = v
    kko_ref[...] = kk
    b_ref[...] = kk * a
    w_ref[...] = jnp.exp(logw)
    g_ref[...] = g
    bon_ref[...] = bonus


LANE_GROUP = 4


def _wkv_lanes_kernel(r_ref, k_ref, v_ref, kk_ref, b_ref, w_ref, g_ref, bon_ref, lg_ref, lb_ref, s_ref,
                      y_ref, sout_ref, ft_s, yt_s, ytok_s, *, layer, n_new, whole_state):
    nb = ft_s.shape[-1]
    n = R_HEAD_DIM
    feats = (r_ref, k_ref, v_ref, kk_ref, b_ref, w_ref)
    for fi, ref in enumerate(feats):
        for t in range(n_new):
            ft_s[fi, t] = ref[pl.ds(t, nb, stride=n_new), :].T
    load_s = (lambda hh, v: s_ref[layer, hh, v]) if whole_state else (lambda hh, v: s_ref[hh, v])

    def store_s(hh, v, val):
        if whole_state:
            sout_ref[layer, hh, v] = val
        else:
            sout_ref[hh, v] = val

    for hh in range(2):
        hs = slice(hh * n, (hh + 1) * n)

        def rows8(vg, carry_):
            v0 = pl.multiple_of(vg * SUBLANES, SUBLANES)
            vrows = [ft_s[2, t, pl.ds(hh * n + v0, SUBLANES), :] for t in range(n_new)]
            ys = [[None] * SUBLANES for _ in range(n_new)]
            for j0 in range(0, SUBLANES, LANE_GROUP):
                js = range(j0, j0 + LANE_GROUP)
                sv = {j: load_s(hh, v0 + j) for j in js}
                for t in range(n_new):
                    nk, w, bb = -ft_s[3, t, hs, :], ft_s[5, t, hs, :], ft_s[4, t, hs, :]
                    kt, rt = ft_s[1, t, hs, :], ft_s[0, t, hs, :]
                    sa = {j: jnp.sum(sv[j] * nk, axis=0, keepdims=True) for j in js}
                    sv = {j: sv[j] * w + sa[j] * bb + vrows[t][j:j + 1, :] * kt for j in js}
                    for j in js:
                        ys[t][j] = jnp.sum(sv[j] * rt, axis=0, keepdims=True)
                for j in js:
                    store_s(hh, v0 + j, sv[j])
            for t in range(n_new):
                yt_s[t, pl.ds(hh * n + v0, SUBLANES), :] = jnp.concatenate(ys[t], axis=0)
            return carry_

        lax.fori_loop(0, n // SUBLANES, rows8, 0)

    if whole_state:
        for l in range(s_ref.shape[0]):
            if l != layer:
                sout_ref[l] = s_ref[l]

    for t in range(n_new):
        ytok_s[pl.ds(t, nb, stride=n_new), :] = yt_s[t].T
    out = _group_norm_out(ytok_s[...], bon_ref[...], g_ref[...], lg_ref[layer:layer + 1, :],
                          lb_ref[layer:layer + 1, :])
    y_ref[...] = out.astype(y_ref.dtype)


def _wkv_short(zs, s_lanes, P, layer, n_new):
    n_tok = zs.shape[0]
    depth, b = s_lanes.shape[0], s_lanes.shape[-1]
    assert n_tok == b * n_new
    rows = 256
    params = [P['decay_base'], P['lora_up'], P['iclr_base'], P['gate_up'], P['k_k'], P['k_a'], P['r_k']]
    param_specs = [_layer_spec(a, layer) if a.ndim == 3 else _const_spec(a.shape) for a in params]
    tok_out = pl.BlockSpec((rows, R_WIDTH), lambda g: (g, 0))
    feats = pl.pallas_call(
        functools.partial(_wkv_prep_kernel, layer=layer),
        grid=(n_tok // rows,),
        in_specs=[pl.BlockSpec((rows, SHIFT_W), lambda g: (g, 0))] + param_specs,
        out_specs=[tok_out] * 8,
        out_shape=[jax.ShapeDtypeStruct((n_tok, R_WIDTH), F32)] * 8,
        compiler_params=_cparams("parallel"),
        name="wkv7_prep",
    )(zs, *params)

    whole = layer == 0
    pair_col = pl.BlockSpec((n_tok, LANES), lambda p: (0, p))
    par_col = pl.BlockSpec((depth, LANES), lambda p: (0, p))
    n = R_HEAD_DIM
    if whole:
        state_spec = pl.BlockSpec((depth, 2, n, n, b), lambda p: (0, p, 0, 0, 0))
    else:
        state_spec = pl.BlockSpec((None, 2, n, n, b), lambda p: (layer, p, 0, 0, 0))
    y, s_new = pl.pallas_call(
        functools.partial(_wkv_lanes_kernel, layer=layer, n_new=n_new, whole_state=whole),
        grid=(N_PAIRS,),
        in_specs=[pair_col] * 8 + [par_col, par_col, state_spec],
        out_specs=[pair_col, state_spec],
        out_shape=[jax.ShapeDtypeStruct((n_tok, R_WIDTH), BF16), jax.ShapeDtypeStruct(s_lanes.shape, F32)],
        scratch_shapes=[pltpu.VMEM((6, n_new, LANES, b), F32), pltpu.VMEM((n_new, LANES, b), F32),
                        pltpu.VMEM((n_tok, LANES), F32)],
        input_output_aliases={} if whole else {10: 1},
        compiler_params=_cparams("arbitrary"),
        name="wkv7_lanes",
    )(*feats, P['lnx_g'], P['lnx_b'], s_lanes)
    return y, s_new


def _pair_kv(t):
    lane = lax.broadcasted_iota(jnp.int32, (1, LANES), 1)
    even = lane < HEAD_DIM
    sw = pltpu.roll(t, HEAD_DIM, 1)
    return jnp.where(even, t, sw), jnp.where(even, sw, t)


def _attend(qs, kcs, vcs, mask, sink_cols):
    n = range(len(qs))
    s = [jnp.where(mask, _dot_nt(qs[i], kcs[i]), -jnp.inf) for i in n]
    m = [jnp.maximum(jnp.max(s[i], axis=-1, keepdims=True), sink_cols[i]) for i in n]
    p = [jnp.exp(s[i] - m[i]) for i in n]
    den = [jnp.sum(p[i], axis=-1, keepdims=True) + jnp.exp(sink_cols[i] - m[i]) for i in n]
    o = [_dot(p[i], vcs[i]) for i in n]
    return [o[i] / den[i] for i in n]


def _attn_prompt_kernel(sink_ref, q_ref, kp_ref, kc_ref, vp_ref, vc_ref, o_ref, *, layer):
    n = pl.program_id(1)
    sink = lambda head: sink_ref[layer * A_HEADS + head]
    lane = lax.broadcasted_iota(jnp.int32, (1, LANES), 1)
    even = lane < HEAD_DIM
    kcat = jnp.concatenate([kp_ref[...], kc_ref[...]], axis=0)
    vcat = jnp.concatenate([vp_ref[...], vc_ref[...]], axis=0)
    ks = _pair_kv(kcat)
    vs = _pair_kv(vcat)
    qi = _pmod(lax.broadcasted_iota(jnp.int32, (2 * BLOCK, 2 * BLOCK), 0), BLOCK) + BLOCK
    kj = lax.broadcasted_iota(jnp.int32, (2 * BLOCK, 2 * BLOCK), 1)
    dist = qi - kj
    mask = (dist >= 0) & (dist <= WINDOW) & ((n > 0) | (kj >= BLOCK))
    top = lax.broadcasted_iota(jnp.int32, (2 * BLOCK, 1), 0) < BLOCK
    blocks = range(Q_WIDTH // LANES)
    qs, sink_cols = [], []
    for j in blocks:
        qb = q_ref[:, j * LANES:(j + 1) * LANES] * ATTN_SCALE
        qs.append(jnp.concatenate([jnp.where(even, qb, 0.0), jnp.where(even, 0.0, qb)], axis=0))
        sink_cols.append(jnp.where(top, sink(2 * j), sink(2 * j + 1)))
    kvh = [(2 * j) // GROUP for j in blocks]
    o = _attend(qs, [ks[h] for h in kvh], [vs[h] for h in kvh], mask, sink_cols)
    for j in blocks:
        o_ref[:, j * LANES:(j + 1) * LANES] = jnp.where(even, o[j][:BLOCK], o[j][BLOCK:]).astype(o_ref.dtype)


def _attn_prompt(q, k, v, sinks, layer, batch, seq):
    nb = seq // BLOCK
    cur = lambda b, n: (b * nb + n, 0)
    prv = lambda b, n: (b * nb + jnp.maximum(n - 1, 0), 0)
    kv = lambda f: pl.BlockSpec((BLOCK, KV_WIDTH), f)
    return pl.pallas_call(
        functools.partial(_attn_prompt_kernel, layer=layer),
        grid=(batch, nb),
        in_specs=[pl.BlockSpec(memory_space=pltpu.SMEM), pl.BlockSpec((BLOCK, Q_WIDTH), cur),
                  kv(prv), kv(cur), kv(prv), kv(cur)],
        out_specs=pl.BlockSpec((BLOCK, Q_WIDTH), cur),
        out_shape=jax.ShapeDtypeStruct((batch * seq, Q_WIDTH), BF16),
        compiler_params=_cparams("parallel", "parallel"),
        name="attn_prompt",
    )(sinks, q, k, k, v, v)


def _attn_sample_kernel(sink_ref, q_ref, kt_ref, kn_ref, vt_ref, vn_ref, o_ref, kt_out_ref, vt_out_ref,
                        *, layer, n_seq, wbuf, n_new, whole):
    assert wbuf == LANES
    win = (lambda ref, g, h: ref[layer, g, h]) if whole else (lambda ref, g, h: ref[g, h])
    lane = lax.broadcasted_iota(jnp.int32, (1, LANES), 1)
    even = lane < HEAD_DIM
    rows = GROUP * SAMPLE_PAD
    n_pad = 2 * SAMPLE_PAD
    qt = _pmod(lax.broadcasted_iota(jnp.int32, (rows, 1), 0), SAMPLE_PAD)
    dist_c = qt + wbuf - lax.broadcasted_iota(jnp.int32, (rows, wbuf), 1)
    dist_n = qt - lax.broadcasted_iota(jnp.int32, (rows, n_pad), 1)
    mask_c = (dist_c >= 0) & (dist_c <= WINDOW)
    mask_n = (dist_n >= 0) & (dist_n <= WINDOW)
    rid = _pdiv(lax.broadcasted_iota(jnp.int32, (rows, 1), 0), SAMPLE_PAD)
    sink_cols = []
    for h in range(KV_HEADS):
        col = jnp.zeros((rows, 1), F32)
        for gq in range(GROUP):
            col = jnp.where(rid == gq, sink_ref[layer * A_HEADS + h * GROUP + gq], col)
        sink_cols.append(col)
    unroll = 2
    pad = jnp.zeros((SAMPLE_PAD, KV_WIDTH), F32)
    blocks_per_kv = GROUP // 2
    keep = lane < wbuf - n_new
    sub = lax.broadcasted_iota(jnp.int32, (SAMPLE_PAD, 1), 0)

    def new_columns(t_new):
        low = jnp.where(sub >= SAMPLE_PAD - n_new, pltpu.roll(t_new, SAMPLE_PAD - n_new, 0), 0.0)
        return jnp.concatenate([jnp.zeros((wbuf - SAMPLE_PAD, KV_WIDTH), F32), low], axis=0).T

    def body(i, carry_):
        items = [(i * unroll + u, h) for u in range(unroll) for h in range(KV_HEADS)]
        n = range(len(items))
        qs, kc, vc, kn, vn = [], [], [], [], []
        for u in range(unroll):
            g = i * unroll + u
            kn2 = _pair_kv(jnp.concatenate([kn_ref[g], pad], axis=0))
            vn2 = _pair_kv(jnp.concatenate([vn_ref[g], pad], axis=0))
            for h in range(KV_HEADS):
                parts = []
                for j in range(h * blocks_per_kv, (h + 1) * blocks_per_kv):
                    qb = q_ref[g, :, j * LANES:(j + 1) * LANES] * ATTN_SCALE
                    parts += [jnp.where(even, qb, 0.0), jnp.where(even, 0.0, qb)]
                qs.append(jnp.concatenate(parts, axis=0))
                kt, vt = win(kt_ref, g, h), win(vt_ref, g, h)
                kc.append(jnp.concatenate([kt, kt], axis=0))
                vc.append(jnp.concatenate([vt, vt], axis=0))
                kn.append(kn2[h])
                vn.append(vn2[h])
        s_c = [jnp.where(mask_c, _dot(qs[j], kc[j]), -jnp.inf) for j in n]
        s_n = [jnp.where(mask_n, _dot_nt(qs[j], kn[j]), -jnp.inf) for j in n]
        m = [jnp.maximum(jnp.maximum(jnp.max(s_c[j], axis=-1, keepdims=True),
                                     jnp.max(s_n[j], axis=-1, keepdims=True)), sink_cols[items[j][1]])
             for j in n]
        p_c = [jnp.exp(s_c[j] - m[j]) for j in n]
        p_n = [jnp.exp(s_n[j] - m[j]) for j in n]
        den = [jnp.sum(p_c[j], axis=-1, keepdims=True) + jnp.sum(p_n[j], axis=-1, keepdims=True)
               + jnp.exp(sink_cols[items[j][1]] - m[j]) for j in n]
        o = [(_dot_nt(p_c[j], vc[j]) + _dot(p_n[j], vn[j])) / den[j] for j in n]
        for j, (g, h) in enumerate(items):
            for jj in range(blocks_per_kv):
                blk = h * blocks_per_kv + jj
                r0 = 2 * jj * SAMPLE_PAD
                o_ref[g, :, blk * LANES:(blk + 1) * LANES] = jnp.where(
                    even, o[j][r0:r0 + SAMPLE_PAD], o[j][r0 + SAMPLE_PAD:r0 + 2 * SAMPLE_PAD]).astype(o_ref.dtype)
        for u in range(unroll):
            g = i * unroll + u
            k_cols, v_cols = new_columns(kn_ref[g]), new_columns(vn_ref[g])
            for h in range(KV_HEADS):
                hs = slice(h * HEAD_DIM, (h + 1) * HEAD_DIM)
                for src, dst, cols in ((kt_ref, kt_out_ref, k_cols), (vt_ref, vt_out_ref, v_cols)):
                    slid = jnp.where(keep, pltpu.roll(win(src, g, h), wbuf - n_new, 1), cols[hs])
                    if whole:
                        dst[layer, g, h] = slid
                    else:
                        dst[g, h] = slid
            if whole:
                for l in range(kt_ref.shape[0]):
                    if l != layer:
                        kt_out_ref[l, g] = kt_ref[l, g]
                        vt_out_ref[l, g] = vt_ref[l, g]
        return carry_

    lax.fori_loop(0, n_seq // unroll, body, 0)


def _attn_sample(q, k, v, kt_all, vt_all, sinks, layer, n_seq, n_new):
    depth, b, _, _, wbuf = kt_all.shape
    whole = layer == 0
    idx = lambda i: (i, 0, 0)
    new = pl.BlockSpec((n_seq, SAMPLE_PAD, KV_WIDTH), idx)
    if whole:
        buf = pl.BlockSpec((depth, n_seq, KV_HEADS, HEAD_DIM, wbuf), lambda i: (0, i, 0, 0, 0))
    else:
        buf = pl.BlockSpec((None, n_seq, KV_HEADS, HEAD_DIM, wbuf), lambda i: (layer, i, 0, 0, 0))
    win_shape = jax.ShapeDtypeStruct(kt_all.shape, F32)
    return pl.pallas_call(
        functools.partial(_attn_sample_kernel, layer=layer, n_seq=n_seq, wbuf=wbuf, n_new=n_new, whole=whole),
        grid=(b // n_seq,),
        in_specs=[pl.BlockSpec(memory_space=pltpu.SMEM), pl.BlockSpec((n_seq, SAMPLE_PAD, Q_WIDTH), idx),
                  buf, new, buf, new],
        out_specs=[pl.BlockSpec((n_seq, SAMPLE_PAD, Q_WIDTH), idx), buf, buf],
        out_shape=[jax.ShapeDtypeStruct((b, SAMPLE_PAD, Q_WIDTH), BF16), win_shape, win_shape],
        input_output_aliases={} if whole else {2: 1, 4: 2},
        compiler_params=_cparams("parallel"),
        name="attn_sample",
    )(sinks, q, kt_all, k, vt_all, v)


def _layer_norm(x, g, b):
    mu = jnp.mean(x, axis=-1, keepdims=True)
    d = x - mu
    var = jnp.mean(d * d, axis=-1, keepdims=True)
    return d * lax.rsqrt(var + LN_EPS) * g + b


def _mix_ffn_kernel(x_ref, yr_ref, ya_ref, win_ref, wbr_ref, wba_ref, wo_ref, g1_ref, b1_ref,
                    wu_ref, wd_ref, g2_ref, b2_ref, o_ref, *, layer, ff_chunk):
    par = lambda ref: ref[layer:layer + 1, :]
    dotf = functools.partial(jnp.dot, preferred_element_type=F32)
    tm = x_ref.shape[0]
    groups = [slice(i * tm // ROW_GROUPS, (i + 1) * tm // ROW_GROUPS) for i in range(ROW_GROUPS)]
    n = range(ROW_GROUPS)
    x = [x_ref[g, :] for g in groups]
    xb = [x[i].astype(BF16) for i in n]
    gate_r = [jax.nn.sigmoid(dotf(xb[i], win_ref[:, QKV_END:QKV_END + D_MODEL])) for i in n]
    mix = [gate_r[i] * dotf(yr_ref[groups[i], :], wbr_ref[...]) for i in n]
    gate_a = [jax.nn.sigmoid(dotf(xb[i], win_ref[:, QKV_END + D_MODEL:])) for i in n]
    mix = [mix[i] + gate_a[i] * dotf(ya_ref[groups[i], :], wba_ref[...]) for i in n]
    x1 = [_layer_norm(ALPHA * x[i] + _dot(mix[i], wo_ref[...]), par(g1_ref), par(b1_ref)) for i in n]
    x1b = [x1[i].astype(BF16) for i in n]
    acc = [ALPHA * x1[i] for i in n]
    for c in range(D_FF // ff_chunk):
        cs = slice(c * ff_chunk, (c + 1) * ff_chunk)
        h = [jnp.maximum(dotf(x1b[i], wu_ref[:, cs]), 0.0) for i in n]
        acc = [acc[i] + _dot(h[i] * h[i], wd_ref[cs, :]) for i in n]
    for i in n:
        o_ref[groups[i], :] = _layer_norm(acc[i], par(g2_ref), par(b2_ref))


def _mix_ffn(x, yr, ya, P, layer, tm):
    n = x.shape[0]
    row = lambda i: (i, 0)
    ws = [P['w_in'], P['w_br_rwkv'], P['w_br_attn'], P['w_out'], P['ln1_g'], P['ln1_b'],
          P['w_ff_up'], P['w_ff_down'], P['ln2_g'], P['ln2_b']]
    wspec = lambda a: _layer_spec(a, layer, single_buffer=True) if a.ndim == 3 else _const_spec(a.shape)
    return pl.pallas_call(
        functools.partial(_mix_ffn_kernel, layer=layer, ff_chunk=1024),
        grid=(n // tm,),
        in_specs=[pl.BlockSpec((tm, D_MODEL), row), pl.BlockSpec((tm, R_WIDTH), row),
                  pl.BlockSpec((tm, Q_WIDTH), row)] + [wspec(a) for a in ws],
        out_specs=pl.BlockSpec((tm, D_MODEL), row),
        out_shape=jax.ShapeDtypeStruct((n, D_MODEL), F32),
        compiler_params=_cparams("parallel"),
        name="mix_ffn",
    )(x, yr, ya, *ws)


def _prepare_params(w_in, mu_shift, decay_base, decay_up, iclr_base, iclr_up, gate_up, k_k, k_a, r_k,
                    lnx_g, lnx_b, sinks, w_br_rwkv, w_br_attn, w_out, ln1_g, ln1_b, w_ff_up, w_ff_down,
                    ln2_g, ln2_b):
    depth = w_in.shape[0]
    zeros = jnp.zeros((depth, DECAY_LORA, R_WIDTH), F32)
    lora_up = jnp.concatenate([jnp.concatenate([decay_up, zeros], axis=2),
                               jnp.concatenate([zeros, iclr_up], axis=2)], axis=1)
    bf = lambda a: a.astype(BF16)
    return dict(
        w_in=bf(w_in), mu_shift=mu_shift, decay_base=decay_base, lora_up=bf(lora_up), iclr_base=iclr_base,
        gate_up=bf(gate_up), k_k=k_k, k_a=k_a, r_k=r_k.reshape(depth, R_WIDTH), lnx_g=lnx_g, lnx_b=lnx_b,
        sinks=sinks.reshape(depth * A_HEADS), w_br_rwkv=bf(w_br_rwkv), w_br_attn=bf(w_br_attn),
        w_out=bf(w_out), ln1_g=ln1_g, ln1_b=ln1_b, w_ff_up=bf(w_ff_up), w_ff_down=bf(w_ff_down),
        ln2_g=ln2_g, ln2_b=ln2_b)


def _prompt_layer(x, P, layer, tables, s_all, batch, seq, wbuf):
    tm = 512
    shift0 = jnp.zeros((batch, SHIFT_W), F32)
    zs, zlast, q, k, v = _inproj(x, P, layer, tables, seq // tm, tm, shift0, seq_rows=tm, last_row=tm - 1)
    wkv_rows = tm // batch
    yr, s_all = _wkv(zs.reshape(batch, seq, SHIFT_W), s_all, P, layer, seq_rows=wkv_rows)
    ya = _attn_prompt(q, k, v, P['sinks'], layer, batch, seq)
    x = _mix_ffn(x, yr.reshape(batch * seq, R_WIDTH), ya, P, layer, tm)
    tail = lambda t, w: t.reshape(batch, seq, w)[:, seq - wbuf:].reshape(batch, wbuf, KV_HEADS, HEAD_DIM)
    return x, s_all, zlast.reshape(batch, SHIFT_W), tail(k, KV_WIDTH), tail(v, KV_WIDTH)


def _sample_layer(x, P, layer, tables, shift_prev, s_all, kt_all, vt_all, batch, seq):
    n = batch * seq
    tm = min(n, 512)
    n_seq = 16
    zs, zlast, q, k, v = _inproj(x, P, layer, tables, 1, tm, shift_prev, seq_rows=seq, last_row=seq - 1)
    yr, s_all = _wkv_short(zs, s_all, P, layer, seq)
    pad = lambda t, w: jnp.pad(t.reshape(batch, seq, w), ((0, 0), (0, SAMPLE_PAD - seq), (0, 0)))
    ya, kt_all, vt_all = _attn_sample(pad(q, Q_WIDTH), pad(k, KV_WIDTH), pad(v, KV_WIDTH), kt_all, vt_all,
                                      P['sinks'], layer, n_seq, seq)
    x = _mix_ffn(x, yr, ya[:, :seq].reshape(n, Q_WIDTH), P, layer, tm)
    return x, s_all, kt_all, vt_all, zlast


def kernel(x_prompt, x_sample, state_wkv, state_shift, cache_k_win, cache_v_win, w_in, mu_shift, decay_base, decay_up, iclr_base, iclr_up, gate_up, k_k, k_a, r_k, lnx_g, lnx_b, sinks, w_br_rwkv, w_br_attn, w_out, ln1_g, ln1_b, w_ff_up, w_ff_down, ln2_g, ln2_b):
    bp, tp, _ = x_prompt.shape
    bs, ts, _ = x_sample.shape
    wbuf = cache_k_win.shape[2]
    half = HEAD_DIM // 2
    inv_freq = ROPE_THETA ** (-jnp.arange(half, dtype=F32) / half)
    tab_p = _rope_tables(inv_freq, tp, 0, tp)
    tab_s = _rope_tables(inv_freq, min(bs * ts, 512), PAST_LEN, ts)

    hp = x_prompt.reshape(bp * tp, D_MODEL)
    hs = x_sample.reshape(bs * ts, D_MODEL)
    outs_p, outs_s = [], []
    P = _prepare_params(w_in, mu_shift, decay_base, decay_up, iclr_base, iclr_up, gate_up, k_k, k_a, r_k,
                        lnx_g, lnx_b, sinks, w_br_rwkv, w_br_attn, w_out, ln1_g, ln1_b, w_ff_up,
                        w_ff_down, ln2_g, ln2_b)
    to_lanes = lambda c: jnp.transpose(c, (0, 1, 3, 4, 2))
    from_lanes = lambda c: jnp.transpose(c, (0, 1, 4, 2, 3))
    kt_all, vt_all = to_lanes(cache_k_win), to_lanes(cache_v_win)
    s_all_p = jnp.zeros((DEPTH, bp, R_HEADS, R_HEAD_DIM, R_HEAD_DIM), F32)
    s_all_s = jnp.transpose(state_wkv, (0, 2, 3, 4, 1))
    for l in range(DEPTH):
        hp, s_all_p, *st = _prompt_layer(hp, P, l, tab_p, s_all_p, bp, tp, wbuf)
        outs_p.append(st)
        hs, s_all_s, kt_all, vt_all, zlast = _sample_layer(hs, P, l, tab_s, state_shift[l], s_all_s, kt_all, vt_all,
                                                           bs, ts)
        outs_s.append(zlast)
    stack = lambda outs, i: jnp.stack([o[i] for o in outs])
    y_p = hp.reshape(bp, tp, D_MODEL)
    y_s = hs.reshape(bs, ts, D_MODEL)
    return (y_p, y_s,
            s_all_p, stack(outs_p, 0), stack(outs_p, 1), stack(outs_p, 2),
            jnp.transpose(s_all_s, (0, 4, 1, 2, 3)), jnp.stack(outs_s), from_lanes(kt_all), from_lanes(vt_all))
```

```python
import functools
import math

import jax
import jax.numpy as jnp
from jax import lax
from jax.experimental import pallas as pl
from jax.experimental.pallas import tpu as pltpu

F32 = jnp.float32
BF16 = jnp.bfloat16

D_MODEL = 1024
DEPTH = 2
PAST_LEN = 8192
R_HEADS = 8
R_HEAD_DIM = 64
R_WIDTH = R_HEADS * R_HEAD_DIM
DECAY_LORA = 64
ICLR_LORA = 64
GATE_LORA = 128
SHIFT_W = 3 * R_WIDTH + DECAY_LORA + ICLR_LORA + GATE_LORA
A_HEADS = 8
KV_HEADS = 2
HEAD_DIM = 64
Q_WIDTH = A_HEADS * HEAD_DIM
KV_WIDTH = KV_HEADS * HEAD_DIM
GROUP = A_HEADS // KV_HEADS
WINDOW = 128
BLOCK = 128
ROPE_THETA = 10000.0
ATTN_SCALE = HEAD_DIM ** -0.5
D_FF = 4 * D_MODEL
ALPHA = (2 * DEPTH) ** 0.25
LN_EPS = 1e-5
GN_EPS = 64e-5
QKV_END = SHIFT_W + Q_WIDTH + 2 * KV_WIDTH

LANES = 128
SUBLANES = 8
VMEM_LIMIT = 56 * 1024 * 1024

SAMPLE_PAD = SUBLANES
N_PAIRS = R_HEADS // 2
WKV_CHUNK = LANES // 2
ROW_GROUPS = 2
SHIFT_BLOCK = 2 * LANES
assert SHIFT_W % SHIFT_BLOCK == 0
PHASE1_CHUNKS = 4


def _cparams(*sem):
    return pltpu.CompilerParams(dimension_semantics=sem, vmem_limit_bytes=VMEM_LIMIT)


def _const_spec(shape):
    nd = len(shape)
    return pl.BlockSpec(shape, lambda *_: (0,) * nd)


def _layer_spec(a, layer, single_buffer=False):
    mode = pl.Buffered(1) if single_buffer else None
    return pl.BlockSpec((None,) + a.shape[1:], lambda *_: (layer, 0, 0), pipeline_mode=mode)


def _dot(a, b):
    return jnp.dot(a.astype(BF16), b.astype(BF16), preferred_element_type=F32)


def _dot_nt(a, b):
    return lax.dot_general(a.astype(BF16), b.astype(BF16), (((1,), (1,)), ((), ())),
                           preferred_element_type=F32)


def _pmod(x, n):
    assert n & (n - 1) == 0
    return x & (n - 1)


def _pdiv(x, n):
    assert n & (n - 1) == 0
    return x >> (n.bit_length() - 1)


def _split(x):
    hi = x.astype(BF16)
    lo = (x - hi.astype(F32)).astype(BF16)
    return hi, lo


def _dot2_nt(a, b):
    bh, bl = _split(b)
    d = functools.partial(lax.dot_general, dimension_numbers=(((1,), (1,)), ((), ())),
                          preferred_element_type=F32)
    ab = a.astype(BF16)
    return d(ab, bh) + d(ab, bl)


def _rope_table_kernel(invf_ref, cos_ref, sa_ref, sb_ref, *, rows, offset, period):
    i = pl.program_id(0)
    row = lax.broadcasted_iota(jnp.int32, (rows, LANES), 0) + i * rows
    pos = offset + _pmod(row, period)
    ang = pos.astype(F32) * invf_ref[...]
    lane = lax.broadcasted_iota(jnp.int32, (rows, LANES), 1)
    first = _pmod(lane, HEAD_DIM) < (HEAD_DIM // 2)
    c = jnp.cos(ang)
    s = jnp.sin(ang)
    cos_ref[...] = c
    sa_ref[...] = jnp.where(first, -s, 0.0)
    sb_ref[...] = jnp.where(first, 0.0, s)


def _rope_tables(inv_freq, n_rows, offset, period):
    rows = min(n_rows, 1024)
    invf = jnp.tile(inv_freq, LANES // (HEAD_DIM // 2)).reshape(1, LANES)
    out = jax.ShapeDtypeStruct((n_rows, LANES), F32)
    return pl.pallas_call(
        functools.partial(_rope_table_kernel, rows=rows, offset=offset, period=period),
        grid=(n_rows // rows,),
        in_specs=[_const_spec((1, LANES))],
        out_specs=[pl.BlockSpec((rows, LANES), lambda i: (i, 0))] * 3,
        out_shape=[out] * 3,
        compiler_params=_cparams("parallel"),
        name="rope_tables",
    )(invf)


def _inproj_kernel(x_ref, w_ref, cos_ref, sa_ref, sb_ref, mu_ref, shift_ref,
                   zs_ref, zlast_ref, q_ref, k_ref, v_ref, carry, first_s, zraw_s,
                   *, layer, n_seq, seq_rows, last_row, tiles_per_seq):
    xb = x_ref[...].astype(BF16)
    tm = xb.shape[0]
    t_step = lax.rem(pl.program_id(0), tiles_per_seq)

    @pl.when(pl.program_id(0) == 0)
    def _():
        first_s[...] = jnp.zeros(first_s.shape, F32)

    @pl.when(t_step == 0)
    def _():
        carry[...] = shift_ref[0] if n_seq == 1 else shift_ref[...]
    lane_groups = SHIFT_BLOCK // LANES
    for c in range(SHIFT_W // LANES):
        if n_seq == 1:
            first_s[c, 0:1, :] = carry[:, c * LANES:(c + 1) * LANES]
        else:
            first_s[c, pl.ds(0, n_seq, stride=seq_rows), :] = carry[:, c * LANES:(c + 1) * LANES]
    is_first = _pmod(lax.broadcasted_iota(jnp.int32, (tm, 1), 0), seq_rows) == 0
    for j in range(SHIFT_W // SHIFT_BLOCK):
        cs = slice(j * SHIFT_BLOCK, (j + 1) * SHIFT_BLOCK)
        z = jnp.dot(xb, w_ref[:, cs], preferred_element_type=F32)
        first = jnp.concatenate([first_s[j * lane_groups + c] for c in range(lane_groups)], axis=1)
        prev = jnp.where(is_first, first, pltpu.roll(z, 1, 0))
        zs_ref[:, cs] = z + (prev - z) * mu_ref[layer:layer + 1, cs]
        if n_seq == 1:
            carry[:, cs] = z[last_row:last_row + 1, :]
        else:
            for c in range(lane_groups):
                zraw_s[c] = z[:, c * LANES:(c + 1) * LANES]
                carry[:, cs.start + c * LANES:cs.start + (c + 1) * LANES] = zraw_s[
                    c, pl.ds(last_row, n_seq, stride=seq_rows), :]
    if n_seq == 1:
        zlast_ref[0] = carry[...]
    else:
        zlast_ref[...] = carry[...]
    cos, sa, sb = cos_ref[...], sa_ref[...], sb_ref[...]

    def rope(t):
        return (t * cos + pltpu.roll(t, LANES - HEAD_DIM // 2, 1) * sa
                + pltpu.roll(t, HEAD_DIM // 2, 1) * sb)

    zq = jnp.dot(xb, w_ref[:, SHIFT_W:SHIFT_W + Q_WIDTH], preferred_element_type=F32)
    for j in range(Q_WIDTH // LANES):
        q_ref[:, j * LANES:(j + 1) * LANES] = rope(zq[:, j * LANES:(j + 1) * LANES])
    zkv = jnp.dot(xb, w_ref[:, SHIFT_W + Q_WIDTH:QKV_END], preferred_element_type=F32)
    k_ref[...] = rope(zkv[:, :KV_WIDTH])
    v_ref[...] = zkv[:, KV_WIDTH:]


def _inproj(x, P, layer, tables, tab_blocks, tm, shift_in, *, seq_rows, last_row):
    n = x.shape[0]
    b = shift_in.shape[0]
    n_seq = tm // seq_rows
    if n_seq == 1:
        tiles_per_seq = (n // b) // tm
        shift_in = shift_in.reshape(b, 1, SHIFT_W)
        seq_spec = pl.BlockSpec((1, 1, SHIFT_W), lambda i: (i // tiles_per_seq, 0, 0))
    else:
        tiles_per_seq = 1
        seq_spec = pl.BlockSpec((n_seq, SHIFT_W), lambda i: (i, 0))
    row = lambda i: (i, 0)
    tab = pl.BlockSpec((tm, LANES), lambda i: (lax.rem(i, tab_blocks), 0))
    w_spec = pl.BlockSpec((None, D_MODEL, QKV_END), lambda i: (layer, 0, 0))
    return pl.pallas_call(
        functools.partial(_inproj_kernel, layer=layer, n_seq=n_seq, seq_rows=seq_rows, last_row=last_row,
                          tiles_per_seq=tiles_per_seq),
        grid=(n // tm,),
        in_specs=[pl.BlockSpec((tm, D_MODEL), row), w_spec, tab, tab, tab, _const_spec(P['mu_shift'].shape),
                  seq_spec],
        out_specs=[pl.BlockSpec((tm, SHIFT_W), row), seq_spec, pl.BlockSpec((tm, Q_WIDTH), row),
                   pl.BlockSpec((tm, KV_WIDTH), row), pl.BlockSpec((tm, KV_WIDTH), row)],
        out_shape=[jax.ShapeDtypeStruct((n, SHIFT_W), F32), jax.ShapeDtypeStruct(shift_in.shape, F32),
                   jax.ShapeDtypeStruct((n, Q_WIDTH), F32),
                   jax.ShapeDtypeStruct((n, KV_WIDTH), F32), jax.ShapeDtypeStruct((n, KV_WIDTH), F32)],
        scratch_shapes=[pltpu.VMEM((n_seq, SHIFT_W), F32), pltpu.VMEM((SHIFT_W // LANES, tm, LANES), F32),
                        pltpu.VMEM((SHIFT_BLOCK // LANES, tm, LANES), F32)],
        compiler_params=_cparams("arbitrary"),
        name="inproj",
    )(x, P['w_in'], *tables, P['mu_shift'], shift_in)


def _pair_block_diag():
    bi = _pdiv(lax.broadcasted_iota(jnp.int32, (LANES, LANES), 0), R_HEAD_DIM)
    bj = _pdiv(lax.broadcasted_iota(jnp.int32, (LANES, LANES), 1), R_HEAD_DIM)
    return (bi == bj).astype(F32)


def _head_sum(x):
    ones = _pair_block_diag().astype(BF16)
    xb = x.astype(BF16)
    return jnp.concatenate([jnp.dot(xb[:, j * LANES:(j + 1) * LANES], ones, preferred_element_type=F32)
                            for j in range(x.shape[1] // LANES)], axis=1)


def _token_features(zs, decay_base, lora_up, iclr_base, gate_up, k_k, k_a, r_k):
    o1, o2, o3 = R_WIDTH, 2 * R_WIDTH, 3 * R_WIDTH
    o4 = o3 + DECAY_LORA + ICLR_LORA
    r = zs[:, :o1]
    k = zs[:, o1:o2]
    v = zs[:, o2:o3]
    wa = zs[:, o3:o4]
    gd = zs[:, o4:]
    lane = lax.broadcasted_iota(jnp.int32, (1, LANES), 1)
    lora_in = jnp.where(lane < DECAY_LORA, jnp.tanh(wa), wa)
    lora = _dot(lora_in, lora_up)
    pre_w = decay_base + lora[:, :R_WIDTH]
    logw = -math.exp(-0.5) * jax.nn.sigmoid(pre_w)
    a = jax.nn.sigmoid(iclr_base + lora[:, R_WIDTH:])
    g = _dot(jax.nn.sigmoid(gd), gate_up)
    kk = k * k_k
    kk = kk * lax.rsqrt(jnp.maximum(_head_sum(kk * kk), 1e-24))
    kp = k * (1.0 + (a - 1.0) * k_a)
    bonus = _head_sum(r * kp * r_k) * v
    return r, kp, v, kk, a, logw, g, bonus


def _group_norm_out(y, bonus, g, lnx_g, lnx_b):
    inv_n = 1.0 / R_HEAD_DIM
    d = y - _head_sum(y) * inv_n
    var = _head_sum(d * d) * inv_n
    return (d * lax.rsqrt(var + GN_EPS) * lnx_g + lnx_b + bonus) * g


def _wkv_kernel(zs_ref, sin_ref, db_ref, lora_ref, ib_ref, gu_ref, kk_ref, ka_ref,
                rk_ref, lg_ref, lb_ref,
                y_ref, sout_ref,
                sbd, r_s, k_s, v_s, kk_s, b_s, lw_s, cw_s, y_s, bon_s, g_s, xg_s, uy_s, bk_s,
                *, layer, n_seq, seq_rows, chunk):
    t_step = pl.program_id(1)
    n_steps = pl.num_programs(1)
    rows = n_seq * seq_rows
    n_chunks = rows // chunk
    c2 = 2 * chunk
    par = lambda ref: ref[layer:layer + 1, :]

    zs = zs_ref[...].reshape(rows, SHIFT_W)
    r, kp, v, kk, a, logw, g, bonus = _token_features(
        zs, par(db_ref), lora_ref[...], par(ib_ref), gu_ref[...], par(kk_ref), par(ka_ref), par(rk_ref))
    g_s[...] = g
    bon_s[...] = bonus
    r_s[...] = r
    k_s[...] = kp
    v_s[...] = v
    kk_s[...] = kk
    b_s[...] = kk * a
    lw_s[...] = logw

    ci = lax.broadcasted_iota(jnp.int32, (chunk, chunk), 0)
    cj = lax.broadcasted_iota(jnp.int32, (chunk, chunk), 1)
    tril_ones = (cj <= ci).astype(BF16)
    lw_hi, lw_lo = _split(logw)
    for c in range(n_chunks):
        cr = slice(c * chunk, (c + 1) * chunk)
        cw_s[cr, :] = (jnp.dot(tril_ones, lw_hi[cr], preferred_element_type=F32)
                       + jnp.dot(tril_ones, lw_lo[cr], preferred_element_type=F32))

    lane = lax.broadcasted_iota(jnp.int32, (1, LANES), 1)
    left = (lane < R_HEAD_DIM).astype(F32)
    right = 1.0 - left
    head_lanes = (left, right)
    m_bd = _pair_block_diag()
    ti = lax.broadcasted_iota(jnp.int32, (chunk, LANES), 0)
    tj = _pmod(lax.broadcasted_iota(jnp.int32, (chunk, LANES), 1), chunk)
    tri_strict = (tj < ti).astype(F32)
    tri_incl = (tj <= ti).astype(F32)
    eye = (tj == ti).astype(F32)
    is_left = lax.broadcasted_iota(jnp.int32, (chunk, LANES), 1) < chunk
    n_fact = int(math.log2(chunk))

    def load_state(u):
        for p in range(N_PAIRS):
            zero = jnp.zeros((R_HEAD_DIM, R_HEAD_DIM), F32)
            sbd[u * N_PAIRS + p, 0:R_HEAD_DIM, :] = jnp.concatenate([sin_ref[u, 2 * p], zero], axis=1)
            sbd[u * N_PAIRS + p, R_HEAD_DIM:LANES, :] = jnp.concatenate([zero, sin_ref[u, 2 * p + 1]], axis=1)

    def store_state(u):
        for p in range(N_PAIRS):
            sout_ref[u, 2 * p] = sbd[u * N_PAIRS + p, 0:R_HEAD_DIM, 0:R_HEAD_DIM]
            sout_ref[u, 2 * p + 1] = sbd[u * N_PAIRS + p, R_HEAD_DIM:LANES, R_HEAD_DIM:LANES]

    @pl.when(t_step == 0)
    def _():
        for u in range(n_seq):
            load_state(u)

    pairs = range(N_PAIRS)
    col = lambda p: slice(p * LANES, (p + 1) * LANES)
    stack2 = lambda t: jnp.concatenate([t * left, t * right], axis=0)
    half = lambda t: t[:, :chunk]

    def phase1(i, carry_):
        groups = [(i * PHASE1_CHUNKS + u, p) for u in range(PHASE1_CHUNKS) for p in pairs]
        at, rt, vv, bk = [], [], [], []
        for c, p in groups:
            rs = pl.ds(pl.multiple_of(c * chunk, chunk), chunk)
            cwc = cw_s[rs, col(p)]
            w_in = jnp.exp(cwc)
            w_ex = jnp.exp(cwc - lw_s[rs, col(p)])
            w_inv = jnp.exp(-cwc)
            at.append(-kk_s[rs, col(p)] * w_ex)
            rt.append(r_s[rs, col(p)] * w_in)
            vv.append(v_s[rs, col(p)])
            bk.append(jnp.concatenate([b_s[rs, col(p)] * w_inv, k_s[rs, col(p)] * w_inv], axis=0))
        items = [(gi, h) for gi in range(len(groups)) for h in range(2)]
        n = range(len(items))
        ats = [at[gi] * head_lanes[h] for gi, h in items]
        rts = [rt[gi] * head_lanes[h] for gi, h in items]
        sc = [_dot_nt(jnp.concatenate([ats[j], rts[j]], axis=0), bk[items[j][0]]) for j in n]
        top = [sc[j][:chunk] * tri_strict for j in n]
        bot = [sc[j][chunk:] * tri_incl for j in n]
        top_sw = [pltpu.roll(top[j], chunk, 1) for j in n]
        bot_sw = [pltpu.roll(bot[j], chunk, 1) for j in n]

        pt = [jnp.where(is_left, top[j], eye) for j in n]
        for _ in range(n_fact - 1):
            z = [_dot(half(pt[j]), pt[j]) for j in n]
            pt = [jnp.where(is_left, z[j], pt[j] + z[j]) for j in n]
        pt = [pt[j] + _dot(half(pt[j]), pt[j]) for j in n]
        tm = [half(pltpu.roll(pt[j], chunk, 1)) for j in n]

        ta = [_dot(tm[j], jnp.concatenate([ats[j], top_sw[j]], axis=1)) for j in n]
        ab = [_dot(half(bot[j]), ta[j]) for j in n]
        yk = [ab[j][:, LANES:] + bot_sw[j] for j in n]
        uy = [_dot(jnp.concatenate([half(ta[j][:, LANES:]), half(yk[j])], axis=0),
                   vv[items[j][0]] * head_lanes[items[j][1]]) for j in n]
        for j, (gi, h) in enumerate(items):
            c, p = groups[gi]
            idx = c * N_PAIRS + p
            xg_s[idx, h * chunk:(h + 1) * chunk, :] = ta[j][:, :LANES]
            xg_s[idx, c2 + h * chunk:c2 + (h + 1) * chunk, :] = rts[j] + ab[j][:, :LANES]
            uy_s[idx, h * chunk:(h + 1) * chunk, :] = uy[j][:chunk]
            uy_s[idx, c2 + h * chunk:c2 + (h + 1) * chunk, :] = uy[j][chunk:]
        for gi, (c, p) in enumerate(groups):
            bk_s[c * N_PAIRS + p] = bk[gi]
        return carry_

    lax.fori_loop(0, n_chunks // PHASE1_CHUNKS, phase1, 0)

    chunks_per_seq = seq_rows // chunk

    def phase2(i, carry_):
        cks = [u * chunks_per_seq + i for u in range(n_seq)]
        items = [(u, p) for u in range(n_seq) for p in pairs]
        n = range(len(items))
        tok = [pl.ds(pl.multiple_of(ck * chunk, chunk), chunk) for ck in cks]
        tail = [pl.ds(pl.multiple_of(ck * chunk + chunk - SUBLANES, SUBLANES), SUBLANES) for ck in cks]
        s0 = [sbd[u * N_PAIRS + p] for u, p in items]
        uyv = [_dot_nt(xg_s[cks[u] * N_PAIRS + p], s0[j]) + uy_s[cks[u] * N_PAIRS + p]
               for j, (u, p) in enumerate(items)]
        uvt = [jnp.concatenate([uyv[j][:c2], stack2(v_s[tok[u], col(p)])], axis=0).T
               for j, (u, p) in enumerate(items)]
        rhs = []
        for u, p in items:
            bkc = bk_s[cks[u] * N_PAIRS + p]
            bt, kt = bkc[:chunk], bkc[chunk:]
            rhs.append(jnp.concatenate([bt, bt, kt, kt], axis=0))
        upd = [_dot(uvt[j], rhs[j]) for j in n]
        for j, (u, p) in enumerate(items):
            w_end = jnp.exp(cw_s[tail[u], col(p)][SUBLANES - 1:SUBLANES, :])
            sbd[u * N_PAIRS + p] = (s0[j] + upd[j] * m_bd) * w_end
            y_s[tok[u], col(p)] = uyv[j][c2:c2 + chunk] + uyv[j][c2 + chunk:]
        return carry_

    lax.fori_loop(0, chunks_per_seq, phase2, 0)

    @pl.when(t_step == n_steps - 1)
    def _():
        for u in range(n_seq):
            store_state(u)

    out = _group_norm_out(y_s[...], bon_s[...], g_s[...], par(lg_ref), par(lb_ref))
    y_ref[...] = out.reshape(n_seq, seq_rows, R_WIDTH).astype(y_ref.dtype)


def _wkv(zs, s_all, P, layer, *, seq_rows):
    b, t_len, _ = zs.shape
    n_seq = b
    rows = n_seq * seq_rows
    chunk = WKV_CHUNK
    n_pc = (rows // chunk) * N_PAIRS
    assert seq_rows % chunk == 0 and rows % (chunk * PHASE1_CHUNKS) == 0 and t_len % seq_rows == 0
    params = [P['decay_base'], P['lora_up'], P['iclr_base'], P['gate_up'], P['k_k'], P['k_a'],
              P['r_k'], P['lnx_g'], P['lnx_b']]
    param_specs = [_layer_spec(a, layer) if a.ndim == 3 else _const_spec(a.shape) for a in params]
    tok = lambda g, t: (g, t, 0)
    state_spec = pl.BlockSpec((None, n_seq, R_HEADS, R_HEAD_DIM, R_HEAD_DIM), lambda g, t: (layer, g, 0, 0, 0))
    tile = lambda w: pltpu.VMEM((rows, w), F32)
    return pl.pallas_call(
        functools.partial(_wkv_kernel, layer=layer, n_seq=n_seq, seq_rows=seq_rows, chunk=chunk),
        grid=(b // n_seq, t_len // seq_rows),
        in_specs=[pl.BlockSpec((n_seq, seq_rows, SHIFT_W), tok), state_spec] + param_specs,
        out_specs=[pl.BlockSpec((n_seq, seq_rows, R_WIDTH), tok), state_spec],
        out_shape=[jax.ShapeDtypeStruct((b, t_len, R_WIDTH), BF16), jax.ShapeDtypeStruct(s_all.shape, F32)],
        input_output_aliases={1: 1},
        scratch_shapes=[pltpu.VMEM((n_seq * N_PAIRS, LANES, LANES), F32)] + [tile(R_WIDTH)] * 10 + [
            pltpu.VMEM((n_pc, 4 * chunk, LANES), F32), pltpu.VMEM((n_pc, 4 * chunk, LANES), F32),
            pltpu.VMEM((n_pc, 2 * chunk, LANES), F32)],
        compiler_params=_cparams("parallel", "arbitrary"),
        name="wkv7",
    )(zs, s_all, *params)


def _wkv_prep_kernel(zs_ref, db_ref, lora_ref, ib_ref, gu_ref, kk_ref, ka_ref, rk_ref,
                     r_ref, k_ref, v_ref, kko_ref, b_ref, w_ref, g_ref, bon_ref, *, layer):
    par = lambda ref: ref[layer:layer + 1, :]
    r, kp, v, kk, a, logw, g, bonus = _token_features(
        zs_ref[...], par(db_ref), lora_ref[...], par(ib_ref), gu_ref[...], par(kk_ref), par(ka_ref), par(rk_ref))
    r_ref[...] = r
    k_ref[...] = kp
    v_ref[...] = v
    kko_ref[...] = kk
    b_ref[...] = kk * a
    w_ref[...] = jnp.exp(logw)
    g_ref[...] = g
    bon_ref[...] = bonus


LANE_GROUP = 4


def _wkv_lanes_kernel(r_ref, k_ref, v_ref, kk_ref, b_ref, w_ref, g_ref, bon_ref, lg_ref, lb_ref, s_ref,
                      y_ref, sout_ref, ft_s, yt_s, ytok_s, *, layer, n_new, whole_state):
    nb = ft_s.shape[-1]
    n = R_HEAD_DIM
    feats = (r_ref, k_ref, v_ref, kk_ref, b_ref, w_ref)
    for fi, ref in enumerate(feats):
        for t in range(n_new):
            ft_s[fi, t] = ref[pl.ds(t, nb, stride=n_new), :].T
    load_s = (lambda hh, v: s_ref[layer, hh, v]) if whole_state else (lambda hh, v: s_ref[hh, v])

    def store_s(hh, v, val):
        if whole_state:
            sout_ref[layer, hh, v] = val
        else:
            sout_ref[hh, v] = val

    for hh in range(2):
        hs = slice(hh * n, (hh + 1) * n)

        def rows8(vg, carry_):
            v0 = pl.multiple_of(vg * SUBLANES, SUBLANES)
            vrows = [ft_s[2, t, pl.ds(hh * n + v0, SUBLANES), :] for t in range(n_new)]
            ys = [[None] * SUBLANES for _ in range(n_new)]
            for j0 in range(0, SUBLANES, LANE_GROUP):
                js = range(j0, j0 + LANE_GROUP)
                sv = {j: load_s(hh, v0 + j) for j in js}
                for t in range(n_new):
                    nk, w, bb = -ft_s[3, t, hs, :], ft_s[5, t, hs, :], ft_s[4, t, hs, :]
                    kt, rt = ft_s[1, t, hs, :], ft_s[0, t, hs, :]
                    sa = {j: jnp.sum(sv[j] * nk, axis=0, keepdims=True) for j in js}
                    sv = {j: sv[j] * w + sa[j] * bb + vrows[t][j:j + 1, :] * kt for j in js}
                    for j in js:
                        ys[t][j] = jnp.sum(sv[j] * rt, axis=0, keepdims=True)
                for j in js:
                    store_s(hh, v0 + j, sv[j])
            for t in range(n_new):
                yt_s[t, pl.ds(hh * n + v0, SUBLANES), :] = jnp.concatenate(ys[t], axis=0)
            return carry_

        lax.fori_loop(0, n // SUBLANES, rows8, 0)

    if whole_state:
        for l in range(s_ref.shape[0]):
            if l != layer:
                sout_ref[l] = s_ref[l]

    for t in range(n_new):
        ytok_s[pl.ds(t, nb, stride=n_new), :] = yt_s[t].T
    out = _group_norm_out(ytok_s[...], bon_ref[...], g_ref[...], lg_ref[layer:layer + 1, :],
                          lb_ref[layer:layer + 1, :])
    y_ref[...] = out.astype(y_ref.dtype)


def _wkv_short(zs, s_lanes, P, layer, n_new):
    n_tok = zs.shape[0]
    depth, b = s_lanes.shape[0], s_lanes.shape[-1]
    assert n_tok == b * n_new
    rows = 256
    params = [P['decay_base'], P['lora_up'], P['iclr_base'], P['gate_up'], P['k_k'], P['k_a'], P['r_k']]
    param_specs = [_layer_spec(a, layer) if a.ndim == 3 else _const_spec(a.shape) for a in params]
    tok_out = pl.BlockSpec((rows, R_WIDTH), lambda g: (g, 0))
    feats = pl.pallas_call(
        functools.partial(_wkv_prep_kernel, layer=layer),
        grid=(n_tok // rows,),
        in_specs=[pl.BlockSpec((rows, SHIFT_W), lambda g: (g, 0))] + param_specs,
        out_specs=[tok_out] * 8,
        out_shape=[jax.ShapeDtypeStruct((n_tok, R_WIDTH), F32)] * 8,
        compiler_params=_cparams("parallel"),
        name="wkv7_prep",
    )(zs, *params)

    whole = layer == 0
    pair_col = pl.BlockSpec((n_tok, LANES), lambda p: (0, p))
    par_col = pl.BlockSpec((depth, LANES), lambda p: (0, p))
    n = R_HEAD_DIM
    if whole:
        state_spec = pl.BlockSpec((depth, 2, n, n, b), lambda p: (0, p, 0, 0, 0))
    else:
        state_spec = pl.BlockSpec((None, 2, n, n, b), lambda p: (layer, p, 0, 0, 0))
    y, s_new = pl.pallas_call(
        functools.partial(_wkv_lanes_kernel, layer=layer, n_new=n_new, whole_state=whole),
        grid=(N_PAIRS,),
        in_specs=[pair_col] * 8 + [par_col, par_col, state_spec],
        out_specs=[pair_col, state_spec],
        out_shape=[jax.ShapeDtypeStruct((n_tok, R_WIDTH), BF16), jax.ShapeDtypeStruct(s_lanes.shape, F32)],
        scratch_shapes=[pltpu.VMEM((6, n_new, LANES, b), F32), pltpu.VMEM((n_new, LANES, b), F32),
                        pltpu.VMEM((n_tok, LANES), F32)],
        input_output_aliases={} if whole else {10: 1},
        compiler_params=_cparams("arbitrary"),
        name="wkv7_lanes",
    )(*feats, P['lnx_g'], P['lnx_b'], s_lanes)
    return y, s_new


def _pair_kv(t):
    lane = lax.broadcasted_iota(jnp.int32, (1, LANES), 1)
    even = lane < HEAD_DIM
    sw = pltpu.roll(t, HEAD_DIM, 1)
    return jnp.where(even, t, sw), jnp.where(even, sw, t)


def _attend(qs, kcs, vcs, masks, sink_cols):
    n = range(len(qs))
    s = [jnp.where(masks[i], _dot_nt(qs[i], kcs[i]), -jnp.inf) for i in n]
    m = [jnp.maximum(jnp.max(s[i], axis=-1, keepdims=True), sink_cols[i]) for i in n]
    p = [jnp.exp(s[i] - m[i]) for i in n]
    den = [jnp.sum(p[i], axis=-1, keepdims=True) + jnp.exp(sink_cols[i] - m[i]) for i in n]
    o = [_dot(p[i], vcs[i]) for i in n]
    return [o[i] / den[i] for i in n]


def _attn_prompt_kernel(sink_ref, q_ref, kp_ref, kc_ref, vp_ref, vc_ref, o_ref, *, layer, q_blocks):
    n = pl.program_id(1)
    sink = lambda head: sink_ref[layer * A_HEADS + head]
    lane = lax.broadcasted_iota(jnp.int32, (1, LANES), 1)
    even = lane < HEAD_DIM
    ks = _pair_kv(jnp.concatenate([kp_ref[...], kc_ref[...]], axis=0))
    vs = _pair_kv(jnp.concatenate([vp_ref[...], vc_ref[...]], axis=0))
    qi = _pmod(lax.broadcasted_iota(jnp.int32, (2 * BLOCK, 2 * BLOCK), 0), BLOCK) + BLOCK
    kj = lax.broadcasted_iota(jnp.int32, (2 * BLOCK, 2 * BLOCK), 1)
    dist = qi - kj
    in_win = (dist >= 0) & (dist <= WINDOW)
    first_mask = in_win & ((n > 0) | (kj >= BLOCK))
    top = lax.broadcasted_iota(jnp.int32, (2 * BLOCK, 1), 0) < BLOCK
    blocks = range(Q_WIDTH // LANES)
    items = [(s, j) for s in range(q_blocks) for j in blocks]
    qs, kcs, vcs, masks, sink_cols = [], [], [], [], []
    for s, j in items:
        qb = q_ref[s * BLOCK:(s + 1) * BLOCK, j * LANES:(j + 1) * LANES] * ATTN_SCALE
        qs.append(jnp.concatenate([jnp.where(even, qb, 0.0), jnp.where(even, 0.0, qb)], axis=0))
        kvh = (2 * j) // GROUP
        kcs.append(ks[kvh][s * BLOCK:(s + 2) * BLOCK])
        vcs.append(vs[kvh][s * BLOCK:(s + 2) * BLOCK])
        masks.append(first_mask if s == 0 else in_win)
        sink_cols.append(jnp.where(top, sink(2 * j), sink(2 * j + 1)))
    o = _attend(qs, kcs, vcs, masks, sink_cols)
    for i, (s, j) in enumerate(items):
        o_ref[s * BLOCK:(s + 1) * BLOCK, j * LANES:(j + 1) * LANES] = jnp.where(
            even, o[i][:BLOCK], o[i][BLOCK:]).astype(o_ref.dtype)


def _attn_prompt(q, k, v, sinks, layer, batch, seq):
    q_blocks = 4
    rows = q_blocks * BLOCK
    steps = seq // rows
    cur = lambda b, n: (b * steps + n, 0)
    prv = lambda b, n: (b * steps * q_blocks + jnp.maximum(n * q_blocks - 1, 0), 0)
    return pl.pallas_call(
        functools.partial(_attn_prompt_kernel, layer=layer, q_blocks=q_blocks),
        grid=(batch, steps),
        in_specs=[pl.BlockSpec(memory_space=pltpu.SMEM), pl.BlockSpec((rows, Q_WIDTH), cur),
                  pl.BlockSpec((BLOCK, KV_WIDTH), prv), pl.BlockSpec((rows, KV_WIDTH), cur),
                  pl.BlockSpec((BLOCK, KV_WIDTH), prv), pl.BlockSpec((rows, KV_WIDTH), cur)],
        out_specs=pl.BlockSpec((rows, Q_WIDTH), cur),
        out_shape=jax.ShapeDtypeStruct((batch * seq, Q_WIDTH), BF16),
        compiler_params=_cparams("parallel", "parallel"),
        name="attn_prompt",
    )(sinks, q, k, k, v, v)


def _attn_sample_kernel(sink_ref, q_ref, kt_ref, kn_ref, vt_ref, vn_ref, o_ref, kt_out_ref, vt_out_ref,
                        *, layer, n_seq, wbuf, n_new, whole):
    assert wbuf == LANES
    win = (lambda ref, g, h: ref[layer, g, h]) if whole else (lambda ref, g, h: ref[g, h])
    lane = lax.broadcasted_iota(jnp.int32, (1, LANES), 1)
    even = lane < HEAD_DIM
    rows = GROUP * SAMPLE_PAD
    n_pad = 2 * SAMPLE_PAD
    qt = _pmod(lax.broadcasted_iota(jnp.int32, (rows, 1), 0), SAMPLE_PAD)
    dist_c = qt + wbuf - lax.broadcasted_iota(jnp.int32, (rows, wbuf), 1)
    dist_n = qt - lax.broadcasted_iota(jnp.int32, (rows, n_pad), 1)
    mask_c = (dist_c >= 0) & (dist_c <= WINDOW)
    mask_n = (dist_n >= 0) & (dist_n <= WINDOW)
    rid = _pdiv(lax.broadcasted_iota(jnp.int32, (rows, 1), 0), SAMPLE_PAD)
    sink_cols = []
    for h in range(KV_HEADS):
        col = jnp.zeros((rows, 1), F32)
        for gq in range(GROUP):
            col = jnp.where(rid == gq, sink_ref[layer * A_HEADS + h * GROUP + gq], col)
        sink_cols.append(col)
    unroll = 2
    pad = jnp.zeros((SAMPLE_PAD, KV_WIDTH), F32)
    blocks_per_kv = GROUP // 2
    keep = lane < wbuf - n_new
    sub = lax.broadcasted_iota(jnp.int32, (SAMPLE_PAD, 1), 0)

    def new_columns(t_new):
        low = jnp.where(sub >= SAMPLE_PAD - n_new, pltpu.roll(t_new, SAMPLE_PAD - n_new, 0), 0.0)
        return jnp.concatenate([jnp.zeros((wbuf - SAMPLE_PAD, KV_WIDTH), F32), low], axis=0).T

    def body(i, carry_):
        items = [(i * unroll + u, h) for u in range(unroll) for h in range(KV_HEADS)]
        n = range(len(items))
        qs, kc, vc, kn, vn = [], [], [], [], []
        for u in range(unroll):
            g = i * unroll + u
            kn2 = _pair_kv(jnp.concatenate([kn_ref[g], pad], axis=0))
            vn2 = _pair_kv(jnp.concatenate([vn_ref[g], pad], axis=0))
            for h in range(KV_HEADS):
                parts = []
                for j in range(h * blocks_per_kv, (h + 1) * blocks_per_kv):
                    qb = q_ref[g, :, j * LANES:(j + 1) * LANES] * ATTN_SCALE
                    parts += [jnp.where(even, qb, 0.0), jnp.where(even, 0.0, qb)]
                qs.append(jnp.concatenate(parts, axis=0))
                kt, vt = win(kt_ref, g, h), win(vt_ref, g, h)
                kc.append(jnp.concatenate([kt, kt], axis=0))
                vc.append(jnp.concatenate([vt, vt], axis=0))
                kn.append(kn2[h])
                vn.append(vn2[h])
        s_c = [jnp.where(mask_c, _dot(qs[j], kc[j]), -jnp.inf) for j in n]
        s_n = [jnp.where(mask_n, _dot_nt(qs[j], kn[j]), -jnp.inf) for j in n]
        m = [jnp.maximum(jnp.maximum(jnp.max(s_c[j], axis=-1, keepdims=True),
                                     jnp.max(s_n[j], axis=-1, keepdims=True)), sink_cols[items[j][1]])
             for j in n]
        p_c = [jnp.exp(s_c[j] - m[j]) for j in n]
        p_n = [jnp.exp(s_n[j] - m[j]) for j in n]
        den = [jnp.sum(p_c[j], axis=-1, keepdims=True) + jnp.sum(p_n[j], axis=-1, keepdims=True)
               + jnp.exp(sink_cols[items[j][1]] - m[j]) for j in n]
        o = [(_dot_nt(p_c[j], vc[j]) + _dot(p_n[j], vn[j])) / den[j] for j in n]
        for j, (g, h) in enumerate(items):
            for jj in range(blocks_per_kv):
                blk = h * blocks_per_kv + jj
                r0 = 2 * jj * SAMPLE_PAD
                o_ref[g, :, blk * LANES:(blk + 1) * LANES] = jnp.where(
                    even, o[j][r0:r0 + SAMPLE_PAD], o[j][r0 + SAMPLE_PAD:r0 + 2 * SAMPLE_PAD]).astype(o_ref.dtype)
        for u in range(unroll):
            g = i * unroll + u
            k_cols, v_cols = new_columns(kn_ref[g]), new_columns(vn_ref[g])
            for h in range(KV_HEADS):
                hs = slice(h * HEAD_DIM, (h + 1) * HEAD_DIM)
                for src, dst, cols in ((kt_ref, kt_out_ref, k_cols), (vt_ref, vt_out_ref, v_cols)):
                    slid = jnp.where(keep, pltpu.roll(win(src, g, h), wbuf - n_new, 1), cols[hs])
                    if whole:
                        dst[layer, g, h] = slid
                    else:
                        dst[g, h] = slid
            if whole:
                for l in range(kt_ref.shape[0]):
                    if l != layer:
                        kt_out_ref[l, g] = kt_ref[l, g]
                        vt_out_ref[l, g] = vt_ref[l, g]
        return carry_

    lax.fori_loop(0, n_seq // unroll, body, 0)


def _attn_sample(q, k, v, kt_all, vt_all, sinks, layer, n_seq, n_new):
    depth, b, _, _, wbuf = kt_all.shape
    whole = layer == 0
    idx = lambda i: (i, 0, 0)
    new = pl.BlockSpec((n_seq, SAMPLE_PAD, KV_WIDTH), idx)
    if whole:
        buf = pl.BlockSpec((depth, n_seq, KV_HEADS, HEAD_DIM, wbuf), lambda i: (0, i, 0, 0, 0))
    else:
        buf = pl.BlockSpec((None, n_seq, KV_HEADS, HEAD_DIM, wbuf), lambda i: (layer, i, 0, 0, 0))
    win_shape = jax.ShapeDtypeStruct(kt_all.shape, F32)
    return pl.pallas_call(
        functools.partial(_attn_sample_kernel, layer=layer, n_seq=n_seq, wbuf=wbuf, n_new=n_new, whole=whole),
        grid=(b // n_seq,),
        in_specs=[pl.BlockSpec(memory_space=pltpu.SMEM), pl.BlockSpec((n_seq, SAMPLE_PAD, Q_WIDTH), idx),
                  buf, new, buf, new],
        out_specs=[pl.BlockSpec((n_seq, SAMPLE_PAD, Q_WIDTH), idx), buf, buf],
        out_shape=[jax.ShapeDtypeStruct((b, SAMPLE_PAD, Q_WIDTH), BF16), win_shape, win_shape],
        input_output_aliases={} if whole else {2: 1, 4: 2},
        compiler_params=_cparams("parallel"),
        name="attn_sample",
    )(sinks, q, kt_all, k, vt_all, v)


def _layer_norm(x, g, b):
    mu = jnp.mean(x, axis=-1, keepdims=True)
    d = x - mu
    var = jnp.mean(d * d, axis=-1, keepdims=True)
    return d * lax.rsqrt(var + LN_EPS) * g + b


def _mix_ffn_kernel(x_ref, yr_ref, ya_ref, win_ref, wbr_ref, wba_ref, wo_ref, g1_ref, b1_ref,
                    wu_ref, wd_ref, g2_ref, b2_ref, o_ref, *, layer, ff_chunk):
    par = lambda ref: ref[layer:layer + 1, :]
    dotf = functools.partial(jnp.dot, preferred_element_type=F32)
    tm = x_ref.shape[0]
    groups = [slice(i * tm // ROW_GROUPS, (i + 1) * tm // ROW_GROUPS) for i in range(ROW_GROUPS)]
    n = range(ROW_GROUPS)
    x = [x_ref[g, :] for g in groups]
    xb = [x[i].astype(BF16) for i in n]
    gate_r = [jax.nn.sigmoid(dotf(xb[i], win_ref[:, QKV_END:QKV_END + D_MODEL])) for i in n]
    mix = [gate_r[i] * dotf(yr_ref[groups[i], :], wbr_ref[...]) for i in n]
    gate_a = [jax.nn.sigmoid(dotf(xb[i], win_ref[:, QKV_END + D_MODEL:])) for i in n]
    mix = [mix[i] + gate_a[i] * dotf(ya_ref[groups[i], :], wba_ref[...]) for i in n]
    x1 = [_layer_norm(ALPHA * x[i] + _dot(mix[i], wo_ref[...]), par(g1_ref), par(b1_ref)) for i in n]
    x1b = [x1[i].astype(BF16) for i in n]
    acc = [ALPHA * x1[i] for i in n]
    for c in range(D_FF // ff_chunk):
        cs = slice(c * ff_chunk, (c + 1) * ff_chunk)
        h = [jnp.maximum(dotf(x1b[i], wu_ref[:, cs]), 0.0) for i in n]
        acc = [acc[i] + _dot(h[i] * h[i], wd_ref[cs, :]) for i in n]
    for i in n:
        o_ref[groups[i], :] = _layer_norm(acc[i], par(g2_ref), par(b2_ref))


def _mix_ffn(x, yr, ya, P, layer, tm):
    n = x.shape[0]
    row = lambda i: (i, 0)
    ws = [P['w_in'], P['w_br_rwkv'], P['w_br_attn'], P['w_out'], P['ln1_g'], P['ln1_b'],
          P['w_ff_up'], P['w_ff_down'], P['ln2_g'], P['ln2_b']]
    wspec = lambda a: _layer_spec(a, layer, single_buffer=True) if a.ndim == 3 else _const_spec(a.shape)
    return pl.pallas_call(
        functools.partial(_mix_ffn_kernel, layer=layer, ff_chunk=1024),
        grid=(n // tm,),
        in_specs=[pl.BlockSpec((tm, D_MODEL), row), pl.BlockSpec((tm, R_WIDTH), row),
                  pl.BlockSpec((tm, Q_WIDTH), row)] + [wspec(a) for a in ws],
        out_specs=pl.BlockSpec((tm, D_MODEL), row),
        out_shape=jax.ShapeDtypeStruct((n, D_MODEL), F32),
        compiler_params=_cparams("parallel"),
        name="mix_ffn",
    )(x, yr, ya, *ws)


def _prepare_params(w_in, mu_shift, decay_base, decay_up, iclr_base, iclr_up, gate_up, k_k, k_a, r_k,
                    lnx_g, lnx_b, sinks, w_br_rwkv, w_br_attn, w_out, ln1_g, ln1_b, w_ff_up, w_ff_down,
                    ln2_g, ln2_b):
    depth = w_in.shape[0]
    zeros = jnp.zeros((depth, DECAY_LORA, R_WIDTH), F32)
    lora_up = jnp.concatenate([jnp.concatenate([decay_up, zeros], axis=2),
                               jnp.concatenate([zeros, iclr_up], axis=2)], axis=1)
    bf = lambda a: a.astype(BF16)
    return dict(
        w_in=bf(w_in), mu_shift=mu_shift, decay_base=decay_base, lora_up=bf(lora_up), iclr_base=iclr_base,
        gate_up=bf(gate_up), k_k=k_k, k_a=k_a, r_k=r_k.reshape(depth, R_WIDTH), lnx_g=lnx_g, lnx_b=lnx_b,
        sinks=sinks.reshape(depth * A_HEADS), w_br_rwkv=bf(w_br_rwkv), w_br_attn=bf(w_br_attn),
        w_out=bf(w_out), ln1_g=ln1_g, ln1_b=ln1_b, w_ff_up=bf(w_ff_up), w_ff_down=bf(w_ff_down),
        ln2_g=ln2_g, ln2_b=ln2_b)


def _prompt_layer(x, P, layer, tables, s_all, batch, seq, wbuf):
    tm = 512
    tp = 1024
    shift0 = jnp.zeros((batch, SHIFT_W), F32)
    zs, zlast, q, k, v = _inproj(x, P, layer, tables, seq // tp, tp, shift0, seq_rows=tp, last_row=tp - 1)
    wkv_rows = tm // batch
    yr, s_all = _wkv(zs.reshape(batch, seq, SHIFT_W), s_all, P, layer, seq_rows=wkv_rows)
    ya = _attn_prompt(q, k, v, P['sinks'], layer, batch, seq)
    x = _mix_ffn(x, yr.reshape(batch * seq, R_WIDTH), ya, P, layer, tm)
    tail = lambda t, w: t.reshape(batch, seq, w)[:, seq - wbuf:].reshape(batch, wbuf, KV_HEADS, HEAD_DIM)
    return x, s_all, zlast.reshape(batch, SHIFT_W), tail(k, KV_WIDTH), tail(v, KV_WIDTH)


def _sample_layer(x, P, layer, tables, shift_prev, s_all, kt_all, vt_all, batch, seq):
    n = batch * seq
    tm = min(n, 512)
    n_seq = 16
    zs, zlast, q, k, v = _inproj(x, P, layer, tables, 1, tm, shift_prev, seq_rows=seq, last_row=seq - 1)
    yr, s_all = _wkv_short(zs, s_all, P, layer, seq)
    pad = lambda t, w: jnp.pad(t.reshape(batch, seq, w), ((0, 0), (0, SAMPLE_PAD - seq), (0, 0)))
    ya, kt_all, vt_all = _attn_sample(pad(q, Q_WIDTH), pad(k, KV_WIDTH), pad(v, KV_WIDTH), kt_all, vt_all,
                                      P['sinks'], layer, n_seq, seq)
    x = _mix_ffn(x, yr, ya[:, :seq].reshape(n, Q_WIDTH), P, layer, tm)
    return x, s_all, kt_all, vt_all, zlast


def kernel(x_prompt, x_sample, state_wkv, state_shift, cache_k_win, cache_v_win, w_in, mu_shift, decay_base, decay_up, iclr_base, iclr_up, gate_up, k_k, k_a, r_k, lnx_g, lnx_b, sinks, w_br_rwkv, w_br_attn, w_out, ln1_g, ln1_b, w_ff_up, w_ff_down, ln2_g, ln2_b):
    bp, tp, _ = x_prompt.shape
    bs, ts, _ = x_sample.shape
    wbuf = cache_k_win.shape[2]
    half = HEAD_DIM // 2
    inv_freq = ROPE_THETA ** (-jnp.arange(half, dtype=F32) / half)
    tab_p = _rope_tables(inv_freq, tp, 0, tp)
    tab_s = _rope_tables(inv_freq, min(bs * ts, 512), PAST_LEN, ts)

    hp = x_prompt.reshape(bp * tp, D_MODEL)
    hs = x_sample.reshape(bs * ts, D_MODEL)
    outs_p, outs_s = [], []
    P = _prepare_params(w_in, mu_shift, decay_base, decay_up, iclr_base, iclr_up, gate_up, k_k, k_a, r_k,
                        lnx_g, lnx_b, sinks, w_br_rwkv, w_br_attn, w_out, ln1_g, ln1_b, w_ff_up,
                        w_ff_down, ln2_g, ln2_b)
    to_lanes = lambda c: jnp.transpose(c, (0, 1, 3, 4, 2))
    from_lanes = lambda c: jnp.transpose(c, (0, 1, 4, 2, 3))
    kt_all, vt_all = to_lanes(cache_k_win), to_lanes(cache_v_win)
    s_all_p = jnp.zeros((DEPTH, bp, R_HEADS, R_HEAD_DIM, R_HEAD_DIM), F32)
    s_all_s = jnp.transpose(state_wkv, (0, 2, 3, 4, 1))
    for l in range(DEPTH):
        hp, s_all_p, *st = _prompt_layer(hp, P, l, tab_p, s_all_p, bp, tp, wbuf)
        outs_p.append(st)
        hs, s_all_s, kt_all, vt_all, zlast = _sample_layer(hs, P, l, tab_s, state_shift[l], s_all_s, kt_all, vt_all,
                                                           bs, ts)
        outs_s.append(zlast)
    stack = lambda outs, i: jnp.stack([o[i] for o in outs])
    y_p = hp.reshape(bp, tp, D_MODEL)
    y_s = hs.reshape(bs, ts, D_MODEL)
    return (y_p, y_s,
            s_all_p, stack(outs_p, 0), stack(outs_p, 1), stack(outs_p, 2),
            jnp.transpose(s_all_s, (0, 4, 1, 2, 3)), jnp.stack(outs_s), from_lanes(kt_all), from_lanes(vt_all))
```

```python
import functools
import math

import jax
import jax.numpy as jnp
from jax import lax
from jax.experimental import pallas as pl
from jax.experimental.pallas import tpu as pltpu

F32 = jnp.float32
BF16 = jnp.bfloat16

D_MODEL = 1024
DEPTH = 2
PAST_LEN = 8192
R_HEADS = 8
R_HEAD_DIM = 64
R_WIDTH = R_HEADS * R_HEAD_DIM
DECAY_LORA = 64
ICLR_LORA = 64
GATE_LORA = 128
SHIFT_W = 3 * R_WIDTH + DECAY_LORA + ICLR_LORA + GATE_LORA
A_HEADS = 8
KV_HEADS = 2
HEAD_DIM = 64
Q_WIDTH = A_HEADS * HEAD_DIM
KV_WIDTH = KV_HEADS * HEAD_DIM
GROUP = A_HEADS // KV_HEADS
WINDOW = 128
BLOCK = 128
ROPE_THETA = 10000.0
ATTN_SCALE = HEAD_DIM ** -0.5
D_FF = 4 * D_MODEL
ALPHA = (2 * DEPTH) ** 0.25
LN_EPS = 1e-5
GN_EPS = 64e-5
QKV_END = SHIFT_W + Q_WIDTH + 2 * KV_WIDTH

LANES = 128
SUBLANES = 8
VMEM_LIMIT = 56 * 1024 * 1024

SAMPLE_PAD = SUBLANES
N_PAIRS = R_HEADS // 2
WKV_CHUNK = LANES // 2
ROW_GROUPS = 2
SHIFT_BLOCK = 2 * LANES
assert SHIFT_W % SHIFT_BLOCK == 0
PHASE1_CHUNKS = 4


def _cparams(*sem):
    return pltpu.CompilerParams(dimension_semantics=sem, vmem_limit_bytes=VMEM_LIMIT)


def _const_spec(shape):
    nd = len(shape)
    return pl.BlockSpec(shape, lambda *_: (0,) * nd)


def _layer_spec(a, layer, single_buffer=False):
    mode = pl.Buffered(1) if single_buffer else None
    return pl.BlockSpec((None,) + a.shape[1:], lambda *_: (layer, 0, 0), pipeline_mode=mode)


def _dot(a, b):
    return jnp.dot(a.astype(BF16), b.astype(BF16), preferred_element_type=F32)


def _dot_nt(a, b):
    return lax.dot_general(a.astype(BF16), b.astype(BF16), (((1,), (1,)), ((), ())),
                           preferred_element_type=F32)


def _dot_tn(a, b):
    return lax.dot_general(a.astype(BF16), b.astype(BF16), (((0,), (0,)), ((), ())),
                           preferred_element_type=F32)


def _pmod(x, n):
    assert n & (n - 1) == 0
    return x & (n - 1)


def _pdiv(x, n):
    assert n & (n - 1) == 0
    return x >> (n.bit_length() - 1)


def _split(x):
    hi = x.astype(BF16)
    lo = (x - hi.astype(F32)).astype(BF16)
    return hi, lo


def _dot2_nt(a, b):
    bh, bl = _split(b)
    d = functools.partial(lax.dot_general, dimension_numbers=(((1,), (1,)), ((), ())),
                          preferred_element_type=F32)
    ab = a.astype(BF16)
    return d(ab, bh) + d(ab, bl)


def _rope_table_kernel(invf_ref, cos_ref, sa_ref, sb_ref, *, rows, offset, period):
    i = pl.program_id(0)
    row = lax.broadcasted_iota(jnp.int32, (rows, LANES), 0) + i * rows
    pos = offset + _pmod(row, period)
    ang = pos.astype(F32) * invf_ref[...]
    lane = lax.broadcasted_iota(jnp.int32, (rows, LANES), 1)
    first = _pmod(lane, HEAD_DIM) < (HEAD_DIM // 2)
    c = jnp.cos(ang)
    s = jnp.sin(ang)
    cos_ref[...] = c
    sa_ref[...] = jnp.where(first, -s, 0.0)
    sb_ref[...] = jnp.where(first, 0.0, s)


def _rope_tables(inv_freq, n_rows, offset, period):
    rows = min(n_rows, 1024)
    invf = jnp.tile(inv_freq, LANES // (HEAD_DIM // 2)).reshape(1, LANES)
    out = jax.ShapeDtypeStruct((n_rows, LANES), F32)
    return pl.pallas_call(
        functools.partial(_rope_table_kernel, rows=rows, offset=offset, period=period),
        grid=(n_rows // rows,),
        in_specs=[_const_spec((1, LANES))],
        out_specs=[pl.BlockSpec((rows, LANES), lambda i: (i, 0))] * 3,
        out_shape=[out] * 3,
        compiler_params=_cparams("parallel"),
        name="rope_tables",
    )(invf)


def _inproj_kernel(x_ref, w_ref, cos_ref, sa_ref, sb_ref, mu_ref, shift_ref,
                   zs_ref, zlast_ref, q_ref, k_ref, v_ref, carry, first_s, zraw_s,
                   *, layer, n_seq, seq_rows, last_row, tiles_per_seq):
    xb = x_ref[...].astype(BF16)
    tm = xb.shape[0]
    t_step = lax.rem(pl.program_id(0), tiles_per_seq)

    @pl.when(pl.program_id(0) == 0)
    def _():
        first_s[...] = jnp.zeros(first_s.shape, F32)

    @pl.when(t_step == 0)
    def _():
        carry[...] = shift_ref[0] if n_seq == 1 else shift_ref[...]
    lane_groups = SHIFT_BLOCK // LANES
    for c in range(SHIFT_W // LANES):
        if n_seq == 1:
            first_s[c, 0:1, :] = carry[:, c * LANES:(c + 1) * LANES]
        else:
            first_s[c, pl.ds(0, n_seq, stride=seq_rows), :] = carry[:, c * LANES:(c + 1) * LANES]
    is_first = _pmod(lax.broadcasted_iota(jnp.int32, (tm, 1), 0), seq_rows) == 0
    for j in range(SHIFT_W // SHIFT_BLOCK):
        cs = slice(j * SHIFT_BLOCK, (j + 1) * SHIFT_BLOCK)
        z = jnp.dot(xb, w_ref[:, cs], preferred_element_type=F32)
        first = jnp.concatenate([first_s[j * lane_groups + c] for c in range(lane_groups)], axis=1)
        prev = jnp.where(is_first, first, pltpu.roll(z, 1, 0))
        zs_ref[:, cs] = z + (prev - z) * mu_ref[layer:layer + 1, cs]
        if n_seq == 1:
            carry[:, cs] = z[last_row:last_row + 1, :]
        else:
            for c in range(lane_groups):
                zraw_s[c] = z[:, c * LANES:(c + 1) * LANES]
                carry[:, cs.start + c * LANES:cs.start + (c + 1) * LANES] = zraw_s[
                    c, pl.ds(last_row, n_seq, stride=seq_rows), :]
    if n_seq == 1:
        zlast_ref[0] = carry[...]
    else:
        zlast_ref[...] = carry[...]
    cos, sa, sb = cos_ref[...], sa_ref[...], sb_ref[...]

    def rope(t):
        return (t * cos + pltpu.roll(t, LANES - HEAD_DIM // 2, 1) * sa
                + pltpu.roll(t, HEAD_DIM // 2, 1) * sb)

    zq = jnp.dot(xb, w_ref[:, SHIFT_W:SHIFT_W + Q_WIDTH], preferred_element_type=F32)
    for j in range(Q_WIDTH // LANES):
        q_ref[:, j * LANES:(j + 1) * LANES] = rope(zq[:, j * LANES:(j + 1) * LANES])
    zkv = jnp.dot(xb, w_ref[:, SHIFT_W + Q_WIDTH:QKV_END], preferred_element_type=F32)
    k_ref[...] = rope(zkv[:, :KV_WIDTH])
    v_ref[...] = zkv[:, KV_WIDTH:]


def _inproj(x, P, layer, tables, tab_blocks, tm, shift_in, *, seq_rows, last_row):
    n = x.shape[0]
    b = shift_in.shape[0]
    n_seq = tm // seq_rows
    if n_seq == 1:
        tiles_per_seq = (n // b) // tm
        shift_in = shift_in.reshape(b, 1, SHIFT_W)
        seq_spec = pl.BlockSpec((1, 1, SHIFT_W), lambda i: (i // tiles_per_seq, 0, 0))
    else:
        tiles_per_seq = 1
        seq_spec = pl.BlockSpec((n_seq, SHIFT_W), lambda i: (i, 0))
    row = lambda i: (i, 0)
    tab = pl.BlockSpec((tm, LANES), lambda i: (lax.rem(i, tab_blocks), 0))
    w_spec = pl.BlockSpec((None, D_MODEL, QKV_END), lambda i: (layer, 0, 0))
    return pl.pallas_call(
        functools.partial(_inproj_kernel, layer=layer, n_seq=n_seq, seq_rows=seq_rows, last_row=last_row,
                          tiles_per_seq=tiles_per_seq),
        grid=(n // tm,),
        in_specs=[pl.BlockSpec((tm, D_MODEL), row), w_spec, tab, tab, tab, _const_spec(P['mu_shift'].shape),
                  seq_spec],
        out_specs=[pl.BlockSpec((tm, SHIFT_W), row), seq_spec, pl.BlockSpec((tm, Q_WIDTH), row),
                   pl.BlockSpec((tm, KV_WIDTH), row), pl.BlockSpec((tm, KV_WIDTH), row)],
        out_shape=[jax.ShapeDtypeStruct((n, SHIFT_W), F32), jax.ShapeDtypeStruct(shift_in.shape, F32),
                   jax.ShapeDtypeStruct((n, Q_WIDTH), F32),
                   jax.ShapeDtypeStruct((n, KV_WIDTH), F32), jax.ShapeDtypeStruct((n, KV_WIDTH), F32)],
        scratch_shapes=[pltpu.VMEM((n_seq, SHIFT_W), F32), pltpu.VMEM((SHIFT_W // LANES, tm, LANES), F32),
                        pltpu.VMEM((SHIFT_BLOCK // LANES, tm, LANES), F32)],
        compiler_params=_cparams("arbitrary"),
        name="inproj",
    )(x, P['w_in'], *tables, P['mu_shift'], shift_in)


def _pair_block_diag():
    bi = _pdiv(lax.broadcasted_iota(jnp.int32, (LANES, LANES), 0), R_HEAD_DIM)
    bj = _pdiv(lax.broadcasted_iota(jnp.int32, (LANES, LANES), 1), R_HEAD_DIM)
    return (bi == bj).astype(F32)


def _head_sum(x):
    ones = _pair_block_diag().astype(BF16)
    xb = x.astype(BF16)
    return jnp.concatenate([jnp.dot(xb[:, j * LANES:(j + 1) * LANES], ones, preferred_element_type=F32)
                            for j in range(x.shape[1] // LANES)], axis=1)


def _token_features(zs, decay_base, lora_up, iclr_base, gate_up, k_k, k_a, r_k):
    o1, o2, o3 = R_WIDTH, 2 * R_WIDTH, 3 * R_WIDTH
    o4 = o3 + DECAY_LORA + ICLR_LORA
    r = zs[:, :o1]
    k = zs[:, o1:o2]
    v = zs[:, o2:o3]
    wa = zs[:, o3:o4]
    gd = zs[:, o4:]
    lane = lax.broadcasted_iota(jnp.int32, (1, LANES), 1)
    lora_in = jnp.where(lane < DECAY_LORA, jnp.tanh(wa), wa)
    lora = _dot(lora_in, lora_up)
    pre_w = decay_base + lora[:, :R_WIDTH]
    logw = -math.exp(-0.5) * jax.nn.sigmoid(pre_w)
    a = jax.nn.sigmoid(iclr_base + lora[:, R_WIDTH:])
    g = _dot(jax.nn.sigmoid(gd), gate_up)
    kk = k * k_k
    kk = kk * lax.rsqrt(jnp.maximum(_head_sum(kk * kk), 1e-24))
    kp = k * (1.0 + (a - 1.0) * k_a)
    bonus = _head_sum(r * kp * r_k) * v
    return r, kp, v, kk, a, logw, g, bonus


def _group_norm_out(y, bonus, g, lnx_g, lnx_b):
    inv_n = 1.0 / R_HEAD_DIM
    d = y - _head_sum(y) * inv_n
    var = _head_sum(d * d) * inv_n
    return (d * lax.rsqrt(var + GN_EPS) * lnx_g + lnx_b + bonus) * g


def _wkv_kernel(zs_ref, sin_ref, db_ref, lora_ref, ib_ref, gu_ref, kk_ref, ka_ref,
                rk_ref, lg_ref, lb_ref,
                y_ref, sout_ref,
                sbd, r_s, k_s, v_s, kk_s, b_s, lw_s, cw_s, y_s, bon_s, g_s, xg_s, uy_s, bk_s,
                *, layer, n_seq, seq_rows, chunk):
    t_step = pl.program_id(1)
    n_steps = pl.num_programs(1)
    rows = n_seq * seq_rows
    n_chunks = rows // chunk
    c2 = 2 * chunk
    par = lambda ref: ref[layer:layer + 1, :]

    zs = zs_ref[...].reshape(rows, SHIFT_W)
    r, kp, v, kk, a, logw, g, bonus = _token_features(
        zs, par(db_ref), lora_ref[...], par(ib_ref), gu_ref[...], par(kk_ref), par(ka_ref), par(rk_ref))
    g_s[...] = g
    bon_s[...] = bonus
    r_s[...] = r
    k_s[...] = kp
    v_s[...] = v
    kk_s[...] = kk
    b_s[...] = kk * a
    lw_s[...] = logw

    ci = lax.broadcasted_iota(jnp.int32, (chunk, chunk), 0)
    cj = lax.broadcasted_iota(jnp.int32, (chunk, chunk), 1)
    tril_ones = (cj <= ci).astype(BF16)
    lw_hi, lw_lo = _split(logw)
    for c in range(n_chunks):
        cr = slice(c * chunk, (c + 1) * chunk)
        cw_s[cr, :] = (jnp.dot(tril_ones, lw_hi[cr], preferred_element_type=F32)
                       + jnp.dot(tril_ones, lw_lo[cr], preferred_element_type=F32))

    lane = lax.broadcasted_iota(jnp.int32, (1, LANES), 1)
    left = (lane < R_HEAD_DIM).astype(F32)
    right = 1.0 - left
    head_lanes = (left, right)
    m_bd = _pair_block_diag()
    ti = lax.broadcasted_iota(jnp.int32, (chunk, LANES), 0)
    tj = _pmod(lax.broadcasted_iota(jnp.int32, (chunk, LANES), 1), chunk)
    tri_strict = (tj < ti).astype(F32)
    tri_incl = (tj <= ti).astype(F32)
    eye = (tj == ti).astype(F32)
    is_left = lax.broadcasted_iota(jnp.int32, (chunk, LANES), 1) < chunk
    n_fact = int(math.log2(chunk))

    def load_state(u):
        for p in range(N_PAIRS):
            zero = jnp.zeros((R_HEAD_DIM, R_HEAD_DIM), F32)
            sbd[u * N_PAIRS + p, 0:R_HEAD_DIM, :] = jnp.concatenate([sin_ref[u, 2 * p], zero], axis=1)
            sbd[u * N_PAIRS + p, R_HEAD_DIM:LANES, :] = jnp.concatenate([zero, sin_ref[u, 2 * p + 1]], axis=1)

    def store_state(u):
        for p in range(N_PAIRS):
            sout_ref[u, 2 * p] = sbd[u * N_PAIRS + p, 0:R_HEAD_DIM, 0:R_HEAD_DIM]
            sout_ref[u, 2 * p + 1] = sbd[u * N_PAIRS + p, R_HEAD_DIM:LANES, R_HEAD_DIM:LANES]

    @pl.when(t_step == 0)
    def _():
        for u in range(n_seq):
            load_state(u)

    pairs = range(N_PAIRS)
    col = lambda p: slice(p * LANES, (p + 1) * LANES)
    stack2 = lambda t: jnp.concatenate([t * left, t * right], axis=0)
    half = lambda t: t[:, :chunk]

    def phase1(i, carry_):
        groups = [(i * PHASE1_CHUNKS + u, p) for u in range(PHASE1_CHUNKS) for p in pairs]
        at, rt, vv, bk = [], [], [], []
        for c, p in groups:
            rs = pl.ds(pl.multiple_of(c * chunk, chunk), chunk)
            cwc = cw_s[rs, col(p)]
            w_in = jnp.exp(cwc)
            w_ex = jnp.exp(cwc - lw_s[rs, col(p)])
            w_inv = jnp.exp(-cwc)
            at.append(-kk_s[rs, col(p)] * w_ex)
            rt.append(r_s[rs, col(p)] * w_in)
            vv.append(v_s[rs, col(p)])
            bk.append(jnp.concatenate([b_s[rs, col(p)] * w_inv, k_s[rs, col(p)] * w_inv], axis=0))
        items = [(gi, h) for gi in range(len(groups)) for h in range(2)]
        n = range(len(items))
        ats = [at[gi] * head_lanes[h] for gi, h in items]
        rts = [rt[gi] * head_lanes[h] for gi, h in items]
        sc = [_dot_nt(jnp.concatenate([ats[j], rts[j]], axis=0), bk[items[j][0]]) for j in n]
        top = [sc[j][:chunk] * tri_strict for j in n]
        bot = [sc[j][chunk:] * tri_incl for j in n]
        top_sw = [pltpu.roll(top[j], chunk, 1) for j in n]
        bot_sw = [pltpu.roll(bot[j], chunk, 1) for j in n]

        pt = [jnp.where(is_left, top[j], eye) for j in n]
        for _ in range(n_fact - 1):
            z = [_dot(half(pt[j]), pt[j]) for j in n]
            pt = [jnp.where(is_left, z[j], pt[j] + z[j]) for j in n]
        pt = [pt[j] + _dot(half(pt[j]), pt[j]) for j in n]
        tm = [half(pltpu.roll(pt[j], chunk, 1)) for j in n]

        ta = [_dot(tm[j], jnp.concatenate([ats[j], top_sw[j]], axis=1)) for j in n]
        ab = [_dot(half(bot[j]), ta[j]) for j in n]
        yk = [ab[j][:, LANES:] + bot_sw[j] for j in n]
        uy = [_dot(jnp.concatenate([half(ta[j][:, LANES:]), half(yk[j])], axis=0),
                   vv[items[j][0]] * head_lanes[items[j][1]]) for j in n]
        for j, (gi, h) in enumerate(items):
            c, p = groups[gi]
            idx = c * N_PAIRS + p
            xg_s[idx, h * chunk:(h + 1) * chunk, :] = ta[j][:, :LANES]
            xg_s[idx, c2 + h * chunk:c2 + (h + 1) * chunk, :] = rts[j] + ab[j][:, :LANES]
            uy_s[idx, h * chunk:(h + 1) * chunk, :] = uy[j][:chunk]
            uy_s[idx, c2 + h * chunk:c2 + (h + 1) * chunk, :] = uy[j][chunk:]
        for gi, (c, p) in enumerate(groups):
            bk_s[c * N_PAIRS + p] = bk[gi]
        return carry_

    lax.fori_loop(0, n_chunks // PHASE1_CHUNKS, phase1, 0)

    chunks_per_seq = seq_rows // chunk

    def phase2(i, carry_):
        cks = [u * chunks_per_seq + i for u in range(n_seq)]
        items = [(u, p) for u in range(n_seq) for p in pairs]
        n = range(len(items))
        tok = [pl.ds(pl.multiple_of(ck * chunk, chunk), chunk) for ck in cks]
        tail = [pl.ds(pl.multiple_of(ck * chunk + chunk - SUBLANES, SUBLANES), SUBLANES) for ck in cks]
        s0 = [sbd[u * N_PAIRS + p] for u, p in items]
        uyv = [_dot_nt(xg_s[cks[u] * N_PAIRS + p], s0[j]) + uy_s[cks[u] * N_PAIRS + p]
               for j, (u, p) in enumerate(items)]
        uv = [jnp.concatenate([uyv[j][:c2], stack2(v_s[tok[u], col(p)])], axis=0)
              for j, (u, p) in enumerate(items)]
        rhs = []
        for u, p in items:
            bkc = bk_s[cks[u] * N_PAIRS + p]
            bt, kt = bkc[:chunk], bkc[chunk:]
            rhs.append(jnp.concatenate([bt, bt, kt, kt], axis=0))
        upd = [_dot_tn(uv[j], rhs[j]) for j in n]
        for j, (u, p) in enumerate(items):
            w_end = jnp.exp(cw_s[tail[u], col(p)][SUBLANES - 1:SUBLANES, :])
            sbd[u * N_PAIRS + p] = (s0[j] + upd[j] * m_bd) * w_end
            y_s[tok[u], col(p)] = uyv[j][c2:c2 + chunk] + uyv[j][c2 + chunk:]
        return carry_

    lax.fori_loop(0, chunks_per_seq, phase2, 0)

    @pl.when(t_step == n_steps - 1)
    def _():
        for u in range(n_seq):
            store_state(u)

    out = _group_norm_out(y_s[...], bon_s[...], g_s[...], par(lg_ref), par(lb_ref))
    y_ref[...] = out.reshape(n_seq, seq_rows, R_WIDTH).astype(y_ref.dtype)


def _wkv(zs, s_all, P, layer, *, seq_rows):
    b, t_len, _ = zs.shape
    n_seq = b
    rows = n_seq * seq_rows
    chunk = WKV_CHUNK
    n_pc = (rows // chunk) * N_PAIRS
    assert seq_rows % chunk == 0 and rows % (chunk * PHASE1_CHUNKS) == 0 and t_len % seq_rows == 0
    params = [P['decay_base'], P['lora_up'], P['iclr_base'], P['gate_up'], P['k_k'], P['k_a'],
              P['r_k'], P['lnx_g'], P['lnx_b']]
    param_specs = [_layer_spec(a, layer) if a.ndim == 3 else _const_spec(a.shape) for a in params]
    tok = lambda g, t: (g, t, 0)
    state_spec = pl.BlockSpec((None, n_seq, R_HEADS, R_HEAD_DIM, R_HEAD_DIM), lambda g, t: (layer, g, 0, 0, 0))
    tile = lambda w: pltpu.VMEM((rows, w), F32)
    return pl.pallas_call(
        functools.partial(_wkv_kernel, layer=layer, n_seq=n_seq, seq_rows=seq_rows, chunk=chunk),
        grid=(b // n_seq, t_len // seq_rows),
        in_specs=[pl.BlockSpec((n_seq, seq_rows, SHIFT_W), tok), state_spec] + param_specs,
        out_specs=[pl.BlockSpec((n_seq, seq_rows, R_WIDTH), tok), state_spec],
        out_shape=[jax.ShapeDtypeStruct((b, t_len, R_WIDTH), BF16), jax.ShapeDtypeStruct(s_all.shape, F32)],
        input_output_aliases={1: 1},
        scratch_shapes=[pltpu.VMEM((n_seq * N_PAIRS, LANES, LANES), F32)] + [tile(R_WIDTH)] * 10 + [
            pltpu.VMEM((n_pc, 4 * chunk, LANES), F32), pltpu.VMEM((n_pc, 4 * chunk, LANES), F32),
            pltpu.VMEM((n_pc, 2 * chunk, LANES), F32)],
        compiler_params=_cparams("parallel", "arbitrary"),
        name="wkv7",
    )(zs, s_all, *params)


def _wkv_prep_kernel(zs_ref, db_ref, lora_ref, ib_ref, gu_ref, kk_ref, ka_ref, rk_ref,
                     r_ref, k_ref, v_ref, kko_ref, b_ref, w_ref, g_ref, bon_ref, *, layer):
    par = lambda ref: ref[layer:layer + 1, :]
    r, kp, v, kk, a, logw, g, bonus = _token_features(
        zs_ref[...], par(db_ref), lora_ref[...], par(ib_ref), gu_ref[...], par(kk_ref), par(ka_ref), par(rk_ref))
    r_ref[...] = r
    k_ref[...] = kp
    v_ref[...] = v
    kko_ref[...] = kk
    b_ref[...] = kk * a
    w_ref[...] = jnp.exp(logw)
    g_ref[...] = g
    bon_ref[...] = bonus


LANE_GROUP = 4


def _wkv_lanes_kernel(r_ref, k_ref, v_ref, kk_ref, b_ref, w_ref, g_ref, bon_ref, lg_ref, lb_ref, s_ref,
                      y_ref, sout_ref, ft_s, yt_s, ytok_s, *, layer, n_new, whole_state):
    nb = ft_s.shape[-1]
    n = R_HEAD_DIM
    feats = (r_ref, k_ref, v_ref, kk_ref, b_ref, w_ref)
    for fi, ref in enumerate(feats):
        for t in range(n_new):
            ft_s[fi, t] = ref[pl.ds(t, nb, stride=n_new), :].T
    load_s = (lambda hh, v: s_ref[layer, hh, v]) if whole_state else (lambda hh, v: s_ref[hh, v])

    def store_s(hh, v, val):
        if whole_state:
            sout_ref[layer, hh, v] = val
        else:
            sout_ref[hh, v] = val

    for hh in range(2):
        hs = slice(hh * n, (hh + 1) * n)

        def rows8(vg, carry_):
            v0 = pl.multiple_of(vg * SUBLANES, SUBLANES)
            vrows = [ft_s[2, t, pl.ds(hh * n + v0, SUBLANES), :] for t in range(n_new)]
            ys = [[None] * SUBLANES for _ in range(n_new)]
            for j0 in range(0, SUBLANES, LANE_GROUP):
                js = range(j0, j0 + LANE_GROUP)
                sv = {j: load_s(hh, v0 + j) for j in js}
                for t in range(n_new):
                    nk, w, bb = -ft_s[3, t, hs, :], ft_s[5, t, hs, :], ft_s[4, t, hs, :]
                    kt, rt = ft_s[1, t, hs, :], ft_s[0, t, hs, :]
                    sa = {j: jnp.sum(sv[j] * nk, axis=0, keepdims=True) for j in js}
                    sv = {j: sv[j] * w + sa[j] * bb + vrows[t][j:j + 1, :] * kt for j in js}
                    for j in js:
                        ys[t][j] = jnp.sum(sv[j] * rt, axis=0, keepdims=True)
                for j in js:
                    store_s(hh, v0 + j, sv[j])
            for t in range(n_new):
                yt_s[t, pl.ds(hh * n + v0, SUBLANES), :] = jnp.concatenate(ys[t], axis=0)
            return carry_

        lax.fori_loop(0, n // SUBLANES, rows8, 0)

    if whole_state:
        for l in range(s_ref.shape[0]):
            if l != layer:
                sout_ref[l] = s_ref[l]

    for t in range(n_new):
        ytok_s[pl.ds(t, nb, stride=n_new), :] = yt_s[t].T
    out = _group_norm_out(ytok_s[...], bon_ref[...], g_ref[...], lg_ref[layer:layer + 1, :],
                          lb_ref[layer:layer + 1, :])
    y_ref[...] = out.astype(y_ref.dtype)


def _wkv_short(zs, s_lanes, P, layer, n_new):
    n_tok = zs.shape[0]
    depth, b = s_lanes.shape[0], s_lanes.shape[-1]
    assert n_tok == b * n_new
    rows = 256
    params = [P['decay_base'], P['lora_up'], P['iclr_base'], P['gate_up'], P['k_k'], P['k_a'], P['r_k']]
    param_specs = [_layer_spec(a, layer) if a.ndim == 3 else _const_spec(a.shape) for a in params]
    tok_out = pl.BlockSpec((rows, R_WIDTH), lambda g: (g, 0))
    feats = pl.pallas_call(
        functools.partial(_wkv_prep_kernel, layer=layer),
        grid=(n_tok // rows,),
        in_specs=[pl.BlockSpec((rows, SHIFT_W), lambda g: (g, 0))] + param_specs,
        out_specs=[tok_out] * 8,
        out_shape=[jax.ShapeDtypeStruct((n_tok, R_WIDTH), F32)] * 8,
        compiler_params=_cparams("parallel"),
        name="wkv7_prep",
    )(zs, *params)

    whole = layer == 0
    pair_col = pl.BlockSpec((n_tok, LANES), lambda p: (0, p))
    par_col = pl.BlockSpec((depth, LANES), lambda p: (0, p))
    n = R_HEAD_DIM
    if whole:
        state_spec = pl.BlockSpec((depth, 2, n, n, b), lambda p: (0, p, 0, 0, 0))
    else:
        state_spec = pl.BlockSpec((None, 2, n, n, b), lambda p: (layer, p, 0, 0, 0))
    y, s_new = pl.pallas_call(
        functools.partial(_wkv_lanes_kernel, layer=layer, n_new=n_new, whole_state=whole),
        grid=(N_PAIRS,),
        in_specs=[pair_col] * 8 + [par_col, par_col, state_spec],
        out_specs=[pair_col, state_spec],
        out_shape=[jax.ShapeDtypeStruct((n_tok, R_WIDTH), BF16), jax.ShapeDtypeStruct(s_lanes.shape, F32)],
        scratch_shapes=[pltpu.VMEM((6, n_new, LANES, b), F32), pltpu.VMEM((n_new, LANES, b), F32),
                        pltpu.VMEM((n_tok, LANES), F32)],
        input_output_aliases={} if whole else {10: 1},
        compiler_params=_cparams("arbitrary"),
        name="wkv7_lanes",
    )(*feats, P['lnx_g'], P['lnx_b'], s_lanes)
    return y, s_new


def _pair_kv(t):
    lane = lax.broadcasted_iota(jnp.int32, (1, LANES), 1)
    even = lane < HEAD_DIM
    sw = pltpu.roll(t, HEAD_DIM, 1)
    return jnp.where(even, t, sw), jnp.where(even, sw, t)


def _attend(qs, kcs, vcs, masks, sink_cols):
    n = range(len(qs))
    s = [jnp.where(masks[i], _dot_nt(qs[i], kcs[i]), -jnp.inf) for i in n]
    m = [jnp.maximum(jnp.max(s[i], axis=-1, keepdims=True), sink_cols[i]) for i in n]
    p = [jnp.exp(s[i] - m[i]) for i in n]
    den = [jnp.sum(p[i], axis=-1, keepdims=True) + jnp.exp(sink_cols[i] - m[i]) for i in n]
    o = [_dot(p[i], vcs[i]) for i in n]
    return [o[i] / den[i] for i in n]


def _attn_prompt_kernel(sink_ref, q_ref, kp_ref, kc_ref, vp_ref, vc_ref, o_ref, *, layer, q_blocks):
    n = pl.program_id(1)
    sink = lambda head: sink_ref[layer * A_HEADS + head]
    lane = lax.broadcasted_iota(jnp.int32, (1, LANES), 1)
    even = lane < HEAD_DIM
    ks = _pair_kv(jnp.concatenate([kp_ref[...], kc_ref[...]], axis=0))
    vs = _pair_kv(jnp.concatenate([vp_ref[...], vc_ref[...]], axis=0))
    qi = _pmod(lax.broadcasted_iota(jnp.int32, (2 * BLOCK, 2 * BLOCK), 0), BLOCK) + BLOCK
    kj = lax.broadcasted_iota(jnp.int32, (2 * BLOCK, 2 * BLOCK), 1)
    dist = qi - kj
    in_win = (dist >= 0) & (dist <= WINDOW)
    first_mask = in_win & ((n > 0) | (kj >= BLOCK))
    top = lax.broadcasted_iota(jnp.int32, (2 * BLOCK, 1), 0) < BLOCK
    blocks = range(Q_WIDTH // LANES)
    items = [(s, j) for s in range(q_blocks) for j in blocks]
    qs, kcs, vcs, masks, sink_cols = [], [], [], [], []
    for s, j in items:
        qb = q_ref[s * BLOCK:(s + 1) * BLOCK, j * LANES:(j + 1) * LANES] * ATTN_SCALE
        qs.append(jnp.concatenate([jnp.where(even, qb, 0.0), jnp.where(even, 0.0, qb)], axis=0))
        kvh = (2 * j) // GROUP
        kcs.append(ks[kvh][s * BLOCK:(s + 2) * BLOCK])
        vcs.append(vs[kvh][s * BLOCK:(s + 2) * BLOCK])
        masks.append(first_mask if s == 0 else in_win)
        sink_cols.append(jnp.where(top, sink(2 * j), sink(2 * j + 1)))
    o = _attend(qs, kcs, vcs, masks, sink_cols)
    for i, (s, j) in enumerate(items):
        o_ref[s * BLOCK:(s + 1) * BLOCK, j * LANES:(j + 1) * LANES] = jnp.where(
            even, o[i][:BLOCK], o[i][BLOCK:]).astype(o_ref.dtype)


def _attn_prompt(q, k, v, sinks, layer, batch, seq):
    q_blocks = 4
    rows = q_blocks * BLOCK
    steps = seq // rows
    cur = lambda b, n: (b * steps + n, 0)
    prv = lambda b, n: (b * steps * q_blocks + jnp.maximum(n * q_blocks - 1, 0), 0)
    return pl.pallas_call(
        functools.partial(_attn_prompt_kernel, layer=layer, q_blocks=q_blocks),
        grid=(batch, steps),
        in_specs=[pl.BlockSpec(memory_space=pltpu.SMEM), pl.BlockSpec((rows, Q_WIDTH), cur),
                  pl.BlockSpec((BLOCK, KV_WIDTH), prv), pl.BlockSpec((rows, KV_WIDTH), cur),
                  pl.BlockSpec((BLOCK, KV_WIDTH), prv), pl.BlockSpec((rows, KV_WIDTH), cur)],
        out_specs=pl.BlockSpec((rows, Q_WIDTH), cur),
        out_shape=jax.ShapeDtypeStruct((batch * seq, Q_WIDTH), BF16),
        compiler_params=_cparams("parallel", "parallel"),
        name="attn_prompt",
    )(sinks, q, k, k, v, v)


def _attn_sample_kernel(sink_ref, q_ref, kt_ref, kn_ref, vt_ref, vn_ref, o_ref, kt_out_ref, vt_out_ref,
                        *, layer, n_seq, wbuf, n_new, whole):
    assert wbuf == LANES
    win = (lambda ref, g, h: ref[layer, g, h]) if whole else (lambda ref, g, h: ref[g, h])
    lane = lax.broadcasted_iota(jnp.int32, (1, LANES), 1)
    even = lane < HEAD_DIM
    rows = GROUP * SAMPLE_PAD
    n_pad = 2 * SAMPLE_PAD
    qt = _pmod(lax.broadcasted_iota(jnp.int32, (rows, 1), 0), SAMPLE_PAD)
    dist_c = qt + wbuf - lax.broadcasted_iota(jnp.int32, (rows, wbuf), 1)
    dist_n = qt - lax.broadcasted_iota(jnp.int32, (rows, n_pad), 1)
    mask_c = (dist_c >= 0) & (dist_c <= WINDOW)
    mask_n = (dist_n >= 0) & (dist_n <= WINDOW)
    rid = _pdiv(lax.broadcasted_iota(jnp.int32, (rows, 1), 0), SAMPLE_PAD)
    sink_cols = []
    for h in range(KV_HEADS):
        col = jnp.zeros((rows, 1), F32)
        for gq in range(GROUP):
            col = jnp.where(rid == gq, sink_ref[layer * A_HEADS + h * GROUP + gq], col)
        sink_cols.append(col)
    unroll = 4
    pad = jnp.zeros((SAMPLE_PAD, KV_WIDTH), F32)
    blocks_per_kv = GROUP // 2
    keep = lane < wbuf - n_new
    sub = lax.broadcasted_iota(jnp.int32, (SAMPLE_PAD, 1), 0)

    def new_columns(t_new):
        low = jnp.where(sub >= SAMPLE_PAD - n_new, pltpu.roll(t_new, SAMPLE_PAD - n_new, 0), 0.0)
        return jnp.concatenate([jnp.zeros((wbuf - SAMPLE_PAD, KV_WIDTH), F32), low], axis=0).T

    def body(i, carry_):
        items = [(i * unroll + u, h) for u in range(unroll) for h in range(KV_HEADS)]
        n = range(len(items))
        qs, kc, vc, kn, vn = [], [], [], [], []
        for u in range(unroll):
            g = i * unroll + u
            kn2 = _pair_kv(jnp.concatenate([kn_ref[g], pad], axis=0))
            vn2 = _pair_kv(jnp.concatenate([vn_ref[g], pad], axis=0))
            for h in range(KV_HEADS):
                parts = []
                for j in range(h * blocks_per_kv, (h + 1) * blocks_per_kv):
                    qb = q_ref[g, :, j * LANES:(j + 1) * LANES] * ATTN_SCALE
                    parts += [jnp.where(even, qb, 0.0), jnp.where(even, 0.0, qb)]
                qs.append(jnp.concatenate(parts, axis=0))
                kt, vt = win(kt_ref, g, h), win(vt_ref, g, h)
                kc.append(jnp.concatenate([kt, kt], axis=0))
                vc.append(jnp.concatenate([vt, vt], axis=0))
                kn.append(kn2[h])
                vn.append(vn2[h])
        s_c = [jnp.where(mask_c, _dot(qs[j], kc[j]), -jnp.inf) for j in n]
        s_n = [jnp.where(mask_n, _dot_nt(qs[j], kn[j]), -jnp.inf) for j in n]
        m = [jnp.maximum(jnp.maximum(jnp.max(s_c[j], axis=-1, keepdims=True),
                                     jnp.max(s_n[j], axis=-1, keepdims=True)), sink_cols[items[j][1]])
             for j in n]
        p_c = [jnp.exp(s_c[j] - m[j]) for j in n]
        p_n = [jnp.exp(s_n[j] - m[j]) for j in n]
        den = [jnp.sum(p_c[j], axis=-1, keepdims=True) + jnp.sum(p_n[j], axis=-1, keepdims=True)
               + jnp.exp(sink_cols[items[j][1]] - m[j]) for j in n]
        o = [(_dot_nt(p_c[j], vc[j]) + _dot(p_n[j], vn[j])) / den[j] for j in n]
        for j, (g, h) in enumerate(items):
            for jj in range(blocks_per_kv):
                blk = h * blocks_per_kv + jj
                r0 = 2 * jj * SAMPLE_PAD
                o_ref[g, :, blk * LANES:(blk + 1) * LANES] = jnp.where(
                    even, o[j][r0:r0 + SAMPLE_PAD], o[j][r0 + SAMPLE_PAD:r0 + 2 * SAMPLE_PAD]).astype(o_ref.dtype)
        for u in range(unroll):
            g = i * unroll + u
            k_cols, v_cols = new_columns(kn_ref[g]), new_columns(vn_ref[g])
            for h in range(KV_HEADS):
                hs = slice(h * HEAD_DIM, (h + 1) * HEAD_DIM)
                for src, dst, cols in ((kt_ref, kt_out_ref, k_cols), (vt_ref, vt_out_ref, v_cols)):
                    slid = jnp.where(keep, pltpu.roll(win(src, g, h), wbuf - n_new, 1), cols[hs])
                    if whole:
                        dst[layer, g, h] = slid
                    else:
                        dst[g, h] = slid
            if whole:
                for l in range(kt_ref.shape[0]):
                    if l != layer:
                        kt_out_ref[l, g] = kt_ref[l, g]
                        vt_out_ref[l, g] = vt_ref[l, g]
        return carry_

    lax.fori_loop(0, n_seq // unroll, body, 0)


def _attn_sample(q, k, v, kt_all, vt_all, sinks, layer, n_seq, n_new):
    depth, b, _, _, wbuf = kt_all.shape
    whole = layer == 0
    idx = lambda i: (i, 0, 0)
    new = pl.BlockSpec((n_seq, SAMPLE_PAD, KV_WIDTH), idx)
    if whole:
        buf = pl.BlockSpec((depth, n_seq, KV_HEADS, HEAD_DIM, wbuf), lambda i: (0, i, 0, 0, 0))
    else:
        buf = pl.BlockSpec((None, n_seq, KV_HEADS, HEAD_DIM, wbuf), lambda i: (layer, i, 0, 0, 0))
    win_shape = jax.ShapeDtypeStruct(kt_all.shape, F32)
    return pl.pallas_call(
        functools.partial(_attn_sample_kernel, layer=layer, n_seq=n_seq, wbuf=wbuf, n_new=n_new, whole=whole),
        grid=(b // n_seq,),
        in_specs=[pl.BlockSpec(memory_space=pltpu.SMEM), pl.BlockSpec((n_seq, SAMPLE_PAD, Q_WIDTH), idx),
                  buf, new, buf, new],
        out_specs=[pl.BlockSpec((n_seq, SAMPLE_PAD, Q_WIDTH), idx), buf, buf],
        out_shape=[jax.ShapeDtypeStruct((b, SAMPLE_PAD, Q_WIDTH), BF16), win_shape, win_shape],
        input_output_aliases={} if whole else {2: 1, 4: 2},
        compiler_params=_cparams("parallel"),
        name="attn_sample",
    )(sinks, q, kt_all, k, vt_all, v)


def _layer_norm(x, g, b):
    mu = jnp.mean(x, axis=-1, keepdims=True)
    d = x - mu
    var = jnp.mean(d * d, axis=-1, keepdims=True)
    return d * lax.rsqrt(var + LN_EPS) * g + b


def _mix_ffn_kernel(x_ref, yr_ref, ya_ref, win_ref, wbr_ref, wba_ref, wo_ref, g1_ref, b1_ref,
                    wu_ref, wd_ref, g2_ref, b2_ref, o_ref, *, layer, ff_chunk):
    par = lambda ref: ref[layer:layer + 1, :]
    dotf = functools.partial(jnp.dot, preferred_element_type=F32)
    tm = x_ref.shape[0]
    groups = [slice(i * tm // ROW_GROUPS, (i + 1) * tm // ROW_GROUPS) for i in range(ROW_GROUPS)]
    n = range(ROW_GROUPS)
    x = [x_ref[g, :] for g in groups]
    xb = [x[i].astype(BF16) for i in n]
    gate_r = [jax.nn.sigmoid(dotf(xb[i], win_ref[:, QKV_END:QKV_END + D_MODEL])) for i in n]
    mix = [gate_r[i] * dotf(yr_ref[groups[i], :], wbr_ref[...]) for i in n]
    gate_a = [jax.nn.sigmoid(dotf(xb[i], win_ref[:, QKV_END + D_MODEL:])) for i in n]
    mix = [mix[i] + gate_a[i] * dotf(ya_ref[groups[i], :], wba_ref[...]) for i in n]
    x1 = [_layer_norm(ALPHA * x[i] + _dot(mix[i], wo_ref[...]), par(g1_ref), par(b1_ref)) for i in n]
    x1b = [x1[i].astype(BF16) for i in n]
    acc = [ALPHA * x1[i] for i in n]
    for c in range(D_FF // ff_chunk):
        cs = slice(c * ff_chunk, (c + 1) * ff_chunk)
        h = [jnp.maximum(dotf(x1b[i], wu_ref[:, cs]), 0.0) for i in n]
        acc = [acc[i] + _dot(h[i] * h[i], wd_ref[cs, :]) for i in n]
    for i in n:
        o_ref[groups[i], :] = _layer_norm(acc[i], par(g2_ref), par(b2_ref))


def _mix_ffn(x, yr, ya, P, layer, tm):
    n = x.shape[0]
    row = lambda i: (i, 0)
    ws = [P['w_in'], P['w_br_rwkv'], P['w_br_attn'], P['w_out'], P['ln1_g'], P['ln1_b'],
          P['w_ff_up'], P['w_ff_down'], P['ln2_g'], P['ln2_b']]
    wspec = lambda a: _layer_spec(a, layer, single_buffer=True) if a.ndim == 3 else _const_spec(a.shape)
    return pl.pallas_call(
        functools.partial(_mix_ffn_kernel, layer=layer, ff_chunk=1024),
        grid=(n // tm,),
        in_specs=[pl.BlockSpec((tm, D_MODEL), row), pl.BlockSpec((tm, R_WIDTH), row),
                  pl.BlockSpec((tm, Q_WIDTH), row)] + [wspec(a) for a in ws],
        out_specs=pl.BlockSpec((tm, D_MODEL), row),
        out_shape=jax.ShapeDtypeStruct((n, D_MODEL), F32),
        compiler_params=_cparams("parallel"),
        name="mix_ffn",
    )(x, yr, ya, *ws)


def _prepare_params(w_in, mu_shift, decay_base, decay_up, iclr_base, iclr_up, gate_up, k_k, k_a, r_k,
                    lnx_g, lnx_b, sinks, w_br_rwkv, w_br_attn, w_out, ln1_g, ln1_b, w_ff_up, w_ff_down,
                    ln2_g, ln2_b):
    depth = w_in.shape[0]
    zeros = jnp.zeros((depth, DECAY_LORA, R_WIDTH), F32)
    lora_up = jnp.concatenate([jnp.concatenate([decay_up, zeros], axis=2),
                               jnp.concatenate([zeros, iclr_up], axis=2)], axis=1)
    bf = lambda a: a.astype(BF16)
    return dict(
        w_in=bf(w_in), mu_shift=mu_shift, decay_base=decay_base, lora_up=bf(lora_up), iclr_base=iclr_base,
        gate_up=bf(gate_up), k_k=k_k, k_a=k_a, r_k=r_k.reshape(depth, R_WIDTH), lnx_g=lnx_g, lnx_b=lnx_b,
        sinks=sinks.reshape(depth * A_HEADS), w_br_rwkv=bf(w_br_rwkv), w_br_attn=bf(w_br_attn),
        w_out=bf(w_out), ln1_g=ln1_g, ln1_b=ln1_b, w_ff_up=bf(w_ff_up), w_ff_down=bf(w_ff_down),
        ln2_g=ln2_g, ln2_b=ln2_b)


def _prompt_layer(x, P, layer, tables, s_all, batch, seq, wbuf):
    tm = 512
    tp = 1024
    shift0 = jnp.zeros((batch, SHIFT_W), F32)
    zs, zlast, q, k, v = _inproj(x, P, layer, tables, seq // tp, tp, shift0, seq_rows=tp, last_row=tp - 1)
    wkv_rows = tm // batch
    yr, s_all = _wkv(zs.reshape(batch, seq, SHIFT_W), s_all, P, layer, seq_rows=wkv_rows)
    ya = _attn_prompt(q, k, v, P['sinks'], layer, batch, seq)
    x = _mix_ffn(x, yr.reshape(batch * seq, R_WIDTH), ya, P, layer, tm)
    tail = lambda t, w: t.reshape(batch, seq, w)[:, seq - wbuf:].reshape(batch, wbuf, KV_HEADS, HEAD_DIM)
    return x, s_all, zlast.reshape(batch, SHIFT_W), tail(k, KV_WIDTH), tail(v, KV_WIDTH)


def _sample_layer(x, P, layer, tables, shift_prev, s_all, kt_all, vt_all, batch, seq):
    n = batch * seq
    tm = min(n, 512)
    n_seq = 16
    zs, zlast, q, k, v = _inproj(x, P, layer, tables, 1, tm, shift_prev, seq_rows=seq, last_row=seq - 1)
    yr, s_all = _wkv_short(zs, s_all, P, layer, seq)
    pad = lambda t, w: jnp.pad(t.reshape(batch, seq, w), ((0, 0), (0, SAMPLE_PAD - seq), (0, 0)))
    ya, kt_all, vt_all = _attn_sample(pad(q, Q_WIDTH), pad(k, KV_WIDTH), pad(v, KV_WIDTH), kt_all, vt_all,
                                      P['sinks'], layer, n_seq, seq)
    x = _mix_ffn(x, yr, ya[:, :seq].reshape(n, Q_WIDTH), P, layer, tm)
    return x, s_all, kt_all, vt_all, zlast


def kernel(x_prompt, x_sample, state_wkv, state_shift, cache_k_win, cache_v_win, w_in, mu_shift, decay_base, decay_up, iclr_base, iclr_up, gate_up, k_k, k_a, r_k, lnx_g, lnx_b, sinks, w_br_rwkv, w_br_attn, w_out, ln1_g, ln1_b, w_ff_up, w_ff_down, ln2_g, ln2_b):
    bp, tp, _ = x_prompt.shape
    bs, ts, _ = x_sample.shape
    wbuf = cache_k_win.shape[2]
    half = HEAD_DIM // 2
    inv_freq = ROPE_THETA ** (-jnp.arange(half, dtype=F32) / half)
    tab_p = _rope_tables(inv_freq, tp, 0, tp)
    tab_s = _rope_tables(inv_freq, min(bs * ts, 512), PAST_LEN, ts)

    hp = x_prompt.reshape(bp * tp, D_MODEL)
    hs = x_sample.reshape(bs * ts, D_MODEL)
    outs_p, outs_s = [], []
    P = _prepare_params(w_in, mu_shift, decay_base, decay_up, iclr_base, iclr_up, gate_up, k_k, k_a, r_k,
                        lnx_g, lnx_b, sinks, w_br_rwkv, w_br_attn, w_out, ln1_g, ln1_b, w_ff_up,
                        w_ff_down, ln2_g, ln2_b)
    to_lanes = lambda c: jnp.transpose(c, (0, 1, 3, 4, 2))
    from_lanes = lambda c: jnp.transpose(c, (0, 1, 4, 2, 3))
    kt_all, vt_all = to_lanes(cache_k_win), to_lanes(cache_v_win)
    s_all_p = jnp.zeros((DEPTH, bp, R_HEADS, R_HEAD_DIM, R_HEAD_DIM), F32)
    s_all_s = jnp.transpose(state_wkv, (0, 2, 3, 4, 1))
    for l in range(DEPTH):
        hp, s_all_p, *st = _prompt_layer(hp, P, l, tab_p, s_all_p, bp, tp, wbuf)
        outs_p.append(st)
        hs, s_all_s, kt_all, vt_all, zlast = _sample_layer(hs, P, l, tab_s, state_shift[l], s_all_s, kt_all, vt_all,
                                                           bs, ts)
        outs_s.append(zlast)
    stack = lambda outs, i: jnp.stack([o[i] for o in outs])
    y_p = hp.reshape(bp, tp, D_MODEL)
    y_s = hs.reshape(bs, ts, D_MODEL)
    return (y_p, y_s,
            s_all_p, stack(outs_p, 0), stack(outs_p, 1), stack(outs_p, 2),
            jnp.transpose(s_all_s, (0, 4, 1, 2, 3)), jnp.stack(outs_s), from_lanes(kt_all), from_lanes(vt_all))
```

```python
import functools
import math

import jax
import jax.numpy as jnp
from jax import lax
from jax.experimental import pallas as pl
from jax.experimental.pallas import tpu as pltpu

F32 = jnp.float32
BF16 = jnp.bfloat16

D_MODEL = 1024
DEPTH = 2
PAST_LEN = 8192
R_HEADS = 8
R_HEAD_DIM = 64
R_WIDTH = R_HEADS * R_HEAD_DIM
DECAY_LORA = 64
ICLR_LORA = 64
GATE_LORA = 128
SHIFT_W = 3 * R_WIDTH + DECAY_LORA + ICLR_LORA + GATE_LORA
A_HEADS = 8
KV_HEADS = 2
HEAD_DIM = 64
Q_WIDTH = A_HEADS * HEAD_DIM
KV_WIDTH = KV_HEADS * HEAD_DIM
GROUP = A_HEADS // KV_HEADS
WINDOW = 128
BLOCK = 128
ROPE_THETA = 10000.0
ATTN_SCALE = HEAD_DIM ** -0.5
D_FF = 4 * D_MODEL
ALPHA = (2 * DEPTH) ** 0.25
LN_EPS = 1e-5
GN_EPS = 64e-5
QKV_END = SHIFT_W + Q_WIDTH + 2 * KV_WIDTH

LANES = 128
SUBLANES = 8
VMEM_LIMIT = 56 * 1024 * 1024

SAMPLE_PAD = SUBLANES
N_PAIRS = R_HEADS // 2
WKV_CHUNK = LANES // 2
ROW_GROUPS = 2
SHIFT_BLOCK = 2 * LANES
assert SHIFT_W % SHIFT_BLOCK == 0
PHASE1_CHUNKS = 4


def _cparams(*sem):
    return pltpu.CompilerParams(dimension_semantics=sem, vmem_limit_bytes=VMEM_LIMIT)


def _const_spec(shape):
    nd = len(shape)
    return pl.BlockSpec(shape, lambda *_: (0,) * nd)


def _layer_spec(a, layer, single_buffer=False):
    mode = pl.Buffered(1) if single_buffer else None
    return pl.BlockSpec((None,) + a.shape[1:], lambda *_: (layer, 0, 0), pipeline_mode=mode)


def _dot(a, b):
    return jnp.dot(a.astype(BF16), b.astype(BF16), preferred_element_type=F32)


def _dot_nt(a, b):
    return lax.dot_general(a.astype(BF16), b.astype(BF16), (((1,), (1,)), ((), ())),
                           preferred_element_type=F32)


def _dot_tn(a, b):
    return lax.dot_general(a.astype(BF16), b.astype(BF16), (((0,), (0,)), ((), ())),
                           preferred_element_type=F32)


def _pmod(x, n):
    assert n & (n - 1) == 0
    return x & (n - 1)


def _pdiv(x, n):
    assert n & (n - 1) == 0
    return x >> (n.bit_length() - 1)


def _split(x):
    hi = x.astype(BF16)
    lo = (x - hi.astype(F32)).astype(BF16)
    return hi, lo


def _rope_table_kernel(invf_ref, cos_ref, sa_ref, sb_ref, *, rows, offset, period):
    half = HEAD_DIM // 2
    n_groups = LANES // half
    q_rows = rows // n_groups
    i = pl.program_id(0)
    lane = lax.broadcasted_iota(jnp.int32, (q_rows, LANES), 1)
    group = _pdiv(lane, half)
    row = lax.broadcasted_iota(jnp.int32, (q_rows, LANES), 0) + group * q_rows + i * rows
    pos = offset + _pmod(row, period)
    ang = pos.astype(F32) * invf_ref[...]
    c_all, s_all = jnp.cos(ang), jnp.sin(ang)
    first = _pmod(lane, HEAD_DIM) < half

    def spread(x, c):
        y = jnp.where(group == c, x, 0.0)
        sh = half
        while sh < LANES:
            y = y + pltpu.roll(y, sh, 1)
            sh *= 2
        return y

    for c in range(n_groups):
        rs = slice(c * q_rows, (c + 1) * q_rows)
        s = spread(s_all, c)
        cos_ref[rs, :] = spread(c_all, c)
        sa_ref[rs, :] = jnp.where(first, -s, 0.0)
        sb_ref[rs, :] = jnp.where(first, 0.0, s)


def _rope_tables(inv_freq, n_rows, offset, period):
    rows = min(n_rows, 1024)
    invf = jnp.tile(inv_freq, LANES // (HEAD_DIM // 2)).reshape(1, LANES)
    out = jax.ShapeDtypeStruct((n_rows, LANES), F32)
    return pl.pallas_call(
        functools.partial(_rope_table_kernel, rows=rows, offset=offset, period=period),
        grid=(n_rows // rows,),
        in_specs=[_const_spec((1, LANES))],
        out_specs=[pl.BlockSpec((rows, LANES), lambda i: (i, 0))] * 3,
        out_shape=[out] * 3,
        compiler_params=_cparams("parallel"),
        name="rope_tables",
    )(invf)


def _inproj_kernel(x_ref, w_ref, cos_ref, sa_ref, sb_ref, mu_ref, shift_ref,
                   zs_ref, zlast_ref, q_ref, k_ref, v_ref, carry, first_s, zraw_s,
                   *, layer, n_seq, seq_rows, last_row, tiles_per_seq):
    xb = x_ref[...].astype(BF16)
    tm = xb.shape[0]
    t_step = lax.rem(pl.program_id(0), tiles_per_seq)

    @pl.when(pl.program_id(0) == 0)
    def _():
        first_s[...] = jnp.zeros(first_s.shape, F32)

    @pl.when(t_step == 0)
    def _():
        carry[...] = shift_ref[0] if n_seq == 1 else shift_ref[...]
    lane_groups = SHIFT_BLOCK // LANES
    for c in range(SHIFT_W // LANES):
        if n_seq == 1:
            first_s[c, 0:1, :] = carry[:, c * LANES:(c + 1) * LANES]
        else:
            first_s[c, pl.ds(0, n_seq, stride=seq_rows), :] = carry[:, c * LANES:(c + 1) * LANES]
    is_first = _pmod(lax.broadcasted_iota(jnp.int32, (tm, 1), 0), seq_rows) == 0
    for j in range(SHIFT_W // SHIFT_BLOCK):
        cs = slice(j * SHIFT_BLOCK, (j + 1) * SHIFT_BLOCK)
        z = jnp.dot(xb, w_ref[:, cs], preferred_element_type=F32)
        first = jnp.concatenate([first_s[j * lane_groups + c] for c in range(lane_groups)], axis=1)
        prev = jnp.where(is_first, first, pltpu.roll(z, 1, 0))
        zs_ref[:, cs] = z + (prev - z) * mu_ref[layer:layer + 1, cs]
        if n_seq == 1:
            carry[:, cs] = z[last_row:last_row + 1, :]
        else:
            for c in range(lane_groups):
                zraw_s[c] = z[:, c * LANES:(c + 1) * LANES]
                carry[:, cs.start + c * LANES:cs.start + (c + 1) * LANES] = zraw_s[
                    c, pl.ds(last_row, n_seq, stride=seq_rows), :]
    if n_seq == 1:
        zlast_ref[0] = carry[...]
    else:
        zlast_ref[...] = carry[...]
    cos, sa, sb = cos_ref[...], sa_ref[...], sb_ref[...]

    def rope(t):
        return (t * cos + pltpu.roll(t, LANES - HEAD_DIM // 2, 1) * sa
                + pltpu.roll(t, HEAD_DIM // 2, 1) * sb)

    zq = jnp.dot(xb, w_ref[:, SHIFT_W:SHIFT_W + Q_WIDTH], preferred_element_type=F32)
    for j in range(Q_WIDTH // LANES):
        q_ref[:, j * LANES:(j + 1) * LANES] = rope(zq[:, j * LANES:(j + 1) * LANES])
    zkv = jnp.dot(xb, w_ref[:, SHIFT_W + Q_WIDTH:QKV_END], preferred_element_type=F32)
    k_ref[...] = rope(zkv[:, :KV_WIDTH])
    v_ref[...] = zkv[:, KV_WIDTH:]


def _inproj(x, P, layer, tables, tab_blocks, tm, shift_in, *, seq_rows, last_row):
    n = x.shape[0]
    b = shift_in.shape[0]
    n_seq = tm // seq_rows
    if n_seq == 1:
        tiles_per_seq = (n // b) // tm
        shift_in = shift_in.reshape(b, 1, SHIFT_W)
        seq_spec = pl.BlockSpec((1, 1, SHIFT_W), lambda i: (i // tiles_per_seq, 0, 0))
    else:
        tiles_per_seq = 1
        seq_spec = pl.BlockSpec((n_seq, SHIFT_W), lambda i: (i, 0))
    row = lambda i: (i, 0)
    tab = pl.BlockSpec((tm, LANES), lambda i: (lax.rem(i, tab_blocks), 0))
    w_spec = pl.BlockSpec((None, D_MODEL, QKV_END), lambda i: (layer, 0, 0))
    return pl.pallas_call(
        functools.partial(_inproj_kernel, layer=layer, n_seq=n_seq, seq_rows=seq_rows, last_row=last_row,
                          tiles_per_seq=tiles_per_seq),
        grid=(n // tm,),
        in_specs=[pl.BlockSpec((tm, D_MODEL), row), w_spec, tab, tab, tab, _const_spec(P['mu_shift'].shape),
                  seq_spec],
        out_specs=[pl.BlockSpec((tm, SHIFT_W), row), seq_spec, pl.BlockSpec((tm, Q_WIDTH), row),
                   pl.BlockSpec((tm, KV_WIDTH), row), pl.BlockSpec((tm, KV_WIDTH), row)],
        out_shape=[jax.ShapeDtypeStruct((n, SHIFT_W), F32), jax.ShapeDtypeStruct(shift_in.shape, F32),
                   jax.ShapeDtypeStruct((n, Q_WIDTH), F32),
                   jax.ShapeDtypeStruct((n, KV_WIDTH), F32), jax.ShapeDtypeStruct((n, KV_WIDTH), F32)],
        scratch_shapes=[pltpu.VMEM((n_seq, SHIFT_W), F32), pltpu.VMEM((SHIFT_W // LANES, tm, LANES), F32),
                        pltpu.VMEM((SHIFT_BLOCK // LANES, tm, LANES), F32)],
        compiler_params=_cparams("arbitrary"),
        name="inproj",
    )(x, P['w_in'], *tables, P['mu_shift'], shift_in)


def _pair_block_diag():
    bi = _pdiv(lax.broadcasted_iota(jnp.int32, (LANES, LANES), 0), R_HEAD_DIM)
    bj = _pdiv(lax.broadcasted_iota(jnp.int32, (LANES, LANES), 1), R_HEAD_DIM)
    return (bi == bj).astype(F32)


def _head_sum(x):
    ones = _pair_block_diag().astype(BF16)
    xb = x.astype(BF16)
    return jnp.concatenate([jnp.dot(xb[:, j * LANES:(j + 1) * LANES], ones, preferred_element_type=F32)
                            for j in range(x.shape[1] // LANES)], axis=1)


def _token_features(zs, decay_base, lora_up, iclr_base, gate_up, k_k, k_a, r_k):
    o1, o2, o3 = R_WIDTH, 2 * R_WIDTH, 3 * R_WIDTH
    o4 = o3 + DECAY_LORA + ICLR_LORA
    r = zs[:, :o1]
    k = zs[:, o1:o2]
    v = zs[:, o2:o3]
    wa = zs[:, o3:o4]
    gd = zs[:, o4:]
    lane = lax.broadcasted_iota(jnp.int32, (1, LANES), 1)
    lora_in = jnp.where(lane < DECAY_LORA, jnp.tanh(wa), wa)
    lora = _dot(lora_in, lora_up)
    pre_w = decay_base + lora[:, :R_WIDTH]
    logw = -math.exp(-0.5) * jax.nn.sigmoid(pre_w)
    a = jax.nn.sigmoid(iclr_base + lora[:, R_WIDTH:])
    g = _dot(jax.nn.sigmoid(gd), gate_up)
    kk = k * k_k
    kk = kk * lax.rsqrt(jnp.maximum(_head_sum(kk * kk), 1e-24))
    kp = k * (1.0 + (a - 1.0) * k_a)
    bonus = _head_sum(r * kp * r_k) * v
    return r, kp, v, kk, a, logw, g, bonus


def _group_norm_out(y, bonus, g, lnx_g, lnx_b):
    inv_n = 1.0 / R_HEAD_DIM
    d = y - _head_sum(y) * inv_n
    var = _head_sum(d * d) * inv_n
    return (d * lax.rsqrt(var + GN_EPS) * lnx_g + lnx_b + bonus) * g


def _wkv_kernel(zs_ref, sin_ref, db_ref, lora_ref, ib_ref, gu_ref, kk_ref, ka_ref,
                rk_ref, lg_ref, lb_ref,
                y_ref, sout_ref,
                sbd, r_s, k_s, v_s, kk_s, b_s, lw_s, cw_s, y_s, bon_s, g_s, xg_s, uy_s, bk_s,
                *, layer, n_seq, seq_rows, chunk):
    t_step = pl.program_id(1)
    n_steps = pl.num_programs(1)
    rows = n_seq * seq_rows
    n_chunks = rows // chunk
    c2 = 2 * chunk
    par = lambda ref: ref[layer:layer + 1, :]

    zs = zs_ref[...].reshape(rows, SHIFT_W)
    r, kp, v, kk, a, logw, g, bonus = _token_features(
        zs, par(db_ref), lora_ref[...], par(ib_ref), gu_ref[...], par(kk_ref), par(ka_ref), par(rk_ref))
    g_s[...] = g
    bon_s[...] = bonus
    r_s[...] = r
    k_s[...] = kp
    v_s[...] = v
    kk_s[...] = kk
    b_s[...] = kk * a
    lw_s[...] = logw

    ci = lax.broadcasted_iota(jnp.int32, (chunk, chunk), 0)
    cj = lax.broadcasted_iota(jnp.int32, (chunk, chunk), 1)
    tril_ones = (cj <= ci).astype(BF16)
    lw_hi, lw_lo = _split(logw)
    for c in range(n_chunks):
        cr = slice(c * chunk, (c + 1) * chunk)
        cw_s[cr, :] = (jnp.dot(tril_ones, lw_hi[cr], preferred_element_type=F32)
                       + jnp.dot(tril_ones, lw_lo[cr], preferred_element_type=F32))

    lane = lax.broadcasted_iota(jnp.int32, (1, LANES), 1)
    left = (lane < R_HEAD_DIM).astype(F32)
    right = 1.0 - left
    head_lanes = (left, right)
    m_bd = _pair_block_diag()
    ti = lax.broadcasted_iota(jnp.int32, (chunk, LANES), 0)
    tj = _pmod(lax.broadcasted_iota(jnp.int32, (chunk, LANES), 1), chunk)
    tri_strict = (tj < ti).astype(F32)
    tri_incl = (tj <= ti).astype(F32)
    eye = (tj == ti).astype(F32)
    is_left = lax.broadcasted_iota(jnp.int32, (chunk, LANES), 1) < chunk
    n_fact = int(math.log2(chunk))

    def load_state(u):
        for p in range(N_PAIRS):
            zero = jnp.zeros((R_HEAD_DIM, R_HEAD_DIM), F32)
            sbd[u * N_PAIRS + p, 0:R_HEAD_DIM, :] = jnp.concatenate([sin_ref[u, 2 * p], zero], axis=1)
            sbd[u * N_PAIRS + p, R_HEAD_DIM:LANES, :] = jnp.concatenate([zero, sin_ref[u, 2 * p + 1]], axis=1)

    def store_state(u):
        for p in range(N_PAIRS):
            sout_ref[u, 2 * p] = sbd[u * N_PAIRS + p, 0:R_HEAD_DIM, 0:R_HEAD_DIM]
            sout_ref[u, 2 * p + 1] = sbd[u * N_PAIRS + p, R_HEAD_DIM:LANES, R_HEAD_DIM:LANES]

    @pl.when(t_step == 0)
    def _():
        for u in range(n_seq):
            load_state(u)

    pairs = range(N_PAIRS)
    col = lambda p: slice(p * LANES, (p + 1) * LANES)
    stack2 = lambda t: jnp.concatenate([t * left, t * right], axis=0)
    half = lambda t: t[:, :chunk]

    def phase1(i, carry_):
        groups = [(i * PHASE1_CHUNKS + u, p) for u in range(PHASE1_CHUNKS) for p in pairs]
        at, rt, vv, bk = [], [], [], []
        for c, p in groups:
            rs = pl.ds(pl.multiple_of(c * chunk, chunk), chunk)
            cwc = cw_s[rs, col(p)]
            w_in = jnp.exp(cwc)
            w_ex = jnp.exp(cwc - lw_s[rs, col(p)])
            w_inv = jnp.exp(-cwc)
            at.append(-kk_s[rs, col(p)] * w_ex)
            rt.append(r_s[rs, col(p)] * w_in)
            vv.append(v_s[rs, col(p)])
            bk.append(jnp.concatenate([b_s[rs, col(p)] * w_inv, k_s[rs, col(p)] * w_inv], axis=0))
        items = [(gi, h) for gi in range(len(groups)) for h in range(2)]
        n = range(len(items))
        ats = [at[gi] * head_lanes[h] for gi, h in items]
        rts = [rt[gi] * head_lanes[h] for gi, h in items]
        sc = [_dot_nt(jnp.concatenate([ats[j], rts[j]], axis=0), bk[items[j][0]]) for j in n]
        top = [sc[j][:chunk] * tri_strict for j in n]
        bot = [sc[j][chunk:] * tri_incl for j in n]
        top_sw = [pltpu.roll(top[j], chunk, 1) for j in n]
        bot_sw = [pltpu.roll(bot[j], chunk, 1) for j in n]

        pt = [jnp.where(is_left, top[j], eye) for j in n]
        for _ in range(n_fact - 1):
            z = [_dot(half(pt[j]), pt[j]) for j in n]
            pt = [jnp.where(is_left, z[j], pt[j] + z[j]) for j in n]
        pt = [pt[j] + _dot(half(pt[j]), pt[j]) for j in n]
        tm = [half(pltpu.roll(pt[j], chunk, 1)) for j in n]

        ta = [_dot(tm[j], jnp.concatenate([ats[j], top_sw[j]], axis=1)) for j in n]
        ab = [_dot(half(bot[j]), ta[j]) for j in n]
        yk = [ab[j][:, LANES:] + bot_sw[j] for j in n]
        uy = [_dot(jnp.concatenate([half(ta[j][:, LANES:]), half(yk[j])], axis=0),
                   vv[items[j][0]] * head_lanes[items[j][1]]) for j in n]
        for j, (gi, h) in enumerate(items):
            c, p = groups[gi]
            idx = c * N_PAIRS + p
            xg_s[idx, h * chunk:(h + 1) * chunk, :] = ta[j][:, :LANES]
            xg_s[idx, c2 + h * chunk:c2 + (h + 1) * chunk, :] = rts[j] + ab[j][:, :LANES]
            uy_s[idx, h * chunk:(h + 1) * chunk, :] = uy[j][:chunk]
            uy_s[idx, c2 + h * chunk:c2 + (h + 1) * chunk, :] = uy[j][chunk:]
        for gi, (c, p) in enumerate(groups):
            bk_s[c * N_PAIRS + p] = bk[gi]
        return carry_

    lax.fori_loop(0, n_chunks // PHASE1_CHUNKS, phase1, 0)

    chunks_per_seq = seq_rows // chunk

    def phase2(i, carry_):
        cks = [u * chunks_per_seq + i for u in range(n_seq)]
        items = [(u, p) for u in range(n_seq) for p in pairs]
        n = range(len(items))
        tok = [pl.ds(pl.multiple_of(ck * chunk, chunk), chunk) for ck in cks]
        tail = [pl.ds(pl.multiple_of(ck * chunk + chunk - SUBLANES, SUBLANES), SUBLANES) for ck in cks]
        s0 = [sbd[u * N_PAIRS + p] for u, p in items]
        uyv = [_dot_nt(xg_s[cks[u] * N_PAIRS + p], s0[j]) + uy_s[cks[u] * N_PAIRS + p]
               for j, (u, p) in enumerate(items)]
        uv = [jnp.concatenate([uyv[j][:c2], stack2(v_s[tok[u], col(p)])], axis=0)
              for j, (u, p) in enumerate(items)]
        rhs = []
        for u, p in items:
            bkc = bk_s[cks[u] * N_PAIRS + p]
            bt, kt = bkc[:chunk], bkc[chunk:]
            rhs.append(jnp.concatenate([bt, bt, kt, kt], axis=0))
        upd = [_dot_tn(uv[j], rhs[j]) for j in n]
        for j, (u, p) in enumerate(items):
            w_end = jnp.exp(cw_s[tail[u], col(p)][SUBLANES - 1:SUBLANES, :])
            sbd[u * N_PAIRS + p] = (s0[j] + upd[j] * m_bd) * w_end
            y_s[tok[u], col(p)] = uyv[j][c2:c2 + chunk] + uyv[j][c2 + chunk:]
        return carry_

    lax.fori_loop(0, chunks_per_seq, phase2, 0)

    @pl.when(t_step == n_steps - 1)
    def _():
        for u in range(n_seq):
            store_state(u)

    out = _group_norm_out(y_s[...], bon_s[...], g_s[...], par(lg_ref), par(lb_ref))
    y_ref[...] = out.reshape(n_seq, seq_rows, R_WIDTH).astype(y_ref.dtype)


def _wkv(zs, s_all, P, layer, *, seq_rows):
    b, t_len, _ = zs.shape
    n_seq = b
    rows = n_seq * seq_rows
    chunk = WKV_CHUNK
    n_pc = (rows // chunk) * N_PAIRS
    assert seq_rows % chunk == 0 and rows % (chunk * PHASE1_CHUNKS) == 0 and t_len % seq_rows == 0
    params = [P['decay_base'], P['lora_up'], P['iclr_base'], P['gate_up'], P['k_k'], P['k_a'],
              P['r_k'], P['lnx_g'], P['lnx_b']]
    param_specs = [_layer_spec(a, layer) if a.ndim == 3 else _const_spec(a.shape) for a in params]
    tok = lambda g, t: (g, t, 0)
    state_spec = pl.BlockSpec((None, n_seq, R_HEADS, R_HEAD_DIM, R_HEAD_DIM), lambda g, t: (layer, g, 0, 0, 0))
    tile = lambda w: pltpu.VMEM((rows, w), F32)
    return pl.pallas_call(
        functools.partial(_wkv_kernel, layer=layer, n_seq=n_seq, seq_rows=seq_rows, chunk=chunk),
        grid=(b // n_seq, t_len // seq_rows),
        in_specs=[pl.BlockSpec((n_seq, seq_rows, SHIFT_W), tok), state_spec] + param_specs,
        out_specs=[pl.BlockSpec((n_seq, seq_rows, R_WIDTH), tok), state_spec],
        out_shape=[jax.ShapeDtypeStruct((b, t_len, R_WIDTH), BF16), jax.ShapeDtypeStruct(s_all.shape, F32)],
        input_output_aliases={1: 1},
        scratch_shapes=[pltpu.VMEM((n_seq * N_PAIRS, LANES, LANES), F32)] + [tile(R_WIDTH)] * 10 + [
            pltpu.VMEM((n_pc, 4 * chunk, LANES), F32), pltpu.VMEM((n_pc, 4 * chunk, LANES), F32),
            pltpu.VMEM((n_pc, 2 * chunk, LANES), F32)],
        compiler_params=_cparams("parallel", "arbitrary"),
        name="wkv7",
    )(zs, s_all, *params)


def _wkv_prep_kernel(zs_ref, db_ref, lora_ref, ib_ref, gu_ref, kk_ref, ka_ref, rk_ref,
                     r_ref, k_ref, v_ref, kko_ref, b_ref, w_ref, g_ref, bon_ref, *, layer):
    par = lambda ref: ref[layer:layer + 1, :]
    r, kp, v, kk, a, logw, g, bonus = _token_features(
        zs_ref[...], par(db_ref), lora_ref[...], par(ib_ref), gu_ref[...], par(kk_ref), par(ka_ref), par(rk_ref))
    r_ref[...] = r
    k_ref[...] = kp
    v_ref[...] = v
    kko_ref[...] = kk
    b_ref[...] = kk * a
    w_ref[...] = jnp.exp(logw)
    g_ref[...] = g
    bon_ref[...] = bonus


LANE_GROUP = 4


def _wkv_lanes_kernel(r_ref, k_ref, v_ref, kk_ref, b_ref, w_ref, g_ref, bon_ref, lg_ref, lb_ref, s_ref,
                      y_ref, sout_ref, ft_s, yt_s, ytok_s, *, layer, n_new, whole_state):
    nb = ft_s.shape[-1]
    n = R_HEAD_DIM
    feats = (r_ref, k_ref, v_ref, kk_ref, b_ref, w_ref)
    for fi, ref in enumerate(feats):
        for t in range(n_new):
            ft_s[fi, t] = ref[pl.ds(t, nb, stride=n_new), :].T
    load_s = (lambda hh, v: s_ref[layer, hh, v]) if whole_state else (lambda hh, v: s_ref[hh, v])

    def store_s(hh, v, val):
        if whole_state:
            sout_ref[layer, hh, v] = val
        else:
            sout_ref[hh, v] = val

    for hh in range(2):
        hs = slice(hh * n, (hh + 1) * n)

        def rows8(vg, carry_):
            v0 = pl.multiple_of(vg * SUBLANES, SUBLANES)
            vrows = [ft_s[2, t, pl.ds(hh * n + v0, SUBLANES), :] for t in range(n_new)]
            ys = [[None] * SUBLANES for _ in range(n_new)]
            for j0 in range(0, SUBLANES, LANE_GROUP):
                js = range(j0, j0 + LANE_GROUP)
                sv = {j: load_s(hh, v0 + j) for j in js}
                for t in range(n_new):
                    nk, w, bb = -ft_s[3, t, hs, :], ft_s[5, t, hs, :], ft_s[4, t, hs, :]
                    kt, rt = ft_s[1, t, hs, :], ft_s[0, t, hs, :]
                    sa = {j: jnp.sum(sv[j] * nk, axis=0, keepdims=True) for j in js}
                    sv = {j: sv[j] * w + sa[j] * bb + vrows[t][j:j + 1, :] * kt for j in js}
                    for j in js:
                        ys[t][j] = jnp.sum(sv[j] * rt, axis=0, keepdims=True)
                for j in js:
                    store_s(hh, v0 + j, sv[j])
            for t in range(n_new):
                yt_s[t, pl.ds(hh * n + v0, SUBLANES), :] = jnp.concatenate(ys[t], axis=0)
            return carry_

        lax.fori_loop(0, n // SUBLANES, rows8, 0)

    if whole_state:
        for l in range(s_ref.shape[0]):
            if l != layer:
                sout_ref[l] = s_ref[l]

    for t in range(n_new):
        ytok_s[pl.ds(t, nb, stride=n_new), :] = yt_s[t].T
    out = _group_norm_out(ytok_s[...], bon_ref[...], g_ref[...], lg_ref[layer:layer + 1, :],
                          lb_ref[layer:layer + 1, :])
    y_ref[...] = out.astype(y_ref.dtype)


def _wkv_short(zs, s_lanes, P, layer, n_new):
    n_tok = zs.shape[0]
    depth, b = s_lanes.shape[0], s_lanes.shape[-1]
    assert n_tok == b * n_new
    rows = 256
    params = [P['decay_base'], P['lora_up'], P['iclr_base'], P['gate_up'], P['k_k'], P['k_a'], P['r_k']]
    param_specs = [_layer_spec(a, layer) if a.ndim == 3 else _const_spec(a.shape) for a in params]
    tok_out = pl.BlockSpec((rows, R_WIDTH), lambda g: (g, 0))
    feats = pl.pallas_call(
        functools.partial(_wkv_prep_kernel, layer=layer),
        grid=(n_tok // rows,),
        in_specs=[pl.BlockSpec((rows, SHIFT_W), lambda g: (g, 0))] + param_specs,
        out_specs=[tok_out] * 8,
        out_shape=[jax.ShapeDtypeStruct((n_tok, R_WIDTH), F32)] * 8,
        compiler_params=_cparams("parallel"),
        name="wkv7_prep",
    )(zs, *params)

    whole = layer == 0
    pair_col = pl.BlockSpec((n_tok, LANES), lambda p: (0, p))
    par_col = pl.BlockSpec((depth, LANES), lambda p: (0, p))
    n = R_HEAD_DIM
    if whole:
        state_spec = pl.BlockSpec((depth, 2, n, n, b), lambda p: (0, p, 0, 0, 0))
    else:
        state_spec = pl.BlockSpec((None, 2, n, n, b), lambda p: (layer, p, 0, 0, 0))
    y, s_new = pl.pallas_call(
        functools.partial(_wkv_lanes_kernel, layer=layer, n_new=n_new, whole_state=whole),
        grid=(N_PAIRS,),
        in_specs=[pair_col] * 8 + [par_col, par_col, state_spec],
        out_specs=[pair_col, state_spec],
        out_shape=[jax.ShapeDtypeStruct((n_tok, R_WIDTH), BF16), jax.ShapeDtypeStruct(s_lanes.shape, F32)],
        scratch_shapes=[pltpu.VMEM((6, n_new, LANES, b), F32), pltpu.VMEM((n_new, LANES, b), F32),
                        pltpu.VMEM((n_tok, LANES), F32)],
        input_output_aliases={} if whole else {10: 1},
        compiler_params=_cparams("arbitrary"),
        name="wkv7_lanes",
    )(*feats, P['lnx_g'], P['lnx_b'], s_lanes)
    return y, s_new


def _pair_kv(t):
    lane = lax.broadcasted_iota(jnp.int32, (1, LANES), 1)
    even = lane < HEAD_DIM
    sw = pltpu.roll(t, HEAD_DIM, 1)
    return jnp.where(even, t, sw), jnp.where(even, sw, t)


def _attend(qs, kcs, vcs, masks, sink_cols):
    n = range(len(qs))
    s = [jnp.where(masks[i], _dot_nt(qs[i], kcs[i]), -jnp.inf) for i in n]
    m = [jnp.maximum(jnp.max(s[i], axis=-1, keepdims=True), sink_cols[i]) for i in n]
    p = [jnp.exp(s[i] - m[i]) for i in n]
    den = [jnp.sum(p[i], axis=-1, keepdims=True) + jnp.exp(sink_cols[i] - m[i]) for i in n]
    o = [_dot(p[i], vcs[i]) for i in n]
    return [o[i] / den[i] for i in n]


def _attn_prompt_kernel(sink_ref, q_ref, kp_ref, kc_ref, vp_ref, vc_ref, o_ref, *, layer, q_blocks):
    n = pl.program_id(1)
    sink = lambda head: sink_ref[layer * A_HEADS + head]
    lane = lax.broadcasted_iota(jnp.int32, (1, LANES), 1)
    even = lane < HEAD_DIM
    ks = _pair_kv(jnp.concatenate([kp_ref[...], kc_ref[...]], axis=0))
    vs = _pair_kv(jnp.concatenate([vp_ref[...], vc_ref[...]], axis=0))
    qi = _pmod(lax.broadcasted_iota(jnp.int32, (2 * BLOCK, 2 * BLOCK), 0), BLOCK) + BLOCK
    kj = lax.broadcasted_iota(jnp.int32, (2 * BLOCK, 2 * BLOCK), 1)
    dist = qi - kj
    in_win = (dist >= 0) & (dist <= WINDOW)
    first_mask = in_win & ((n > 0) | (kj >= BLOCK))
    top = lax.broadcasted_iota(jnp.int32, (2 * BLOCK, 1), 0) < BLOCK
    blocks = range(Q_WIDTH // LANES)
    items = [(s, j) for s in range(q_blocks) for j in blocks]
    qs, kcs, vcs, masks, sink_cols = [], [], [], [], []
    for s, j in items:
        qb = q_ref[s * BLOCK:(s + 1) * BLOCK, j * LANES:(j + 1) * LANES] * ATTN_SCALE
        qs.append(jnp.concatenate([jnp.where(even, qb, 0.0), jnp.where(even, 0.0, qb)], axis=0))
        kvh = (2 * j) // GROUP
        kcs.append(ks[kvh][s * BLOCK:(s + 2) * BLOCK])
        vcs.append(vs[kvh][s * BLOCK:(s + 2) * BLOCK])
        masks.append(first_mask if s == 0 else in_win)
        sink_cols.append(jnp.where(top, sink(2 * j), sink(2 * j + 1)))
    o = _attend(qs, kcs, vcs, masks, sink_cols)
    for i, (s, j) in enumerate(items):
        o_ref[s * BLOCK:(s + 1) * BLOCK, j * LANES:(j + 1) * LANES] = jnp.where(
            even, o[i][:BLOCK], o[i][BLOCK:]).astype(o_ref.dtype)


def _attn_prompt(q, k, v, sinks, layer, batch, seq):
    q_blocks = 4
    rows = q_blocks * BLOCK
    steps = seq // rows
    cur = lambda b, n: (b * steps + n, 0)
    prv = lambda b, n: (b * steps * q_blocks + jnp.maximum(n * q_blocks - 1, 0), 0)
    return pl.pallas_call(
        functools.partial(_attn_prompt_kernel, layer=layer, q_blocks=q_blocks),
        grid=(batch, steps),
        in_specs=[pl.BlockSpec(memory_space=pltpu.SMEM), pl.BlockSpec((rows, Q_WIDTH), cur),
                  pl.BlockSpec((BLOCK, KV_WIDTH), prv), pl.BlockSpec((rows, KV_WIDTH), cur),
                  pl.BlockSpec((BLOCK, KV_WIDTH), prv), pl.BlockSpec((rows, KV_WIDTH), cur)],
        out_specs=pl.BlockSpec((rows, Q_WIDTH), cur),
        out_shape=jax.ShapeDtypeStruct((batch * seq, Q_WIDTH), BF16),
        compiler_params=_cparams("parallel", "parallel"),
        name="attn_prompt",
    )(sinks, q, k, k, v, v)


def _attn_sample_kernel(sink_ref, q_ref, kt_ref, kn_ref, vt_ref, vn_ref, o_ref, kt_out_ref, vt_out_ref,
                        *, layer, n_seq, wbuf, n_new, whole):
    assert wbuf == LANES
    win = (lambda ref, g, h: ref[layer, g, h]) if whole else (lambda ref, g, h: ref[g, h])
    lane = lax.broadcasted_iota(jnp.int32, (1, LANES), 1)
    even = lane < HEAD_DIM
    rows = GROUP * SAMPLE_PAD
    n_pad = 2 * SAMPLE_PAD
    qt = _pmod(lax.broadcasted_iota(jnp.int32, (rows, 1), 0), SAMPLE_PAD)
    dist_c = qt + wbuf - lax.broadcasted_iota(jnp.int32, (rows, wbuf), 1)
    dist_n = qt - lax.broadcasted_iota(jnp.int32, (rows, n_pad), 1)
    mask_c = (dist_c >= 0) & (dist_c <= WINDOW)
    mask_n = (dist_n >= 0) & (dist_n <= WINDOW)
    rid = _pdiv(lax.broadcasted_iota(jnp.int32, (rows, 1), 0), SAMPLE_PAD)
    sink_cols = []
    for h in range(KV_HEADS):
        col = jnp.zeros((rows, 1), F32)
        for gq in range(GROUP):
            col = jnp.where(rid == gq, sink_ref[layer * A_HEADS + h * GROUP + gq], col)
        sink_cols.append(col)
    unroll = 4
    pad = jnp.zeros((SAMPLE_PAD, KV_WIDTH), F32)
    blocks_per_kv = GROUP // 2
    keep = lane < wbuf - n_new
    sub = lax.broadcasted_iota(jnp.int32, (SAMPLE_PAD, 1), 0)

    def new_columns(t_new):
        low = jnp.where(sub >= SAMPLE_PAD - n_new, pltpu.roll(t_new, SAMPLE_PAD - n_new, 0), 0.0)
        return jnp.concatenate([jnp.zeros((wbuf - SAMPLE_PAD, KV_WIDTH), F32), low], axis=0).T

    def body(i, carry_):
        items = [(i * unroll + u, h) for u in range(unroll) for h in range(KV_HEADS)]
        n = range(len(items))
        qs, kc, vc, kn, vn = [], [], [], [], []
        for u in range(unroll):
            g = i * unroll + u
            kn2 = _pair_kv(jnp.concatenate([kn_ref[g], pad], axis=0))
            vn2 = _pair_kv(jnp.concatenate([vn_ref[g], pad], axis=0))
            for h in range(KV_HEADS):
                parts = []
                for j in range(h * blocks_per_kv, (h + 1) * blocks_per_kv):
                    qb = q_ref[g, :, j * LANES:(j + 1) * LANES] * ATTN_SCALE
                    parts += [jnp.where(even, qb, 0.0), jnp.where(even, 0.0, qb)]
                qs.append(jnp.concatenate(parts, axis=0))
                kt, vt = win(kt_ref, g, h), win(vt_ref, g, h)
                kc.append(jnp.concatenate([kt, kt], axis=0))
                vc.append(jnp.concatenate([vt, vt], axis=0))
                kn.append(kn2[h])
                vn.append(vn2[h])
        s_c = [jnp.where(mask_c, _dot(qs[j], kc[j]), -jnp.inf) for j in n]
        s_n = [jnp.where(mask_n, _dot_nt(qs[j], kn[j]), -jnp.inf) for j in n]
        m = [jnp.maximum(jnp.maximum(jnp.max(s_c[j], axis=-1, keepdims=True),
                                     jnp.max(s_n[j], axis=-1, keepdims=True)), sink_cols[items[j][1]])
             for j in n]
        p_c = [jnp.exp(s_c[j] - m[j]) for j in n]
        p_n = [jnp.exp(s_n[j] - m[j]) for j in n]
        den = [jnp.sum(p_c[j], axis=-1, keepdims=True) + jnp.sum(p_n[j], axis=-1, keepdims=True)
               + jnp.exp(sink_cols[items[j][1]] - m[j]) for j in n]
        o = [(_dot_nt(p_c[j], vc[j]) + _dot(p_n[j], vn[j])) / den[j] for j in n]
        for j, (g, h) in enumerate(items):
            for jj in range(blocks_per_kv):
                blk = h * blocks_per_kv + jj
                r0 = 2 * jj * SAMPLE_PAD
                o_ref[g, :, blk * LANES:(blk + 1) * LANES] = jnp.where(
                    even, o[j][r0:r0 + SAMPLE_PAD], o[j][r0 + SAMPLE_PAD:r0 + 2 * SAMPLE_PAD]).astype(o_ref.dtype)
        for u in range(unroll):
            g = i * unroll + u
            k_cols, v_cols = new_columns(kn_ref[g]), new_columns(vn_ref[g])
            for h in range(KV_HEADS):
                hs = slice(h * HEAD_DIM, (h + 1) * HEAD_DIM)
                for src, dst, cols in ((kt_ref, kt_out_ref, k_cols), (vt_ref, vt_out_ref, v_cols)):
                    slid = jnp.where(keep, pltpu.roll(win(src, g, h), wbuf - n_new, 1), cols[hs])
                    if whole:
                        dst[layer, g, h] = slid
                    else:
                        dst[g, h] = slid
            if whole:
                for l in range(kt_ref.shape[0]):
                    if l != layer:
                        kt_out_ref[l, g] = kt_ref[l, g]
                        vt_out_ref[l, g] = vt_ref[l, g]
        return carry_

    lax.fori_loop(0, n_seq // unroll, body, 0)


def _attn_sample(q, k, v, kt_all, vt_all, sinks, layer, n_seq, n_new):
    depth, b, _, _, wbuf = kt_all.shape
    whole = layer == 0
    idx = lambda i: (i, 0, 0)
    new = pl.BlockSpec((n_seq, SAMPLE_PAD, KV_WIDTH), idx)
    if whole:
        buf = pl.BlockSpec((depth, n_seq, KV_HEADS, HEAD_DIM, wbuf), lambda i: (0, i, 0, 0, 0))
    else:
        buf = pl.BlockSpec((None, n_seq, KV_HEADS, HEAD_DIM, wbuf), lambda i: (layer, i, 0, 0, 0))
    win_shape = jax.ShapeDtypeStruct(kt_all.shape, F32)
    return pl.pallas_call(
        functools.partial(_attn_sample_kernel, layer=layer, n_seq=n_seq, wbuf=wbuf, n_new=n_new, whole=whole),
        grid=(b // n_seq,),
        in_specs=[pl.BlockSpec(memory_space=pltpu.SMEM), pl.BlockSpec((n_seq, SAMPLE_PAD, Q_WIDTH), idx),
                  buf, new, buf, new],
        out_specs=[pl.BlockSpec((n_seq, SAMPLE_PAD, Q_WIDTH), idx), buf, buf],
        out_shape=[jax.ShapeDtypeStruct((b, SAMPLE_PAD, Q_WIDTH), BF16), win_shape, win_shape],
        input_output_aliases={} if whole else {2: 1, 4: 2},
        compiler_params=_cparams("parallel"),
        name="attn_sample",
    )(sinks, q, kt_all, k, vt_all, v)


def _layer_norm(x, g, b):
    mu = jnp.mean(x, axis=-1, keepdims=True)
    d = x - mu
    var = jnp.mean(d * d, axis=-1, keepdims=True)
    return d * lax.rsqrt(var + LN_EPS) * g + b


def _mix_ffn_kernel(x_ref, yr_ref, ya_ref, win_ref, wbr_ref, wba_ref, wo_ref, g1_ref, b1_ref,
                    wu_ref, wd_ref, g2_ref, b2_ref, o_ref, *, layer, ff_chunk):
    par = lambda ref: ref[layer:layer + 1, :]
    dotf = functools.partial(jnp.dot, preferred_element_type=F32)
    tm = x_ref.shape[0]
    groups = [slice(i * tm // ROW_GROUPS, (i + 1) * tm // ROW_GROUPS) for i in range(ROW_GROUPS)]
    n = range(ROW_GROUPS)
    x = [x_ref[g, :] for g in groups]
    xb = [x[i].astype(BF16) for i in n]
    gate_r = [jax.nn.sigmoid(dotf(xb[i], win_ref[:, QKV_END:QKV_END + D_MODEL])) for i in n]
    mix = [gate_r[i] * dotf(yr_ref[groups[i], :], wbr_ref[...]) for i in n]
    gate_a = [jax.nn.sigmoid(dotf(xb[i], win_ref[:, QKV_END + D_MODEL:])) for i in n]
    mix = [mix[i] + gate_a[i] * dotf(ya_ref[groups[i], :], wba_ref[...]) for i in n]
    x1 = [_layer_norm(ALPHA * x[i] + _dot(mix[i], wo_ref[...]), par(g1_ref), par(b1_ref)) for i in n]
    x1b = [x1[i].astype(BF16) for i in n]
    acc = [ALPHA * x1[i] for i in n]
    for c in range(D_FF // ff_chunk):
        cs = slice(c * ff_chunk, (c + 1) * ff_chunk)
        h = [jnp.maximum(dotf(x1b[i], wu_ref[:, cs]), 0.0) for i in n]
        acc = [acc[i] + _dot(h[i] * h[i], wd_ref[cs, :]) for i in n]
    for i in n:
        o_ref[groups[i], :] = _layer_norm(acc[i], par(g2_ref), par(b2_ref))


def _mix_ffn(x, yr, ya, P, layer, tm):
    n = x.shape[0]
    row = lambda i: (i, 0)
    ws = [P['w_in'], P['w_br_rwkv'], P['w_br_attn'], P['w_out'], P['ln1_g'], P['ln1_b'],
          P['w_ff_up'], P['w_ff_down'], P['ln2_g'], P['ln2_b']]
    wspec = lambda a: _layer_spec(a, layer, single_buffer=True) if a.ndim == 3 else _const_spec(a.shape)
    return pl.pallas_call(
        functools.partial(_mix_ffn_kernel, layer=layer, ff_chunk=1024),
        grid=(n // tm,),
        in_specs=[pl.BlockSpec((tm, D_MODEL), row), pl.BlockSpec((tm, R_WIDTH), row),
                  pl.BlockSpec((tm, Q_WIDTH), row)] + [wspec(a) for a in ws],
        out_specs=pl.BlockSpec((tm, D_MODEL), row),
        out_shape=jax.ShapeDtypeStruct((n, D_MODEL), F32),
        compiler_params=_cparams("parallel"),
        name="mix_ffn",
    )(x, yr, ya, *ws)


def _prepare_params(w_in, mu_shift, decay_base, decay_up, iclr_base, iclr_up, gate_up, k_k, k_a, r_k,
                    lnx_g, lnx_b, sinks, w_br_rwkv, w_br_attn, w_out, ln1_g, ln1_b, w_ff_up, w_ff_down,
                    ln2_g, ln2_b):
    depth = w_in.shape[0]
    zeros = jnp.zeros((depth, DECAY_LORA, R_WIDTH), F32)
    lora_up = jnp.concatenate([jnp.concatenate([decay_up, zeros], axis=2),
                               jnp.concatenate([zeros, iclr_up], axis=2)], axis=1)
    bf = lambda a: a.astype(BF16)
    return dict(
        w_in=bf(w_in), mu_shift=mu_shift, decay_base=decay_base, lora_up=bf(lora_up), iclr_base=iclr_base,
        gate_up=bf(gate_up), k_k=k_k, k_a=k_a, r_k=r_k.reshape(depth, R_WIDTH), lnx_g=lnx_g, lnx_b=lnx_b,
        sinks=sinks.reshape(depth * A_HEADS), w_br_rwkv=bf(w_br_rwkv), w_br_attn=bf(w_br_attn),
        w_out=bf(w_out), ln1_g=ln1_g, ln1_b=ln1_b, w_ff_up=bf(w_ff_up), w_ff_down=bf(w_ff_down),
        ln2_g=ln2_g, ln2_b=ln2_b)


def _prompt_layer(x, P, layer, tables, s_all, batch, seq, wbuf):
    tm = 512
    tp = 1024
    shift0 = jnp.zeros((batch, SHIFT_W), F32)
    zs, zlast, q, k, v = _inproj(x, P, layer, tables, seq // tp, tp, shift0, seq_rows=tp, last_row=tp - 1)
    wkv_rows = tm // batch
    yr, s_all = _wkv(zs.reshape(batch, seq, SHIFT_W), s_all, P, layer, seq_rows=wkv_rows)
    ya = _attn_prompt(q, k, v, P['sinks'], layer, batch, seq)
    x = _mix_ffn(x, yr.reshape(batch * seq, R_WIDTH), ya, P, layer, tm)
    tail = lambda t, w: t.reshape(batch, seq, w)[:, seq - wbuf:].reshape(batch, wbuf, KV_HEADS, HEAD_DIM)
    return x, s_all, zlast.reshape(batch, SHIFT_W), tail(k, KV_WIDTH), tail(v, KV_WIDTH)


def _sample_layer(x, P, layer, tables, shift_prev, s_all, kt_all, vt_all, batch, seq):
    n = batch * seq
    tm = min(n, 512)
    n_seq = 16
    zs, zlast, q, k, v = _inproj(x, P, layer, tables, 1, tm, shift_prev, seq_rows=seq, last_row=seq - 1)
    yr, s_all = _wkv_short(zs, s_all, P, layer, seq)
    pad = lambda t, w: jnp.pad(t.reshape(batch, seq, w), ((0, 0), (0, SAMPLE_PAD - seq), (0, 0)))
    ya, kt_all, vt_all = _attn_sample(pad(q, Q_WIDTH), pad(k, KV_WIDTH), pad(v, KV_WIDTH), kt_all, vt_all,
                                      P['sinks'], layer, n_seq, seq)
    x = _mix_ffn(x, yr, ya[:, :seq].reshape(n, Q_WIDTH), P, layer, tm)
    return x, s_all, kt_all, vt_all, zlast


def kernel(x_prompt, x_sample, state_wkv, state_shift, cache_k_win, cache_v_win, w_in, mu_shift, decay_base, decay_up, iclr_base, iclr_up, gate_up, k_k, k_a, r_k, lnx_g, lnx_b, sinks, w_br_rwkv, w_br_attn, w_out, ln1_g, ln1_b, w_ff_up, w_ff_down, ln2_g, ln2_b):
    bp, tp, _ = x_prompt.shape
    bs, ts, _ = x_sample.shape
    wbuf = cache_k_win.shape[2]
    half = HEAD_DIM // 2
    inv_freq = ROPE_THETA ** (-jnp.arange(half, dtype=F32) / half)
    tab_p = _rope_tables(inv_freq, tp, 0, tp)
    tab_s = _rope_tables(inv_freq, min(bs * ts, 512), PAST_LEN, ts)

    hp = x_prompt.reshape(bp * tp, D_MODEL)
    hs = x_sample.reshape(bs * ts, D_MODEL)
    outs_p, outs_s = [], []
    P = _prepare_params(w_in, mu_shift, decay_base, decay_up, iclr_base, iclr_up, gate_up, k_k, k_a, r_k,
                        lnx_g, lnx_b, sinks, w_br_rwkv, w_br_attn, w_out, ln1_g, ln1_b, w_ff_up,
                        w_ff_down, ln2_g, ln2_b)
    to_lanes = lambda c: jnp.transpose(c, (0, 1, 3, 4, 2))
    from_lanes = lambda c: jnp.transpose(c, (0, 1, 4, 2, 3))
    kt_all, vt_all = to_lanes(cache_k_win), to_lanes(cache_v_win)
    s_all_p = jnp.zeros((DEPTH, bp, R_HEADS, R_HEAD_DIM, R_HEAD_DIM), F32)
    s_all_s = jnp.transpose(state_wkv, (0, 2, 3, 4, 1))
    for l in range(DEPTH):
        hp, s_all_p, *st = _prompt_layer(hp, P, l, tab_p, s_all_p, bp, tp, wbuf)
        outs_p.append(st)
        hs, s_all_s, kt_all, vt_all, zlast = _sample_layer(hs, P, l, tab_s, state_shift[l], s_all_s, kt_all, vt_all,
                                                           bs, ts)
        outs_s.append(zlast)
    stack = lambda outs, i: jnp.stack([o[i] for o in outs])
    y_p = hp.reshape(bp, tp, D_MODEL)
    y_s = hs.reshape(bs, ts, D_MODEL)
    return (y_p, y_s,
            s_all_p, stack(outs_p, 0), stack(outs_p, 1), stack(outs_p, 2),
            jnp.transpose(s_all_s, (0, 4, 1, 2, 3)), jnp.stack(outs_s), from_lanes(kt_all), from_lanes(vt_all))
```

```python
import functools
import math

import jax
import jax.numpy as jnp
from jax import lax
from jax.experimental import pallas as pl
from jax.experimental.pallas import tpu as pltpu

F32 = jnp.float32
BF16 = jnp.bfloat16

D_MODEL = 1024
DEPTH = 2
PAST_LEN = 8192
R_HEADS = 8
R_HEAD_DIM = 64
R_WIDTH = R_HEADS * R_HEAD_DIM
DECAY_LORA = 64
ICLR_LORA = 64
GATE_LORA = 128
SHIFT_W = 3 * R_WIDTH + DECAY_LORA + ICLR_LORA + GATE_LORA
A_HEADS = 8
KV_HEADS = 2
HEAD_DIM = 64
Q_WIDTH = A_HEADS * HEAD_DIM
KV_WIDTH = KV_HEADS * HEAD_DIM
GROUP = A_HEADS // KV_HEADS
WINDOW = 128
BLOCK = 128
ROPE_THETA = 10000.0
ATTN_SCALE = HEAD_DIM ** -0.5
D_FF = 4 * D_MODEL
ALPHA = (2 * DEPTH) ** 0.25
LN_EPS = 1e-5
GN_EPS = 64e-5
QKV_END = SHIFT_W + Q_WIDTH + 2 * KV_WIDTH

LANES = 128
SUBLANES = 8
VMEM_LIMIT = 56 * 1024 * 1024

SAMPLE_PAD = SUBLANES
N_PAIRS = R_HEADS // 2
WKV_CHUNK = LANES // 2
INPROJ_ROWS = 1024
TOKEN_ROWS = 512
PREP_ROWS = 256
ATTN_Q_BLOCKS = 8
ATTN_SEQS = 16
ROW_GROUPS = 2
SHIFT_BLOCK = 2 * LANES
assert SHIFT_W % SHIFT_BLOCK == 0
PHASE1_CHUNKS = 4


def _cparams(*sem):
    return pltpu.CompilerParams(dimension_semantics=sem, vmem_limit_bytes=VMEM_LIMIT)


def _const_spec(shape):
    nd = len(shape)
    return pl.BlockSpec(shape, lambda *_: (0,) * nd)


def _layer_spec(a, layer, single_buffer=False):
    mode = pl.Buffered(1) if single_buffer else None
    return pl.BlockSpec((None,) + a.shape[1:], lambda *_: (layer, 0, 0), pipeline_mode=mode)


def _dot(a, b):
    return jnp.dot(a.astype(BF16), b.astype(BF16), preferred_element_type=F32)


def _dot_nt(a, b):
    return lax.dot_general(a.astype(BF16), b.astype(BF16), (((1,), (1,)), ((), ())),
                           preferred_element_type=F32)


def _dot_tn(a, b):
    return lax.dot_general(a.astype(BF16), b.astype(BF16), (((0,), (0,)), ((), ())),
                           preferred_element_type=F32)


def _pmod(x, n):
    assert n & (n - 1) == 0
    return x & (n - 1)


def _pdiv(x, n):
    assert n & (n - 1) == 0
    return x >> (n.bit_length() - 1)


def _split(x):
    hi = x.astype(BF16)
    lo = (x - hi.astype(F32)).astype(BF16)
    return hi, lo


def _rope_table_kernel(invf_ref, cos_ref, sa_ref, sb_ref, *, rows, offset, period):
    half = HEAD_DIM // 2
    n_groups = LANES // half
    q_rows = rows // n_groups
    i = pl.program_id(0)
    lane = lax.broadcasted_iota(jnp.int32, (q_rows, LANES), 1)
    group = _pdiv(lane, half)
    row = lax.broadcasted_iota(jnp.int32, (q_rows, LANES), 0) + group * q_rows + i * rows
    pos = offset + _pmod(row, period)
    ang = pos.astype(F32) * invf_ref[...]
    c_all, s_all = jnp.cos(ang), jnp.sin(ang)
    first = _pmod(lane, HEAD_DIM) < half

    def spread(x, c):
        y = jnp.where(group == c, x, 0.0)
        sh = half
        while sh < LANES:
            y = y + pltpu.roll(y, sh, 1)
            sh *= 2
        return y

    for c in range(n_groups):
        rs = slice(c * q_rows, (c + 1) * q_rows)
        s = spread(s_all, c)
        cos_ref[rs, :] = spread(c_all, c)
        sa_ref[rs, :] = jnp.where(first, -s, 0.0)
        sb_ref[rs, :] = jnp.where(first, 0.0, s)


def _rope_tables(inv_freq, n_rows, offset, period):
    rows = min(n_rows, 1024)
    invf = jnp.tile(inv_freq, LANES // (HEAD_DIM // 2)).reshape(1, LANES)
    out = jax.ShapeDtypeStruct((n_rows, LANES), F32)
    return pl.pallas_call(
        functools.partial(_rope_table_kernel, rows=rows, offset=offset, period=period),
        grid=(n_rows // rows,),
        in_specs=[_const_spec((1, LANES))],
        out_specs=[pl.BlockSpec((rows, LANES), lambda i: (i, 0))] * 3,
        out_shape=[out] * 3,
        compiler_params=_cparams("parallel"),
        name="rope_tables",
    )(invf)


def _inproj_kernel(x_ref, w_ref, cos_ref, sa_ref, sb_ref, mu_ref, shift_ref,
                   zs_ref, zlast_ref, q_ref, k_ref, v_ref, carry, first_s, zraw_s,
                   *, layer, n_seq, seq_rows, last_row, tiles_per_seq):
    xb = x_ref[...].astype(BF16)
    tm = xb.shape[0]
    t_step = lax.rem(pl.program_id(0), tiles_per_seq)

    @pl.when(pl.program_id(0) == 0)
    def _():
        first_s[...] = jnp.zeros(first_s.shape, F32)

    @pl.when(t_step == 0)
    def _():
        carry[...] = shift_ref[0] if n_seq == 1 else shift_ref[...]
    lane_groups = SHIFT_BLOCK // LANES
    for c in range(SHIFT_W // LANES):
        if n_seq == 1:
            first_s[c, 0:1, :] = carry[:, c * LANES:(c + 1) * LANES]
        else:
            first_s[c, pl.ds(0, n_seq, stride=seq_rows), :] = carry[:, c * LANES:(c + 1) * LANES]
    is_first = _pmod(lax.broadcasted_iota(jnp.int32, (tm, 1), 0), seq_rows) == 0
    for j in range(SHIFT_W // SHIFT_BLOCK):
        cs = slice(j * SHIFT_BLOCK, (j + 1) * SHIFT_BLOCK)
        z = jnp.dot(xb, w_ref[:, cs], preferred_element_type=F32)
        first = jnp.concatenate([first_s[j * lane_groups + c] for c in range(lane_groups)], axis=1)
        prev = jnp.where(is_first, first, pltpu.roll(z, 1, 0))
        zs_ref[:, cs] = z + (prev - z) * mu_ref[layer:layer + 1, cs]
        if n_seq == 1:
            carry[:, cs] = z[last_row:last_row + 1, :]
        else:
            for c in range(lane_groups):
                zraw_s[c] = z[:, c * LANES:(c + 1) * LANES]
                carry[:, cs.start + c * LANES:cs.start + (c + 1) * LANES] = zraw_s[
                    c, pl.ds(last_row, n_seq, stride=seq_rows), :]
    if n_seq == 1:
        zlast_ref[0] = carry[...]
    else:
        zlast_ref[...] = carry[...]
    cos, sa, sb = cos_ref[...], sa_ref[...], sb_ref[...]

    def rope(t):
        return (t * cos + pltpu.roll(t, LANES - HEAD_DIM // 2, 1) * sa
                + pltpu.roll(t, HEAD_DIM // 2, 1) * sb)

    zq = jnp.dot(xb, w_ref[:, SHIFT_W:SHIFT_W + Q_WIDTH], preferred_element_type=F32)
    for j in range(Q_WIDTH // LANES):
        q_ref[:, j * LANES:(j + 1) * LANES] = rope(zq[:, j * LANES:(j + 1) * LANES])
    zkv = jnp.dot(xb, w_ref[:, SHIFT_W + Q_WIDTH:QKV_END], preferred_element_type=F32)
    k_ref[...] = rope(zkv[:, :KV_WIDTH])
    v_ref[...] = zkv[:, KV_WIDTH:]


def _inproj(x, P, layer, tables, tab_blocks, tm, shift_in, *, seq_rows, last_row):
    n = x.shape[0]
    b = shift_in.shape[0]
    n_seq = tm // seq_rows
    if n_seq == 1:
        tiles_per_seq = (n // b) // tm
        shift_in = shift_in.reshape(b, 1, SHIFT_W)
        seq_spec = pl.BlockSpec((1, 1, SHIFT_W), lambda i: (i // tiles_per_seq, 0, 0))
    else:
        tiles_per_seq = 1
        seq_spec = pl.BlockSpec((n_seq, SHIFT_W), lambda i: (i, 0))
    row = lambda i: (i, 0)
    tab = pl.BlockSpec((tm, LANES), lambda i: (lax.rem(i, tab_blocks), 0))
    w_spec = pl.BlockSpec((None, D_MODEL, QKV_END), lambda i: (layer, 0, 0))
    return pl.pallas_call(
        functools.partial(_inproj_kernel, layer=layer, n_seq=n_seq, seq_rows=seq_rows, last_row=last_row,
                          tiles_per_seq=tiles_per_seq),
        grid=(n // tm,),
        in_specs=[pl.BlockSpec((tm, D_MODEL), row), w_spec, tab, tab, tab, _const_spec(P['mu_shift'].shape),
                  seq_spec],
        out_specs=[pl.BlockSpec((tm, SHIFT_W), row), seq_spec, pl.BlockSpec((tm, Q_WIDTH), row),
                   pl.BlockSpec((tm, KV_WIDTH), row), pl.BlockSpec((tm, KV_WIDTH), row)],
        out_shape=[jax.ShapeDtypeStruct((n, SHIFT_W), F32), jax.ShapeDtypeStruct(shift_in.shape, F32),
                   jax.ShapeDtypeStruct((n, Q_WIDTH), F32),
                   jax.ShapeDtypeStruct((n, KV_WIDTH), F32), jax.ShapeDtypeStruct((n, KV_WIDTH), F32)],
        scratch_shapes=[pltpu.VMEM((n_seq, SHIFT_W), F32), pltpu.VMEM((SHIFT_W // LANES, tm, LANES), F32),
                        pltpu.VMEM((SHIFT_BLOCK // LANES, tm, LANES), F32)],
        compiler_params=_cparams("arbitrary"),
        name="inproj",
    )(x, P['w_in'], *tables, P['mu_shift'], shift_in)


def _pair_block_diag():
    bi = _pdiv(lax.broadcasted_iota(jnp.int32, (LANES, LANES), 0), R_HEAD_DIM)
    bj = _pdiv(lax.broadcasted_iota(jnp.int32, (LANES, LANES), 1), R_HEAD_DIM)
    return (bi == bj).astype(F32)


def _head_sum(x):
    ones = _pair_block_diag().astype(BF16)
    xb = x.astype(BF16)
    return jnp.concatenate([jnp.dot(xb[:, j * LANES:(j + 1) * LANES], ones, preferred_element_type=F32)
                            for j in range(x.shape[1] // LANES)], axis=1)


def _token_features(zs, decay_base, lora_up, iclr_base, gate_up, k_k, k_a, r_k):
    o1, o2, o3 = R_WIDTH, 2 * R_WIDTH, 3 * R_WIDTH
    o4 = o3 + DECAY_LORA + ICLR_LORA
    r = zs[:, :o1]
    k = zs[:, o1:o2]
    v = zs[:, o2:o3]
    wa = zs[:, o3:o4]
    gd = zs[:, o4:]
    lane = lax.broadcasted_iota(jnp.int32, (1, LANES), 1)
    lora_in = jnp.where(lane < DECAY_LORA, jnp.tanh(wa), wa)
    lora = _dot(lora_in, lora_up)
    pre_w = decay_base + lora[:, :R_WIDTH]
    logw = -math.exp(-0.5) * jax.nn.sigmoid(pre_w)
    a = jax.nn.sigmoid(iclr_base + lora[:, R_WIDTH:])
    g = _dot(jax.nn.sigmoid(gd), gate_up)
    kk = k * k_k
    kk = kk * lax.rsqrt(jnp.maximum(_head_sum(kk * kk), 1e-24))
    kp = k * (1.0 + (a - 1.0) * k_a)
    bonus = _head_sum(r * kp * r_k) * v
    return r, kp, v, kk, a, logw, g, bonus


def _group_norm_out(y, bonus, g, lnx_g, lnx_b):
    inv_n = 1.0 / R_HEAD_DIM
    d = y - _head_sum(y) * inv_n
    var = _head_sum(d * d) * inv_n
    return (d * lax.rsqrt(var + GN_EPS) * lnx_g + lnx_b + bonus) * g


def _wkv_kernel(zs_ref, sin_ref, db_ref, lora_ref, ib_ref, gu_ref, kk_ref, ka_ref,
                rk_ref, lg_ref, lb_ref,
                y_ref, sout_ref,
                sbd, r_s, k_s, v_s, kk_s, b_s, lw_s, cw_s, y_s, bon_s, g_s, xg_s, uy_s, bk_s,
                *, layer, n_seq, seq_rows, chunk):
    t_step = pl.program_id(1)
    n_steps = pl.num_programs(1)
    rows = n_seq * seq_rows
    n_chunks = rows // chunk
    c2 = 2 * chunk
    par = lambda ref: ref[layer:layer + 1, :]

    zs = zs_ref[...].reshape(rows, SHIFT_W)
    r, kp, v, kk, a, logw, g, bonus = _token_features(
        zs, par(db_ref), lora_ref[...], par(ib_ref), gu_ref[...], par(kk_ref), par(ka_ref), par(rk_ref))
    g_s[...] = g
    bon_s[...] = bonus
    r_s[...] = r
    k_s[...] = kp
    v_s[...] = v
    kk_s[...] = kk
    b_s[...] = kk * a
    lw_s[...] = logw

    ci = lax.broadcasted_iota(jnp.int32, (chunk, chunk), 0)
    cj = lax.broadcasted_iota(jnp.int32, (chunk, chunk), 1)
    tril_ones = (cj <= ci).astype(BF16)
    lw_hi, lw_lo = _split(logw)
    for c in range(n_chunks):
        cr = slice(c * chunk, (c + 1) * chunk)
        cw_s[cr, :] = (jnp.dot(tril_ones, lw_hi[cr], preferred_element_type=F32)
                       + jnp.dot(tril_ones, lw_lo[cr], preferred_element_type=F32))

    lane = lax.broadcasted_iota(jnp.int32, (1, LANES), 1)
    left = (lane < R_HEAD_DIM).astype(F32)
    right = 1.0 - left
    head_lanes = (left, right)
    m_bd = _pair_block_diag()
    ti = lax.broadcasted_iota(jnp.int32, (chunk, LANES), 0)
    tj = _pmod(lax.broadcasted_iota(jnp.int32, (chunk, LANES), 1), chunk)
    tri_strict = (tj < ti).astype(F32)
    tri_incl = (tj <= ti).astype(F32)
    eye = (tj == ti).astype(F32)
    is_left = lax.broadcasted_iota(jnp.int32, (chunk, LANES), 1) < chunk
    n_fact = int(math.log2(chunk))

    def load_state(u):
        for p in range(N_PAIRS):
            zero = jnp.zeros((R_HEAD_DIM, R_HEAD_DIM), F32)
            sbd[u * N_PAIRS + p, 0:R_HEAD_DIM, :] = jnp.concatenate([sin_ref[u, 2 * p], zero], axis=1)
            sbd[u * N_PAIRS + p, R_HEAD_DIM:LANES, :] = jnp.concatenate([zero, sin_ref[u, 2 * p + 1]], axis=1)

    def store_state(u):
        for p in range(N_PAIRS):
            sout_ref[u, 2 * p] = sbd[u * N_PAIRS + p, 0:R_HEAD_DIM, 0:R_HEAD_DIM]
            sout_ref[u, 2 * p + 1] = sbd[u * N_PAIRS + p, R_HEAD_DIM:LANES, R_HEAD_DIM:LANES]

    @pl.when(t_step == 0)
    def _():
        for u in range(n_seq):
            load_state(u)

    pairs = range(N_PAIRS)
    col = lambda p: slice(p * LANES, (p + 1) * LANES)
    stack2 = lambda t: jnp.concatenate([t * left, t * right], axis=0)
    half = lambda t: t[:, :chunk]

    def phase1(i, carry_):
        groups = [(i * PHASE1_CHUNKS + u, p) for u in range(PHASE1_CHUNKS) for p in pairs]
        at, rt, vv, bk = [], [], [], []
        for c, p in groups:
            rs = pl.ds(pl.multiple_of(c * chunk, chunk), chunk)
            cwc = cw_s[rs, col(p)]
            w_in = jnp.exp(cwc)
            w_ex = jnp.exp(cwc - lw_s[rs, col(p)])
            w_inv = jnp.exp(-cwc)
            at.append(-kk_s[rs, col(p)] * w_ex)
            rt.append(r_s[rs, col(p)] * w_in)
            vv.append(v_s[rs, col(p)])
            bk.append(jnp.concatenate([b_s[rs, col(p)] * w_inv, k_s[rs, col(p)] * w_inv], axis=0))
        items = [(gi, h) for gi in range(len(groups)) for h in range(2)]
        n = range(len(items))
        ats = [at[gi] * head_lanes[h] for gi, h in items]
        rts = [rt[gi] * head_lanes[h] for gi, h in items]
        sc = [_dot_nt(jnp.concatenate([ats[j], rts[j]], axis=0), bk[items[j][0]]) for j in n]
        top = [sc[j][:chunk] * tri_strict for j in n]
        bot = [sc[j][chunk:] * tri_incl for j in n]
        top_sw = [pltpu.roll(top[j], chunk, 1) for j in n]
        bot_sw = [pltpu.roll(bot[j], chunk, 1) for j in n]

        pt = [jnp.where(is_left, top[j], eye) for j in n]
        for _ in range(n_fact - 1):
            z = [_dot(half(pt[j]), pt[j]) for j in n]
            pt = [jnp.where(is_left, z[j], pt[j] + z[j]) for j in n]
        pt = [pt[j] + _dot(half(pt[j]), pt[j]) for j in n]
        tm = [half(pltpu.roll(pt[j], chunk, 1)) for j in n]

        ta = [_dot(tm[j], jnp.concatenate([ats[j], top_sw[j]], axis=1)) for j in n]
        ab = [_dot(half(bot[j]), ta[j]) for j in n]
        yk = [ab[j][:, LANES:] + bot_sw[j] for j in n]
        uy = [_dot(jnp.concatenate([half(ta[j][:, LANES:]), half(yk[j])], axis=0),
                   vv[items[j][0]] * head_lanes[items[j][1]]) for j in n]
        for j, (gi, h) in enumerate(items):
            c, p = groups[gi]
            idx = c * N_PAIRS + p
            xg_s[idx, h * chunk:(h + 1) * chunk, :] = ta[j][:, :LANES]
            xg_s[idx, c2 + h * chunk:c2 + (h + 1) * chunk, :] = rts[j] + ab[j][:, :LANES]
            uy_s[idx, h * chunk:(h + 1) * chunk, :] = uy[j][:chunk]
            uy_s[idx, c2 + h * chunk:c2 + (h + 1) * chunk, :] = uy[j][chunk:]
        for gi, (c, p) in enumerate(groups):
            bk_s[c * N_PAIRS + p] = bk[gi]
        return carry_

    lax.fori_loop(0, n_chunks // PHASE1_CHUNKS, phase1, 0)

    chunks_per_seq = seq_rows // chunk

    def phase2(i, carry_):
        cks = [u * chunks_per_seq + i for u in range(n_seq)]
        items = [(u, p) for u in range(n_seq) for p in pairs]
        n = range(len(items))
        tok = [pl.ds(pl.multiple_of(ck * chunk, chunk), chunk) for ck in cks]
        tail = [pl.ds(pl.multiple_of(ck * chunk + chunk - SUBLANES, SUBLANES), SUBLANES) for ck in cks]
        s0 = [sbd[u * N_PAIRS + p] for u, p in items]
        uyv = [_dot_nt(xg_s[cks[u] * N_PAIRS + p], s0[j]) + uy_s[cks[u] * N_PAIRS + p]
               for j, (u, p) in enumerate(items)]
        uv = [jnp.concatenate([uyv[j][:c2], stack2(v_s[tok[u], col(p)])], axis=0)
              for j, (u, p) in enumerate(items)]
        rhs = []
        for u, p in items:
            bkc = bk_s[cks[u] * N_PAIRS + p]
            bt, kt = bkc[:chunk], bkc[chunk:]
            rhs.append(jnp.concatenate([bt, bt, kt, kt], axis=0))
        upd = [_dot_tn(uv[j], rhs[j]) for j in n]
        for j, (u, p) in enumerate(items):
            w_end = jnp.exp(cw_s[tail[u], col(p)][SUBLANES - 1:SUBLANES, :])
            sbd[u * N_PAIRS + p] = (s0[j] + upd[j] * m_bd) * w_end
            y_s[tok[u], col(p)] = uyv[j][c2:c2 + chunk] + uyv[j][c2 + chunk:]
        return carry_

    lax.fori_loop(0, chunks_per_seq, phase2, 0)

    @pl.when(t_step == n_steps - 1)
    def _():
        for u in range(n_seq):
            store_state(u)

    out = _group_norm_out(y_s[...], bon_s[...], g_s[...], par(lg_ref), par(lb_ref))
    y_ref[...] = out.reshape(n_seq, seq_rows, R_WIDTH).astype(y_ref.dtype)


def _wkv(zs, s_all, P, layer, *, seq_rows):
    b, t_len, _ = zs.shape
    n_seq = b
    rows = n_seq * seq_rows
    chunk = WKV_CHUNK
    n_pc = (rows // chunk) * N_PAIRS
    assert seq_rows % chunk == 0 and rows % (chunk * PHASE1_CHUNKS) == 0 and t_len % seq_rows == 0
    params = [P['decay_base'], P['lora_up'], P['iclr_base'], P['gate_up'], P['k_k'], P['k_a'],
              P['r_k'], P['lnx_g'], P['lnx_b']]
    param_specs = [_layer_spec(a, layer) if a.ndim == 3 else _const_spec(a.shape) for a in params]
    tok = lambda g, t: (g, t, 0)
    state_spec = pl.BlockSpec((None, n_seq, R_HEADS, R_HEAD_DIM, R_HEAD_DIM), lambda g, t: (layer, g, 0, 0, 0))
    tile = lambda w: pltpu.VMEM((rows, w), F32)
    return pl.pallas_call(
        functools.partial(_wkv_kernel, layer=layer, n_seq=n_seq, seq_rows=seq_rows, chunk=chunk),
        grid=(b // n_seq, t_len // seq_rows),
        in_specs=[pl.BlockSpec((n_seq, seq_rows, SHIFT_W), tok), state_spec] + param_specs,
        out_specs=[pl.BlockSpec((n_seq, seq_rows, R_WIDTH), tok), state_spec],
        out_shape=[jax.ShapeDtypeStruct((b, t_len, R_WIDTH), BF16), jax.ShapeDtypeStruct(s_all.shape, F32)],
        input_output_aliases={1: 1},
        scratch_shapes=[pltpu.VMEM((n_seq * N_PAIRS, LANES, LANES), F32)] + [tile(R_WIDTH)] * 10 + [
            pltpu.VMEM((n_pc, 4 * chunk, LANES), F32), pltpu.VMEM((n_pc, 4 * chunk, LANES), F32),
            pltpu.VMEM((n_pc, 2 * chunk, LANES), F32)],
        compiler_params=_cparams("parallel", "arbitrary"),
        name="wkv7",
    )(zs, s_all, *params)


def _wkv_prep_kernel(zs_ref, db_ref, lora_ref, ib_ref, gu_ref, kk_ref, ka_ref, rk_ref,
                     r_ref, k_ref, v_ref, kko_ref, b_ref, w_ref, g_ref, bon_ref, *, layer):
    par = lambda ref: ref[layer:layer + 1, :]
    r, kp, v, kk, a, logw, g, bonus = _token_features(
        zs_ref[...], par(db_ref), lora_ref[...], par(ib_ref), gu_ref[...], par(kk_ref), par(ka_ref), par(rk_ref))
    r_ref[...] = r
    k_ref[...] = kp
    v_ref[...] = v
    kko_ref[...] = kk
    b_ref[...] = kk * a
    w_ref[...] = jnp.exp(logw)
    g_ref[...] = g
    bon_ref[...] = bonus


LANE_GROUP = 4


def _wkv_lanes_kernel(r_ref, k_ref, v_ref, kk_ref, b_ref, w_ref, g_ref, bon_ref, lg_ref, lb_ref, s_ref,
                      y_ref, sout_ref, ft_s, yt_s, ytok_s, *, layer, n_new, whole_state):
    nb = ft_s.shape[-1]
    n = R_HEAD_DIM
    feats = (r_ref, k_ref, v_ref, kk_ref, b_ref, w_ref)
    for fi, ref in enumerate(feats):
        for t in range(n_new):
            ft_s[fi, t] = ref[pl.ds(t, nb, stride=n_new), :].T
    load_s = (lambda hh, v: s_ref[layer, hh, v]) if whole_state else (lambda hh, v: s_ref[hh, v])

    def store_s(hh, v, val):
        if whole_state:
            sout_ref[layer, hh, v] = val
        else:
            sout_ref[hh, v] = val

    for hh in range(2):
        hs = slice(hh * n, (hh + 1) * n)

        def rows8(vg, carry_):
            v0 = pl.multiple_of(vg * SUBLANES, SUBLANES)
            vrows = [ft_s[2, t, pl.ds(hh * n + v0, SUBLANES), :] for t in range(n_new)]
            ys = [[None] * SUBLANES for _ in range(n_new)]
            for j0 in range(0, SUBLANES, LANE_GROUP):
                js = range(j0, j0 + LANE_GROUP)
                sv = {j: load_s(hh, v0 + j) for j in js}
                for t in range(n_new):
                    nk, w, bb = -ft_s[3, t, hs, :], ft_s[5, t, hs, :], ft_s[4, t, hs, :]
                    kt, rt = ft_s[1, t, hs, :], ft_s[0, t, hs, :]
                    sa = {j: jnp.sum(sv[j] * nk, axis=0, keepdims=True) for j in js}
                    sv = {j: sv[j] * w + sa[j] * bb + vrows[t][j:j + 1, :] * kt for j in js}
                    for j in js:
                        ys[t][j] = jnp.sum(sv[j] * rt, axis=0, keepdims=True)
                for j in js:
                    store_s(hh, v0 + j, sv[j])
            for t in range(n_new):
                yt_s[t, pl.ds(hh * n + v0, SUBLANES), :] = jnp.concatenate(ys[t], axis=0)
            return carry_

        lax.fori_loop(0, n // SUBLANES, rows8, 0)

    if whole_state:
        for l in range(s_ref.shape[0]):
            if l != layer:
                sout_ref[l] = s_ref[l]

    for t in range(n_new):
        ytok_s[pl.ds(t, nb, stride=n_new), :] = yt_s[t].T
    out = _group_norm_out(ytok_s[...], bon_ref[...], g_ref[...], lg_ref[layer:layer + 1, :],
                          lb_ref[layer:layer + 1, :])
    y_ref[...] = out.astype(y_ref.dtype)


def _wkv_short(zs, s_lanes, P, layer, n_new):
    n_tok = zs.shape[0]
    depth, b = s_lanes.shape[0], s_lanes.shape[-1]
    assert n_tok == b * n_new
    rows = min(PREP_ROWS, n_tok)
    params = [P['decay_base'], P['lora_up'], P['iclr_base'], P['gate_up'], P['k_k'], P['k_a'], P['r_k']]
    param_specs = [_layer_spec(a, layer) if a.ndim == 3 else _const_spec(a.shape) for a in params]
    tok_out = pl.BlockSpec((rows, R_WIDTH), lambda g: (g, 0))
    feats = pl.pallas_call(
        functools.partial(_wkv_prep_kernel, layer=layer),
        grid=(n_tok // rows,),
        in_specs=[pl.BlockSpec((rows, SHIFT_W), lambda g: (g, 0))] + param_specs,
        out_specs=[tok_out] * 8,
        out_shape=[jax.ShapeDtypeStruct((n_tok, R_WIDTH), F32)] * 8,
        compiler_params=_cparams("parallel"),
        name="wkv7_prep",
    )(zs, *params)

    whole = layer == 0
    pair_col = pl.BlockSpec((n_tok, LANES), lambda p: (0, p))
    par_col = pl.BlockSpec((depth, LANES), lambda p: (0, p))
    n = R_HEAD_DIM
    if whole:
        state_spec = pl.BlockSpec((depth, 2, n, n, b), lambda p: (0, p, 0, 0, 0))
    else:
        state_spec = pl.BlockSpec((None, 2, n, n, b), lambda p: (layer, p, 0, 0, 0))
    y, s_new = pl.pallas_call(
        functools.partial(_wkv_lanes_kernel, layer=layer, n_new=n_new, whole_state=whole),
        grid=(N_PAIRS,),
        in_specs=[pair_col] * 8 + [par_col, par_col, state_spec],
        out_specs=[pair_col, state_spec],
        out_shape=[jax.ShapeDtypeStruct((n_tok, R_WIDTH), BF16), jax.ShapeDtypeStruct(s_lanes.shape, F32)],
        scratch_shapes=[pltpu.VMEM((6, n_new, LANES, b), F32), pltpu.VMEM((n_new, LANES, b), F32),
                        pltpu.VMEM((n_tok, LANES), F32)],
        input_output_aliases={} if whole else {10: 1},
        compiler_params=_cparams("arbitrary"),
        name="wkv7_lanes",
    )(*feats, P['lnx_g'], P['lnx_b'], s_lanes)
    return y, s_new


def _pair_kv(t):
    lane = lax.broadcasted_iota(jnp.int32, (1, LANES), 1)
    even = lane < HEAD_DIM
    sw = pltpu.roll(t, HEAD_DIM, 1)
    return jnp.where(even, t, sw), jnp.where(even, sw, t)


def _attend(qs, kcs, vcs, masks, sink_cols):
    n = range(len(qs))
    s = [jnp.where(masks[i], _dot_nt(qs[i], kcs[i]), -jnp.inf) for i in n]
    m = [jnp.maximum(jnp.max(s[i], axis=-1, keepdims=True), sink_cols[i]) for i in n]
    p = [jnp.exp(s[i] - m[i]) for i in n]
    den = [jnp.sum(p[i], axis=-1, keepdims=True) + jnp.exp(sink_cols[i] - m[i]) for i in n]
    o = [_dot(p[i], vcs[i]) for i in n]
    return [o[i] / den[i] for i in n]


def _attn_prompt_kernel(sink_ref, q_ref, kp_ref, kc_ref, vp_ref, vc_ref, o_ref, *, layer, q_blocks):
    n = pl.program_id(1)
    sink = lambda head: sink_ref[layer * A_HEADS + head]
    lane = lax.broadcasted_iota(jnp.int32, (1, LANES), 1)
    even = lane < HEAD_DIM
    ks = _pair_kv(jnp.concatenate([kp_ref[...], kc_ref[...]], axis=0))
    vs = _pair_kv(jnp.concatenate([vp_ref[...], vc_ref[...]], axis=0))
    qi = _pmod(lax.broadcasted_iota(jnp.int32, (2 * BLOCK, 2 * BLOCK), 0), BLOCK) + BLOCK
    kj = lax.broadcasted_iota(jnp.int32, (2 * BLOCK, 2 * BLOCK), 1)
    dist = qi - kj
    in_win = (dist >= 0) & (dist <= WINDOW)
    first_mask = in_win & ((n > 0) | (kj >= BLOCK))
    top = lax.broadcasted_iota(jnp.int32, (2 * BLOCK, 1), 0) < BLOCK
    blocks = range(Q_WIDTH // LANES)
    items = [(s, j) for s in range(q_blocks) for j in blocks]
    qs, kcs, vcs, masks, sink_cols = [], [], [], [], []
    for s, j in items:
        qb = q_ref[s * BLOCK:(s + 1) * BLOCK, j * LANES:(j + 1) * LANES] * ATTN_SCALE
        qs.append(jnp.concatenate([jnp.where(even, qb, 0.0), jnp.where(even, 0.0, qb)], axis=0))
        kvh = (2 * j) // GROUP
        kcs.append(ks[kvh][s * BLOCK:(s + 2) * BLOCK])
        vcs.append(vs[kvh][s * BLOCK:(s + 2) * BLOCK])
        masks.append(first_mask if s == 0 else in_win)
        sink_cols.append(jnp.where(top, sink(2 * j), sink(2 * j + 1)))
    o = _attend(qs, kcs, vcs, masks, sink_cols)
    for i, (s, j) in enumerate(items):
        o_ref[s * BLOCK:(s + 1) * BLOCK, j * LANES:(j + 1) * LANES] = jnp.where(
            even, o[i][:BLOCK], o[i][BLOCK:]).astype(o_ref.dtype)


def _attn_prompt(q, k, v, sinks, layer, batch, seq):
    q_blocks = ATTN_Q_BLOCKS
    rows = q_blocks * BLOCK
    steps = seq // rows
    cur = lambda b, n: (b * steps + n, 0)
    prv = lambda b, n: (b * steps * q_blocks + jnp.maximum(n * q_blocks - 1, 0), 0)
    return pl.pallas_call(
        functools.partial(_attn_prompt_kernel, layer=layer, q_blocks=q_blocks),
        grid=(batch, steps),
        in_specs=[pl.BlockSpec(memory_space=pltpu.SMEM), pl.BlockSpec((rows, Q_WIDTH), cur),
                  pl.BlockSpec((BLOCK, KV_WIDTH), prv), pl.BlockSpec((rows, KV_WIDTH), cur),
                  pl.BlockSpec((BLOCK, KV_WIDTH), prv), pl.BlockSpec((rows, KV_WIDTH), cur)],
        out_specs=pl.BlockSpec((rows, Q_WIDTH), cur),
        out_shape=jax.ShapeDtypeStruct((batch * seq, Q_WIDTH), BF16),
        compiler_params=_cparams("parallel", "parallel"),
        name="attn_prompt",
    )(sinks, q, k, k, v, v)


def _attn_sample_kernel(sink_ref, q_ref, kt_ref, kn_ref, vt_ref, vn_ref, o_ref, kt_out_ref, vt_out_ref,
                        *, layer, n_seq, wbuf, n_new, whole):
    assert wbuf == LANES
    win = (lambda ref, g, h: ref[layer, g, h]) if whole else (lambda ref, g, h: ref[g, h])
    lane = lax.broadcasted_iota(jnp.int32, (1, LANES), 1)
    even = lane < HEAD_DIM
    rows = GROUP * SAMPLE_PAD
    n_pad = 2 * SAMPLE_PAD
    qt = _pmod(lax.broadcasted_iota(jnp.int32, (rows, 1), 0), SAMPLE_PAD)
    dist_c = qt + wbuf - lax.broadcasted_iota(jnp.int32, (rows, wbuf), 1)
    dist_n = qt - lax.broadcasted_iota(jnp.int32, (rows, n_pad), 1)
    mask_c = (dist_c >= 0) & (dist_c <= WINDOW)
    mask_n = (dist_n >= 0) & (dist_n <= WINDOW)
    rid = _pdiv(lax.broadcasted_iota(jnp.int32, (rows, 1), 0), SAMPLE_PAD)
    sink_cols = []
    for h in range(KV_HEADS):
        col = jnp.zeros((rows, 1), F32)
        for gq in range(GROUP):
            col = jnp.where(rid == gq, sink_ref[layer * A_HEADS + h * GROUP + gq], col)
        sink_cols.append(col)
    unroll = 4
    pad = jnp.zeros((SAMPLE_PAD, KV_WIDTH), F32)
    blocks_per_kv = GROUP // 2
    keep = lane < wbuf - n_new
    sub = lax.broadcasted_iota(jnp.int32, (SAMPLE_PAD, 1), 0)

    def new_columns(t_new):
        low = jnp.where(sub >= SAMPLE_PAD - n_new, pltpu.roll(t_new, SAMPLE_PAD - n_new, 0), 0.0)
        return jnp.concatenate([jnp.zeros((wbuf - SAMPLE_PAD, KV_WIDTH), F32), low], axis=0).T

    def body(i, carry_):
        items = [(i * unroll + u, h) for u in range(unroll) for h in range(KV_HEADS)]
        n = range(len(items))
        qs, kc, vc, kn, vn = [], [], [], [], []
        for u in range(unroll):
            g = i * unroll + u
            kn2 = _pair_kv(jnp.concatenate([kn_ref[g], pad], axis=0))
            vn2 = _pair_kv(jnp.concatenate([vn_ref[g], pad], axis=0))
            for h in range(KV_HEADS):
                parts = []
                for j in range(h * blocks_per_kv, (h + 1) * blocks_per_kv):
                    qb = q_ref[g, :, j * LANES:(j + 1) * LANES] * ATTN_SCALE
                    parts += [jnp.where(even, qb, 0.0), jnp.where(even, 0.0, qb)]
                qs.append(jnp.concatenate(parts, axis=0))
                kt, vt = win(kt_ref, g, h), win(vt_ref, g, h)
                kc.append(jnp.concatenate([kt, kt], axis=0))
                vc.append(jnp.concatenate([vt, vt], axis=0))
                kn.append(kn2[h])
                vn.append(vn2[h])
        s_c = [jnp.where(mask_c, _dot(qs[j], kc[j]), -jnp.inf) for j in n]
        s_n = [jnp.where(mask_n, _dot_nt(qs[j], kn[j]), -jnp.inf) for j in n]
        m = [jnp.maximum(jnp.maximum(jnp.max(s_c[j], axis=-1, keepdims=True),
                                     jnp.max(s_n[j], axis=-1, keepdims=True)), sink_cols[items[j][1]])
             for j in n]
        p_c = [jnp.exp(s_c[j] - m[j]) for j in n]
        p_n = [jnp.exp(s_n[j] - m[j]) for j in n]
        den = [jnp.sum(p_c[j], axis=-1, keepdims=True) + jnp.sum(p_n[j], axis=-1, keepdims=True)
               + jnp.exp(sink_cols[items[j][1]] - m[j]) for j in n]
        o = [(_dot_nt(p_c[j], vc[j]) + _dot(p_n[j], vn[j])) / den[j] for j in n]
        for j, (g, h) in enumerate(items):
            for jj in range(blocks_per_kv):
                blk = h * blocks_per_kv + jj
                r0 = 2 * jj * SAMPLE_PAD
                o_ref[g, :, blk * LANES:(blk + 1) * LANES] = jnp.where(
                    even, o[j][r0:r0 + SAMPLE_PAD], o[j][r0 + SAMPLE_PAD:r0 + 2 * SAMPLE_PAD]).astype(o_ref.dtype)
        for u in range(unroll):
            g = i * unroll + u
            k_cols, v_cols = new_columns(kn_ref[g]), new_columns(vn_ref[g])
            for h in range(KV_HEADS):
                hs = slice(h * HEAD_DIM, (h + 1) * HEAD_DIM)
                for src, dst, cols in ((kt_ref, kt_out_ref, k_cols), (vt_ref, vt_out_ref, v_cols)):
                    slid = jnp.where(keep, pltpu.roll(win(src, g, h), wbuf - n_new, 1), cols[hs])
                    if whole:
                        dst[layer, g, h] = slid
                    else:
                        dst[g, h] = slid
            if whole:
                for l in range(kt_ref.shape[0]):
                    if l != layer:
                        kt_out_ref[l, g] = kt_ref[l, g]
                        vt_out_ref[l, g] = vt_ref[l, g]
        return carry_

    lax.fori_loop(0, n_seq // unroll, body, 0)


def _attn_sample(q, k, v, kt_all, vt_all, sinks, layer, n_seq, n_new):
    depth, b, _, _, wbuf = kt_all.shape
    whole = layer == 0
    idx = lambda i: (i, 0, 0)
    new = pl.BlockSpec((n_seq, SAMPLE_PAD, KV_WIDTH), idx)
    if whole:
        buf = pl.BlockSpec((depth, n_seq, KV_HEADS, HEAD_DIM, wbuf), lambda i: (0, i, 0, 0, 0))
    else:
        buf = pl.BlockSpec((None, n_seq, KV_HEADS, HEAD_DIM, wbuf), lambda i: (layer, i, 0, 0, 0))
    win_shape = jax.ShapeDtypeStruct(kt_all.shape, F32)
    return pl.pallas_call(
        functools.partial(_attn_sample_kernel, layer=layer, n_seq=n_seq, wbuf=wbuf, n_new=n_new, whole=whole),
        grid=(b // n_seq,),
        in_specs=[pl.BlockSpec(memory_space=pltpu.SMEM), pl.BlockSpec((n_seq, SAMPLE_PAD, Q_WIDTH), idx),
                  buf, new, buf, new],
        out_specs=[pl.BlockSpec((n_seq, SAMPLE_PAD, Q_WIDTH), idx), buf, buf],
        out_shape=[jax.ShapeDtypeStruct((b, SAMPLE_PAD, Q_WIDTH), BF16), win_shape, win_shape],
        input_output_aliases={} if whole else {2: 1, 4: 2},
        compiler_params=_cparams("parallel"),
        name="attn_sample",
    )(sinks, q, kt_all, k, vt_all, v)


def _layer_norm(x, g, b):
    mu = jnp.mean(x, axis=-1, keepdims=True)
    d = x - mu
    var = jnp.mean(d * d, axis=-1, keepdims=True)
    return d * lax.rsqrt(var + LN_EPS) * g + b


def _mix_ffn_kernel(x_ref, yr_ref, ya_ref, win_ref, wbr_ref, wba_ref, wo_ref, g1_ref, b1_ref,
                    wu_ref, wd_ref, g2_ref, b2_ref, o_ref, *, layer, ff_chunk):
    par = lambda ref: ref[layer:layer + 1, :]
    dotf = functools.partial(jnp.dot, preferred_element_type=F32)
    tm = x_ref.shape[0]
    groups = [slice(i * tm // ROW_GROUPS, (i + 1) * tm // ROW_GROUPS) for i in range(ROW_GROUPS)]
    n = range(ROW_GROUPS)
    x = [x_ref[g, :] for g in groups]
    xb = [x[i].astype(BF16) for i in n]
    gate_r = [jax.nn.sigmoid(dotf(xb[i], win_ref[:, QKV_END:QKV_END + D_MODEL])) for i in n]
    mix = [gate_r[i] * dotf(yr_ref[groups[i], :], wbr_ref[...]) for i in n]
    gate_a = [jax.nn.sigmoid(dotf(xb[i], win_ref[:, QKV_END + D_MODEL:])) for i in n]
    mix = [mix[i] + gate_a[i] * dotf(ya_ref[groups[i], :], wba_ref[...]) for i in n]
    x1 = [_layer_norm(ALPHA * x[i] + _dot(mix[i], wo_ref[...]), par(g1_ref), par(b1_ref)) for i in n]
    x1b = [x1[i].astype(BF16) for i in n]
    acc = [ALPHA * x1[i] for i in n]
    for c in range(D_FF // ff_chunk):
        cs = slice(c * ff_chunk, (c + 1) * ff_chunk)
        h = [jnp.maximum(dotf(x1b[i], wu_ref[:, cs]), 0.0) for i in n]
        acc = [acc[i] + _dot(h[i] * h[i], wd_ref[cs, :]) for i in n]
    for i in n:
        o_ref[groups[i], :] = _layer_norm(acc[i], par(g2_ref), par(b2_ref))


def _mix_ffn(x, yr, ya, P, layer, tm):
    n = x.shape[0]
    row = lambda i: (i, 0)
    ws = [P['w_in'], P['w_br_rwkv'], P['w_br_attn'], P['w_out'], P['ln1_g'], P['ln1_b'],
          P['w_ff_up'], P['w_ff_down'], P['ln2_g'], P['ln2_b']]
    wspec = lambda a: _layer_spec(a, layer, single_buffer=True) if a.ndim == 3 else _const_spec(a.shape)
    return pl.pallas_call(
        functools.partial(_mix_ffn_kernel, layer=layer, ff_chunk=1024),
        grid=(n // tm,),
        in_specs=[pl.BlockSpec((tm, D_MODEL), row), pl.BlockSpec((tm, R_WIDTH), row),
                  pl.BlockSpec((tm, Q_WIDTH), row)] + [wspec(a) for a in ws],
        out_specs=pl.BlockSpec((tm, D_MODEL), row),
        out_shape=jax.ShapeDtypeStruct((n, D_MODEL), F32),
        compiler_params=_cparams("parallel"),
        name="mix_ffn",
    )(x, yr, ya, *ws)


def _prepare_params(w_in, mu_shift, decay_base, decay_up, iclr_base, iclr_up, gate_up, k_k, k_a, r_k,
                    lnx_g, lnx_b, sinks, w_br_rwkv, w_br_attn, w_out, ln1_g, ln1_b, w_ff_up, w_ff_down,
                    ln2_g, ln2_b):
    depth = w_in.shape[0]
    zeros = jnp.zeros((depth, DECAY_LORA, R_WIDTH), F32)
    lora_up = jnp.concatenate([jnp.concatenate([decay_up, zeros], axis=2),
                               jnp.concatenate([zeros, iclr_up], axis=2)], axis=1)
    bf = lambda a: a.astype(BF16)
    return dict(
        w_in=bf(w_in), mu_shift=mu_shift, decay_base=decay_base, lora_up=bf(lora_up), iclr_base=iclr_base,
        gate_up=bf(gate_up), k_k=k_k, k_a=k_a, r_k=r_k.reshape(depth, R_WIDTH), lnx_g=lnx_g, lnx_b=lnx_b,
        sinks=sinks.reshape(depth * A_HEADS), w_br_rwkv=bf(w_br_rwkv), w_br_attn=bf(w_br_attn),
        w_out=bf(w_out), ln1_g=ln1_g, ln1_b=ln1_b, w_ff_up=bf(w_ff_up), w_ff_down=bf(w_ff_down),
        ln2_g=ln2_g, ln2_b=ln2_b)


def _prompt_layer(x, P, layer, tables, s_all, batch, seq, wbuf):
    tm = TOKEN_ROWS
    tp = INPROJ_ROWS
    shift0 = jnp.zeros((batch, SHIFT_W), F32)
    zs, zlast, q, k, v = _inproj(x, P, layer, tables, seq // tp, tp, shift0, seq_rows=tp, last_row=tp - 1)
    wkv_rows = tm // batch
    yr, s_all = _wkv(zs.reshape(batch, seq, SHIFT_W), s_all, P, layer, seq_rows=wkv_rows)
    ya = _attn_prompt(q, k, v, P['sinks'], layer, batch, seq)
    x = _mix_ffn(x, yr.reshape(batch * seq, R_WIDTH), ya, P, layer, tm)
    tail = lambda t, w: t.reshape(batch, seq, w)[:, seq - wbuf:].reshape(batch, wbuf, KV_HEADS, HEAD_DIM)
    return x, s_all, zlast.reshape(batch, SHIFT_W), tail(k, KV_WIDTH), tail(v, KV_WIDTH)


def _sample_layer(x, P, layer, tables, shift_prev, s_all, kt_all, vt_all, batch, seq):
    n = batch * seq
    tm = min(n, TOKEN_ROWS)
    n_seq = ATTN_SEQS
    zs, zlast, q, k, v = _inproj(x, P, layer, tables, 1, tm, shift_prev, seq_rows=seq, last_row=seq - 1)
    yr, s_all = _wkv_short(zs, s_all, P, layer, seq)
    pad = lambda t, w: jnp.pad(t.reshape(batch, seq, w), ((0, 0), (0, SAMPLE_PAD - seq), (0, 0)))
    ya, kt_all, vt_all = _attn_sample(pad(q, Q_WIDTH), pad(k, KV_WIDTH), pad(v, KV_WIDTH), kt_all, vt_all,
                                      P['sinks'], layer, n_seq, seq)
    x = _mix_ffn(x, yr, ya[:, :seq].reshape(n, Q_WIDTH), P, layer, tm)
    return x, s_all, kt_all, vt_all, zlast


def kernel(x_prompt, x_sample, state_wkv, state_shift, cache_k_win, cache_v_win, w_in, mu_shift, decay_base, decay_up, iclr_base, iclr_up, gate_up, k_k, k_a, r_k, lnx_g, lnx_b, sinks, w_br_rwkv, w_br_attn, w_out, ln1_g, ln1_b, w_ff_up, w_ff_down, ln2_g, ln2_b):
    bp, tp, _ = x_prompt.shape
    bs, ts, _ = x_sample.shape
    wbuf = cache_k_win.shape[2]
    half = HEAD_DIM // 2
    inv_freq = ROPE_THETA ** (-jnp.arange(half, dtype=F32) / half)
    tab_p = _rope_tables(inv_freq, tp, 0, tp)
    tab_s = _rope_tables(inv_freq, min(bs * ts, 512), PAST_LEN, ts)

    hp = x_prompt.reshape(bp * tp, D_MODEL)
    hs = x_sample.reshape(bs * ts, D_MODEL)
    outs_p, outs_s = [], []
    P = _prepare_params(w_in, mu_shift, decay_base, decay_up, iclr_base, iclr_up, gate_up, k_k, k_a, r_k,
                        lnx_g, lnx_b, sinks, w_br_rwkv, w_br_attn, w_out, ln1_g, ln1_b, w_ff_up,
                        w_ff_down, ln2_g, ln2_b)
    to_lanes = lambda c: jnp.transpose(c, (0, 1, 3, 4, 2))
    from_lanes = lambda c: jnp.transpose(c, (0, 1, 4, 2, 3))
    kt_all, vt_all = to_lanes(cache_k_win), to_lanes(cache_v_win)
    s_all_p = jnp.zeros((DEPTH, bp, R_HEADS, R_HEAD_DIM, R_HEAD_DIM), F32)
    s_all_s = jnp.transpose(state_wkv, (0, 2, 3, 4, 1))
    for l in range(DEPTH):
        hp, s_all_p, *st = _prompt_layer(hp, P, l, tab_p, s_all_p, bp, tp, wbuf)
        outs_p.append(st)
        hs, s_all_s, kt_all, vt_all, zlast = _sample_layer(hs, P, l, tab_s, state_shift[l], s_all_s, kt_all, vt_all,
                                                           bs, ts)
        outs_s.append(zlast)
    stack = lambda outs, i: jnp.stack([o[i] for o in outs])
    y_p = hp.reshape(bp, tp, D_MODEL)
    y_s = hs.reshape(bs, ts, D_MODEL)
    return (y_p, y_s,
            s_all_p, stack(outs_p, 0), stack(outs_p, 1), stack(outs_p, 2),
            jnp.transpose(s_all_s, (0, 4, 1, 2, 3)), jnp.stack(outs_s), from_lanes(kt_all), from_lanes(vt_all))
```

```python
import functools
import math

import jax
import jax.numpy as jnp
from jax import lax
from jax.experimental import pallas as pl
from jax.experimental.pallas import tpu as pltpu

F32 = jnp.float32
BF16 = jnp.bfloat16

D_MODEL = 1024
DEPTH = 2
PAST_LEN = 8192
R_HEADS = 8
R_HEAD_DIM = 64
R_WIDTH = R_HEADS * R_HEAD_DIM
DECAY_LORA = 64
ICLR_LORA = 64
GATE_LORA = 128
SHIFT_W = 3 * R_WIDTH + DECAY_LORA + ICLR_LORA + GATE_LORA
A_HEADS = 8
KV_HEADS = 2
HEAD_DIM = 64
Q_WIDTH = A_HEADS * HEAD_DIM
KV_WIDTH = KV_HEADS * HEAD_DIM
GROUP = A_HEADS // KV_HEADS
WINDOW = 128
BLOCK = 128
ROPE_THETA = 10000.0
ATTN_SCALE = HEAD_DIM ** -0.5
D_FF = 4 * D_MODEL
ALPHA = (2 * DEPTH) ** 0.25
LN_EPS = 1e-5
GN_EPS = 64e-5
QKV_END = SHIFT_W + Q_WIDTH + 2 * KV_WIDTH

LANES = 128
SUBLANES = 8
VMEM_LIMIT = 56 * 1024 * 1024

SAMPLE_PAD = SUBLANES
N_PAIRS = R_HEADS // 2
WKV_CHUNK = LANES // 2
INPROJ_ROWS = 1024
TOKEN_ROWS = 512
PREP_ROWS = 256
ATTN_Q_BLOCKS = 8
ATTN_SEQS = 32
ROW_GROUPS = 2
SHIFT_BLOCK = 2 * LANES
assert SHIFT_W % SHIFT_BLOCK == 0
PHASE1_CHUNKS = 4


def _cparams(*sem):
    return pltpu.CompilerParams(dimension_semantics=sem, vmem_limit_bytes=VMEM_LIMIT)


def _const_spec(shape):
    nd = len(shape)
    return pl.BlockSpec(shape, lambda *_: (0,) * nd)


def _layer_spec(a, layer, single_buffer=False):
    mode = pl.Buffered(1) if single_buffer else None
    return pl.BlockSpec((None,) + a.shape[1:], lambda *_: (layer, 0, 0), pipeline_mode=mode)


def _dot(a, b):
    return jnp.dot(a.astype(BF16), b.astype(BF16), preferred_element_type=F32)


def _dot_nt(a, b):
    return lax.dot_general(a.astype(BF16), b.astype(BF16), (((1,), (1,)), ((), ())),
                           preferred_element_type=F32)


def _dot_tn(a, b):
    return lax.dot_general(a.astype(BF16), b.astype(BF16), (((0,), (0,)), ((), ())),
                           preferred_element_type=F32)


def _pmod(x, n):
    assert n & (n - 1) == 0
    return x & (n - 1)


def _pdiv(x, n):
    assert n & (n - 1) == 0
    return x >> (n.bit_length() - 1)


def _split(x):
    hi = x.astype(BF16)
    lo = (x - hi.astype(F32)).astype(BF16)
    return hi, lo


def _rope_table_kernel(invf_ref, cos_ref, sa_ref, sb_ref, *, rows, offset, period):
    half = HEAD_DIM // 2
    n_groups = LANES // half
    q_rows = rows // n_groups
    i = pl.program_id(0)
    lane = lax.broadcasted_iota(jnp.int32, (q_rows, LANES), 1)
    group = _pdiv(lane, half)
    row = lax.broadcasted_iota(jnp.int32, (q_rows, LANES), 0) + group * q_rows + i * rows
    pos = offset + _pmod(row, period)
    ang = pos.astype(F32) * invf_ref[...]
    c_all, s_all = jnp.cos(ang), jnp.sin(ang)
    first = _pmod(lane, HEAD_DIM) < half

    def spread(x, c):
        y = jnp.where(group == c, x, 0.0)
        sh = half
        while sh < LANES:
            y = y + pltpu.roll(y, sh, 1)
            sh *= 2
        return y

    for c in range(n_groups):
        rs = slice(c * q_rows, (c + 1) * q_rows)
        s = spread(s_all, c)
        cos_ref[rs, :] = spread(c_all, c)
        sa_ref[rs, :] = jnp.where(first, -s, 0.0)
        sb_ref[rs, :] = jnp.where(first, 0.0, s)


def _rope_tables(inv_freq, n_rows, offset, period):
    rows = min(n_rows, 1024)
    invf = jnp.tile(inv_freq, LANES // (HEAD_DIM // 2)).reshape(1, LANES)
    out = jax.ShapeDtypeStruct((n_rows, LANES), F32)
    return pl.pallas_call(
        functools.partial(_rope_table_kernel, rows=rows, offset=offset, period=period),
        grid=(n_rows // rows,),
        in_specs=[_const_spec((1, LANES))],
        out_specs=[pl.BlockSpec((rows, LANES), lambda i: (i, 0))] * 3,
        out_shape=[out] * 3,
        compiler_params=_cparams("parallel"),
        name="rope_tables",
    )(invf)


def _inproj_kernel(x_ref, w_ref, cos_ref, sa_ref, sb_ref, mu_ref, shift_ref,
                   zs_ref, zlast_ref, q_ref, k_ref, v_ref, carry, first_s, zraw_s,
                   *, layer, n_seq, seq_rows, last_row, tiles_per_seq):
    xb = x_ref[...].astype(BF16)
    tm = xb.shape[0]
    t_step = lax.rem(pl.program_id(0), tiles_per_seq)

    @pl.when(pl.program_id(0) == 0)
    def _():
        first_s[...] = jnp.zeros(first_s.shape, F32)

    @pl.when(t_step == 0)
    def _():
        carry[...] = shift_ref[0] if n_seq == 1 else shift_ref[...]
    lane_groups = SHIFT_BLOCK // LANES
    for c in range(SHIFT_W // LANES):
        if n_seq == 1:
            first_s[c, 0:1, :] = carry[:, c * LANES:(c + 1) * LANES]
        else:
            first_s[c, pl.ds(0, n_seq, stride=seq_rows), :] = carry[:, c * LANES:(c + 1) * LANES]
    is_first = _pmod(lax.broadcasted_iota(jnp.int32, (tm, 1), 0), seq_rows) == 0
    for j in range(SHIFT_W // SHIFT_BLOCK):
        cs = slice(j * SHIFT_BLOCK, (j + 1) * SHIFT_BLOCK)
        z = jnp.dot(xb, w_ref[:, cs], preferred_element_type=F32)
        first = jnp.concatenate([first_s[j * lane_groups + c] for c in range(lane_groups)], axis=1)
        prev = jnp.where(is_first, first, pltpu.roll(z, 1, 0))
        zs_ref[:, cs] = z + (prev - z) * mu_ref[layer:layer + 1, cs]
        if n_seq == 1:
            carry[:, cs] = z[last_row:last_row + 1, :]
        else:
            for c in range(lane_groups):
                zraw_s[c] = z[:, c * LANES:(c + 1) * LANES]
                carry[:, cs.start + c * LANES:cs.start + (c + 1) * LANES] = zraw_s[
                    c, pl.ds(last_row, n_seq, stride=seq_rows), :]
    if n_seq == 1:
        zlast_ref[0] = carry[...]
    else:
        zlast_ref[...] = carry[...]
    cos, sa, sb = cos_ref[...], sa_ref[...], sb_ref[...]

    def rope(t):
        return (t * cos + pltpu.roll(t, LANES - HEAD_DIM // 2, 1) * sa
                + pltpu.roll(t, HEAD_DIM // 2, 1) * sb)

    zq = jnp.dot(xb, w_ref[:, SHIFT_W:SHIFT_W + Q_WIDTH], preferred_element_type=F32)
    for j in range(Q_WIDTH // LANES):
        q_ref[:, j * LANES:(j + 1) * LANES] = rope(zq[:, j * LANES:(j + 1) * LANES])
    zkv = jnp.dot(xb, w_ref[:, SHIFT_W + Q_WIDTH:QKV_END], preferred_element_type=F32)
    k_ref[...] = rope(zkv[:, :KV_WIDTH])
    v_ref[...] = zkv[:, KV_WIDTH:]


def _inproj(x, P, layer, tables, tab_blocks, tm, shift_in, *, seq_rows, last_row):
    n = x.shape[0]
    b = shift_in.shape[0]
    n_seq = tm // seq_rows
    if n_seq == 1:
        tiles_per_seq = (n // b) // tm
        shift_in = shift_in.reshape(b, 1, SHIFT_W)
        seq_spec = pl.BlockSpec((1, 1, SHIFT_W), lambda i: (i // tiles_per_seq, 0, 0))
    else:
        tiles_per_seq = 1
        seq_spec = pl.BlockSpec((n_seq, SHIFT_W), lambda i: (i, 0))
    row = lambda i: (i, 0)
    tab = pl.BlockSpec((tm, LANES), lambda i: (lax.rem(i, tab_blocks), 0))
    w_spec = pl.BlockSpec((None, D_MODEL, QKV_END), lambda i: (layer, 0, 0))
    return pl.pallas_call(
        functools.partial(_inproj_kernel, layer=layer, n_seq=n_seq, seq_rows=seq_rows, last_row=last_row,
                          tiles_per_seq=tiles_per_seq),
        grid=(n // tm,),
        in_specs=[pl.BlockSpec((tm, D_MODEL), row), w_spec, tab, tab, tab, _const_spec(P['mu_shift'].shape),
                  seq_spec],
        out_specs=[pl.BlockSpec((tm, SHIFT_W), row), seq_spec, pl.BlockSpec((tm, Q_WIDTH), row),
                   pl.BlockSpec((tm, KV_WIDTH), row), pl.BlockSpec((tm, KV_WIDTH), row)],
        out_shape=[jax.ShapeDtypeStruct((n, SHIFT_W), F32), jax.ShapeDtypeStruct(shift_in.shape, F32),
                   jax.ShapeDtypeStruct((n, Q_WIDTH), F32),
                   jax.ShapeDtypeStruct((n, KV_WIDTH), F32), jax.ShapeDtypeStruct((n, KV_WIDTH), F32)],
        scratch_shapes=[pltpu.VMEM((n_seq, SHIFT_W), F32), pltpu.VMEM((SHIFT_W // LANES, tm, LANES), F32),
                        pltpu.VMEM((SHIFT_BLOCK // LANES, tm, LANES), F32)],
        compiler_params=_cparams("arbitrary"),
        name="inproj",
    )(x, P['w_in'], *tables, P['mu_shift'], shift_in)


def _pair_block_diag():
    bi = _pdiv(lax.broadcasted_iota(jnp.int32, (LANES, LANES), 0), R_HEAD_DIM)
    bj = _pdiv(lax.broadcasted_iota(jnp.int32, (LANES, LANES), 1), R_HEAD_DIM)
    return (bi == bj).astype(F32)


def _head_sum(x):
    ones = _pair_block_diag().astype(BF16)
    xb = x.astype(BF16)
    return jnp.concatenate([jnp.dot(xb[:, j * LANES:(j + 1) * LANES], ones, preferred_element_type=F32)
                            for j in range(x.shape[1] // LANES)], axis=1)


def _token_features(zs, decay_base, lora_up, iclr_base, gate_up, k_k, k_a, r_k):
    o1, o2, o3 = R_WIDTH, 2 * R_WIDTH, 3 * R_WIDTH
    o4 = o3 + DECAY_LORA + ICLR_LORA
    r = zs[:, :o1]
    k = zs[:, o1:o2]
    v = zs[:, o2:o3]
    wa = zs[:, o3:o4]
    gd = zs[:, o4:]
    lane = lax.broadcasted_iota(jnp.int32, (1, LANES), 1)
    lora_in = jnp.where(lane < DECAY_LORA, jnp.tanh(wa), wa)
    lora = _dot(lora_in, lora_up)
    pre_w = decay_base + lora[:, :R_WIDTH]
    logw = -math.exp(-0.5) * jax.nn.sigmoid(pre_w)
    a = jax.nn.sigmoid(iclr_base + lora[:, R_WIDTH:])
    g = _dot(jax.nn.sigmoid(gd), gate_up)
    kk = k * k_k
    kk = kk * lax.rsqrt(jnp.maximum(_head_sum(kk * kk), 1e-24))
    kp = k * (1.0 + (a - 1.0) * k_a)
    bonus = _head_sum(r * kp * r_k) * v
    return r, kp, v, kk, a, logw, g, bonus


def _group_norm_out(y, bonus, g, lnx_g, lnx_b):
    inv_n = 1.0 / R_HEAD_DIM
    d = y - _head_sum(y) * inv_n
    var = _head_sum(d * d) * inv_n
    return (d * lax.rsqrt(var + GN_EPS) * lnx_g + lnx_b + bonus) * g


def _wkv_kernel(zs_ref, sin_ref, db_ref, lora_ref, ib_ref, gu_ref, kk_ref, ka_ref,
                rk_ref, lg_ref, lb_ref,
                y_ref, sout_ref,
                sbd, r_s, k_s, v_s, kk_s, b_s, lw_s, cw_s, y_s, bon_s, g_s, xg_s, uy_s, bk_s,
                *, layer, n_seq, seq_rows, chunk):
    t_step = pl.program_id(1)
    n_steps = pl.num_programs(1)
    rows = n_seq * seq_rows
    n_chunks = rows // chunk
    c2 = 2 * chunk
    par = lambda ref: ref[layer:layer + 1, :]

    zs = zs_ref[...].reshape(rows, SHIFT_W)
    r, kp, v, kk, a, logw, g, bonus = _token_features(
        zs, par(db_ref), lora_ref[...], par(ib_ref), gu_ref[...], par(kk_ref), par(ka_ref), par(rk_ref))
    g_s[...] = g
    bon_s[...] = bonus
    r_s[...] = r
    k_s[...] = kp
    v_s[...] = v
    kk_s[...] = kk
    b_s[...] = kk * a
    lw_s[...] = logw

    ci = lax.broadcasted_iota(jnp.int32, (chunk, chunk), 0)
    cj = lax.broadcasted_iota(jnp.int32, (chunk, chunk), 1)
    tril_ones = (cj <= ci).astype(BF16)
    lw_hi, lw_lo = _split(logw)
    for c in range(n_chunks):
        cr = slice(c * chunk, (c + 1) * chunk)
        cw_s[cr, :] = (jnp.dot(tril_ones, lw_hi[cr], preferred_element_type=F32)
                       + jnp.dot(tril_ones, lw_lo[cr], preferred_element_type=F32))

    lane = lax.broadcasted_iota(jnp.int32, (1, LANES), 1)
    left = (lane < R_HEAD_DIM).astype(F32)
    right = 1.0 - left
    head_lanes = (left, right)
    m_bd = _pair_block_diag()
    ti = lax.broadcasted_iota(jnp.int32, (chunk, LANES), 0)
    tj = _pmod(lax.broadcasted_iota(jnp.int32, (chunk, LANES), 1), chunk)
    tri_strict = (tj < ti).astype(F32)
    tri_incl = (tj <= ti).astype(F32)
    eye = (tj == ti).astype(F32)
    is_left = lax.broadcasted_iota(jnp.int32, (chunk, LANES), 1) < chunk
    n_fact = int(math.log2(chunk))

    def load_state(u):
        for p in range(N_PAIRS):
            zero = jnp.zeros((R_HEAD_DIM, R_HEAD_DIM), F32)
            sbd[u * N_PAIRS + p, 0:R_HEAD_DIM, :] = jnp.concatenate([sin_ref[u, 2 * p], zero], axis=1)
            sbd[u * N_PAIRS + p, R_HEAD_DIM:LANES, :] = jnp.concatenate([zero, sin_ref[u, 2 * p + 1]], axis=1)

    def store_state(u):
        for p in range(N_PAIRS):
            sout_ref[u, 2 * p] = sbd[u * N_PAIRS + p, 0:R_HEAD_DIM, 0:R_HEAD_DIM]
            sout_ref[u, 2 * p + 1] = sbd[u * N_PAIRS + p, R_HEAD_DIM:LANES, R_HEAD_DIM:LANES]

    @pl.when(t_step == 0)
    def _():
        for u in range(n_seq):
            load_state(u)

    pairs = range(N_PAIRS)
    col = lambda p: slice(p * LANES, (p + 1) * LANES)
    stack2 = lambda t: jnp.concatenate([t * left, t * right], axis=0)
    half = lambda t: t[:, :chunk]

    def phase1(i, carry_):
        groups = [(i * PHASE1_CHUNKS + u, p) for u in range(PHASE1_CHUNKS) for p in pairs]
        at, rt, vv, bk = [], [], [], []
        for c, p in groups:
            rs = pl.ds(pl.multiple_of(c * chunk, chunk), chunk)
            cwc = cw_s[rs, col(p)]
            w_in = jnp.exp(cwc)
            w_ex = jnp.exp(cwc - lw_s[rs, col(p)])
            w_inv = jnp.exp(-cwc)
            at.append(-kk_s[rs, col(p)] * w_ex)
            rt.append(r_s[rs, col(p)] * w_in)
            vv.append(v_s[rs, col(p)])
            bk.append(jnp.concatenate([b_s[rs, col(p)] * w_inv, k_s[rs, col(p)] * w_inv], axis=0))
        items = [(gi, h) for gi in range(len(groups)) for h in range(2)]
        n = range(len(items))
        ats = [at[gi] * head_lanes[h] for gi, h in items]
        rts = [rt[gi] * head_lanes[h] for gi, h in items]
        sc = [_dot_nt(jnp.concatenate([ats[j], rts[j]], axis=0), bk[items[j][0]]) for j in n]
        top = [sc[j][:chunk] * tri_strict for j in n]
        bot = [sc[j][chunk:] * tri_incl for j in n]
        top_sw = [pltpu.roll(top[j], chunk, 1) for j in n]
        bot_sw = [pltpu.roll(bot[j], chunk, 1) for j in n]

        pt = [jnp.where(is_left, top[j], eye) for j in n]
        for _ in range(n_fact - 1):
            z = [_dot(half(pt[j]), pt[j]) for j in n]
            pt = [jnp.where(is_left, z[j], pt[j] + z[j]) for j in n]
        pt = [pt[j] + _dot(half(pt[j]), pt[j]) for j in n]
        tm = [half(pltpu.roll(pt[j], chunk, 1)) for j in n]

        ta = [_dot(tm[j], jnp.concatenate([ats[j], top_sw[j]], axis=1)) for j in n]
        ab = [_dot(half(bot[j]), ta[j]) for j in n]
        yk = [ab[j][:, LANES:] + bot_sw[j] for j in n]
        uy = [_dot(jnp.concatenate([half(ta[j][:, LANES:]), half(yk[j])], axis=0),
                   vv[items[j][0]] * head_lanes[items[j][1]]) for j in n]
        for j, (gi, h) in enumerate(items):
            c, p = groups[gi]
            idx = c * N_PAIRS + p
            xg_s[idx, h * chunk:(h + 1) * chunk, :] = ta[j][:, :LANES]
            xg_s[idx, c2 + h * chunk:c2 + (h + 1) * chunk, :] = rts[j] + ab[j][:, :LANES]
            uy_s[idx, h * chunk:(h + 1) * chunk, :] = uy[j][:chunk]
            uy_s[idx, c2 + h * chunk:c2 + (h + 1) * chunk, :] = uy[j][chunk:]
        for gi, (c, p) in enumerate(groups):
            bk_s[c * N_PAIRS + p] = bk[gi]
        return carry_

    lax.fori_loop(0, n_chunks // PHASE1_CHUNKS, phase1, 0)

    chunks_per_seq = seq_rows // chunk

    def phase2(i, carry_):
        cks = [u * chunks_per_seq + i for u in range(n_seq)]
        items = [(u, p) for u in range(n_seq) for p in pairs]
        n = range(len(items))
        tok = [pl.ds(pl.multiple_of(ck * chunk, chunk), chunk) for ck in cks]
        tail = [pl.ds(pl.multiple_of(ck * chunk + chunk - SUBLANES, SUBLANES), SUBLANES) for ck in cks]
        s0 = [sbd[u * N_PAIRS + p] for u, p in items]
        uyv = [_dot_nt(xg_s[cks[u] * N_PAIRS + p], s0[j]) + uy_s[cks[u] * N_PAIRS + p]
               for j, (u, p) in enumerate(items)]
        uv = [jnp.concatenate([uyv[j][:c2], stack2(v_s[tok[u], col(p)])], axis=0)
              for j, (u, p) in enumerate(items)]
        rhs = []
        for u, p in items:
            bkc = bk_s[cks[u] * N_PAIRS + p]
            bt, kt = bkc[:chunk], bkc[chunk:]
            rhs.append(jnp.concatenate([bt, bt, kt, kt], axis=0))
        upd = [_dot_tn(uv[j], rhs[j]) for j in n]
        for j, (u, p) in enumerate(items):
            w_end = jnp.exp(cw_s[tail[u], col(p)][SUBLANES - 1:SUBLANES, :])
            sbd[u * N_PAIRS + p] = (s0[j] + upd[j] * m_bd) * w_end
            y_s[tok[u], col(p)] = uyv[j][c2:c2 + chunk] + uyv[j][c2 + chunk:]
        return carry_

    lax.fori_loop(0, chunks_per_seq, phase2, 0)

    @pl.when(t_step == n_steps - 1)
    def _():
        for u in range(n_seq):
            store_state(u)

    out = _group_norm_out(y_s[...], bon_s[...], g_s[...], par(lg_ref), par(lb_ref))
    y_ref[...] = out.reshape(n_seq, seq_rows, R_WIDTH).astype(y_ref.dtype)


def _wkv(zs, s_all, P, layer, *, seq_rows):
    b, t_len, _ = zs.shape
    n_seq = b
    rows = n_seq * seq_rows
    chunk = WKV_CHUNK
    n_pc = (rows // chunk) * N_PAIRS
    assert seq_rows % chunk == 0 and rows % (chunk * PHASE1_CHUNKS) == 0 and t_len % seq_rows == 0
    params = [P['decay_base'], P['lora_up'], P['iclr_base'], P['gate_up'], P['k_k'], P['k_a'],
              P['r_k'], P['lnx_g'], P['lnx_b']]
    param_specs = [_layer_spec(a, layer) if a.ndim == 3 else _const_spec(a.shape) for a in params]
    tok = lambda g, t: (g, t, 0)
    state_spec = pl.BlockSpec((None, n_seq, R_HEADS, R_HEAD_DIM, R_HEAD_DIM), lambda g, t: (layer, g, 0, 0, 0))
    tile = lambda w: pltpu.VMEM((rows, w), F32)
    return pl.pallas_call(
        functools.partial(_wkv_kernel, layer=layer, n_seq=n_seq, seq_rows=seq_rows, chunk=chunk),
        grid=(b // n_seq, t_len // seq_rows),
        in_specs=[pl.BlockSpec((n_seq, seq_rows, SHIFT_W), tok), state_spec] + param_specs,
        out_specs=[pl.BlockSpec((n_seq, seq_rows, R_WIDTH), tok), state_spec],
        out_shape=[jax.ShapeDtypeStruct((b, t_len, R_WIDTH), BF16), jax.ShapeDtypeStruct(s_all.shape, F32)],
        input_output_aliases={1: 1},
        scratch_shapes=[pltpu.VMEM((n_seq * N_PAIRS, LANES, LANES), F32)] + [tile(R_WIDTH)] * 10 + [
            pltpu.VMEM((n_pc, 4 * chunk, LANES), F32), pltpu.VMEM((n_pc, 4 * chunk, LANES), F32),
            pltpu.VMEM((n_pc, 2 * chunk, LANES), F32)],
        compiler_params=_cparams("parallel", "arbitrary"),
        name="wkv7",
    )(zs, s_all, *params)


def _wkv_prep_kernel(zs_ref, db_ref, lora_ref, ib_ref, gu_ref, kk_ref, ka_ref, rk_ref,
                     r_ref, k_ref, v_ref, kko_ref, b_ref, w_ref, g_ref, bon_ref, *, layer):
    par = lambda ref: ref[layer:layer + 1, :]
    r, kp, v, kk, a, logw, g, bonus = _token_features(
        zs_ref[...], par(db_ref), lora_ref[...], par(ib_ref), gu_ref[...], par(kk_ref), par(ka_ref), par(rk_ref))
    r_ref[...] = r
    k_ref[...] = kp
    v_ref[...] = v
    kko_ref[...] = kk
    b_ref[...] = kk * a
    w_ref[...] = jnp.exp(logw)
    g_ref[...] = g
    bon_ref[...] = bonus


LANE_GROUP = 4


def _wkv_lanes_kernel(r_ref, k_ref, v_ref, kk_ref, b_ref, w_ref, g_ref, bon_ref, lg_ref, lb_ref, s_ref,
                      y_ref, sout_ref, ft_s, yt_s, ytok_s, *, layer, n_new, whole_state):
    nb = ft_s.shape[-1]
    n = R_HEAD_DIM
    feats = (r_ref, k_ref, v_ref, kk_ref, b_ref, w_ref)
    for fi, ref in enumerate(feats):
        for t in range(n_new):
            ft_s[fi, t] = ref[pl.ds(t, nb, stride=n_new), :].T
    load_s = (lambda hh, v: s_ref[layer, hh, v]) if whole_state else (lambda hh, v: s_ref[hh, v])

    def store_s(hh, v, val):
        if whole_state:
            sout_ref[layer, hh, v] = val
        else:
            sout_ref[hh, v] = val

    for hh in range(2):
        hs = slice(hh * n, (hh + 1) * n)

        def rows8(vg, carry_):
            v0 = pl.multiple_of(vg * SUBLANES, SUBLANES)
            vrows = [ft_s[2, t, pl.ds(hh * n + v0, SUBLANES), :] for t in range(n_new)]
            ys = [[None] * SUBLANES for _ in range(n_new)]
            for j0 in range(0, SUBLANES, LANE_GROUP):
                js = range(j0, j0 + LANE_GROUP)
                sv = {j: load_s(hh, v0 + j) for j in js}
                for t in range(n_new):
                    nk, w, bb = -ft_s[3, t, hs, :], ft_s[5, t, hs, :], ft_s[4, t, hs, :]
                    kt, rt = ft_s[1, t, hs, :], ft_s[0, t, hs, :]
                    sa = {j: jnp.sum(sv[j] * nk, axis=0, keepdims=True) for j in js}
                    sv = {j: sv[j] * w + sa[j] * bb + vrows[t][j:j + 1, :] * kt for j in js}
                    for j in js:
                        ys[t][j] = jnp.sum(sv[j] * rt, axis=0, keepdims=True)
                for j in js:
                    store_s(hh, v0 + j, sv[j])
            for t in range(n_new):
                yt_s[t, pl.ds(hh * n + v0, SUBLANES), :] = jnp.concatenate(ys[t], axis=0)
            return carry_

        lax.fori_loop(0, n // SUBLANES, rows8, 0)

    if whole_state:
        for l in range(s_ref.shape[0]):
            if l != layer:
                sout_ref[l] = s_ref[l]

    for t in range(n_new):
        ytok_s[pl.ds(t, nb, stride=n_new), :] = yt_s[t].T
    out = _group_norm_out(ytok_s[...], bon_ref[...], g_ref[...], lg_ref[layer:layer + 1, :],
                          lb_ref[layer:layer + 1, :])
    y_ref[...] = out.astype(y_ref.dtype)


def _wkv_short(zs, s_lanes, P, layer, n_new):
    n_tok = zs.shape[0]
    depth, b = s_lanes.shape[0], s_lanes.shape[-1]
    assert n_tok == b * n_new
    rows = min(PREP_ROWS, n_tok)
    params = [P['decay_base'], P['lora_up'], P['iclr_base'], P['gate_up'], P['k_k'], P['k_a'], P['r_k']]
    param_specs = [_layer_spec(a, layer) if a.ndim == 3 else _const_spec(a.shape) for a in params]
    tok_out = pl.BlockSpec((rows, R_WIDTH), lambda g: (g, 0))
    feats = pl.pallas_call(
        functools.partial(_wkv_prep_kernel, layer=layer),
        grid=(n_tok // rows,),
        in_specs=[pl.BlockSpec((rows, SHIFT_W), lambda g: (g, 0))] + param_specs,
        out_specs=[tok_out] * 8,
        out_shape=[jax.ShapeDtypeStruct((n_tok, R_WIDTH), F32)] * 8,
        compiler_params=_cparams("parallel"),
        name="wkv7_prep",
    )(zs, *params)

    whole = layer == 0
    pair_col = pl.BlockSpec((n_tok, LANES), lambda p: (0, p))
    par_col = pl.BlockSpec((depth, LANES), lambda p: (0, p))
    n = R_HEAD_DIM
    if whole:
        state_spec = pl.BlockSpec((depth, 2, n, n, b), lambda p: (0, p, 0, 0, 0))
    else:
        state_spec = pl.BlockSpec((None, 2, n, n, b), lambda p: (layer, p, 0, 0, 0))
    y, s_new = pl.pallas_call(
        functools.partial(_wkv_lanes_kernel, layer=layer, n_new=n_new, whole_state=whole),
        grid=(N_PAIRS,),
        in_specs=[pair_col] * 8 + [par_col, par_col, state_spec],
        out_specs=[pair_col, state_spec],
        out_shape=[jax.ShapeDtypeStruct((n_tok, R_WIDTH), BF16), jax.ShapeDtypeStruct(s_lanes.shape, F32)],
        scratch_shapes=[pltpu.VMEM((6, n_new, LANES, b), F32), pltpu.VMEM((n_new, LANES, b), F32),
                        pltpu.VMEM((n_tok, LANES), F32)],
        input_output_aliases={} if whole else {10: 1},
        compiler_params=_cparams("arbitrary"),
        name="wkv7_lanes",
    )(*feats, P['lnx_g'], P['lnx_b'], s_lanes)
    return y, s_new


def _pair_kv(t):
    lane = lax.broadcasted_iota(jnp.int32, (1, LANES), 1)
    even = lane < HEAD_DIM
    sw = pltpu.roll(t, HEAD_DIM, 1)
    return jnp.where(even, t, sw), jnp.where(even, sw, t)


def _attend(qs, kcs, vcs, masks, sink_cols):
    n = range(len(qs))
    s = [jnp.where(masks[i], _dot_nt(qs[i], kcs[i]), -jnp.inf) for i in n]
    m = [jnp.maximum(jnp.max(s[i], axis=-1, keepdims=True), sink_cols[i]) for i in n]
    p = [jnp.exp(s[i] - m[i]) for i in n]
    den = [jnp.sum(p[i], axis=-1, keepdims=True) + jnp.exp(sink_cols[i] - m[i]) for i in n]
    o = [_dot(p[i], vcs[i]) for i in n]
    return [o[i] / den[i] for i in n]


def _attn_prompt_kernel(sink_ref, q_ref, kp_ref, kc_ref, vp_ref, vc_ref, o_ref, *, layer, q_blocks):
    n = pl.program_id(1)
    sink = lambda head: sink_ref[layer * A_HEADS + head]
    lane = lax.broadcasted_iota(jnp.int32, (1, LANES), 1)
    even = lane < HEAD_DIM
    ks = _pair_kv(jnp.concatenate([kp_ref[...], kc_ref[...]], axis=0))
    vs = _pair_kv(jnp.concatenate([vp_ref[...], vc_ref[...]], axis=0))
    qi = _pmod(lax.broadcasted_iota(jnp.int32, (2 * BLOCK, 2 * BLOCK), 0), BLOCK) + BLOCK
    kj = lax.broadcasted_iota(jnp.int32, (2 * BLOCK, 2 * BLOCK), 1)
    dist = qi - kj
    in_win = (dist >= 0) & (dist <= WINDOW)
    first_mask = in_win & ((n > 0) | (kj >= BLOCK))
    top = lax.broadcasted_iota(jnp.int32, (2 * BLOCK, 1), 0) < BLOCK
    blocks = range(Q_WIDTH // LANES)
    items = [(s, j) for s in range(q_blocks) for j in blocks]
    qs, kcs, vcs, masks, sink_cols = [], [], [], [], []
    for s, j in items:
        qb = q_ref[s * BLOCK:(s + 1) * BLOCK, j * LANES:(j + 1) * LANES] * ATTN_SCALE
        qs.append(jnp.concatenate([jnp.where(even, qb, 0.0), jnp.where(even, 0.0, qb)], axis=0))
        kvh = (2 * j) // GROUP
        kcs.append(ks[kvh][s * BLOCK:(s + 2) * BLOCK])
        vcs.append(vs[kvh][s * BLOCK:(s + 2) * BLOCK])
        masks.append(first_mask if s == 0 else in_win)
        sink_cols.append(jnp.where(top, sink(2 * j), sink(2 * j + 1)))
    o = _attend(qs, kcs, vcs, masks, sink_cols)
    for i, (s, j) in enumerate(items):
        o_ref[s * BLOCK:(s + 1) * BLOCK, j * LANES:(j + 1) * LANES] = jnp.where(
            even, o[i][:BLOCK], o[i][BLOCK:]).astype(o_ref.dtype)


def _attn_prompt(q, k, v, sinks, layer, batch, seq):
    q_blocks = ATTN_Q_BLOCKS
    rows = q_blocks * BLOCK
    steps = seq // rows
    cur = lambda b, n: (b * steps + n, 0)
    prv = lambda b, n: (b * steps * q_blocks + jnp.maximum(n * q_blocks - 1, 0), 0)
    return pl.pallas_call(
        functools.partial(_attn_prompt_kernel, layer=layer, q_blocks=q_blocks),
        grid=(batch, steps),
        in_specs=[pl.BlockSpec(memory_space=pltpu.SMEM), pl.BlockSpec((rows, Q_WIDTH), cur),
                  pl.BlockSpec((BLOCK, KV_WIDTH), prv), pl.BlockSpec((rows, KV_WIDTH), cur),
                  pl.BlockSpec((BLOCK, KV_WIDTH), prv), pl.BlockSpec((rows, KV_WIDTH), cur)],
        out_specs=pl.BlockSpec((rows, Q_WIDTH), cur),
        out_shape=jax.ShapeDtypeStruct((batch * seq, Q_WIDTH), BF16),
        compiler_params=_cparams("parallel", "parallel"),
        name="attn_prompt",
    )(sinks, q, k, k, v, v)


def _attn_sample_kernel(sink_ref, q_ref, kt_ref, kn_ref, vt_ref, vn_ref, o_ref, kt_out_ref, vt_out_ref,
                        *, layer, n_seq, wbuf, n_new, whole):
    assert wbuf == LANES
    win = (lambda ref, g, h: ref[layer, g, h]) if whole else (lambda ref, g, h: ref[g, h])
    lane = lax.broadcasted_iota(jnp.int32, (1, LANES), 1)
    even = lane < HEAD_DIM
    rows = GROUP * SAMPLE_PAD
    n_pad = 2 * SAMPLE_PAD
    qt = _pmod(lax.broadcasted_iota(jnp.int32, (rows, 1), 0), SAMPLE_PAD)
    dist_c = qt + wbuf - lax.broadcasted_iota(jnp.int32, (rows, wbuf), 1)
    dist_n = qt - lax.broadcasted_iota(jnp.int32, (rows, n_pad), 1)
    mask_c = (dist_c >= 0) & (dist_c <= WINDOW)
    mask_n = (dist_n >= 0) & (dist_n <= WINDOW)
    rid = _pdiv(lax.broadcasted_iota(jnp.int32, (rows, 1), 0), SAMPLE_PAD)
    sink_cols = []
    for h in range(KV_HEADS):
        col = jnp.zeros((rows, 1), F32)
        for gq in range(GROUP):
            col = jnp.where(rid == gq, sink_ref[layer * A_HEADS + h * GROUP + gq], col)
        sink_cols.append(col)
    unroll = 4
    pad = jnp.zeros((SAMPLE_PAD, KV_WIDTH), F32)
    blocks_per_kv = GROUP // 2
    keep = lane < wbuf - n_new
    sub = lax.broadcasted_iota(jnp.int32, (SAMPLE_PAD, 1), 0)

    def new_columns(t_new):
        low = jnp.where(sub >= SAMPLE_PAD - n_new, pltpu.roll(t_new, SAMPLE_PAD - n_new, 0), 0.0)
        return jnp.concatenate([jnp.zeros((wbuf - SAMPLE_PAD, KV_WIDTH), F32), low], axis=0).T

    def body(i, carry_):
        items = [(i * unroll + u, h) for u in range(unroll) for h in range(KV_HEADS)]
        n = range(len(items))
        qs, kc, vc, kn, vn = [], [], [], [], []
        for u in range(unroll):
            g = i * unroll + u
            kn2 = _pair_kv(jnp.concatenate([kn_ref[g], pad], axis=0))
            vn2 = _pair_kv(jnp.concatenate([vn_ref[g], pad], axis=0))
            for h in range(KV_HEADS):
                parts = []
                for j in range(h * blocks_per_kv, (h + 1) * blocks_per_kv):
                    qb = q_ref[g, :, j * LANES:(j + 1) * LANES] * ATTN_SCALE
                    parts += [jnp.where(even, qb, 0.0), jnp.where(even, 0.0, qb)]
                qs.append(jnp.concatenate(parts, axis=0))
                kt, vt = win(kt_ref, g, h), win(vt_ref, g, h)
                kc.append(jnp.concatenate([kt, kt], axis=0))
                vc.append(jnp.concatenate([vt, vt], axis=0))
                kn.append(kn2[h])
                vn.append(vn2[h])
        s_c = [jnp.where(mask_c, _dot(qs[j], kc[j]), -jnp.inf) for j in n]
        s_n = [jnp.where(mask_n, _dot_nt(qs[j], kn[j]), -jnp.inf) for j in n]
        m = [jnp.maximum(jnp.maximum(jnp.max(s_c[j], axis=-1, keepdims=True),
                                     jnp.max(s_n[j], axis=-1, keepdims=True)), sink_cols[items[j][1]])
             for j in n]
        p_c = [jnp.exp(s_c[j] - m[j]) for j in n]
        p_n = [jnp.exp(s_n[j] - m[j]) for j in n]
        den = [jnp.sum(p_c[j], axis=-1, keepdims=True) + jnp.sum(p_n[j], axis=-1, keepdims=True)
               + jnp.exp(sink_cols[items[j][1]] - m[j]) for j in n]
        o = [(_dot_nt(p_c[j], vc[j]) + _dot(p_n[j], vn[j])) / den[j] for j in n]
        for j, (g, h) in enumerate(items):
            for jj in range(blocks_per_kv):
                blk = h * blocks_per_kv + jj
                r0 = 2 * jj * SAMPLE_PAD
                o_ref[g, :, blk * LANES:(blk + 1) * LANES] = jnp.where(
                    even, o[j][r0:r0 + SAMPLE_PAD], o[j][r0 + SAMPLE_PAD:r0 + 2 * SAMPLE_PAD]).astype(o_ref.dtype)
        for u in range(unroll):
            g = i * unroll + u
            k_cols, v_cols = new_columns(kn_ref[g]), new_columns(vn_ref[g])
            for h in range(KV_HEADS):
                hs = slice(h * HEAD_DIM, (h + 1) * HEAD_DIM)
                for src, dst, cols in ((kt_ref, kt_out_ref, k_cols), (vt_ref, vt_out_ref, v_cols)):
                    slid = jnp.where(keep, pltpu.roll(win(src, g, h), wbuf - n_new, 1), cols[hs])
                    if whole:
                        dst[layer, g, h] = slid
                    else:
                        dst[g, h] = slid
            if whole:
                for l in range(kt_ref.shape[0]):
                    if l != layer:
                        kt_out_ref[l, g] = kt_ref[l, g]
                        vt_out_ref[l, g] = vt_ref[l, g]
        return carry_

    lax.fori_loop(0, n_seq // unroll, body, 0)


def _attn_sample(q, k, v, kt_all, vt_all, sinks, layer, n_seq, n_new):
    depth, b, _, _, wbuf = kt_all.shape
    whole = layer == 0
    idx = lambda i: (i, 0, 0)
    new = pl.BlockSpec((n_seq, SAMPLE_PAD, KV_WIDTH), idx)
    if whole:
        buf = pl.BlockSpec((depth, n_seq, KV_HEADS, HEAD_DIM, wbuf), lambda i: (0, i, 0, 0, 0))
    else:
        buf = pl.BlockSpec((None, n_seq, KV_HEADS, HEAD_DIM, wbuf), lambda i: (layer, i, 0, 0, 0))
    win_shape = jax.ShapeDtypeStruct(kt_all.shape, F32)
    return pl.pallas_call(
        functools.partial(_attn_sample_kernel, layer=layer, n_seq=n_seq, wbuf=wbuf, n_new=n_new, whole=whole),
        grid=(b // n_seq,),
        in_specs=[pl.BlockSpec(memory_space=pltpu.SMEM), pl.BlockSpec((n_seq, SAMPLE_PAD, Q_WIDTH), idx),
                  buf, new, buf, new],
        out_specs=[pl.BlockSpec((n_seq, SAMPLE_PAD, Q_WIDTH), idx), buf, buf],
        out_shape=[jax.ShapeDtypeStruct((b, SAMPLE_PAD, Q_WIDTH), BF16), win_shape, win_shape],
        input_output_aliases={} if whole else {2: 1, 4: 2},
        compiler_params=_cparams("parallel"),
        name="attn_sample",
    )(sinks, q, kt_all, k, vt_all, v)


def _layer_norm(x, g, b):
    mu = jnp.mean(x, axis=-1, keepdims=True)
    d = x - mu
    var = jnp.mean(d * d, axis=-1, keepdims=True)
    return d * lax.rsqrt(var + LN_EPS) * g + b


def _mix_ffn_kernel(x_ref, yr_ref, ya_ref, win_ref, wbr_ref, wba_ref, wo_ref, g1_ref, b1_ref,
                    wu_ref, wd_ref, g2_ref, b2_ref, o_ref, *, layer, ff_chunk):
    par = lambda ref: ref[layer:layer + 1, :]
    dotf = functools.partial(jnp.dot, preferred_element_type=F32)
    tm = x_ref.shape[0]
    groups = [slice(i * tm // ROW_GROUPS, (i + 1) * tm // ROW_GROUPS) for i in range(ROW_GROUPS)]
    n = range(ROW_GROUPS)
    x = [x_ref[g, :] for g in groups]
    xb = [x[i].astype(BF16) for i in n]
    gate_r = [jax.nn.sigmoid(dotf(xb[i], win_ref[:, QKV_END:QKV_END + D_MODEL])) for i in n]
    mix = [gate_r[i] * dotf(yr_ref[groups[i], :], wbr_ref[...]) for i in n]
    gate_a = [jax.nn.sigmoid(dotf(xb[i], win_ref[:, QKV_END + D_MODEL:])) for i in n]
    mix = [mix[i] + gate_a[i] * dotf(ya_ref[groups[i], :], wba_ref[...]) for i in n]
    x1 = [_layer_norm(ALPHA * x[i] + _dot(mix[i], wo_ref[...]), par(g1_ref), par(b1_ref)) for i in n]
    x1b = [x1[i].astype(BF16) for i in n]
    acc = [ALPHA * x1[i] for i in n]
    for c in range(D_FF // ff_chunk):
        cs = slice(c * ff_chunk, (c + 1) * ff_chunk)
        h = [jnp.maximum(dotf(x1b[i], wu_ref[:, cs]), 0.0) for i in n]
        acc = [acc[i] + _dot(h[i] * h[i], wd_ref[cs, :]) for i in n]
    for i in n:
        o_ref[groups[i], :] = _layer_norm(acc[i], par(g2_ref), par(b2_ref))


def _mix_ffn(x, yr, ya, P, layer, tm):
    n = x.shape[0]
    row = lambda i: (i, 0)
    ws = [P['w_in'], P['w_br_rwkv'], P['w_br_attn'], P['w_out'], P['ln1_g'], P['ln1_b'],
          P['w_ff_up'], P['w_ff_down'], P['ln2_g'], P['ln2_b']]
    wspec = lambda a: _layer_spec(a, layer, single_buffer=True) if a.ndim == 3 else _const_spec(a.shape)
    return pl.pallas_call(
        functools.partial(_mix_ffn_kernel, layer=layer, ff_chunk=1024),
        grid=(n // tm,),
        in_specs=[pl.BlockSpec((tm, D_MODEL), row), pl.BlockSpec((tm, R_WIDTH), row),
                  pl.BlockSpec((tm, Q_WIDTH), row)] + [wspec(a) for a in ws],
        out_specs=pl.BlockSpec((tm, D_MODEL), row),
        out_shape=jax.ShapeDtypeStruct((n, D_MODEL), F32),
        compiler_params=_cparams("parallel"),
        name="mix_ffn",
    )(x, yr, ya, *ws)


def _prepare_params(w_in, mu_shift, decay_base, decay_up, iclr_base, iclr_up, gate_up, k_k, k_a, r_k,
                    lnx_g, lnx_b, sinks, w_br_rwkv, w_br_attn, w_out, ln1_g, ln1_b, w_ff_up, w_ff_down,
                    ln2_g, ln2_b):
    depth = w_in.shape[0]
    zeros = jnp.zeros((depth, DECAY_LORA, R_WIDTH), F32)
    lora_up = jnp.concatenate([jnp.concatenate([decay_up, zeros], axis=2),
                               jnp.concatenate([zeros, iclr_up], axis=2)], axis=1)
    bf = lambda a: a.astype(BF16)
    return dict(
        w_in=bf(w_in), mu_shift=mu_shift, decay_base=decay_base, lora_up=bf(lora_up), iclr_base=iclr_base,
        gate_up=bf(gate_up), k_k=k_k, k_a=k_a, r_k=r_k.reshape(depth, R_WIDTH), lnx_g=lnx_g, lnx_b=lnx_b,
        sinks=sinks.reshape(depth * A_HEADS), w_br_rwkv=bf(w_br_rwkv), w_br_attn=bf(w_br_attn),
        w_out=bf(w_out), ln1_g=ln1_g, ln1_b=ln1_b, w_ff_up=bf(w_ff_up), w_ff_down=bf(w_ff_down),
        ln2_g=ln2_g, ln2_b=ln2_b)


def _prompt_layer(x, P, layer, tables, s_all, batch, seq, wbuf):
    tm = TOKEN_ROWS
    tp = INPROJ_ROWS
    shift0 = jnp.zeros((batch, SHIFT_W), F32)
    zs, zlast, q, k, v = _inproj(x, P, layer, tables, seq // tp, tp, shift0, seq_rows=tp, last_row=tp - 1)
    wkv_rows = tm // batch
    yr, s_all = _wkv(zs.reshape(batch, seq, SHIFT_W), s_all, P, layer, seq_rows=wkv_rows)
    ya = _attn_prompt(q, k, v, P['sinks'], layer, batch, seq)
    x = _mix_ffn(x, yr.reshape(batch * seq, R_WIDTH), ya, P, layer, tm)
    tail = lambda t, w: t.reshape(batch, seq, w)[:, seq - wbuf:].reshape(batch, wbuf, KV_HEADS, HEAD_DIM)
    return x, s_all, zlast.reshape(batch, SHIFT_W), tail(k, KV_WIDTH), tail(v, KV_WIDTH)


def _sample_layer(x, P, layer, tables, shift_prev, s_all, kt_all, vt_all, batch, seq):
    n = batch * seq
    tm = min(n, TOKEN_ROWS)
    n_seq = ATTN_SEQS
    zs, zlast, q, k, v = _inproj(x, P, layer, tables, 1, tm, shift_prev, seq_rows=seq, last_row=seq - 1)
    yr, s_all = _wkv_short(zs, s_all, P, layer, seq)
    pad = lambda t, w: jnp.pad(t.reshape(batch, seq, w), ((0, 0), (0, SAMPLE_PAD - seq), (0, 0)))
    ya, kt_all, vt_all = _attn_sample(pad(q, Q_WIDTH), pad(k, KV_WIDTH), pad(v, KV_WIDTH), kt_all, vt_all,
                                      P['sinks'], layer, n_seq, seq)
    x = _mix_ffn(x, yr, ya[:, :seq].reshape(n, Q_WIDTH), P, layer, tm)
    return x, s_all, kt_all, vt_all, zlast


def kernel(x_prompt, x_sample, state_wkv, state_shift, cache_k_win, cache_v_win, w_in, mu_shift, decay_base, decay_up, iclr_base, iclr_up, gate_up, k_k, k_a, r_k, lnx_g, lnx_b, sinks, w_br_rwkv, w_br_attn, w_out, ln1_g, ln1_b, w_ff_up, w_ff_down, ln2_g, ln2_b):
    bp, tp, _ = x_prompt.shape
    bs, ts, _ = x_sample.shape
    wbuf = cache_k_win.shape[2]
    half = HEAD_DIM // 2
    inv_freq = ROPE_THETA ** (-jnp.arange(half, dtype=F32) / half)
    tab_p = _rope_tables(inv_freq, tp, 0, tp)
    tab_s = _rope_tables(inv_freq, min(bs * ts, 512), PAST_LEN, ts)

    hp = x_prompt.reshape(bp * tp, D_MODEL)
    hs = x_sample.reshape(bs * ts, D_MODEL)
    outs_p, outs_s = [], []
    P = _prepare_params(w_in, mu_shift, decay_base, decay_up, iclr_base, iclr_up, gate_up, k_k, k_a, r_k,
                        lnx_g, lnx_b, sinks, w_br_rwkv, w_br_attn, w_out, ln1_g, ln1_b, w_ff_up,
                        w_ff_down, ln2_g, ln2_b)
    to_lanes = lambda c: jnp.transpose(c, (0, 1, 3, 4, 2))
    from_lanes = lambda c: jnp.transpose(c, (0, 1, 4, 2, 3))
    kt_all, vt_all = to_lanes(cache_k_win), to_lanes(cache_v_win)
    s_all_p = jnp.zeros((DEPTH, bp, R_HEADS, R_HEAD_DIM, R_HEAD_DIM), F32)
    s_all_s = jnp.transpose(state_wkv, (0, 2, 3, 4, 1))
    for l in range(DEPTH):
        hp, s_all_p, *st = _prompt_layer(hp, P, l, tab_p, s_all_p, bp, tp, wbuf)
        outs_p.append(st)
        hs, s_all_s, kt_all, vt_all, zlast = _sample_layer(hs, P, l, tab_s, state_shift[l], s_all_s, kt_all, vt_all,
                                                           bs, ts)
        outs_s.append(zlast)
    stack = lambda outs, i: jnp.stack([o[i] for o in outs])
    y_p = hp.reshape(bp, tp, D_MODEL)
    y_s = hs.reshape(bs, ts, D_MODEL)
    return (y_p, y_s,
            s_all_p, stack(outs_p, 0), stack(outs_p, 1), stack(outs_p, 2),
            jnp.transpose(s_all_s, (0, 4, 1, 2, 3)), jnp.stack(outs_s), from_lanes(kt_all), from_lanes(vt_all))
```
